```python
import jax, jax.numpy as jnp
from jax import lax
import numpy as np

D_MODEL = 1024
BATCH = 8
SEQ = 2048
DEPTH = 1
DEC_BATCH = 128
DEC_SEQ = 8
PAST_LEN = 16384
PAGE_SIZE = 128

POOL_WIDTH = D_MODEL
POOL_WINDOWS = (2, 4, 8, 16)
N_POOL_GROUPS = len(POOL_WINDOWS)
POOL_GROUP = POOL_WIDTH // N_POOL_GROUPS
POOL_HIST = max(POOL_WINDOWS) - 1
SSD_WIDTH = 2 * D_MODEL
SSD_HEAD_DIM = 64
SSD_HEADS = SSD_WIDTH // SSD_HEAD_DIM
SSD_GROUPS = 4
SSD_HPG = SSD_HEADS // SSD_GROUPS
SSD_STATE = 128
CONV_WIDTH = 4
CONV_DIM = SSD_WIDTH + 2 * SSD_GROUPS * SSD_STATE
SSD_CHUNK = 128
MEM_LEN = 256
ATT_HEADS = 4
ATT_WIDTH = D_MODEL
ATT_HEAD_DIM = ATT_WIDTH // ATT_HEADS
N_BRANCH = 3
EPS = 1e-6
IN_SPLITS = (POOL_WIDTH, POOL_WIDTH, SSD_WIDTH, CONV_DIM, SSD_HEADS, ATT_WIDTH, ATT_WIDTH, N_BRANCH * D_MODEL)
IN_COLS = sum(IN_SPLITS)

kernel_name = 'hybrid_pool_ssd_memattn_decoder_step'


def _rmsnorm(x, w):
    xf = x.astype(jnp.float32)
    y = xf * lax.rsqrt(jnp.mean(xf * xf, axis=-1, keepdims=True) + EPS)
    return (y * w.astype(jnp.float32)).astype(x.dtype)


def _split_points():
    pts, acc = [], 0
    for s in IN_SPLITS[:-1]:
        acc += s
        pts.append(acc)
    return pts


def _pool_mixer(ext, pos0, w_grp, scale):
    b, T, C = ext.shape
    L = T - POOL_HIST
    cs = jnp.cumsum(ext.astype(jnp.float32), axis=1)
    cs0 = jnp.concatenate([jnp.zeros((b, 1, C), jnp.float32), cs], axis=1)
    pos = pos0 + jnp.arange(L)
    outs = []
    for g, w in enumerate(POOL_WINDOWS):
        lo, hi = g * POOL_GROUP, (g + 1) * POOL_GROUP
        win = cs0[:, POOL_HIST + 1:POOL_HIST + 1 + L, lo:hi] - cs0[:, POOL_HIST + 1 - w:POOL_HIST + 1 - w + L, lo:hi]
        cnt = jnp.minimum(w, pos + 1).astype(jnp.float32)
        outs.append(win / cnt[None, :, None])
    pooled = jnp.concatenate(outs, axis=-1)
    u = ext[:, POOL_HIST:]
    d = (pooled - u.astype(jnp.float32)).astype(ext.dtype).reshape(b, L, N_POOL_GROUPS, POOL_GROUP)
    y = jnp.einsum('blgc,gcd->blgd', d, w_grp).reshape(b, L, C)
    return y * scale


def _causal_dwconv(ext, w, bias):
    L = ext.shape[1] - (CONV_WIDTH - 1)
    out = bias
    for k in range(CONV_WIDTH):
        out = out + ext[:, k:k + L] * w[k]
    return out


def _ssd_scan(x, dt, A, B, C, h0):
    b, L, G, R, P = x.shape
    N = B.shape[-1]
    Q = SSD_CHUNK if L % SSD_CHUNK == 0 else L
    nc = L // Q
    xf = x.astype(jnp.float32).reshape(b, nc, Q, G, R, P)
    dtc = dt.reshape(b, nc, Q, G, R)
    Bc = B.astype(jnp.float32).reshape(b, nc, Q, G, N)
    Cc = C.astype(jnp.float32).reshape(b, nc, Q, G, N)
    acs = jnp.cumsum(dtc * A, axis=2)
    acs_t = jnp.moveaxis(acs, 2, -1)
    seg = acs_t[..., :, None] - acs_t[..., None, :]
    mask = jnp.tril(jnp.ones((Q, Q), dtype=bool))
    Lmat = jnp.where(mask, jnp.exp(jnp.where(mask, seg, 0.0)), 0.0)
    xdt = xf * dtc[..., None]
    CB = jnp.einsum('bcqgn,bckgn->bcgqk', Cc, Bc)
    y_diag = jnp.einsum('bcgqk,bcgrqk,bckgrp->bcqgrp', CB, Lmat, xdt)
    decay_out = jnp.exp(acs[:, :, -1:] - acs)
    states = jnp.einsum('bckgn,bckgr,bckgrp->bcgrpn', Bc, decay_out, xdt)
    chunk_decay = jnp.exp(acs[:, :, -1])

    def step(h, inp):
        cd, st = inp
        return h * cd[..., None, None] + st, h

    h_final, h_prev = lax.scan(step, h0.astype(jnp.float32),
                               (jnp.moveaxis(chunk_decay, 1, 0), jnp.moveaxis(states, 1, 0)))
    h_prev = jnp.moveaxis(h_prev, 0, 1)
    y_off = jnp.einsum('bcqgn,bcgrpn,bcqgr->bcqgrp', Cc, h_prev, jnp.exp(acs))
    y = (y_diag + y_off).reshape(b, L, G, R, P)
    return y, h_final.astype(h0.dtype)


def _mem_kv(mem, mem_norm_w, w_mem_k, w_mem_v):
    b, M, _ = mem.shape
    mh = _rmsnorm(mem, mem_norm_w)
    k = jnp.einsum('bmd,de->bme', mh, w_mem_k).reshape(b, M, ATT_HEADS, ATT_HEAD_DIM)
    v = jnp.einsum('bmd,de->bme', mh, w_mem_v).reshape(b, M, ATT_HEADS, ATT_HEAD_DIM)
    return k, v


def _mem_attention(q, k, v):
    b, L, _ = q.shape
    qh = q.reshape(b, L, ATT_HEADS, ATT_HEAD_DIM)
    s = jnp.einsum('blhd,bmhd->bhlm', qh, k).astype(jnp.float32) * (ATT_HEAD_DIM ** -0.5)
    p = jax.nn.softmax(s, axis=-1).astype(v.dtype)
    return jnp.einsum('bhlm,bmhd->blhd', p, v).reshape(b, L, ATT_WIDTH)


def _layer(x, hist_pool, hist_conv, h0, mem_k, mem_v, pos0, norm_w, w_in, w_pool_grp, pool_scale,
           conv_w, conv_b, dt_bias, a_log, d_skip, ssd_norm_w, w_pool_out, w_ssd_out, w_att_out, w_out):
    b, L, _ = x.shape
    h = _rmsnorm(x, norm_w)
    proj = jnp.einsum('bld,de->ble', h, w_in)
    u_pool, z_pool, z_ssd, xbc, dt_raw, q, z_att, gate_logits = jnp.split(proj, _split_points(), axis=-1)
    pool_ext = jnp.concatenate([hist_pool.astype(x.dtype), u_pool], axis=1)
    y_pool = _pool_mixer(pool_ext, pos0, w_pool_grp, pool_scale) * jax.nn.silu(z_pool)
    conv_ext = jnp.concatenate([hist_conv.astype(x.dtype), xbc], axis=1)
    xbc_c = jax.nn.silu(_causal_dwconv(conv_ext, conv_w, conv_b))
    xs, Bm, Cm = jnp.split(xbc_c, [SSD_WIDTH, SSD_WIDTH + SSD_GROUPS * SSD_STATE], axis=-1)
    xs = xs.reshape(b, L, SSD_GROUPS, SSD_HPG, SSD_HEAD_DIM)
    Bm = Bm.reshape(b, L, SSD_GROUPS, SSD_STATE)
    Cm = Cm.reshape(b, L, SSD_GROUPS, SSD_STATE)
    dt = jax.nn.softplus(dt_raw.astype(jnp.float32) + dt_bias.astype(jnp.float32)).reshape(b, L, SSD_GROUPS, SSD_HPG)
    A = -jnp.exp(a_log.astype(jnp.float32)).reshape(SSD_GROUPS, SSD_HPG)
    y_ssm, h_new = _ssd_scan(xs, dt, A, Bm, Cm, h0)
    y_ssm = y_ssm + d_skip.astype(jnp.float32).reshape(SSD_GROUPS, SSD_HPG, 1) * xs.astype(jnp.float32)
    y_ssm = y_ssm.reshape(b, L, SSD_WIDTH).astype(x.dtype)
    y_ssd = _rmsnorm(y_ssm * jax.nn.silu(z_ssd), ssd_norm_w)
    y_att = _mem_attention(q, mem_k, mem_v) * jax.nn.silu(z_att)
    gates = jax.nn.sigmoid(gate_logits.astype(jnp.float32)).astype(x.dtype).reshape(b, L, N_BRANCH, D_MODEL)
    merged = (gates[:, :, 0] * (y_pool @ w_pool_out)
              + gates[:, :, 1] * (y_ssd @ w_ssd_out)
              + gates[:, :, 2] * (y_att @ w_att_out))
    x_out = x + merged @ w_out
    return x_out, pool_ext[:, -POOL_HIST:], conv_ext[:, -(CONV_WIDTH - 1):], h_new


def setup_inputs(seed: int = 0) -> dict:
    key = jax.random.key(seed)
    ks = jax.random.split(key, 32)
    f32 = jnp.float32
    nrm = lambda k, shape, s: jax.random.normal(k, shape, f32) * s
    dt_init = jnp.exp(jax.random.uniform(ks[14], (DEPTH, SSD_HEADS), f32, np.log(1e-3), np.log(1e-1)))
    return {
        'x_prompt': nrm(ks[0], (BATCH, SEQ, D_MODEL), 1.0),
        'x_sample': nrm(ks[1], (DEC_BATCH, DEC_SEQ, D_MODEL), 1.0),
        'mem_prompt': nrm(ks[2], (BATCH, MEM_LEN, D_MODEL), 1.0),
        'state_pool': nrm(ks[3], (DEPTH, DEC_BATCH, POOL_HIST, POOL_WIDTH), 1.0),
        'state_conv': nrm(ks[4], (DEPTH, DEC_BATCH, CONV_WIDTH - 1, CONV_DIM), 1.0),
        'state_ssm': nrm(ks[5], (DEPTH, DEC_BATCH, SSD_GROUPS, SSD_HPG, SSD_HEAD_DIM, SSD_STATE), 0.1),
        'cache_mem_k': nrm(ks[6], (DEPTH, DEC_BATCH, MEM_LEN, ATT_HEADS, ATT_HEAD_DIM), 1.0),
        'cache_mem_v': nrm(ks[7], (DEPTH, DEC_BATCH, MEM_LEN, ATT_HEADS, ATT_HEAD_DIM), 1.0),
        'norm_w': 1.0 + nrm(ks[8], (DEPTH, D_MODEL), 0.02),
        'w_in': nrm(ks[9], (DEPTH, D_MODEL, IN_COLS), D_MODEL ** -0.5),
        'w_pool_grp': nrm(ks[10], (DEPTH, N_POOL_GROUPS, POOL_GROUP, POOL_GROUP), POOL_GROUP ** -0.5),
        'pool_scale': 1.0 + nrm(ks[11], (DEPTH, POOL_WIDTH), 0.1),
        'conv_w': nrm(ks[12], (DEPTH, CONV_WIDTH, CONV_DIM), CONV_WIDTH ** -0.5),
        'conv_b': nrm(ks[13], (DEPTH, CONV_DIM), 0.02),
        'dt_bias': dt_init + jnp.log(-jnp.expm1(-dt_init)),
        'a_log': jnp.log(jax.random.uniform(ks[15], (DEPTH, SSD_HEADS), f32, 1.0, 16.0)),
        'd_skip': 1.0 + nrm(ks[16], (DEPTH, SSD_HEADS), 0.1),
        'ssd_norm_w': 1.0 + nrm(ks[17], (DEPTH, SSD_WIDTH), 0.02),
        'mem_norm_w': 1.0 + nrm(ks[18], (DEPTH, D_MODEL), 0.02),
        'w_mem_k': nrm(ks[19], (DEPTH, D_MODEL, ATT_WIDTH), D_MODEL ** -0.5),
        'w_mem_v': nrm(ks[20], (DEPTH, D_MODEL, ATT_WIDTH), D_MODEL ** -0.5),
        'w_pool_out': nrm(ks[21], (DEPTH, POOL_WIDTH, D_MODEL), POOL_WIDTH ** -0.5),
        'w_ssd_out': nrm(ks[22], (DEPTH, SSD_WIDTH, D_MODEL), SSD_WIDTH ** -0.5),
        'w_att_out': nrm(ks[23], (DEPTH, ATT_WIDTH, D_MODEL), ATT_WIDTH ** -0.5),
        'w_out': nrm(ks[24], (DEPTH, D_MODEL, D_MODEL), D_MODEL ** -0.5),
        'final_norm_w': 1.0 + nrm(ks[25], (D_MODEL,), 0.02),
    }


def reference(x_prompt, x_sample, mem_prompt, state_pool, state_conv, state_ssm, cache_mem_k, cache_mem_v,
              norm_w, w_in, w_pool_grp, pool_scale, conv_w, conv_b, dt_bias, a_log, d_skip, ssd_norm_w,
              mem_norm_w, w_mem_k, w_mem_v, w_pool_out, w_ssd_out, w_att_out, w_out, final_norm_w):
    xp, xs = x_prompt, x_sample
    bp = xp.shape[0]
    pool_p, conv_p, ssm_p, mk_p, mv_p = [], [], [], [], []
    pool_s, conv_s, ssm_s = [], [], []
    for l in range(DEPTH):
        lw = (norm_w[l], w_in[l], w_pool_grp[l], pool_scale[l], conv_w[l], conv_b[l], dt_bias[l], a_log[l],
              d_skip[l], ssd_norm_w[l], w_pool_out[l], w_ssd_out[l], w_att_out[l], w_out[l])
        mk, mv = _mem_kv(mem_prompt, mem_norm_w[l], w_mem_k[l], w_mem_v[l])
        hp0 = jnp.zeros((bp, POOL_HIST, POOL_WIDTH), xp.dtype)
        hc0 = jnp.zeros((bp, CONV_WIDTH - 1, CONV_DIM), xp.dtype)
        hs0 = jnp.zeros((bp, SSD_GROUPS, SSD_HPG, SSD_HEAD_DIM, SSD_STATE), xp.dtype)
        xp, sp, sc, ss = _layer(xp, hp0, hc0, hs0, mk, mv, 0, *lw)
        pool_p.append(sp); conv_p.append(sc); ssm_p.append(ss); mk_p.append(mk); mv_p.append(mv)
        xs, sp, sc, ss = _layer(xs, state_pool[l], state_conv[l], state_ssm[l], cache_mem_k[l], cache_mem_v[l],
                                PAST_LEN, *lw)
        pool_s.append(sp); conv_s.append(sc); ssm_s.append(ss)
    y_prompt = _rmsnorm(xp, final_norm_w)
    y_sample = _rmsnorm(xs, final_norm_w)
    return (y_prompt, y_sample, jnp.stack(pool_p), jnp.stack(conv_p), jnp.stack(ssm_p), jnp.stack(mk_p),
            jnp.stack(mv_p), jnp.stack(pool_s), jnp.stack(conv_s), jnp.stack(ssm_s))
```

```python
import functools

import jax
import jax.numpy as jnp
from jax import lax
from jax.experimental import pallas as pl
from jax.experimental.pallas import tpu as pltpu

F32 = jnp.float32
BF16 = jnp.bfloat16

D_MODEL = 1024
POOL_WINDOWS = (2, 4, 8, 16)
POOL_GROUP = 256
POOL_HIST = 15
POOL_PAD = 16
SSD_WIDTH = 2048
SSD_HEADS = 32
SSD_HEAD_DIM = 64
SSD_GROUPS = 4
SSD_STATE = 128
CONV_WIDTH = 4
CONV_DIM = 3072
CONV_PAD = 8
SSD_CHUNK = 128
MEM_LEN = 256
ATT_HEADS = 4
ATT_HEAD_DIM = 256
EPS = 1e-6
LANES = 128
MAIN_COLS = 12288
VMEM_LIMIT = 56 * 1024 * 1024

_NT = (((1,), (1,)), ((), ()))
_TN = (((0,), (0,)), ((), ()))


def _sigmoid(x):
    return 1.0 / (1.0 + jnp.exp(-x))


def _silu(x):
    return x * _sigmoid(x)


def _softplus(x):
    return jnp.maximum(x, 0.0) + jnp.log1p(jnp.exp(-jnp.abs(x)))


def _rms(x, w):
    return x * lax.rsqrt(jnp.mean(x * x, axis=-1, keepdims=True) + EPS) * w


def _dot(a, b):
    return jnp.dot(a, b, preferred_element_type=F32)


def _memkv_kernel(mem_ref, nw_ref, wk_ref, wv_ref, k_ref, v_ref):
    mh = _rms(mem_ref[0], nw_ref[...]).astype(BF16)
    k_ref[0] = _dot(mh, wk_ref[...])
    v_ref[0] = _dot(mh, wv_ref[...])


def _memkv(mem, nw, wk, wv):
    b, m, d = mem.shape
    full = lambda shape: pl.BlockSpec(shape, lambda i: (0,) * len(shape))
    blk = pl.BlockSpec((1, m, d), lambda i: (i, 0, 0))
    return pl.pallas_call(
        _memkv_kernel,
        grid=(b,),
        in_specs=[blk, full((1, d)), full((d, d)), full((d, d))],
        out_specs=[blk, blk],
        out_shape=[jax.ShapeDtypeStruct((b, m, d), F32)] * 2,
        compiler_params=pltpu.CompilerParams(dimension_semantics=("arbitrary",), vmem_limit_bytes=VMEM_LIMIT),
        name="memkv",
    )(mem, nw, wk, wv)


INPROJ_ROWS = 256
INPROJ_COL_CHUNK = 1024


def _inproj_kernel(x_ref, nw_ref, w_ref, wdt_ref, main_ref, dt_ref):
    h = _rms(x_ref[...], nw_ref[...]).astype(BF16)
    dt_ref[...] = _dot(h, wdt_ref[...])
    for c in range(MAIN_COLS // INPROJ_COL_CHUNK):
        cols = slice(c * INPROJ_COL_CHUNK, (c + 1) * INPROJ_COL_CHUNK)
        main_ref[:, cols] = _dot(h, w_ref[:, cols]).astype(BF16)


def _inproj(x2d, nw, w_main, w_dt):
    m = x2d.shape[0]
    resident = lambda shape: pl.BlockSpec(shape, lambda i: (0,) * len(shape), pipeline_mode=pl.Buffered(1))
    return pl.pallas_call(
        _inproj_kernel,
        grid=(m // INPROJ_ROWS,),
        in_specs=[
            pl.BlockSpec((INPROJ_ROWS, D_MODEL), lambda i: (i, 0)),
            resident((1, D_MODEL)),
            resident((D_MODEL, MAIN_COLS)),
            resident((D_MODEL, LANES)),
        ],
        out_specs=[
            pl.BlockSpec((INPROJ_ROWS, MAIN_COLS), lambda i: (i, 0)),
            pl.BlockSpec((INPROJ_ROWS, LANES), lambda i: (i, 0)),
        ],
        out_shape=[jax.ShapeDtypeStruct((m, MAIN_COLS), BF16), jax.ShapeDtypeStruct((m, LANES), F32)],
        compiler_params=pltpu.CompilerParams(dimension_semantics=("arbitrary",), vmem_limit_bytes=VMEM_LIMIT),
        name="inproj",
    )(x2d, nw, w_main, w_dt)


def _seq_kernel(nb, t, q, carry, pos0, *refs):
    refs = list(refs)
    u_ref, zp_ref, zs_ref, xbc_ref, dt_ref, q_ref, za_ref = refs[:7]
    refs = refs[7:]
    if not carry:
        ph_ref, ch_ref, hin_ref = refs[:3]
        refs = refs[3:]
    (k_ref, v_ref, wgrp_ref, pscale_ref, convw_ref, convb_ref, dtb_ref, alog_ref, dexp_ref, ssdnw_ref,
     ypool_ref, yssd_ref, yatt_ref, pool_o_ref, conv_o_ref, ssm_o_ref,
     pext, cext, xs_scr, b_scr, c_scr, dt_scr, y_scr, q_scr, att_scr) = refs[:25]
    h_scr = refs[25] if carry else None

    rows = nb * t
    s = pl.program_id(1)
    last = pl.num_programs(1) - 1

    if carry:
        @pl.when(s == 0)
        def _():
            pext[:, 0:POOL_PAD, :] = jnp.zeros((nb, POOL_PAD, D_MODEL), F32)
            cext[:, 0:CONV_PAD, :] = jnp.zeros((nb, CONV_PAD, CONV_DIM), F32)
            h_scr[...] = jnp.zeros(h_scr.shape, F32)
    else:
        pext[:, 0:1, :] = jnp.zeros((nb, 1, D_MODEL), F32)
        pext[:, 1:POOL_PAD, :] = ph_ref[...]
        cext[:, 0:CONV_PAD - 3, :] = jnp.zeros((nb, CONV_PAD - 3, CONV_DIM), F32)
        cext[:, CONV_PAD - 3:CONV_PAD, :] = ch_ref[...]

    u = u_ref[...].astype(F32).reshape(nb, t, D_MODEL)
    pext[:, POOL_PAD:POOL_PAD + t, :] = u
    pos = pos0 + s * t + lax.broadcasted_iota(jnp.int32, (1, t, 1), 1)
    ys = []
    for g, w in enumerate(POOL_WINDOWS):
        cols = slice(g * POOL_GROUP, (g + 1) * POOL_GROUP)
        win = pext[:, POOL_PAD:POOL_PAD + t, cols]
        for j in range(1, w):
            win = win + pext[:, POOL_PAD - j:POOL_PAD - j + t, cols]
        cnt = jnp.minimum(w, pos + 1).astype(F32)
        d = (win / cnt - u[:, :, cols]).astype(BF16).reshape(rows, POOL_GROUP)
        ys.append(_dot(d, wgrp_ref[g]))
    y_pool = jnp.concatenate(ys, axis=1) * pscale_ref[...] * _silu(zp_ref[...].astype(F32))
    ypool_ref[...] = y_pool.astype(BF16)

    xbc = xbc_ref[...].astype(F32).reshape(nb, t, CONV_DIM)
    cext[:, CONV_PAD:CONV_PAD + t, :] = xbc
    conv = convb_ref[...].reshape(1, 1, CONV_DIM)
    for kk in range(CONV_WIDTH):
        lo = CONV_PAD - (CONV_WIDTH - 1) + kk
        conv = conv + cext[:, lo:lo + t, :] * convw_ref[kk:kk + 1, :].reshape(1, 1, CONV_DIM)
    conv = _silu(conv)
    xs_scr[...] = conv[:, :, :SSD_WIDTH]
    b_scr[...] = conv[:, :, SSD_WIDTH:SSD_WIDTH + SSD_GROUPS * SSD_STATE]
    c_scr[...] = conv[:, :, SSD_WIDTH + SSD_GROUPS * SSD_STATE:]
    dt_scr[...] = _softplus(dt_ref[...] + dtb_ref[...]).reshape(nb, t, LANES)

    def write_hist():
        pool_o_ref[...] = pext[:, t + POOL_PAD - POOL_HIST:t + POOL_PAD, :]
        conv_o_ref[...] = cext[:, t + CONV_PAD - 3:t + CONV_PAD, :]

    if carry:
        pl.when(s == last)(write_hist)
        carry_p = pext[:, t:t + POOL_PAD, :]
        carry_c = cext[:, t:t + CONV_PAD, :]
        pext[:, 0:POOL_PAD, :] = carry_p
        cext[:, 0:CONV_PAD, :] = carry_c
    else:
        write_hist()

    a_neg = -jnp.exp(alog_ref[...])
    rq = lax.broadcasted_iota(jnp.int32, (q, q), 0)
    cq = lax.broadcasted_iota(jnp.int32, (q, q), 1)
    tril = rq >= cq
    tri_f = tril.astype(F32)
    lane_lo = lax.broadcasted_iota(jnp.int32, (1, LANES), 1) < SSD_HEAD_DIM
    row_lo = lax.broadcasted_iota(jnp.int32, (LANES, 1), 0) < SSD_HEAD_DIM
    nchunk = t // q

    def chunk_body(i, _):
        b = i // nchunk
        r0 = pl.multiple_of((i % nchunk) * q, q)
        rsl = pl.ds(r0, q)
        dtc = dt_scr[b, rsl, :]
        acs = jnp.dot(tri_f, dtc * a_neg, precision=lax.Precision.HIGHEST, preferred_element_type=F32)
        acs_t = acs.T
        dt_t = dtc.T
        ea = jnp.exp(acs)
        tot = acs[q - 1:q, :]
        wdec = dtc * jnp.exp(tot - acs)
        cdec = jnp.exp(tot)
        for g in range(SSD_GROUPS):
            gsl = slice(g * SSD_STATE, (g + 1) * SSD_STATE)
            bg = b_scr[b, rsl, gsl]
            cg = c_scr[b, rsl, gsl]
            cb = lax.dot_general(cg.astype(BF16), bg.astype(BF16), _NT, preferred_element_type=F32)
            for jp in range(SSD_HEADS // SSD_GROUPS // 2):
                j = g * (SSD_HEADS // SSD_GROUPS // 2) + jp
                lsl = slice(j * LANES, (j + 1) * LANES)
                xp = xs_scr[b, rsl, lsl]
                x_pair = (jnp.where(lane_lo, xp, 0.0).astype(BF16), jnp.where(lane_lo, 0.0, xp).astype(BF16))
                if carry:
                    hp = h_scr[lsl, :]
                else:
                    hp = hin_ref[b, lsl, :]
                h_bd = jnp.concatenate([jnp.where(row_lo, hp, 0.0), jnp.where(row_lo, 0.0, hp)], axis=1).astype(BF16)
                y = dexp_ref[:, lsl] * xp
                ceas = []
                hnew = hp * jnp.where(row_lo, cdec[:, 2 * j:2 * j + 1], cdec[:, 2 * j + 1:2 * j + 2])
                for hh in range(2):
                    r = 2 * j + hh
                    seg = acs[:, r:r + 1] - acs_t[r:r + 1, :]
                    lmat = jnp.where(tril, jnp.exp(jnp.where(tril, seg, 0.0)), 0.0)
                    m = (cb * lmat * dt_t[r:r + 1, :]).astype(BF16)
                    y = y + _dot(m, x_pair[hh])
                    ceas.append((cg * ea[:, r:r + 1]).astype(BF16))
                    bw = (bg * wdec[:, r:r + 1]).astype(BF16)
                    hnew = hnew + lax.dot_general(x_pair[hh], bw, _TN, preferred_element_type=F32)
                y = y + lax.dot_general(jnp.concatenate(ceas, axis=1), h_bd, _NT, preferred_element_type=F32)
                y_scr[b, rsl, lsl] = y
                if carry:
                    h_scr[lsl, :] = hnew
                else:
                    ssm_o_ref[b, lsl, :] = hnew
        return 0

    lax.fori_loop(0, nb * nchunk, chunk_body, 0)
    if carry:
        @pl.when(s == last)
        def _():
            ssm_o_ref[0] = h_scr[...]

    yz = y_scr[...].reshape(rows, SSD_WIDTH) * _silu(zs_ref[...].astype(F32))
    yssd_ref[...] = _rms(yz, ssdnw_ref[...]).astype(BF16)

    q_scr[...] = q_ref[...].astype(F32).reshape(nb, t, D_MODEL)
    scale = ATT_HEAD_DIM ** -0.5
    for b in range(nb):
        qb = q_scr[b].astype(BF16)
        kb = k_ref[b].astype(BF16)
        vb = v_ref[b].astype(BF16)
        outs = []
        for hd in range(ATT_HEADS):
            hsl = slice(hd * ATT_HEAD_DIM, (hd + 1) * ATT_HEAD_DIM)
            sc = lax.dot_general(qb[:, hsl], kb[:, hsl], _NT, preferred_element_type=F32) * scale
            e = jnp.exp(sc - jnp.max(sc, axis=-1, keepdims=True))
            p = e / jnp.sum(e, axis=-1, keepdims=True)
            outs.append(_dot(p.astype(BF16), vb[:, hsl]))
        att_scr[b] = jnp.concatenate(outs, axis=1)
    y_att = att_scr[...].reshape(rows, D_MODEL) * _silu(za_ref[...].astype(F32))
    yatt_ref[...] = y_att.astype(BF16)


def _seqpart(main, dt, k, v, states, params, *, nblk, ntile, nb, t, q, pos0):
    carry = states is None
    rows = nb * t
    m = main.shape[0]
    nseq = nblk * nb
    rowblk = lambda i, s: i * ntile + s

    def col_spec(width, idx):
        return pl.BlockSpec((rows, width), lambda i, s: (rowblk(i, s), idx))

    def seq_spec(shape):
        return pl.BlockSpec((nb,) + shape, lambda i, s: (i, 0, 0))

    def const_spec(a):
        return pl.BlockSpec(a.shape, lambda i, s: (0,) * a.ndim)

    in_specs = [
        col_spec(1024, 8),
        col_spec(1024, 9),
        col_spec(2048, 3),
        col_spec(3072, 0),
        pl.BlockSpec((rows, LANES), lambda i, s: (rowblk(i, s), 0)),
        col_spec(1024, 10),
        col_spec(1024, 11),
    ]
    args = [main, main, main, main, dt, main, main]
    if not carry:
        in_specs += [seq_spec((POOL_HIST, D_MODEL)), seq_spec((CONV_WIDTH - 1, CONV_DIM)),
                     seq_spec((SSD_WIDTH, SSD_STATE))]
        args += list(states)
    in_specs += [seq_spec((MEM_LEN, D_MODEL)), seq_spec((MEM_LEN, D_MODEL))]
    args += [k, v]
    in_specs += [const_spec(p) for p in params]
    args += list(params)

    out_specs = [
        pl.BlockSpec((rows, D_MODEL), lambda i, s: (rowblk(i, s), 0)),
        pl.BlockSpec((rows, SSD_WIDTH), lambda i, s: (rowblk(i, s), 0)),
        pl.BlockSpec((rows, D_MODEL), lambda i, s: (rowblk(i, s), 0)),
        seq_spec((POOL_HIST, D_MODEL)),
        seq_spec((CONV_WIDTH - 1, CONV_DIM)),
        seq_spec((SSD_WIDTH, SSD_STATE)),
    ]
    out_shape = [
        jax.ShapeDtypeStruct((m, D_MODEL), BF16),
        jax.ShapeDtypeStruct((m, SSD_WIDTH), BF16),
        jax.ShapeDtypeStruct((m, D_MODEL), BF16),
        jax.ShapeDtypeStruct((nseq, POOL_HIST, D_MODEL), F32),
        jax.ShapeDtypeStruct((nseq, CONV_WIDTH - 1, CONV_DIM), F32),
        jax.ShapeDtypeStruct((nseq, SSD_WIDTH, SSD_STATE), F32),
    ]
    scratch = [
        pltpu.VMEM((nb, POOL_PAD + t, D_MODEL), F32),
        pltpu.VMEM((nb, CONV_PAD + t, CONV_DIM), F32),
        pltpu.VMEM((nb, t, SSD_WIDTH), F32),
        pltpu.VMEM((nb, t, SSD_GROUPS * SSD_STATE), F32),
        pltpu.VMEM((nb, t, SSD_GROUPS * SSD_STATE), F32),
        pltpu.VMEM((nb, t, LANES), F32),
        pltpu.VMEM((nb, t, SSD_WIDTH), F32),
        pltpu.VMEM((nb, t, D_MODEL), F32),
        pltpu.VMEM((nb, t, D_MODEL), F32),
    ]
    if carry:
        scratch.append(pltpu.VMEM((SSD_WIDTH, SSD_STATE), F32))
    return pl.pallas_call(
        functools.partial(_seq_kernel, nb, t, q, carry, pos0),
        grid=(nblk, ntile),
        in_specs=in_specs,
        out_specs=out_specs,
        out_shape=out_shape,
        scratch_shapes=scratch,
        compiler_params=pltpu.CompilerParams(dimension_semantics=("arbitrary", "arbitrary"),
                                             vmem_limit_bytes=VMEM_LIMIT),
        name="seqpart_carry" if carry else "seqpart_state",
    )(*args)


DENSE_ROWS = 256


def _dense_kernel(x_ref, gt_ref, yp_ref, ys_ref, ya_ref, wpo_ref, wso_ref, wao_ref, wo_ref, fnw_ref, y_ref):
    gates = _sigmoid(gt_ref[...].astype(F32))
    merged = (gates[:, 0:D_MODEL] * _dot(yp_ref[...], wpo_ref[...])
              + gates[:, D_MODEL:2 * D_MODEL] * _dot(ys_ref[...], wso_ref[...])
              + gates[:, 2 * D_MODEL:] * _dot(ya_ref[...], wao_ref[...]))
    x_out = x_ref[...] + _dot(merged.astype(BF16), wo_ref[...])
    y_ref[...] = _rms(x_out, fnw_ref[...])


def _dense(x2d, main, yp, ys, ya, wpo, wso, wao, wo, fnw):
    m = x2d.shape[0]
    row = lambda width, idx=0: pl.BlockSpec((DENSE_ROWS, width), lambda i: (i, idx))
    resident = lambda a: pl.BlockSpec(a.shape, lambda i: (0,) * a.ndim, pipeline_mode=pl.Buffered(1))
    return pl.pallas_call(
        _dense_kernel,
        grid=(m // DENSE_ROWS,),
        in_specs=[row(D_MODEL), row(3072, 1), row(D_MODEL), row(SSD_WIDTH), row(D_MODEL),
                  resident(wpo), resident(wso), resident(wao), resident(wo), resident(fnw)],
        out_specs=row(D_MODEL),
        out_shape=jax.ShapeDtypeStruct((m, D_MODEL), F32),
        compiler_params=pltpu.CompilerParams(dimension_semantics=("arbitrary",), vmem_limit_bytes=VMEM_LIMIT),
        name="dense",
    )(x2d, main, yp, ys, ya, wpo, wso, wao, wo, fnw)


PROMPT_TILE = 256
SAMPLE_SEQS = 4


def kernel(x_prompt, x_sample, mem_prompt, state_pool, state_conv, state_ssm, cache_mem_k, cache_mem_v,
           norm_w, w_in, w_pool_grp, pool_scale, conv_w, conv_b, dt_bias, a_log, d_skip, ssd_norm_w,
           mem_norm_w, w_mem_k, w_mem_v, w_pool_out, w_ssd_out, w_att_out, w_out, final_norm_w):
    depth = w_in.shape[0]
    assert depth == 1
    bp, sp, d = x_prompt.shape
    bs, ss, _ = x_sample.shape
    past_len = 16384

    wi = w_in[0]
    w_u, w_zp, w_zs, w_xbc, w_dtc, w_q, w_za, w_gt = jnp.split(
        wi, [1024, 2048, 4096, 7168, 7200, 8224, 9248], axis=1)
    w_main = jnp.concatenate([w_xbc, w_gt, w_zs, w_u, w_zp, w_q, w_za], axis=1).astype(BF16)
    w_dt = jnp.pad(w_dtc, ((0, 0), (0, LANES - SSD_HEADS))).astype(BF16)
    nw = norm_w[0].reshape(1, d)
    pad_heads = lambda a: jnp.pad(a.reshape(1, SSD_HEADS), ((0, 0), (0, LANES - SSD_HEADS)))
    seq_params = (
        w_pool_grp[0].astype(BF16),
        pool_scale[0].reshape(1, d),
        conv_w[0],
        conv_b[0].reshape(1, CONV_DIM),
        pad_heads(dt_bias[0]),
        pad_heads(a_log[0]),
        jnp.repeat(d_skip[0], SSD_HEAD_DIM).reshape(1, SSD_WIDTH),
        ssd_norm_w[0].reshape(1, SSD_WIDTH),
    )
    dense_w = (w_pool_out[0].astype(BF16), w_ssd_out[0].astype(BF16), w_att_out[0].astype(BF16),
               w_out[0].astype(BF16), final_norm_w.reshape(1, d))

    mk, mv = _memkv(mem_prompt, mem_norm_w[0].reshape(1, d), w_mem_k[0].astype(BF16), w_mem_v[0].astype(BF16))
    xp2 = x_prompt.reshape(bp * sp, d)
    main_p, dt_p = _inproj(xp2, nw, w_main, w_dt)
    yp, ysd, ya, pool_p, conv_p, ssm_p = _seqpart(
        main_p, dt_p, mk, mv, None, seq_params,
        nblk=bp, ntile=sp // PROMPT_TILE, nb=1, t=PROMPT_TILE, q=SSD_CHUNK, pos0=0)
    y_prompt = _dense(xp2, main_p, yp, ysd, ya, *dense_w).reshape(bp, sp, d)

    xs2 = x_sample.reshape(bs * ss, d)
    main_s, dt_s = _inproj(xs2, nw, w_main, w_dt)
    states = (state_pool[0], state_conv[0], state_ssm[0].reshape(bs, SSD_WIDTH, SSD_STATE))
    yp, ysd, ya, pool_s, conv_s, ssm_s = _seqpart(
        main_s, dt_s, cache_mem_k[0].reshape(bs, MEM_LEN, d), cache_mem_v[0].reshape(bs, MEM_LEN, d),
        states, seq_params,
        nblk=bs // SAMPLE_SEQS, ntile=1, nb=SAMPLE_SEQS, t=ss, q=ss, pos0=past_len)
    y_sample = _dense(xs2, main_s, yp, ysd, ya, *dense_w).reshape(bs, ss, d)

    ssm_shape = (SSD_GROUPS, SSD_HEADS // SSD_GROUPS, SSD_HEAD_DIM, SSD_STATE)
    return (y_prompt, y_sample,
            pool_p[None], conv_p[None], ssm_p.reshape((1, bp) + ssm_shape),
            mk.reshape(1, bp, MEM_LEN, ATT_HEADS, ATT_HEAD_DIM), mv.reshape(1, bp, MEM_LEN, ATT_HEADS, ATT_HEAD_DIM),
            pool_s[None], conv_s[None], ssm_s.reshape((1, bs) + ssm_shape))
```

```python
import functools

import jax
import jax.numpy as jnp
from jax import lax
from jax.experimental import pallas as pl
from jax.experimental.pallas import tpu as pltpu

F32 = jnp.float32
BF16 = jnp.bfloat16

D_MODEL = 1024
POOL_WINDOWS = (2, 4, 8, 16)
POOL_GROUP = 256
POOL_HIST = 15
POOL_PAD = 16
SSD_WIDTH = 2048
SSD_HEADS = 32
SSD_HEAD_DIM = 64
SSD_GROUPS = 4
SSD_STATE = 128
CONV_WIDTH = 4
CONV_DIM = 3072
CONV_PAD = 8
SSD_CHUNK = 128
MEM_LEN = 256
ATT_HEADS = 4
ATT_HEAD_DIM = 256
EPS = 1e-6
NEG_BIG = -1e30
LANES = 128
MAIN_COLS = 12288
VMEM_LIMIT = 56 * 1024 * 1024

_NT = (((1,), (1,)), ((), ()))
_TN = (((0,), (0,)), ((), ()))


def _sigmoid(x):
    return 1.0 / (1.0 + jnp.exp(-x))


def _silu(x):
    return x * _sigmoid(x)


def _softplus(x):
    return jnp.maximum(x, 0.0) + jnp.log1p(jnp.exp(-jnp.abs(x)))


def _rms(x, w):
    return x * lax.rsqrt(jnp.mean(x * x, axis=-1, keepdims=True) + EPS) * w


def _dot(a, b):
    return jnp.dot(a, b, preferred_element_type=F32)


def _memkv_kernel(mem_ref, nw_ref, wk_ref, wv_ref, k_ref, v_ref, kb_ref, vb_ref):
    mh = _rms(mem_ref[0], nw_ref[...]).astype(BF16)
    k = _dot(mh, wk_ref[...])
    v = _dot(mh, wv_ref[...])
    for hd in range(ATT_HEADS):
        hsl = slice(hd * ATT_HEAD_DIM, (hd + 1) * ATT_HEAD_DIM)
        k_ref[0, :, hd, :] = k[:, hsl]
        v_ref[0, :, hd, :] = v[:, hsl]
        kb_ref[0, hd] = k[:, hsl].astype(BF16)
        vb_ref[0, hd] = v[:, hsl].astype(BF16)


def _memkv(mem, nw, wk, wv):
    b, m, d = mem.shape
    full = lambda shape: pl.BlockSpec(shape, lambda i: (0,) * len(shape))
    blk = pl.BlockSpec((1, m, d), lambda i: (i, 0, 0))
    oblk = pl.BlockSpec((1, m, ATT_HEADS, ATT_HEAD_DIM), lambda i: (i, 0, 0, 0))
    hblk = pl.BlockSpec((1, ATT_HEADS, m, ATT_HEAD_DIM), lambda i: (i, 0, 0, 0))
    return pl.pallas_call(
        _memkv_kernel,
        grid=(b,),
        in_specs=[blk, full((1, d)), full((d, d)), full((d, d))],
        out_specs=[oblk, oblk, hblk, hblk],
        out_shape=[jax.ShapeDtypeStruct((b, m, ATT_HEADS, ATT_HEAD_DIM), F32)] * 2
        + [jax.ShapeDtypeStruct((b, ATT_HEADS, m, ATT_HEAD_DIM), BF16)] * 2,
        compiler_params=pltpu.CompilerParams(dimension_semantics=("arbitrary",), vmem_limit_bytes=VMEM_LIMIT),
        name="memkv",
    )(mem, nw, wk, wv)


INPROJ_ROWS = 256
INPROJ_COL_CHUNK = 1024


def _inproj_kernel(x_ref, nw_ref, w_ref, wdt_ref, main_ref, dt_ref):
    h = _rms(x_ref[...], nw_ref[...]).astype(BF16)
    dt_ref[...] = _dot(h, wdt_ref[...])
    for c in range(MAIN_COLS // INPROJ_COL_CHUNK):
        cols = slice(c * INPROJ_COL_CHUNK, (c + 1) * INPROJ_COL_CHUNK)
        main_ref[:, cols] = _dot(h, w_ref[:, cols]).astype(BF16)


def _inproj(x2d, nw, w_main, w_dt):
    m = x2d.shape[0]
    resident = lambda shape: pl.BlockSpec(shape, lambda i: (0,) * len(shape), pipeline_mode=pl.Buffered(1))
    return pl.pallas_call(
        _inproj_kernel,
        grid=(m // INPROJ_ROWS,),
        in_specs=[
            pl.BlockSpec((INPROJ_ROWS, D_MODEL), lambda i: (i, 0)),
            resident((1, D_MODEL)),
            resident((D_MODEL, MAIN_COLS)),
            resident((D_MODEL, LANES)),
        ],
        out_specs=[
            pl.BlockSpec((INPROJ_ROWS, MAIN_COLS), lambda i: (i, 0)),
            pl.BlockSpec((INPROJ_ROWS, LANES), lambda i: (i, 0)),
        ],
        out_shape=[jax.ShapeDtypeStruct((m, MAIN_COLS), BF16), jax.ShapeDtypeStruct((m, LANES), F32)],
        compiler_params=pltpu.CompilerParams(dimension_semantics=("arbitrary",), vmem_limit_bytes=VMEM_LIMIT),
        name="inproj",
    )(x2d, nw, w_main, w_dt)


def _seq_kernel(nb, t, q, carry, pos0, *refs):
    refs = list(refs)
    u_ref, zp_ref, zs_ref, xbc_ref, dt_ref, q_ref, za_ref = refs[:7]
    refs = refs[7:]
    if not carry:
        ph_ref, ch_ref, hin_ref = refs[:3]
        refs = refs[3:]
    (k_ref, v_ref, wgrp_ref, pscale_ref, convw_ref, convb_ref, dtb_ref, alog_ref, dexp_ref, ssdnw_ref,
     ypool_ref, yssd_ref, yatt_ref, pool_o_ref, conv_o_ref, ssm_o_ref,
     pext, cext, xs_scr, b_scr, c_scr, dt_scr, y_scr, q_scr, att_scr) = refs[:25]
    h_scr = refs[25] if carry else None

    rows = nb * t
    s = pl.program_id(1)
    last = pl.num_programs(1) - 1

    if carry:
        @pl.when(s == 0)
        def _():
            pext[:, 0:POOL_PAD, :] = jnp.zeros((nb, POOL_PAD, D_MODEL), F32)
            cext[:, 0:CONV_PAD, :] = jnp.zeros((nb, CONV_PAD, CONV_DIM), F32)
            h_scr[...] = jnp.zeros(h_scr.shape, F32)
    else:
        pext[:, 0:1, :] = jnp.zeros((nb, 1, D_MODEL), F32)
        pext[:, 1:POOL_PAD, :] = ph_ref[...]
        cext[:, 0:CONV_PAD - 3, :] = jnp.zeros((nb, CONV_PAD - 3, CONV_DIM), F32)
        cext[:, CONV_PAD - 3:CONV_PAD, :] = ch_ref[...]

    u = u_ref[...].astype(F32).reshape(nb, t, D_MODEL)
    pext[:, POOL_PAD:POOL_PAD + t, :] = u
    pos = pos0 + s * t + lax.broadcasted_iota(jnp.int32, (1, t, 1), 1)
    ys = []
    for g, w in enumerate(POOL_WINDOWS):
        cols = slice(g * POOL_GROUP, (g + 1) * POOL_GROUP)
        win = pext[:, :, cols]
        for sh in [1 << e for e in range(g + 1)]:
            win = win + pltpu.roll(win, sh, axis=1)
        win = win[:, POOL_PAD:, :]
        cnt = jnp.minimum(w, pos + 1).astype(F32)
        d = (win / cnt - u[:, :, cols]).astype(BF16).reshape(rows, POOL_GROUP)
        ys.append(_dot(d, wgrp_ref[g]))
    y_pool = jnp.concatenate(ys, axis=1) * pscale_ref[...] * _silu(zp_ref[...].astype(F32))
    ypool_ref[...] = y_pool.astype(BF16)

    xbc = xbc_ref[...].astype(F32).reshape(nb, t, CONV_DIM)
    cext[:, CONV_PAD:CONV_PAD + t, :] = xbc
    cw = SSD_GROUPS * SSD_STATE
    for cc in range(CONV_DIM // cw):
        csl = slice(cc * cw, (cc + 1) * cw)
        ext = cext[:, :, csl]
        conv = convb_ref[:, csl].reshape(1, 1, cw)
        for kk in range(CONV_WIDTH):
            tap = ext if kk == CONV_WIDTH - 1 else pltpu.roll(ext, CONV_WIDTH - 1 - kk, axis=1)
            conv = conv + tap * convw_ref[kk:kk + 1, csl].reshape(1, 1, cw)
        conv = _silu(conv[:, CONV_PAD:, :])
        if cc < SSD_WIDTH // cw:
            xs_scr[:, :, csl] = conv
        elif cc == SSD_WIDTH // cw:
            b_scr[...] = conv
        else:
            c_scr[...] = conv
    dt_scr[...] = _softplus(dt_ref[...] + dtb_ref[...]).reshape(nb, t, LANES)

    def write_hist():
        pool_o_ref[...] = pext[:, t + POOL_PAD - POOL_HIST:t + POOL_PAD, :]
        conv_o_ref[...] = cext[:, t + CONV_PAD - 3:t + CONV_PAD, :]

    if carry:
        pl.when(s == last)(write_hist)
        carry_p = pext[:, t:t + POOL_PAD, :]
        carry_c = cext[:, t:t + CONV_PAD, :]
        pext[:, 0:POOL_PAD, :] = carry_p
        cext[:, 0:CONV_PAD, :] = carry_c
    else:
        write_hist()

    a_neg = -jnp.exp(alog_ref[...])
    rq = lax.broadcasted_iota(jnp.int32, (q, q), 0)
    cq = lax.broadcasted_iota(jnp.int32, (q, q), 1)
    tril = rq >= cq
    eye = rq == cq
    tri_f = tril.astype(F32)
    lane_lo = lax.broadcasted_iota(jnp.int32, (1, LANES), 1) < SSD_HEAD_DIM
    row_lo = lax.broadcasted_iota(jnp.int32, (LANES, 1), 0) < SSD_HEAD_DIM
    nchunk = t // q

    pairs_per_group = SSD_HEADS // SSD_GROUPS // 2

    def chunk_state(i, _):
        b = i // nchunk
        r0 = pl.multiple_of((i % nchunk) * q, q)
        rsl = pl.ds(r0, q)
        dtc = dt_scr[b, rsl, :]
        acs = jnp.dot(tri_f, dtc * a_neg, precision=lax.Precision.HIGHEST, preferred_element_type=F32)
        acs_t = acs.T
        dt_t = dtc.T
        ea = jnp.exp(acs)
        tot = acs[q - 1:q, :]
        wdec = dtc * jnp.exp(tot - acs)
        cdec = jnp.exp(tot)
        for g in range(SSD_GROUPS):
            gsl = slice(g * SSD_STATE, (g + 1) * SSD_STATE)
            bg = b_scr[b, rsl, gsl]
            cg = c_scr[b, rsl, gsl]
            cb = lax.dot_general(cg.astype(BF16), bg.astype(BF16), _NT, preferred_element_type=F32)
            for jp in range(pairs_per_group):
                j = g * pairs_per_group + jp
                lsl = slice(j * LANES, (j + 1) * LANES)
                xp = xs_scr[b, rsl, lsl]
                x_pair = (jnp.where(lane_lo, xp, 0.0).astype(BF16), jnp.where(lane_lo, 0.0, xp).astype(BF16))
                hp = hin_ref[b, lsl, :]
                h_bd = jnp.concatenate([jnp.where(row_lo, hp, 0.0), jnp.where(row_lo, 0.0, hp)], axis=1).astype(BF16)
                y = dexp_ref[:, lsl] * xp
                ceas = []
                hnew = hp * jnp.where(row_lo, cdec[:, 2 * j:2 * j + 1], cdec[:, 2 * j + 1:2 * j + 2])
                for hh in range(2):
                    r = 2 * j + hh
                    seg = acs[:, r:r + 1] - acs_t[r:r + 1, :]
                    lmat = jnp.exp(jnp.where(tril, seg, NEG_BIG))
                    m = (cb * lmat * dt_t[r:r + 1, :]).astype(BF16)
                    y = y + _dot(m, x_pair[hh])
                    ceas.append((cg * ea[:, r:r + 1]).astype(BF16))
                    bw = (bg * wdec[:, r:r + 1]).astype(BF16)
                    hnew = hnew + lax.dot_general(x_pair[hh], bw, _TN, preferred_element_type=F32)
                y = y + lax.dot_general(jnp.concatenate(ceas, axis=1), h_bd, _NT, preferred_element_type=F32)
                y_scr[b, rsl, lsl] = y
                ssm_o_ref[b, lsl, :] = hnew
        return 0

    def chunk_carry(c, _):
        r0 = pl.multiple_of(c * q, q)
        rsl = pl.ds(r0, q)
        dtc = dt_scr[0, rsl, :]
        acs = jnp.dot(tri_f, dtc * a_neg, precision=lax.Precision.HIGHEST, preferred_element_type=F32)
        acs_t = acs.T
        dt_t = dtc.T
        wdec_t = dt_t * jnp.exp(acs_t[:, q - 1:q] - acs_t)
        cdec = jnp.exp(acs[q - 1:q, :])
        for g in range(SSD_GROUPS):
            gsl = slice(g * SSD_STATE, (g + 1) * SSD_STATE)
            bg = b_scr[0, rsl, gsl]
            cg_b = c_scr[0, rsl, gsl].astype(BF16)
            cb = lax.dot_general(cg_b, bg.astype(BF16), _NT, preferred_element_type=F32)
            bg_t = bg.T
            hsl = slice(g * 4 * LANES, (g + 1) * 4 * LANES)
            z_g = _dot(cg_b, h_scr[:, hsl].astype(BF16))
            for jp in range(pairs_per_group):
                j = g * pairs_per_group + jp
                lsl = slice(j * LANES, (j + 1) * LANES)
                xp = xs_scr[0, rsl, lsl]
                x_bd = jnp.concatenate([jnp.where(lane_lo, xp, 0.0).astype(BF16),
                                        jnp.where(lane_lo, 0.0, xp).astype(BF16)], axis=0)
                ms, bws, cols = [], [], []
                for hh in range(2):
                    r = 2 * j + hh
                    cols.append(jnp.broadcast_to(acs[:, r:r + 1], (q, q)))
                    seg = cols[hh] - acs_t[r:r + 1, :]
                    ms.append(cb * jnp.exp(jnp.where(tril, seg, NEG_BIG)) * dt_t[r:r + 1, :])
                    bws.append(bg_t * wdec_t[r:r + 1, :])
                ea_pair = jnp.exp(jnp.where(lane_lo, cols[0], cols[1]))
                y = (_dot(jnp.concatenate(ms, axis=1).astype(BF16), x_bd)
                     + ea_pair * z_g[:, jp * LANES:(jp + 1) * LANES] + dexp_ref[:, lsl] * xp)
                y_scr[0, rsl, lsl] = y
                cd_pair = jnp.where(lane_lo, cdec[:, 2 * j:2 * j + 1], cdec[:, 2 * j + 1:2 * j + 2])
                h_scr[:, lsl] = h_scr[:, lsl] * cd_pair + _dot(jnp.concatenate(bws, axis=1).astype(BF16), x_bd)
        return 0

    if carry:
        lax.fori_loop(0, nchunk, chunk_carry, 0)

        @pl.when(s == last)
        def _():
            for j in range(SSD_HEADS // 2):
                lsl = slice(j * LANES, (j + 1) * LANES)
                ssm_o_ref[0, lsl, :] = h_scr[:, lsl].T
    else:
        lax.fori_loop(0, nb * nchunk, chunk_state, 0)

    yz = y_scr[...].reshape(rows, SSD_WIDTH) * _silu(zs_ref[...].astype(F32))
    yssd_ref[...] = _rms(yz, ssdnw_ref[...]).astype(BF16)

    q_scr[...] = q_ref[...].astype(F32).reshape(nb, t, D_MODEL)
    scale = ATT_HEAD_DIM ** -0.5
    for b in range(nb):
        qb = q_scr[b].astype(BF16)
        outs = []
        for hd in range(ATT_HEADS):
            hsl = slice(hd * ATT_HEAD_DIM, (hd + 1) * ATT_HEAD_DIM)
            if carry:
                kh, vh = k_ref[b, hd], v_ref[b, hd]
            else:
                kh, vh = k_ref[b][:, hsl].astype(BF16), v_ref[b][:, hsl].astype(BF16)
            sc = lax.dot_general(qb[:, hsl], kh, _NT, preferred_element_type=F32) * scale
            e = jnp.exp(sc - jnp.max(sc, axis=-1, keepdims=True))
            p = e / jnp.sum(e, axis=-1, keepdims=True)
            outs.append(_dot(p.astype(BF16), vh))
        att_scr[b] = jnp.concatenate(outs, axis=1)
    y_att = att_scr[...].reshape(rows, D_MODEL) * _silu(za_ref[...].astype(F32))
    yatt_ref[...] = y_att.astype(BF16)


def _seqpart(main, dt, k, v, states, params, *, nblk, ntile, nb, t, q, pos0):
    carry = states is None
    rows = nb * t
    m = main.shape[0]
    nseq = nblk * nb
    rowblk = lambda i, s: i * ntile + s

    def col_spec(width, idx):
        return pl.BlockSpec((rows, width), lambda i, s: (rowblk(i, s), idx))

    def seq_spec(shape):
        return pl.BlockSpec((nb,) + shape, lambda i, s: (i, 0, 0))

    def const_spec(a):
        return pl.BlockSpec(a.shape, lambda i, s: (0,) * a.ndim)

    in_specs = [
        col_spec(1024, 8),
        col_spec(1024, 9),
        col_spec(2048, 3),
        col_spec(3072, 0),
        pl.BlockSpec((rows, LANES), lambda i, s: (rowblk(i, s), 0)),
        col_spec(1024, 10),
        col_spec(1024, 11),
    ]
    args = [main, main, main, main, dt, main, main]
    if not carry:
        in_specs += [seq_spec((POOL_HIST, D_MODEL)), seq_spec((CONV_WIDTH - 1, CONV_DIM)),
                     seq_spec((SSD_WIDTH, SSD_STATE))]
        args += list(states)
    if carry:
        kv_spec = pl.BlockSpec((nb, ATT_HEADS, MEM_LEN, ATT_HEAD_DIM), lambda i, s: (i, 0, 0, 0))
    else:
        kv_spec = seq_spec((MEM_LEN, D_MODEL))
    in_specs += [kv_spec, kv_spec]
    args += [k, v]
    in_specs += [const_spec(p) for p in params]
    args += list(params)

    out_specs = [
        pl.BlockSpec((rows, D_MODEL), lambda i, s: (rowblk(i, s), 0)),
        pl.BlockSpec((rows, SSD_WIDTH), lambda i, s: (rowblk(i, s), 0)),
        pl.BlockSpec((rows, D_MODEL), lambda i, s: (rowblk(i, s), 0)),
        seq_spec((POOL_HIST, D_MODEL)),
        seq_spec((CONV_WIDTH - 1, CONV_DIM)),
        seq_spec((SSD_WIDTH, SSD_STATE)),
    ]
    out_shape = [
        jax.ShapeDtypeStruct((m, D_MODEL), BF16),
        jax.ShapeDtypeStruct((m, SSD_WIDTH), BF16),
        jax.ShapeDtypeStruct((m, D_MODEL), BF16),
        jax.ShapeDtypeStruct((nseq, POOL_HIST, D_MODEL), F32),
        jax.ShapeDtypeStruct((nseq, CONV_WIDTH - 1, CONV_DIM), F32),
        jax.ShapeDtypeStruct((nseq, SSD_WIDTH, SSD_STATE), F32),
    ]
    scratch = [
        pltpu.VMEM((nb, POOL_PAD + t, D_MODEL), F32),
        pltpu.VMEM((nb, CONV_PAD + t, CONV_DIM), F32),
        pltpu.VMEM((nb, t, SSD_WIDTH), F32),
        pltpu.VMEM((nb, t, SSD_GROUPS * SSD_STATE), F32),
        pltpu.VMEM((nb, t, SSD_GROUPS * SSD_STATE), F32),
        pltpu.VMEM((nb, t, LANES), F32),
        pltpu.VMEM((nb, t, SSD_WIDTH), F32),
        pltpu.VMEM((nb, t, D_MODEL), F32),
        pltpu.VMEM((nb, t, D_MODEL), F32),
    ]
    if carry:
        scratch.append(pltpu.VMEM((SSD_STATE, SSD_WIDTH), F32))
    return pl.pallas_call(
        functools.partial(_seq_kernel, nb, t, q, carry, pos0),
        grid=(nblk, ntile),
        in_specs=in_specs,
        out_specs=out_specs,
        out_shape=out_shape,
        scratch_shapes=scratch,
        compiler_params=pltpu.CompilerParams(dimension_semantics=("arbitrary", "arbitrary"),
                                             vmem_limit_bytes=VMEM_LIMIT),
        name="seqpart_carry" if carry else "seqpart_state",
    )(*args)


DENSE_ROWS = 256


def _dense_kernel(x_ref, gt_ref, yp_ref, ys_ref, ya_ref, wpo_ref, wso_ref, wao_ref, wo_ref, fnw_ref, y_ref):
    gates = _sigmoid(gt_ref[...].astype(F32))
    merged = (gates[:, 0:D_MODEL] * _dot(yp_ref[...], wpo_ref[...])
              + gates[:, D_MODEL:2 * D_MODEL] * _dot(ys_ref[...], wso_ref[...])
              + gates[:, 2 * D_MODEL:] * _dot(ya_ref[...], wao_ref[...]))
    x_out = x_ref[...] + _dot(merged.astype(BF16), wo_ref[...])
    y_ref[...] = _rms(x_out, fnw_ref[...])


def _dense(x2d, main, yp, ys, ya, wpo, wso, wao, wo, fnw):
    m = x2d.shape[0]
    row = lambda width, idx=0: pl.BlockSpec((DENSE_ROWS, width), lambda i: (i, idx))
    resident = lambda a: pl.BlockSpec(a.shape, lambda i: (0,) * a.ndim, pipeline_mode=pl.Buffered(1))
    return pl.pallas_call(
        _dense_kernel,
        grid=(m // DENSE_ROWS,),
        in_specs=[row(D_MODEL), row(3072, 1), row(D_MODEL), row(SSD_WIDTH), row(D_MODEL),
                  resident(wpo), resident(wso), resident(wao), resident(wo), resident(fnw)],
        out_specs=row(D_MODEL),
        out_shape=jax.ShapeDtypeStruct((m, D_MODEL), F32),
        compiler_params=pltpu.CompilerParams(dimension_semantics=("arbitrary",), vmem_limit_bytes=VMEM_LIMIT),
        name="dense",
    )(x2d, main, yp, ys, ya, wpo, wso, wao, wo, fnw)


PROMPT_TILE = 256
SAMPLE_SEQS = 4


def kernel(x_prompt, x_sample, mem_prompt, state_pool, state_conv, state_ssm, cache_mem_k, cache_mem_v,
           norm_w, w_in, w_pool_grp, pool_scale, conv_w, conv_b, dt_bias, a_log, d_skip, ssd_norm_w,
           mem_norm_w, w_mem_k, w_mem_v, w_pool_out, w_ssd_out, w_att_out, w_out, final_norm_w):
    depth = w_in.shape[0]
    assert depth == 1
    bp, sp, d = x_prompt.shape
    bs, ss, _ = x_sample.shape
    past_len = 16384

    wi = w_in[0]
    w_u, w_zp, w_zs, w_xbc, w_dtc, w_q, w_za, w_gt = jnp.split(
        wi, [1024, 2048, 4096, 7168, 7200, 8224, 9248], axis=1)
    w_main = jnp.concatenate([w_xbc, w_gt, w_zs, w_u, w_zp, w_q, w_za], axis=1).astype(BF16)
    w_dt = jnp.pad(w_dtc, ((0, 0), (0, LANES - SSD_HEADS))).astype(BF16)
    nw = norm_w[0].reshape(1, d)
    pad_heads = lambda a: jnp.pad(a.reshape(1, SSD_HEADS), ((0, 0), (0, LANES - SSD_HEADS)))
    seq_params = (
        w_pool_grp[0].astype(BF16),
        pool_scale[0].reshape(1, d),
        conv_w[0],
        conv_b[0].reshape(1, CONV_DIM),
        pad_heads(dt_bias[0]),
        pad_heads(a_log[0]),
        jnp.repeat(d_skip[0], SSD_HEAD_DIM).reshape(1, SSD_WIDTH),
        ssd_norm_w[0].reshape(1, SSD_WIDTH),
    )
    dense_w = (w_pool_out[0].astype(BF16), w_ssd_out[0].astype(BF16), w_att_out[0].astype(BF16),
               w_out[0].astype(BF16), final_norm_w.reshape(1, d))

    mk, mv, mkb, mvb = _memkv(mem_prompt, mem_norm_w[0].reshape(1, d), w_mem_k[0].astype(BF16),
                              w_mem_v[0].astype(BF16))
    xp2 = x_prompt.reshape(bp * sp, d)
    main_p, dt_p = _inproj(xp2, nw, w_main, w_dt)
    yp, ysd, ya, pool_p, conv_p, ssm_p = _seqpart(
        main_p, dt_p, mkb, mvb, None, seq_params,
        nblk=bp, ntile=sp // PROMPT_TILE, nb=1, t=PROMPT_TILE, q=SSD_CHUNK, pos0=0)
    y_prompt = _dense(xp2, main_p, yp, ysd, ya, *dense_w).reshape(bp, sp, d)

    xs2 = x_sample.reshape(bs * ss, d)
    main_s, dt_s = _inproj(xs2, nw, w_main, w_dt)
    states = (state_pool[0], state_conv[0], state_ssm[0].reshape(bs, SSD_WIDTH, SSD_STATE))
    yp, ysd, ya, pool_s, conv_s, ssm_s = _seqpart(
        main_s, dt_s, cache_mem_k[0].reshape(bs, MEM_LEN, d), cache_mem_v[0].reshape(bs, MEM_LEN, d),
        states, seq_params,
        nblk=bs // SAMPLE_SEQS, ntile=1, nb=SAMPLE_SEQS, t=ss, q=ss, pos0=past_len)
    y_sample = _dense(xs2, main_s, yp, ysd, ya, *dense_w).reshape(bs, ss, d)

    ssm_shape = (SSD_GROUPS, SSD_HEADS // SSD_GROUPS, SSD_HEAD_DIM, SSD_STATE)
    return (y_prompt, y_sample,
            pool_p[None], conv_p[None], ssm_p.reshape((1, bp) + ssm_shape),
            mk[None], mv[None],
            pool_s[None], conv_s[None], ssm_s.reshape((1, bs) + ssm_shape))
```

```python
import functools

import jax
import jax.numpy as jnp
from jax import lax
from jax.experimental import pallas as pl
from jax.experimental.pallas import tpu as pltpu

F32 = jnp.float32
BF16 = jnp.bfloat16

D_MODEL = 1024
POOL_WINDOWS = (2, 4, 8, 16)
POOL_GROUP = 256
POOL_HIST = 15
POOL_PAD = 16
SSD_WIDTH = 2048
SSD_HEADS = 32
SSD_HEAD_DIM = 64
SSD_GROUPS = 4
SSD_STATE = 128
GROUP_WIDTH = SSD_WIDTH // SSD_GROUPS
CONV_WIDTH = 4
CONV_DIM = 3072
CONV_PAD = 8
SSD_CHUNK = 128
MEM_LEN = 256
ATT_HEADS = 4
ATT_HEAD_DIM = 256
PAST_LEN = 16384
EPS = 1e-6
NEG_BIG = -1e30
SUBLANES = 8
LANES = 128
MAIN_COLS = 12288
VMEM_LIMIT = 56 * 1024 * 1024

_NT = (((1,), (1,)), ((), ()))
_TN = (((0,), (0,)), ((), ()))


def _sigmoid(x):
    return 1.0 / (1.0 + jnp.exp(-x))


def _silu(x):
    return x * _sigmoid(x)


def _softplus(x):
    return jnp.maximum(x, 0.0) + jnp.log1p(jnp.exp(-jnp.abs(x)))


def _rms(x, w):
    return x * lax.rsqrt(jnp.mean(x * x, axis=-1, keepdims=True) + EPS) * w


def _dot(a, b):
    return jnp.dot(a, b, preferred_element_type=F32)


def _softmax_rows(sc):
    e = jnp.exp(sc - jnp.max(sc, axis=-1, keepdims=True))
    return e / jnp.sum(e, axis=-1, keepdims=True)


def _memkv_kernel(mem_ref, nw_ref, wk_ref, wv_ref, k_ref, v_ref, kb_ref, vb_ref):
    mh = _rms(mem_ref[0], nw_ref[...]).astype(BF16)
    k = _dot(mh, wk_ref[...])
    v = _dot(mh, wv_ref[...])
    for hd in range(ATT_HEADS):
        hsl = slice(hd * ATT_HEAD_DIM, (hd + 1) * ATT_HEAD_DIM)
        k_ref[0, :, hd, :] = k[:, hsl]
        v_ref[0, :, hd, :] = v[:, hsl]
        kb_ref[0, hd] = k[:, hsl].astype(BF16)
        vb_ref[0, hd] = v[:, hsl].astype(BF16)


def _memkv(mem, nw, wk, wv):
    b, m, d = mem.shape
    full = lambda shape: pl.BlockSpec(shape, lambda i: (0,) * len(shape))
    blk = pl.BlockSpec((1, m, d), lambda i: (i, 0, 0))
    oblk = pl.BlockSpec((1, m, ATT_HEADS, ATT_HEAD_DIM), lambda i: (i, 0, 0, 0))
    hblk = pl.BlockSpec((1, ATT_HEADS, m, ATT_HEAD_DIM), lambda i: (i, 0, 0, 0))
    return pl.pallas_call(
        _memkv_kernel,
        grid=(b,),
        in_specs=[blk, full((1, d)), full((d, d)), full((d, d))],
        out_specs=[oblk, oblk, hblk, hblk],
        out_shape=[jax.ShapeDtypeStruct((b, m, ATT_HEADS, ATT_HEAD_DIM), F32)] * 2
        + [jax.ShapeDtypeStruct((b, ATT_HEADS, m, ATT_HEAD_DIM), BF16)] * 2,
        compiler_params=pltpu.CompilerParams(dimension_semantics=("arbitrary",), vmem_limit_bytes=VMEM_LIMIT),
        name="memkv",
    )(mem, nw, wk, wv)


INPROJ_ROWS = 256
INPROJ_COL_CHUNK = 1024


def _inproj_kernel(x_ref, nw_ref, w_ref, wdt_ref, main_ref, dt_ref):
    h = _rms(x_ref[...], nw_ref[...]).astype(BF16)
    dt_ref[...] = _dot(h, wdt_ref[...])
    for c in range(MAIN_COLS // INPROJ_COL_CHUNK):
        cols = slice(c * INPROJ_COL_CHUNK, (c + 1) * INPROJ_COL_CHUNK)
        main_ref[:, cols] = _dot(h, w_ref[:, cols]).astype(BF16)


def _inproj(x2d, nw, w_main, w_dt):
    m = x2d.shape[0]
    resident = lambda shape: pl.BlockSpec(shape, lambda i: (0,) * len(shape), pipeline_mode=pl.Buffered(1))
    return pl.pallas_call(
        _inproj_kernel,
        grid=(m // INPROJ_ROWS,),
        in_specs=[
            pl.BlockSpec((INPROJ_ROWS, D_MODEL), lambda i: (i, 0)),
            resident((1, D_MODEL)),
            resident((D_MODEL, MAIN_COLS)),
            resident((D_MODEL, LANES)),
        ],
        out_specs=[
            pl.BlockSpec((INPROJ_ROWS, MAIN_COLS), lambda i: (i, 0)),
            pl.BlockSpec((INPROJ_ROWS, LANES), lambda i: (i, 0)),
        ],
        out_shape=[jax.ShapeDtypeStruct((m, MAIN_COLS), BF16), jax.ShapeDtypeStruct((m, LANES), F32)],
        compiler_params=pltpu.CompilerParams(dimension_semantics=("arbitrary",), vmem_limit_bytes=VMEM_LIMIT),
        name="inproj",
    )(x2d, nw, w_main, w_dt)


def _pool_branch(pext, u, pos, wgrp_ref, pscale_ref, zp):
    nb, t, _ = u.shape
    ys = []
    for g, w in enumerate(POOL_WINDOWS):
        cols = slice(g * POOL_GROUP, (g + 1) * POOL_GROUP)
        win = pext[:, :, cols]
        for sh in [1 << e for e in range(g + 1)]:
            win = win + pltpu.roll(win, sh, axis=1)
        win = win[:, POOL_PAD:, :]
        cnt = jnp.minimum(w, pos + 1).astype(F32)
        d = (win / cnt - u[:, :, cols]).astype(BF16).reshape(nb * t, POOL_GROUP)
        ys.append(_dot(d, wgrp_ref[g]))
    return jnp.concatenate(ys, axis=1) * pscale_ref[...] * _silu(zp)


def _conv_branch(cext, convw_ref, convb_ref, store):
    for cc in range(CONV_DIM // GROUP_WIDTH):
        csl = slice(cc * GROUP_WIDTH, (cc + 1) * GROUP_WIDTH)
        ext = cext[:, :, csl]
        conv = convb_ref[:, csl].reshape(1, 1, GROUP_WIDTH)
        for kk in range(CONV_WIDTH):
            tap = ext if kk == CONV_WIDTH - 1 else pltpu.roll(ext, CONV_WIDTH - 1 - kk, axis=1)
            conv = conv + tap * convw_ref[kk:kk + 1, csl].reshape(1, 1, GROUP_WIDTH)
        store(cc, _silu(conv[:, CONV_PAD:, :]))


PROMPT_TILE = 256


def _seq_prompt_kernel(u_ref, zp_ref, zs_ref, xbc_ref, dt_ref, q_ref, za_ref, k_ref, v_ref,
                       wgrp_ref, pscale_ref, convw_ref, convb_ref, dtb_ref, alog_ref, dexp_ref, ssdnw_ref,
                       ypool_ref, yssd_ref, yatt_ref, pool_o_ref, conv_o_ref, ssm_o_ref,
                       pext, cext, xs_scr, b_scr, c_scr, dt_scr, y_scr, h_scr):
    t, q = PROMPT_TILE, SSD_CHUNK
    s = pl.program_id(1)
    last = pl.num_programs(1) - 1

    @pl.when(s == 0)
    def _():
        pext[:, 0:POOL_PAD, :] = jnp.zeros((1, POOL_PAD, D_MODEL), F32)
        cext[:, 0:CONV_PAD, :] = jnp.zeros((1, CONV_PAD, CONV_DIM), F32)
        h_scr[...] = jnp.zeros(h_scr.shape, F32)

    u = u_ref[...].astype(F32).reshape(1, t, D_MODEL)
    pext[:, POOL_PAD:, :] = u
    pos = s * t + lax.broadcasted_iota(jnp.int32, (1, t, 1), 1)
    ypool_ref[...] = _pool_branch(pext, u, pos, wgrp_ref, pscale_ref, zp_ref[...].astype(F32)).astype(BF16)

    cext[:, CONV_PAD:, :] = xbc_ref[...].astype(F32).reshape(1, t, CONV_DIM)

    def store_conv(cc, val):
        if cc < SSD_GROUPS:
            xs_scr[:, cc * GROUP_WIDTH:(cc + 1) * GROUP_WIDTH] = val[0]
        elif cc == SSD_GROUPS:
            b_scr[...] = val[0]
        else:
            c_scr[...] = val[0]

    _conv_branch(cext, convw_ref, convb_ref, store_conv)
    dt_scr[...] = _softplus(dt_ref[...] + dtb_ref[...])

    @pl.when(s == last)
    def _():
        pool_o_ref[...] = pext[:, t + POOL_PAD - POOL_HIST:t + POOL_PAD, :]
        conv_o_ref[...] = cext[:, t + CONV_PAD - (CONV_WIDTH - 1):t + CONV_PAD, :]

    carry_p = pext[:, t:t + POOL_PAD, :]
    carry_c = cext[:, t:t + CONV_PAD, :]
    pext[:, 0:POOL_PAD, :] = carry_p
    cext[:, 0:CONV_PAD, :] = carry_c

    a_neg = -jnp.exp(alog_ref[...])
    rq = lax.broadcasted_iota(jnp.int32, (q, q), 0)
    cq = lax.broadcasted_iota(jnp.int32, (q, q), 1)
    tril = rq >= cq
    tri_f = tril.astype(F32)
    lane_lo = lax.broadcasted_iota(jnp.int32, (1, LANES), 1) < SSD_HEAD_DIM
    pairs_per_group = SSD_HEADS // SSD_GROUPS // 2

    def chunk(c, _):
        rsl = pl.ds(pl.multiple_of(c * q, q), q)
        dtc = dt_scr[rsl, :]
        acs = jnp.dot(tri_f, dtc * a_neg, precision=lax.Precision.HIGHEST, preferred_element_type=F32)
        acs_t = acs.T
        dt_t = dtc.T
        wdec_t = dt_t * jnp.exp(acs_t[:, q - 1:q] - acs_t)
        cdec = jnp.exp(acs[q - 1:q, :])
        for g in range(SSD_GROUPS):
            gsl = slice(g * SSD_STATE, (g + 1) * SSD_STATE)
            bg = b_scr[rsl, gsl]
            cg_b = c_scr[rsl, gsl].astype(BF16)
            cb = lax.dot_general(cg_b, bg.astype(BF16), _NT, preferred_element_type=F32)
            bg_t = bg.T
            hsl = slice(g * GROUP_WIDTH, (g + 1) * GROUP_WIDTH)
            z_g = _dot(cg_b, h_scr[:, hsl].astype(BF16))
            for jp in range(pairs_per_group):
                j = g * pairs_per_group + jp
                lsl = slice(j * LANES, (j + 1) * LANES)
                xp = xs_scr[rsl, lsl]
                x_bd = jnp.concatenate([jnp.where(lane_lo, xp, 0.0).astype(BF16),
                                        jnp.where(lane_lo, 0.0, xp).astype(BF16)], axis=0)
                ms, bws, cols = [], [], []
                for hh in range(2):
                    r = 2 * j + hh
                    cols.append(jnp.broadcast_to(acs[:, r:r + 1], (q, q)))
                    seg = cols[hh] - acs_t[r:r + 1, :]
                    ms.append(cb * jnp.exp(jnp.where(tril, seg, NEG_BIG)) * dt_t[r:r + 1, :])
                    bws.append(bg_t * wdec_t[r:r + 1, :])
                ea_pair = jnp.exp(jnp.where(lane_lo, cols[0], cols[1]))
                y = (_dot(jnp.concatenate(ms, axis=1).astype(BF16), x_bd)
                     + ea_pair * z_g[:, jp * LANES:(jp + 1) * LANES] + dexp_ref[:, lsl] * xp)
                y_scr[rsl, lsl] = y
                cd_pair = jnp.where(lane_lo, cdec[:, 2 * j:2 * j + 1], cdec[:, 2 * j + 1:2 * j + 2])
                h_scr[:, lsl] = h_scr[:, lsl] * cd_pair + _dot(jnp.concatenate(bws, axis=1).astype(BF16), x_bd)
        return 0

    lax.fori_loop(0, t // q, chunk, 0)

    @pl.when(s == last)
    def _():
        for j in range(SSD_HEADS // 2):
            lsl = slice(j * LANES, (j + 1) * LANES)
            ssm_o_ref[0, lsl, :] = h_scr[:, lsl].T

    yz = y_scr[...] * _silu(zs_ref[...].astype(F32))
    yssd_ref[...] = _rms(yz, ssdnw_ref[...]).astype(BF16)

    qb = q_ref[...]
    scale = ATT_HEAD_DIM ** -0.5
    outs = []
    for hd in range(ATT_HEADS):
        hsl = slice(hd * ATT_HEAD_DIM, (hd + 1) * ATT_HEAD_DIM)
        p = _softmax_rows(lax.dot_general(qb[:, hsl], k_ref[0, hd], _NT, preferred_element_type=F32) * scale)
        outs.append(_dot(p.astype(BF16), v_ref[0, hd]))
    yatt_ref[...] = (jnp.concatenate(outs, axis=1) * _silu(za_ref[...].astype(F32))).astype(BF16)


def _seq_prompt(main, dt, kb, vb, params, *, nseq, ntile):
    t = PROMPT_TILE
    m = main.shape[0]
    rowblk = lambda i, s: i * ntile + s
    col_spec = lambda width, idx: pl.BlockSpec((t, width), lambda i, s: (rowblk(i, s), idx))
    seq_spec = lambda shape: pl.BlockSpec((1,) + shape, lambda i, s: (i,) + (0,) * len(shape))
    const_spec = lambda a: pl.BlockSpec(a.shape, lambda i, s: (0,) * a.ndim)
    in_specs = [
        col_spec(1024, 8),
        col_spec(1024, 9),
        col_spec(2048, 3),
        col_spec(3072, 0),
        pl.BlockSpec((t, LANES), lambda i, s: (rowblk(i, s), 0)),
        col_spec(1024, 10),
        col_spec(1024, 11),
        seq_spec((ATT_HEADS, MEM_LEN, ATT_HEAD_DIM)),
        seq_spec((ATT_HEADS, MEM_LEN, ATT_HEAD_DIM)),
    ] + [const_spec(p) for p in params]
    out_specs = [
        pl.BlockSpec((t, D_MODEL), lambda i, s: (rowblk(i, s), 0)),
        pl.BlockSpec((t, SSD_WIDTH), lambda i, s: (rowblk(i, s), 0)),
        pl.BlockSpec((t, D_MODEL), lambda i, s: (rowblk(i, s), 0)),
        seq_spec((POOL_HIST, D_MODEL)),
        seq_spec((CONV_WIDTH - 1, CONV_DIM)),
        seq_spec((SSD_WIDTH, SSD_STATE)),
    ]
    out_shape = [
        jax.ShapeDtypeStruct((m, D_MODEL), BF16),
        jax.ShapeDtypeStruct((m, SSD_WIDTH), BF16),
        jax.ShapeDtypeStruct((m, D_MODEL), BF16),
        jax.ShapeDtypeStruct((nseq, POOL_HIST, D_MODEL), F32),
        jax.ShapeDtypeStruct((nseq, CONV_WIDTH - 1, CONV_DIM), F32),
        jax.ShapeDtypeStruct((nseq, SSD_WIDTH, SSD_STATE), F32),
    ]
    scratch = [
        pltpu.VMEM((1, POOL_PAD + t, D_MODEL), F32),
        pltpu.VMEM((1, CONV_PAD + t, CONV_DIM), F32),
        pltpu.VMEM((t, SSD_WIDTH), F32),
        pltpu.VMEM((t, GROUP_WIDTH), F32),
        pltpu.VMEM((t, GROUP_WIDTH), F32),
        pltpu.VMEM((t, LANES), F32),
        pltpu.VMEM((t, SSD_WIDTH), F32),
        pltpu.VMEM((SSD_STATE, SSD_WIDTH), F32),
    ]
    return pl.pallas_call(
        _seq_prompt_kernel,
        grid=(nseq, ntile),
        in_specs=in_specs,
        out_specs=out_specs,
        out_shape=out_shape,
        scratch_shapes=scratch,
        compiler_params=pltpu.CompilerParams(dimension_semantics=("arbitrary", "arbitrary"),
                                             vmem_limit_bytes=VMEM_LIMIT),
        name="seq_prompt",
    )(main, main, main, main, dt, main, main, kb, vb, *params)


SAMPLE_BLOCK = 32


def _state_pre_kernel(u_ref, zp_ref, xbc_ref, dt_ref, ph_ref, ch_ref,
                      wgrp_ref, pscale_ref, convw_ref, convb_ref, dtb_ref, alog_ref,
                      ypool_ref, xs_ref, bc_ref, dts_ref, cd_ref, pool_o_ref, conv_o_ref, pext, cext):
    nb, t = SAMPLE_BLOCK, SUBLANES
    pext[:, 0:1, :] = jnp.zeros((nb, 1, D_MODEL), F32)
    pext[:, 1:POOL_PAD, :] = ph_ref[...]
    cext[:, 0:CONV_PAD - (CONV_WIDTH - 1), :] = jnp.zeros((nb, CONV_PAD - (CONV_WIDTH - 1), CONV_DIM), F32)
    cext[:, CONV_PAD - (CONV_WIDTH - 1):CONV_PAD, :] = ch_ref[...]

    u = u_ref[...].astype(F32).reshape(nb, t, D_MODEL)
    pext[:, POOL_PAD:, :] = u
    pos = PAST_LEN + lax.broadcasted_iota(jnp.int32, (1, t, 1), 1)
    ypool_ref[...] = _pool_branch(pext, u, pos, wgrp_ref, pscale_ref, zp_ref[...].astype(F32)).astype(BF16)

    cext[:, CONV_PAD:, :] = xbc_ref[...].astype(F32).reshape(nb, t, CONV_DIM)

    def store_conv(cc, val):
        val = val.reshape(nb * t, GROUP_WIDTH)
        if cc < SSD_GROUPS:
            xs_ref[:, cc * GROUP_WIDTH:(cc + 1) * GROUP_WIDTH] = val
        else:
            bc_ref[:, (cc - SSD_GROUPS) * GROUP_WIDTH:(cc - SSD_GROUPS + 1) * GROUP_WIDTH] = val

    _conv_branch(cext, convw_ref, convb_ref, store_conv)
    dt = _softplus(dt_ref[...] + dtb_ref[...])
    dts_ref[...] = dt
    a = dt * -jnp.exp(alog_ref[...])
    cd_ref[...] = jnp.exp(jnp.sum(a.reshape(nb, t, LANES), axis=1))
    pool_o_ref[...] = pext[:, t + POOL_PAD - POOL_HIST:t + POOL_PAD, :]
    conv_o_ref[...] = cext[:, t + CONV_PAD - (CONV_WIDTH - 1):t + CONV_PAD, :]


def _state_pre(main, dt, state_pool, state_conv, params):
    nb, t = SAMPLE_BLOCK, SUBLANES
    rows = nb * t
    m = main.shape[0]
    nseq = m // t
    col_spec = lambda width, idx: pl.BlockSpec((rows, width), lambda i: (i, idx))
    seq_spec = lambda shape: pl.BlockSpec((nb,) + shape, lambda i: (i,) + (0,) * len(shape))
    const_spec = lambda a: pl.BlockSpec(a.shape, lambda i: (0,) * a.ndim)
    return pl.pallas_call(
        _state_pre_kernel,
        grid=(nseq // nb,),
        in_specs=[col_spec(1024, 8), col_spec(1024, 9), col_spec(3072, 0), col_spec(LANES, 0),
                  seq_spec((POOL_HIST, D_MODEL)), seq_spec((CONV_WIDTH - 1, CONV_DIM))]
        + [const_spec(p) for p in params],
        out_specs=[col_spec(D_MODEL, 0), col_spec(SSD_WIDTH, 0), col_spec(2 * GROUP_WIDTH, 0), col_spec(LANES, 0),
                   pl.BlockSpec((nb, LANES), lambda i: (i, 0)),
                   seq_spec((POOL_HIST, D_MODEL)), seq_spec((CONV_WIDTH - 1, CONV_DIM))],
        out_shape=[jax.ShapeDtypeStruct((m, D_MODEL), BF16), jax.ShapeDtypeStruct((m, SSD_WIDTH), F32),
                   jax.ShapeDtypeStruct((m, 2 * GROUP_WIDTH), F32), jax.ShapeDtypeStruct((m, LANES), F32),
                   jax.ShapeDtypeStruct((nseq, LANES), F32),
                   jax.ShapeDtypeStruct((nseq, POOL_HIST, D_MODEL), F32),
                   jax.ShapeDtypeStruct((nseq, CONV_WIDTH - 1, CONV_DIM), F32)],
        scratch_shapes=[pltpu.VMEM((nb, POOL_PAD + t, D_MODEL), F32), pltpu.VMEM((nb, CONV_PAD + t, CONV_DIM), F32)],
        compiler_params=pltpu.CompilerParams(dimension_semantics=("arbitrary",), vmem_limit_bytes=VMEM_LIMIT),
        name="state_pre",
    )(main, main, main, dt, state_pool, state_conv, *params)


def _two_slot_pipeline(n, start_in, wait_in, compute, start_out=None, wait_out=None):
    start_in(0, 0)

    def body(bb, _):
        i0 = 2 * bb
        start_in(i0 + 1, 1)
        wait_in(i0, 0)
        if wait_out is not None:
            pl.when(bb > 0)(lambda: wait_out(i0 - 2, 0))
        compute(i0, 0)
        if start_out is not None:
            start_out(i0, 0)
        pl.when(bb + 1 < n // 2)(lambda: start_in(i0 + 2, 0))
        wait_in(i0 + 1, 1)
        if wait_out is not None:
            pl.when(bb > 0)(lambda: wait_out(i0 - 1, 1))
        compute(i0 + 1, 1)
        if start_out is not None:
            start_out(i0 + 1, 1)
        return 0

    lax.fori_loop(0, n // 2, body, 0)
    if wait_out is not None:
        wait_out(n - 2, 0)
        wait_out(n - 1, 1)


def _ssd_state_kernel(xs_ref, bc_ref, dt_ref, zs_ref, cd_ref, hin_hbm, expand_ref, segsum_ref, alog_ref, dexp_ref,
                      ssdnw_ref, yssd_ref, hout_hbm, hbuf, obuf, y_scr, sem_in, sem_out):
    nb, t = SAMPLE_BLOCK, SUBLANES
    base = pl.program_id(0) * nb
    a_neg = -jnp.exp(alog_ref[...])
    ridx = lax.broadcasted_iota(jnp.int32, (t, LANES), 0)

    def in_copy(b, slot):
        return pltpu.make_async_copy(hin_hbm.at[base + b], hbuf.at[slot], sem_in.at[slot])

    def out_copy(b, slot):
        return pltpu.make_async_copy(obuf.at[slot], hout_hbm.at[base + b], sem_out.at[slot])

    def compute(b, slot):
        rsl = pl.ds(pl.multiple_of(b * t, t), t)
        dtc = dt_ref[rsl, :]
        acs = dtc * a_neg
        for sh in (1, 2, 4):
            acs = acs + jnp.where(ridx >= sh, pltpu.roll(acs, sh, axis=0), 0.0)
        tot = acs[t - 1:t, :]
        x = xs_ref[rsl, :]
        bc = bc_ref[rsl, :]
        bm, cm = bc[:, :GROUP_WIDTH], bc[:, GROUP_WIDTH:]
        bm_r, cm_r = bm.astype(BF16).astype(F32), cm.astype(BF16).astype(F32)
        gs, ps = [], []
        for k in range(t):
            gs.append(jnp.exp(jnp.where(ridx >= k, acs - acs[k:k + 1, :], NEG_BIG)) * dtc[k:k + 1, :])
            ps.append(cm_r * bm_r[k:k + 1, :])
        cb_heads = _dot(jnp.concatenate(ps, axis=0).astype(BF16), segsum_ref[...])
        per_head = jnp.concatenate([jnp.concatenate(gs, axis=0) * cb_heads, jnp.exp(acs),
                                    dtc * jnp.exp(tot - acs)], axis=0)
        hi = per_head.astype(BF16)
        lo = (per_head - hi.astype(F32)).astype(BF16)
        wide = _dot(hi, expand_ref[...]) + _dot(lo, expand_ref[...])
        y = dexp_ref[...] * x
        for k in range(t):
            y = y + wide[k * t:(k + 1) * t, :] * x[k:k + 1, :]
        ea_wide = wide[t * t:t * t + t, :]
        xw = x * wide[t * t + t:, :]
        for g in range(SSD_GROUPS):
            gsl = slice(g * SSD_STATE, (g + 1) * SSD_STATE)
            wsl = slice(g * GROUP_WIDTH, (g + 1) * GROUP_WIDTH)
            hg = hbuf[slot, wsl, :]
            z_g = lax.dot_general(cm[:, gsl].astype(BF16), hg.astype(BF16), _NT, preferred_element_type=F32)
            y_scr[rsl, wsl] = y[:, wsl] + ea_wide[:, wsl] * z_g
            upd = lax.dot_general(xw[:, wsl].astype(BF16), bm[:, gsl].astype(BF16), _TN, preferred_element_type=F32)
            for r8 in range(SSD_HEADS // SSD_GROUPS):
                r = g * (SSD_HEADS // SSD_GROUPS) + r8
                rows_r = slice(r * SSD_HEAD_DIM, (r + 1) * SSD_HEAD_DIM)
                obuf[slot, rows_r, :] = (hbuf[slot, rows_r, :] * cd_ref[base + b, r]
                                         + upd[r8 * SSD_HEAD_DIM:(r8 + 1) * SSD_HEAD_DIM, :])

    _two_slot_pipeline(nb,
                       lambda b, slot: in_copy(b, slot).start(), lambda b, slot: in_copy(b, slot).wait(), compute,
                       lambda b, slot: out_copy(b, slot).start(), lambda b, slot: out_copy(b, slot).wait())
    yz = y_scr[...] * _silu(zs_ref[...].astype(F32))
    yssd_ref[...] = _rms(yz, ssdnw_ref[...]).astype(BF16)


def _ssd_state(xs, bc, dts, main, cd, hin, expand, segsum, alog, dexp, ssdnw):
    nb, t = SAMPLE_BLOCK, SUBLANES
    rows = nb * t
    m = xs.shape[0]
    nseq = m // t
    col_spec = lambda width, idx: pl.BlockSpec((rows, width), lambda i: (i, idx))
    const_spec = lambda a: pl.BlockSpec(a.shape, lambda i: (0,) * a.ndim)
    return pl.pallas_call(
        _ssd_state_kernel,
        grid=(nseq // nb,),
        in_specs=[col_spec(SSD_WIDTH, 0), col_spec(2 * GROUP_WIDTH, 0), col_spec(LANES, 0), col_spec(2048, 3),
                  pl.BlockSpec(memory_space=pltpu.SMEM), pl.BlockSpec(memory_space=pl.ANY),
                  const_spec(expand), const_spec(segsum), const_spec(alog), const_spec(dexp), const_spec(ssdnw)],
        out_specs=[col_spec(SSD_WIDTH, 0), pl.BlockSpec(memory_space=pl.ANY)],
        out_shape=[jax.ShapeDtypeStruct((m, SSD_WIDTH), BF16), jax.ShapeDtypeStruct(hin.shape, F32)],
        scratch_shapes=[pltpu.VMEM((2, SSD_WIDTH, SSD_STATE), F32), pltpu.VMEM((2, SSD_WIDTH, SSD_STATE), F32),
                        pltpu.VMEM((rows, SSD_WIDTH), F32),
                        pltpu.SemaphoreType.DMA((2,)), pltpu.SemaphoreType.DMA((2,))],
        compiler_params=pltpu.CompilerParams(dimension_semantics=("arbitrary",), vmem_limit_bytes=VMEM_LIMIT),
        name="ssd_state",
    )(xs, bc, dts, main, cd, hin, expand, segsum, alog, dexp, ssdnw)


def _att_state_kernel(q_ref, za_ref, k_hbm, v_hbm, yatt_ref, kbuf, vbuf, q_scr, att_scr, sem):
    nb, t = SAMPLE_BLOCK, SUBLANES
    base = pl.program_id(0) * nb
    scale = ATT_HEAD_DIM ** -0.5
    q_scr[...] = q_ref[...].astype(F32)

    def copies(b, slot):
        return [pltpu.make_async_copy(src.at[base + b, :, hd, :], buf.at[slot, hd], sem.at[slot, kv * ATT_HEADS + hd])
                for kv, (src, buf) in enumerate(((k_hbm, kbuf), (v_hbm, vbuf))) for hd in range(ATT_HEADS)]

    def start_in(b, slot):
        for c in copies(b, slot):
            c.start()

    def wait_in(b, slot):
        for c in copies(b, slot):
            c.wait()

    def compute(b, slot):
        rsl = pl.ds(pl.multiple_of(b * t, t), t)
        qb = q_scr[rsl, :].astype(BF16)
        outs = []
        for hd in range(ATT_HEADS):
            hsl = slice(hd * ATT_HEAD_DIM, (hd + 1) * ATT_HEAD_DIM)
            kh = kbuf[slot, hd].astype(BF16)
            p = _softmax_rows(lax.dot_general(qb[:, hsl], kh, _NT, preferred_element_type=F32) * scale)
            outs.append(_dot(p.astype(BF16), vbuf[slot, hd].astype(BF16)))
        att_scr[rsl, :] = jnp.concatenate(outs, axis=1)

    _two_slot_pipeline(nb, start_in, wait_in, compute)
    yatt_ref[...] = (att_scr[...] * _silu(za_ref[...].astype(F32))).astype(BF16)


def _att_state(main, k, v):
    nb, t = SAMPLE_BLOCK, SUBLANES
    rows = nb * t
    m = main.shape[0]
    col_spec = lambda width, idx: pl.BlockSpec((rows, width), lambda i: (i, idx))
    head_buf = pltpu.VMEM((2, ATT_HEADS, MEM_LEN, ATT_HEAD_DIM), F32)
    return pl.pallas_call(
        _att_state_kernel,
        grid=(m // rows,),
        in_specs=[col_spec(1024, 10), col_spec(1024, 11),
                  pl.BlockSpec(memory_space=pl.ANY), pl.BlockSpec(memory_space=pl.ANY)],
        out_specs=col_spec(D_MODEL, 0),
        out_shape=jax.ShapeDtypeStruct((m, D_MODEL), BF16),
        scratch_shapes=[head_buf, head_buf, pltpu.VMEM((rows, D_MODEL), F32), pltpu.VMEM((rows, D_MODEL), F32),
                        pltpu.SemaphoreType.DMA((2, 2 * ATT_HEADS))],
        compiler_params=pltpu.CompilerParams(dimension_semantics=("arbitrary",), vmem_limit_bytes=VMEM_LIMIT),
        name="att_state",
    )(main, main, k, v)


DENSE_ROWS = 256


def _dense_kernel(x_ref, gt_ref, yp_ref, ys_ref, ya_ref, wpo_ref, wso_ref, wao_ref, wo_ref, fnw_ref, y_ref):
    gates = _sigmoid(gt_ref[...].astype(F32))
    merged = (gates[:, 0:D_MODEL] * _dot(yp_ref[...], wpo_ref[...])
              + gates[:, D_MODEL:2 * D_MODEL] * _dot(ys_ref[...], wso_ref[...])
              + gates[:, 2 * D_MODEL:] * _dot(ya_ref[...], wao_ref[...]))
    x_out = x_ref[...] + _dot(merged.astype(BF16), wo_ref[...])
    y_ref[...] = _rms(x_out, fnw_ref[...])


def _dense(x2d, main, yp, ys, ya, wpo, wso, wao, wo, fnw):
    m = x2d.shape[0]
    row = lambda width, idx=0: pl.BlockSpec((DENSE_ROWS, width), lambda i: (i, idx))
    resident = lambda a: pl.BlockSpec(a.shape, lambda i: (0,) * a.ndim, pipeline_mode=pl.Buffered(1))
    return pl.pallas_call(
        _dense_kernel,
        grid=(m // DENSE_ROWS,),
        in_specs=[row(D_MODEL), row(3072, 1), row(D_MODEL), row(SSD_WIDTH), row(D_MODEL),
                  resident(wpo), resident(wso), resident(wao), resident(wo), resident(fnw)],
        out_specs=row(D_MODEL),
        out_shape=jax.ShapeDtypeStruct((m, D_MODEL), F32),
        compiler_params=pltpu.CompilerParams(dimension_semantics=("arbitrary",), vmem_limit_bytes=VMEM_LIMIT),
        name="dense",
    )(x2d, main, yp, ys, ya, wpo, wso, wao, wo, fnw)


def kernel(x_prompt, x_sample, mem_prompt, state_pool, state_conv, state_ssm, cache_mem_k, cache_mem_v,
           norm_w, w_in, w_pool_grp, pool_scale, conv_w, conv_b, dt_bias, a_log, d_skip, ssd_norm_w,
           mem_norm_w, w_mem_k, w_mem_v, w_pool_out, w_ssd_out, w_att_out, w_out, final_norm_w):
    assert w_in.shape[0] == 1
    bp, sp, d = x_prompt.shape
    bs, ss, _ = x_sample.shape
    assert ss == SUBLANES and sp % PROMPT_TILE == 0 and bs % SAMPLE_BLOCK == 0

    w_u, w_zp, w_zs, w_xbc, w_dtc, w_q, w_za, w_gt = jnp.split(
        w_in[0], [1024, 2048, 4096, 7168, 7200, 8224, 9248], axis=1)
    w_main = jnp.concatenate([w_xbc, w_gt, w_zs, w_u, w_zp, w_q, w_za], axis=1).astype(BF16)
    w_dt = jnp.pad(w_dtc, ((0, 0), (0, LANES - SSD_HEADS))).astype(BF16)
    nw = norm_w[0].reshape(1, d)
    pad_heads = lambda a: jnp.pad(a.reshape(1, SSD_HEADS), ((0, 0), (0, LANES - SSD_HEADS)))
    wgrp = w_pool_grp[0].astype(BF16)
    pscale = pool_scale[0].reshape(1, d)
    convb = conv_b[0].reshape(1, CONV_DIM)
    dtb, alog = pad_heads(dt_bias[0]), pad_heads(a_log[0])
    dexp = jnp.repeat(d_skip[0], SSD_HEAD_DIM).reshape(1, SSD_WIDTH)
    ssdnw = ssd_norm_w[0].reshape(1, SSD_WIDTH)
    dense_w = (w_pool_out[0].astype(BF16), w_ssd_out[0].astype(BF16), w_att_out[0].astype(BF16),
               w_out[0].astype(BF16), final_norm_w.reshape(1, d))
    head_of_lane = jnp.arange(SSD_WIDTH) // SSD_HEAD_DIM
    expand = (jnp.arange(LANES)[:, None] == head_of_lane[None, :]).astype(BF16)
    group_of_head = jnp.where(jnp.arange(LANES) < SSD_HEADS, jnp.arange(LANES) // (SSD_HEADS // SSD_GROUPS), -1)
    segsum = ((jnp.arange(GROUP_WIDTH) // SSD_STATE)[:, None] == group_of_head[None, :]).astype(BF16)

    mk, mv, mkb, mvb = _memkv(mem_prompt, mem_norm_w[0].reshape(1, d), w_mem_k[0].astype(BF16),
                              w_mem_v[0].astype(BF16))
    xp2 = x_prompt.reshape(bp * sp, d)
    main_p, dt_p = _inproj(xp2, nw, w_main, w_dt)
    yp, ysd, ya, pool_p, conv_p, ssm_p = _seq_prompt(
        main_p, dt_p, mkb, mvb, (wgrp, pscale, conv_w[0], convb, dtb, alog, dexp, ssdnw),
        nseq=bp, ntile=sp // PROMPT_TILE)
    y_prompt = _dense(xp2, main_p, yp, ysd, ya, *dense_w).reshape(bp, sp, d)

    xs2 = x_sample.reshape(bs * ss, d)
    main_s, dt_s = _inproj(xs2, nw, w_main, w_dt)
    yp, xs, bc, dts, cd, pool_s, conv_s = _state_pre(
        main_s, dt_s, state_pool[0], state_conv[0], (wgrp, pscale, conv_w[0], convb, dtb, alog))
    ysd, ssm_s = _ssd_state(xs, bc, dts, main_s, cd, state_ssm[0].reshape(bs, SSD_WIDTH, SSD_STATE),
                            expand, segsum, alog, dexp, ssdnw)
    ya = _att_state(main_s, cache_mem_k[0], cache_mem_v[0])
    y_sample = _dense(xs2, main_s, yp, ysd, ya, *dense_w).reshape(bs, ss, d)

    ssm_shape = (SSD_GROUPS, SSD_HEADS // SSD_GROUPS, SSD_HEAD_DIM, SSD_STATE)
    return (y_prompt, y_sample,
            pool_p[None], conv_p[None], ssm_p.reshape((1, bp) + ssm_shape),
            mk[None], mv[None],
            pool_s[None], conv_s[None], ssm_s.reshape((1, bs) + ssm_shape))
```

```python
import functools

import jax
import jax.numpy as jnp
from jax import lax
from jax.experimental import pallas as pl
from jax.experimental.pallas import tpu as pltpu

F32 = jnp.float32
BF16 = jnp.bfloat16

D_MODEL = 1024
POOL_WINDOWS = (2, 4, 8, 16)
POOL_GROUP = 256
POOL_HIST = 15
POOL_PAD = 16
SSD_WIDTH = 2048
SSD_HEADS = 32
SSD_HEAD_DIM = 64
SSD_GROUPS = 4
SSD_STATE = 128
GROUP_WIDTH = SSD_WIDTH // SSD_GROUPS
CONV_WIDTH = 4
CONV_DIM = 3072
CONV_PAD = 8
SSD_CHUNK = 128
MEM_LEN = 256
ATT_HEADS = 4
ATT_HEAD_DIM = 256
PAST_LEN = 16384
EPS = 1e-6
NEG_BIG = -1e30
SUBLANES = 8
LANES = 128
MAIN_COLS = 12288
COL_XBC, COL_GATES, COL_ZS = (0, 3072), (3072, 6144), (6144, 8192)
COL_U, COL_ZP, COL_Q, COL_ZA = (8192, 9216), (9216, 10240), (10240, 11264), (11264, 12288)
VMEM_LIMIT = 56 * 1024 * 1024
SEQ_PROMPT_VMEM_LIMIT = 60 * 1024 * 1024

_NT = (((1,), (1,)), ((), ()))
_TN = (((0,), (0,)), ((), ()))


def _sigmoid(x):
    return 1.0 / (1.0 + jnp.exp(-x))


def _silu(x):
    return x * _sigmoid(x)


def _softplus(x):
    return jnp.maximum(x, 0.0) + jnp.log1p(jnp.exp(-jnp.abs(x)))


def _rms(x, w):
    return x * lax.rsqrt(jnp.mean(x * x, axis=-1, keepdims=True) + EPS) * w


def _dot(a, b):
    return jnp.dot(a, b, preferred_element_type=F32)


def _softmax_rows(sc):
    e = jnp.exp(sc - jnp.max(sc, axis=-1, keepdims=True))
    return e / jnp.sum(e, axis=-1, keepdims=True)


def _memkv_kernel(mem_ref, nw_ref, wk_ref, wv_ref, k_ref, v_ref, kb_ref, vb_ref):
    mh = _rms(mem_ref[0], nw_ref[...]).astype(BF16)
    k = _dot(mh, wk_ref[...])
    v = _dot(mh, wv_ref[...])
    for hd in range(ATT_HEADS):
        hsl = slice(hd * ATT_HEAD_DIM, (hd + 1) * ATT_HEAD_DIM)
        k_ref[0, :, hd, :] = k[:, hsl]
        v_ref[0, :, hd, :] = v[:, hsl]
        kb_ref[0, hd] = k[:, hsl].astype(BF16)
        vb_ref[0, hd] = v[:, hsl].astype(BF16)


def _memkv(mem, nw, wk, wv):
    b, m, d = mem.shape
    full = lambda shape: pl.BlockSpec(shape, lambda i: (0,) * len(shape))
    blk = pl.BlockSpec((1, m, d), lambda i: (i, 0, 0))
    oblk = pl.BlockSpec((1, m, ATT_HEADS, ATT_HEAD_DIM), lambda i: (i, 0, 0, 0))
    hblk = pl.BlockSpec((1, ATT_HEADS, m, ATT_HEAD_DIM), lambda i: (i, 0, 0, 0))
    return pl.pallas_call(
        _memkv_kernel,
        grid=(b,),
        in_specs=[blk, full((1, d)), full((d, d)), full((d, d))],
        out_specs=[oblk, oblk, hblk, hblk],
        out_shape=[jax.ShapeDtypeStruct((b, m, ATT_HEADS, ATT_HEAD_DIM), F32)] * 2
        + [jax.ShapeDtypeStruct((b, ATT_HEADS, m, ATT_HEAD_DIM), BF16)] * 2,
        compiler_params=pltpu.CompilerParams(dimension_semantics=("arbitrary",), vmem_limit_bytes=VMEM_LIMIT),
        name="memkv",
    )(mem, nw, wk, wv)


INPROJ_ROWS = 256
INPROJ_COL_CHUNK = 1024


def _inproj_kernel(x_ref, nw_ref, w_ref, wdt_ref, main_ref, dt_ref):
    h = _rms(x_ref[...], nw_ref[...]).astype(BF16)
    dt_ref[...] = _dot(h, wdt_ref[...])
    for c in range(MAIN_COLS // INPROJ_COL_CHUNK):
        cols = slice(c * INPROJ_COL_CHUNK, (c + 1) * INPROJ_COL_CHUNK)
        val = _dot(h, w_ref[:, cols])
        if COL_GATES[0] <= c * INPROJ_COL_CHUNK < COL_GATES[1]:
            val = _sigmoid(val)
        main_ref[:, cols] = val.astype(BF16)


def _inproj(x2d, nw, w_main, w_dt):
    m = x2d.shape[0]
    resident = lambda shape: pl.BlockSpec(shape, lambda i: (0,) * len(shape), pipeline_mode=pl.Buffered(1))
    return pl.pallas_call(
        _inproj_kernel,
        grid=(m // INPROJ_ROWS,),
        in_specs=[
            pl.BlockSpec((INPROJ_ROWS, D_MODEL), lambda i: (i, 0)),
            resident((1, D_MODEL)),
            resident((D_MODEL, MAIN_COLS)),
            resident((D_MODEL, LANES)),
        ],
        out_specs=[
            pl.BlockSpec((INPROJ_ROWS, MAIN_COLS), lambda i: (i, 0)),
            pl.BlockSpec((INPROJ_ROWS, LANES), lambda i: (i, 0)),
        ],
        out_shape=[jax.ShapeDtypeStruct((m, MAIN_COLS), BF16), jax.ShapeDtypeStruct((m, LANES), F32)],
        compiler_params=pltpu.CompilerParams(dimension_semantics=("arbitrary",), vmem_limit_bytes=VMEM_LIMIT),
        name="inproj",
    )(x2d, nw, w_main, w_dt)


def _pool_branch(pext, u, pos, wgrp_ref, pscale_ref, zp):
    nb, t, _ = u.shape
    ys = []
    for g, w in enumerate(POOL_WINDOWS):
        cols = slice(g * POOL_GROUP, (g + 1) * POOL_GROUP)
        win = pext[:, :, cols]
        for sh in [1 << e for e in range(g + 1)]:
            win = win + pltpu.roll(win, sh, axis=1)
        win = win[:, POOL_PAD:, :]
        cnt = jnp.minimum(w, pos + 1).astype(F32)
        d = (win / cnt - u[:, :, cols]).astype(BF16).reshape(nb * t, POOL_GROUP)
        ys.append(_dot(d, wgrp_ref[g]))
    return jnp.concatenate(ys, axis=1) * pscale_ref[...] * _silu(zp)


def _conv_branch(cext, convw_ref, convb_ref, store):
    for cc in range(CONV_DIM // GROUP_WIDTH):
        csl = slice(cc * GROUP_WIDTH, (cc + 1) * GROUP_WIDTH)
        ext = cext[:, :, csl]
        conv = convb_ref[:, csl].reshape(1, 1, GROUP_WIDTH)
        for kk in range(CONV_WIDTH):
            tap = ext if kk == CONV_WIDTH - 1 else pltpu.roll(ext, CONV_WIDTH - 1 - kk, axis=1)
            conv = conv + tap * convw_ref[kk:kk + 1, csl].reshape(1, 1, GROUP_WIDTH)
        store(cc, _silu(conv[:, CONV_PAD:, :]))


PROMPT_TILE = 256


def _seq_prompt_kernel(x_ref, nw_ref, w_ref, wdt_ref, k_ref, v_ref,
                       wgrp_ref, pscale_ref, convw_ref, convb_ref, dtb_ref, alog_ref, dexp_ref, ssdnw_ref,
                       gates_ref, ypool_ref, yssd_ref, yatt_ref, pool_o_ref, conv_o_ref, ssm_o_ref,
                       pext, cext, xs_scr, b_scr, c_scr, dt_scr, y_scr, h_scr):
    t, q = PROMPT_TILE, SSD_CHUNK
    s = pl.program_id(1)
    last = pl.num_programs(1) - 1
    hn = _rms(x_ref[...], nw_ref[...]).astype(BF16)

    def proj(piece, lo=0, hi=None):
        c0, c1 = piece
        return _dot(hn, w_ref[:, c0 + lo:(c1 if hi is None else c0 + hi)])

    @pl.when(s == 0)
    def _():
        pext[:, 0:POOL_PAD, :] = jnp.zeros((1, POOL_PAD, D_MODEL), F32)
        cext[:, 0:CONV_PAD, :] = jnp.zeros((1, CONV_PAD, CONV_DIM), F32)
        h_scr[...] = jnp.zeros(h_scr.shape, F32)

    u = proj(COL_U).reshape(1, t, D_MODEL)
    pext[:, POOL_PAD:, :] = u
    pos = s * t + lax.broadcasted_iota(jnp.int32, (1, t, 1), 1)
    ypool_ref[...] = _pool_branch(pext, u, pos, wgrp_ref, pscale_ref, proj(COL_ZP)).astype(BF16)

    for c in range(3):
        gates_ref[:, c * D_MODEL:(c + 1) * D_MODEL] = _sigmoid(
            proj(COL_GATES, c * D_MODEL, (c + 1) * D_MODEL)).astype(BF16)

    for c in range(CONV_DIM // D_MODEL):
        cext[:, CONV_PAD:, c * D_MODEL:(c + 1) * D_MODEL] = proj(
            COL_XBC, c * D_MODEL, (c + 1) * D_MODEL).reshape(1, t, D_MODEL)

    def store_conv(cc, val):
        if cc < SSD_GROUPS:
            xs_scr[:, cc * GROUP_WIDTH:(cc + 1) * GROUP_WIDTH] = val[0]
        elif cc == SSD_GROUPS:
            b_scr[...] = val[0]
        else:
            c_scr[...] = val[0]

    _conv_branch(cext, convw_ref, convb_ref, store_conv)
    dt_scr[...] = _softplus(_dot(hn, wdt_ref[...]) + dtb_ref[...])

    @pl.when(s == last)
    def _():
        pool_o_ref[...] = pext[:, t + POOL_PAD - POOL_HIST:t + POOL_PAD, :]
        conv_o_ref[...] = cext[:, t + CONV_PAD - (CONV_WIDTH - 1):t + CONV_PAD, :]

    carry_p = pext[:, t:t + POOL_PAD, :]
    carry_c = cext[:, t:t + CONV_PAD, :]
    pext[:, 0:POOL_PAD, :] = carry_p
    cext[:, 0:CONV_PAD, :] = carry_c

    a_neg = -jnp.exp(alog_ref[...])
    rq = lax.broadcasted_iota(jnp.int32, (q, q), 0)
    cq = lax.broadcasted_iota(jnp.int32, (q, q), 1)
    tril = rq >= cq
    tri_f = tril.astype(F32)
    lane_lo = lax.broadcasted_iota(jnp.int32, (1, LANES), 1) < SSD_HEAD_DIM
    pairs_per_group = SSD_HEADS // SSD_GROUPS // 2

    def chunk(c, _):
        rsl = pl.ds(pl.multiple_of(c * q, q), q)
        dtc = dt_scr[rsl, :]
        acs = jnp.dot(tri_f, dtc * a_neg, precision=lax.Precision.HIGHEST, preferred_element_type=F32)
        acs_t = acs.T
        dt_t = dtc.T
        wdec_t = dt_t * jnp.exp(acs_t[:, q - 1:q] - acs_t)
        cdec = jnp.exp(acs[q - 1:q, :])
        for g in range(SSD_GROUPS):
            gsl = slice(g * SSD_STATE, (g + 1) * SSD_STATE)
            bg = b_scr[rsl, gsl]
            cg_b = c_scr[rsl, gsl].astype(BF16)
            cb = lax.dot_general(cg_b, bg.astype(BF16), _NT, preferred_element_type=F32)
            bg_t = bg.T
            hsl = slice(g * GROUP_WIDTH, (g + 1) * GROUP_WIDTH)
            z_g = _dot(cg_b, h_scr[:, hsl].astype(BF16))
            for jp in range(pairs_per_group):
                j = g * pairs_per_group + jp
                lsl = slice(j * LANES, (j + 1) * LANES)
                xp = xs_scr[rsl, lsl]
                x_bd = jnp.concatenate([jnp.where(lane_lo, xp, 0.0).astype(BF16),
                                        jnp.where(lane_lo, 0.0, xp).astype(BF16)], axis=0)
                ms, bws, cols = [], [], []
                for hh in range(2):
                    r = 2 * j + hh
                    cols.append(jnp.broadcast_to(acs[:, r:r + 1], (q, q)))
                    seg = cols[hh] - acs_t[r:r + 1, :]
                    ms.append(cb * jnp.exp(jnp.where(tril, seg, NEG_BIG)) * dt_t[r:r + 1, :])
                    bws.append(bg_t * wdec_t[r:r + 1, :])
                ea_pair = jnp.exp(jnp.where(lane_lo, cols[0], cols[1]))
                y = (_dot(jnp.concatenate(ms, axis=1).astype(BF16), x_bd)
                     + ea_pair * z_g[:, jp * LANES:(jp + 1) * LANES] + dexp_ref[:, lsl] * xp)
                y_scr[rsl, lsl] = y
                cd_pair = jnp.where(lane_lo, cdec[:, 2 * j:2 * j + 1], cdec[:, 2 * j + 1:2 * j + 2])
                h_scr[:, lsl] = h_scr[:, lsl] * cd_pair + _dot(jnp.concatenate(bws, axis=1).astype(BF16), x_bd)
        return 0

    lax.fori_loop(0, t // q, chunk, 0)

    @pl.when(s == last)
    def _():
        for j in range(SSD_HEADS // 2):
            lsl = slice(j * LANES, (j + 1) * LANES)
            ssm_o_ref[0, lsl, :] = h_scr[:, lsl].T

    yz = y_scr[...] * _silu(proj(COL_ZS))
    yssd_ref[...] = _rms(yz, ssdnw_ref[...]).astype(BF16)

    qb = proj(COL_Q).astype(BF16)
    scale = ATT_HEAD_DIM ** -0.5
    outs = []
    for hd in range(ATT_HEADS):
        hsl = slice(hd * ATT_HEAD_DIM, (hd + 1) * ATT_HEAD_DIM)
        p = _softmax_rows(lax.dot_general(qb[:, hsl], k_ref[0, hd], _NT, preferred_element_type=F32) * scale)
        outs.append(_dot(p.astype(BF16), v_ref[0, hd]))
    yatt_ref[...] = (jnp.concatenate(outs, axis=1) * _silu(proj(COL_ZA))).astype(BF16)


def _seq_prompt(x2d, nw, w_main, w_dt, kb, vb, params, *, nseq, ntile):
    t = PROMPT_TILE
    m = x2d.shape[0]
    rowblk = lambda i, s: i * ntile + s
    row_spec = lambda width: pl.BlockSpec((t, width), lambda i, s: (rowblk(i, s), 0))
    seq_spec = lambda shape: pl.BlockSpec((1,) + shape, lambda i, s: (i,) + (0,) * len(shape))
    const_spec = lambda a: pl.BlockSpec(a.shape, lambda i, s: (0,) * a.ndim)
    resident = lambda a: pl.BlockSpec(a.shape, lambda i, s: (0,) * a.ndim, pipeline_mode=pl.Buffered(1))
    in_specs = [
        row_spec(D_MODEL), resident(nw), resident(w_main), resident(w_dt),
        seq_spec((ATT_HEADS, MEM_LEN, ATT_HEAD_DIM)),
        seq_spec((ATT_HEADS, MEM_LEN, ATT_HEAD_DIM)),
    ] + [const_spec(p) for p in params]
    out_specs = [
        row_spec(3 * D_MODEL), row_spec(D_MODEL), row_spec(SSD_WIDTH), row_spec(D_MODEL),
        seq_spec((POOL_HIST, D_MODEL)),
        seq_spec((CONV_WIDTH - 1, CONV_DIM)),
        seq_spec((SSD_WIDTH, SSD_STATE)),
    ]
    out_shape = [
        jax.ShapeDtypeStruct((m, 3 * D_MODEL), BF16),
        jax.ShapeDtypeStruct((m, D_MODEL), BF16),
        jax.ShapeDtypeStruct((m, SSD_WIDTH), BF16),
        jax.ShapeDtypeStruct((m, D_MODEL), BF16),
        jax.ShapeDtypeStruct((nseq, POOL_HIST, D_MODEL), F32),
        jax.ShapeDtypeStruct((nseq, CONV_WIDTH - 1, CONV_DIM), F32),
        jax.ShapeDtypeStruct((nseq, SSD_WIDTH, SSD_STATE), F32),
    ]
    scratch = [
        pltpu.VMEM((1, POOL_PAD + t, D_MODEL), F32),
        pltpu.VMEM((1, CONV_PAD + t, CONV_DIM), F32),
        pltpu.VMEM((t, SSD_WIDTH), F32),
        pltpu.VMEM((t, GROUP_WIDTH), F32),
        pltpu.VMEM((t, GROUP_WIDTH), F32),
        pltpu.VMEM((t, LANES), F32),
        pltpu.VMEM((t, SSD_WIDTH), F32),
        pltpu.VMEM((SSD_STATE, SSD_WIDTH), F32),
    ]
    return pl.pallas_call(
        _seq_prompt_kernel,
        grid=(nseq, ntile),
        in_specs=in_specs,
        out_specs=out_specs,
        out_shape=out_shape,
        scratch_shapes=scratch,
        compiler_params=pltpu.CompilerParams(dimension_semantics=("arbitrary", "arbitrary"),
                                             vmem_limit_bytes=SEQ_PROMPT_VMEM_LIMIT),
        name="seq_prompt",
    )(x2d, nw, w_main, w_dt, kb, vb, *params)


SAMPLE_BLOCK = 32


def _state_pre_kernel(u_ref, zp_ref, xbc_ref, dt_ref, ph_ref, ch_ref,
                      wgrp_ref, pscale_ref, convw_ref, convb_ref, dtb_ref, alog_ref,
                      ypool_ref, xs_ref, bc_ref, dts_ref, cd_ref, pool_o_ref, conv_o_ref, pext, cext):
    nb, t = SAMPLE_BLOCK, SUBLANES
    pext[:, 0:1, :] = jnp.zeros((nb, 1, D_MODEL), F32)
    pext[:, 1:POOL_PAD, :] = ph_ref[...]
    cext[:, 0:CONV_PAD - (CONV_WIDTH - 1), :] = jnp.zeros((nb, CONV_PAD - (CONV_WIDTH - 1), CONV_DIM), F32)
    cext[:, CONV_PAD - (CONV_WIDTH - 1):CONV_PAD, :] = ch_ref[...]

    u = u_ref[...].astype(F32).reshape(nb, t, D_MODEL)
    pext[:, POOL_PAD:, :] = u
    pos = PAST_LEN + lax.broadcasted_iota(jnp.int32, (1, t, 1), 1)
    ypool_ref[...] = _pool_branch(pext, u, pos, wgrp_ref, pscale_ref, zp_ref[...].astype(F32)).astype(BF16)

    cext[:, CONV_PAD:, :] = xbc_ref[...].astype(F32).reshape(nb, t, CONV_DIM)

    def store_conv(cc, val):
        val = val.reshape(nb * t, GROUP_WIDTH)
        if cc < SSD_GROUPS:
            xs_ref[:, cc * GROUP_WIDTH:(cc + 1) * GROUP_WIDTH] = val
        else:
            bc_ref[:, (cc - SSD_GROUPS) * GROUP_WIDTH:(cc - SSD_GROUPS + 1) * GROUP_WIDTH] = val

    _conv_branch(cext, convw_ref, convb_ref, store_conv)
    dt = _softplus(dt_ref[...] + dtb_ref[...])
    dts_ref[...] = dt
    a = dt * -jnp.exp(alog_ref[...])
    cd_ref[...] = jnp.exp(jnp.sum(a.reshape(nb, t, LANES), axis=1))
    pool_o_ref[...] = pext[:, t + POOL_PAD - POOL_HIST:t + POOL_PAD, :]
    conv_o_ref[...] = cext[:, t + CONV_PAD - (CONV_WIDTH - 1):t + CONV_PAD, :]


def _state_pre(main, dt, state_pool, state_conv, params):
    nb, t = SAMPLE_BLOCK, SUBLANES
    rows = nb * t
    m = main.shape[0]
    nseq = m // t
    col_spec = lambda width, idx: pl.BlockSpec((rows, width), lambda i: (i, idx))
    seq_spec = lambda shape: pl.BlockSpec((nb,) + shape, lambda i: (i,) + (0,) * len(shape))
    const_spec = lambda a: pl.BlockSpec(a.shape, lambda i: (0,) * a.ndim)
    return pl.pallas_call(
        _state_pre_kernel,
        grid=(nseq // nb,),
        in_specs=[col_spec(1024, 8), col_spec(1024, 9), col_spec(3072, 0), col_spec(LANES, 0),
                  seq_spec((POOL_HIST, D_MODEL)), seq_spec((CONV_WIDTH - 1, CONV_DIM))]
        + [const_spec(p) for p in params],
        out_specs=[col_spec(D_MODEL, 0), col_spec(SSD_WIDTH, 0), col_spec(2 * GROUP_WIDTH, 0), col_spec(LANES, 0),
                   pl.BlockSpec((nb, LANES), lambda i: (i, 0)),
                   seq_spec((POOL_HIST, D_MODEL)), seq_spec((CONV_WIDTH - 1, CONV_DIM))],
        out_shape=[jax.ShapeDtypeStruct((m, D_MODEL), BF16), jax.ShapeDtypeStruct((m, SSD_WIDTH), F32),
                   jax.ShapeDtypeStruct((m, 2 * GROUP_WIDTH), F32), jax.ShapeDtypeStruct((m, LANES), F32),
                   jax.ShapeDtypeStruct((nseq, LANES), F32),
                   jax.ShapeDtypeStruct((nseq, POOL_HIST, D_MODEL), F32),
                   jax.ShapeDtypeStruct((nseq, CONV_WIDTH - 1, CONV_DIM), F32)],
        scratch_shapes=[pltpu.VMEM((nb, POOL_PAD + t, D_MODEL), F32), pltpu.VMEM((nb, CONV_PAD + t, CONV_DIM), F32)],
        compiler_params=pltpu.CompilerParams(dimension_semantics=("arbitrary",), vmem_limit_bytes=VMEM_LIMIT),
        name="state_pre",
    )(main, main, main, dt, state_pool, state_conv, *params)


def _two_slot_pipeline(n, start_in, wait_in, compute, start_out=None, wait_out=None):
    start_in(0, 0)

    def body(bb, _):
        i0 = 2 * bb
        start_in(i0 + 1, 1)
        wait_in(i0, 0)
        if wait_out is not None:
            pl.when(bb > 0)(lambda: wait_out(i0 - 2, 0))
        compute(i0, 0)
        if start_out is not None:
            start_out(i0, 0)
        pl.when(bb + 1 < n // 2)(lambda: start_in(i0 + 2, 0))
        wait_in(i0 + 1, 1)
        if wait_out is not None:
            pl.when(bb > 0)(lambda: wait_out(i0 - 1, 1))
        compute(i0 + 1, 1)
        if start_out is not None:
            start_out(i0 + 1, 1)
        return 0

    lax.fori_loop(0, n // 2, body, 0)
    if wait_out is not None:
        wait_out(n - 2, 0)
        wait_out(n - 1, 1)


def _ssd_state_kernel(xs_ref, bc_ref, dt_ref, zs_ref, cd_ref, hin_hbm, expand_ref, segsum_ref, alog_ref, dexp_ref,
                      ssdnw_ref, yssd_ref, hout_hbm, hbuf, obuf, y_scr, sem_in, sem_out):
    nb, t = SAMPLE_BLOCK, SUBLANES
    base = pl.program_id(0) * nb
    a_neg = -jnp.exp(alog_ref[...])
    ridx = lax.broadcasted_iota(jnp.int32, (t, LANES), 0)

    def in_copy(b, slot):
        return pltpu.make_async_copy(hin_hbm.at[base + b], hbuf.at[slot], sem_in.at[slot])

    def out_copy(b, slot):
        return pltpu.make_async_copy(obuf.at[slot], hout_hbm.at[base + b], sem_out.at[slot])

    def compute(b, slot):
        rsl = pl.ds(pl.multiple_of(b * t, t), t)
        dtc = dt_ref[rsl, :]
        acs = dtc * a_neg
        for sh in (1, 2, 4):
            acs = acs + jnp.where(ridx >= sh, pltpu.roll(acs, sh, axis=0), 0.0)
        tot = acs[t - 1:t, :]
        x = xs_ref[rsl, :]
        bc = bc_ref[rsl, :]
        bm, cm = bc[:, :GROUP_WIDTH], bc[:, GROUP_WIDTH:]
        bm_r, cm_r = bm.astype(BF16).astype(F32), cm.astype(BF16).astype(F32)
        gs, ps = [], []
        for k in range(t):
            gs.append(jnp.exp(jnp.where(ridx >= k, acs - acs[k:k + 1, :], NEG_BIG)) * dtc[k:k + 1, :])
            ps.append(cm_r * bm_r[k:k + 1, :])
        cb_heads = _dot(jnp.concatenate(ps, axis=0).astype(BF16), segsum_ref[...])
        per_head = jnp.concatenate([jnp.concatenate(gs, axis=0) * cb_heads, jnp.exp(acs),
                                    dtc * jnp.exp(tot - acs)], axis=0)
        hi = per_head.astype(BF16)
        lo = (per_head - hi.astype(F32)).astype(BF16)
        wide = _dot(hi, expand_ref[...]) + _dot(lo, expand_ref[...])
        y = dexp_ref[...] * x
        for k in range(t):
            y = y + wide[k * t:(k + 1) * t, :] * x[k:k + 1, :]
        ea_wide = wide[t * t:t * t + t, :]
        xw = x * wide[t * t + t:, :]
        for g in range(SSD_GROUPS):
            gsl = slice(g * SSD_STATE, (g + 1) * SSD_STATE)
            wsl = slice(g * GROUP_WIDTH, (g + 1) * GROUP_WIDTH)
            hg = hbuf[slot, wsl, :]
            z_g = lax.dot_general(cm[:, gsl].astype(BF16), hg.astype(BF16), _NT, preferred_element_type=F32)
            y_scr[rsl, wsl] = y[:, wsl] + ea_wide[:, wsl] * z_g
            upd = lax.dot_general(xw[:, wsl].astype(BF16), bm[:, gsl].astype(BF16), _TN, preferred_element_type=F32)
            for r8 in range(SSD_HEADS // SSD_GROUPS):
                r = g * (SSD_HEADS // SSD_GROUPS) + r8
                rows_r = slice(r * SSD_HEAD_DIM, (r + 1) * SSD_HEAD_DIM)
                obuf[slot, rows_r, :] = (hbuf[slot, rows_r, :] * cd_ref[base + b, r]
                                         + upd[r8 * SSD_HEAD_DIM:(r8 + 1) * SSD_HEAD_DIM, :])

    _two_slot_pipeline(nb,
                       lambda b, slot: in_copy(b, slot).start(), lambda b, slot: in_copy(b, slot).wait(), compute,
                       lambda b, slot: out_copy(b, slot).start(), lambda b, slot: out_copy(b, slot).wait())
    yz = y_scr[...] * _silu(zs_ref[...].astype(F32))
    yssd_ref[...] = _rms(yz, ssdnw_ref[...]).astype(BF16)


def _ssd_state(xs, bc, dts, main, cd, hin, expand, segsum, alog, dexp, ssdnw):
    nb, t = SAMPLE_BLOCK, SUBLANES
    rows = nb * t
    m = xs.shape[0]
    nseq = m // t
    col_spec = lambda width, idx: pl.BlockSpec((rows, width), lambda i: (i, idx))
    const_spec = lambda a: pl.BlockSpec(a.shape, lambda i: (0,) * a.ndim)
    return pl.pallas_call(
        _ssd_state_kernel,
        grid=(nseq // nb,),
        in_specs=[col_spec(SSD_WIDTH, 0), col_spec(2 * GROUP_WIDTH, 0), col_spec(LANES, 0), col_spec(2048, 3),
                  pl.BlockSpec(memory_space=pltpu.SMEM), pl.BlockSpec(memory_space=pl.ANY),
                  const_spec(expand), const_spec(segsum), const_spec(alog), const_spec(dexp), const_spec(ssdnw)],
        out_specs=[col_spec(SSD_WIDTH, 0), pl.BlockSpec(memory_space=pl.ANY)],
        out_shape=[jax.ShapeDtypeStruct((m, SSD_WIDTH), BF16), jax.ShapeDtypeStruct(hin.shape, F32)],
        scratch_shapes=[pltpu.VMEM((2, SSD_WIDTH, SSD_STATE), F32), pltpu.VMEM((2, SSD_WIDTH, SSD_STATE), F32),
                        pltpu.VMEM((rows, SSD_WIDTH), F32),
                        pltpu.SemaphoreType.DMA((2,)), pltpu.SemaphoreType.DMA((2,))],
        compiler_params=pltpu.CompilerParams(dimension_semantics=("arbitrary",), vmem_limit_bytes=VMEM_LIMIT),
        name="ssd_state",
    )(xs, bc, dts, main, cd, hin, expand, segsum, alog, dexp, ssdnw)


def _att_state_kernel(q_ref, za_ref, k_hbm, v_hbm, yatt_ref, kbuf, vbuf, q_scr, att_scr, sem):
    nb, t = SAMPLE_BLOCK, SUBLANES
    base = pl.program_id(0) * nb
    scale = ATT_HEAD_DIM ** -0.5
    q_scr[...] = q_ref[...].astype(F32)

    def copies(b, slot):
        return [pltpu.make_async_copy(src.at[base + b, :, hd, :], buf.at[slot, hd], sem.at[slot, kv * ATT_HEADS + hd])
                for kv, (src, buf) in enumerate(((k_hbm, kbuf), (v_hbm, vbuf))) for hd in range(ATT_HEADS)]

    def start_in(b, slot):
        for c in copies(b, slot):
            c.start()

    def wait_in(b, slot):
        for c in copies(b, slot):
            c.wait()

    def compute(b, slot):
        rsl = pl.ds(pl.multiple_of(b * t, t), t)
        qb = q_scr[rsl, :].astype(BF16)
        outs = []
        for hd in range(ATT_HEADS):
            hsl = slice(hd * ATT_HEAD_DIM, (hd + 1) * ATT_HEAD_DIM)
            kh = kbuf[slot, hd].astype(BF16)
            p = _softmax_rows(lax.dot_general(qb[:, hsl], kh, _NT, preferred_element_type=F32) * scale)
            outs.append(_dot(p.astype(BF16), vbuf[slot, hd].astype(BF16)))
        att_scr[rsl, :] = jnp.concatenate(outs, axis=1)

    _two_slot_pipeline(nb, start_in, wait_in, compute)
    yatt_ref[...] = (att_scr[...] * _silu(za_ref[...].astype(F32))).astype(BF16)


def _att_state(main, k, v):
    nb, t = SAMPLE_BLOCK, SUBLANES
    rows = nb * t
    m = main.shape[0]
    col_spec = lambda width, idx: pl.BlockSpec((rows, width), lambda i: (i, idx))
    head_buf = pltpu.VMEM((2, ATT_HEADS, MEM_LEN, ATT_HEAD_DIM), F32)
    return pl.pallas_call(
        _att_state_kernel,
        grid=(m // rows,),
        in_specs=[col_spec(1024, 10), col_spec(1024, 11),
                  pl.BlockSpec(memory_space=pl.ANY), pl.BlockSpec(memory_space=pl.ANY)],
        out_specs=col_spec(D_MODEL, 0),
        out_shape=jax.ShapeDtypeStruct((m, D_MODEL), BF16),
        scratch_shapes=[head_buf, head_buf, pltpu.VMEM((rows, D_MODEL), F32), pltpu.VMEM((rows, D_MODEL), F32),
                        pltpu.SemaphoreType.DMA((2, 2 * ATT_HEADS))],
        compiler_params=pltpu.CompilerParams(dimension_semantics=("arbitrary",), vmem_limit_bytes=VMEM_LIMIT),
        name="att_state",
    )(main, main, k, v)


DENSE_ROWS = 256


def _dense_kernel(x_ref, gt_ref, yp_ref, ys_ref, ya_ref, wpo_ref, wso_ref, wao_ref, wo_ref, fnw_ref, y_ref):
    gates = gt_ref[...].astype(F32)
    merged = (gates[:, 0:D_MODEL] * _dot(yp_ref[...], wpo_ref[...])
              + gates[:, D_MODEL:2 * D_MODEL] * _dot(ys_ref[...], wso_ref[...])
              + gates[:, 2 * D_MODEL:] * _dot(ya_ref[...], wao_ref[...]))
    x_out = x_ref[...] + _dot(merged.astype(BF16), wo_ref[...])
    y_ref[...] = _rms(x_out, fnw_ref[...])


def _dense(x2d, gates, gate_idx, yp, ys, ya, wpo, wso, wao, wo, fnw):
    m = x2d.shape[0]
    row = lambda width, idx=0: pl.BlockSpec((DENSE_ROWS, width), lambda i: (i, idx))
    resident = lambda a: pl.BlockSpec(a.shape, lambda i: (0,) * a.ndim, pipeline_mode=pl.Buffered(1))
    return pl.pallas_call(
        _dense_kernel,
        grid=(m // DENSE_ROWS,),
        in_specs=[row(D_MODEL), row(3 * D_MODEL, gate_idx), row(D_MODEL), row(SSD_WIDTH), row(D_MODEL),
                  resident(wpo), resident(wso), resident(wao), resident(wo), resident(fnw)],
        out_specs=row(D_MODEL),
        out_shape=jax.ShapeDtypeStruct((m, D_MODEL), F32),
        compiler_params=pltpu.CompilerParams(dimension_semantics=("arbitrary",), vmem_limit_bytes=VMEM_LIMIT),
        name="dense",
    )(x2d, gates, yp, ys, ya, wpo, wso, wao, wo, fnw)


def kernel(x_prompt, x_sample, mem_prompt, state_pool, state_conv, state_ssm, cache_mem_k, cache_mem_v,
           norm_w, w_in, w_pool_grp, pool_scale, conv_w, conv_b, dt_bias, a_log, d_skip, ssd_norm_w,
           mem_norm_w, w_mem_k, w_mem_v, w_pool_out, w_ssd_out, w_att_out, w_out, final_norm_w):
    assert w_in.shape[0] == 1
    bp, sp, d = x_prompt.shape
    bs, ss, _ = x_sample.shape
    assert ss == SUBLANES and sp % PROMPT_TILE == 0 and bs % SAMPLE_BLOCK == 0

    w_u, w_zp, w_zs, w_xbc, w_dtc, w_q, w_za, w_gt = jnp.split(
        w_in[0], [1024, 2048, 4096, 7168, 7200, 8224, 9248], axis=1)
    w_main = jnp.concatenate([w_xbc, w_gt, w_zs, w_u, w_zp, w_q, w_za], axis=1).astype(BF16)
    w_dt = jnp.pad(w_dtc, ((0, 0), (0, LANES - SSD_HEADS))).astype(BF16)
    nw = norm_w[0].reshape(1, d)
    pad_heads = lambda a: jnp.pad(a.reshape(1, SSD_HEADS), ((0, 0), (0, LANES - SSD_HEADS)))
    wgrp = w_pool_grp[0].astype(BF16)
    pscale = pool_scale[0].reshape(1, d)
    convb = conv_b[0].reshape(1, CONV_DIM)
    dtb, alog = pad_heads(dt_bias[0]), pad_heads(a_log[0])
    dexp = jnp.repeat(d_skip[0], SSD_HEAD_DIM).reshape(1, SSD_WIDTH)
    ssdnw = ssd_norm_w[0].reshape(1, SSD_WIDTH)
    dense_w = (w_pool_out[0].astype(BF16), w_ssd_out[0].astype(BF16), w_att_out[0].astype(BF16),
               w_out[0].astype(BF16), final_norm_w.reshape(1, d))
    head_of_lane = jnp.arange(SSD_WIDTH) // SSD_HEAD_DIM
    expand = (jnp.arange(LANES)[:, None] == head_of_lane[None, :]).astype(BF16)
    group_of_head = jnp.where(jnp.arange(LANES) < SSD_HEADS, jnp.arange(LANES) // (SSD_HEADS // SSD_GROUPS), -1)
    segsum = ((jnp.arange(GROUP_WIDTH) // SSD_STATE)[:, None] == group_of_head[None, :]).astype(BF16)

    mk, mv, mkb, mvb = _memkv(mem_prompt, mem_norm_w[0].reshape(1, d), w_mem_k[0].astype(BF16),
                              w_mem_v[0].astype(BF16))
    xp2 = x_prompt.reshape(bp * sp, d)
    gates_p, yp, ysd, ya, pool_p, conv_p, ssm_p = _seq_prompt(
        xp2, nw, w_main, w_dt, mkb, mvb, (wgrp, pscale, conv_w[0], convb, dtb, alog, dexp, ssdnw),
        nseq=bp, ntile=sp // PROMPT_TILE)
    y_prompt = _dense(xp2, gates_p, 0, yp, ysd, ya, *dense_w).reshape(bp, sp, d)

    xs2 = x_sample.reshape(bs * ss, d)
    main_s, dt_s = _inproj(xs2, nw, w_main, w_dt)
    yp, xs, bc, dts, cd, pool_s, conv_s = _state_pre(
        main_s, dt_s, state_pool[0], state_conv[0], (wgrp, pscale, conv_w[0], convb, dtb, alog))
    ysd, ssm_s = _ssd_state(xs, bc, dts, main_s, cd, state_ssm[0].reshape(bs, SSD_WIDTH, SSD_STATE),
                            expand, segsum, alog, dexp, ssdnw)
    ya = _att_state(main_s, cache_mem_k[0], cache_mem_v[0])
    y_sample = _dense(xs2, main_s, COL_GATES[0] // (3 * D_MODEL), yp, ysd, ya, *dense_w).reshape(bs, ss, d)

    ssm_shape = (SSD_GROUPS, SSD_HEADS // SSD_GROUPS, SSD_HEAD_DIM, SSD_STATE)
    return (y_prompt, y_sample,
            pool_p[None], conv_p[None], ssm_p.reshape((1, bp) + ssm_shape),
            mk[None], mv[None],
            pool_s[None], conv_s[None], ssm_s.reshape((1, bs) + ssm_shape))
```

```python
import functools

import jax
import jax.numpy as jnp
from jax import lax
from jax.experimental import pallas as pl
from jax.experimental.pallas import tpu as pltpu

F32 = jnp.float32
BF16 = jnp.bfloat16

D_MODEL = 1024
POOL_WINDOWS = (2, 4, 8, 16)
POOL_GROUP = 256
POOL_HIST = 15
POOL_PAD = 16
SSD_WIDTH = 2048
SSD_HEADS = 32
SSD_HEAD_DIM = 64
SSD_GROUPS = 4
SSD_STATE = 128
GROUP_WIDTH = SSD_WIDTH // SSD_GROUPS
CONV_WIDTH = 4
CONV_DIM = 3072
CONV_PAD = 8
SSD_CHUNK = 128
MEM_LEN = 256
ATT_HEADS = 4
ATT_HEAD_DIM = 256
PAST_LEN = 16384
EPS = 1e-6
NEG_BIG = -1e30
SUBLANES = 8
LANES = 128
MAIN_COLS = 12288
COL_XBC, COL_GATES, COL_ZS = (0, 3072), (3072, 6144), (6144, 8192)
COL_U, COL_ZP, COL_Q, COL_ZA = (8192, 9216), (9216, 10240), (10240, 11264), (11264, 12288)
VMEM_LIMIT = 56 * 1024 * 1024
SEQ_PROMPT_VMEM_LIMIT = 60 * 1024 * 1024

_NT = (((1,), (1,)), ((), ()))
_TN = (((0,), (0,)), ((), ()))


def _sigmoid(x):
    return 1.0 / (1.0 + jnp.exp(-x))


def _silu(x):
    return x * _sigmoid(x)


def _softplus(x):
    return jnp.maximum(x, 0.0) + jnp.log1p(jnp.exp(-jnp.abs(x)))


def _rms(x, w):
    return x * lax.rsqrt(jnp.mean(x * x, axis=-1, keepdims=True) + EPS) * w


def _dot(a, b):
    return jnp.dot(a, b, preferred_element_type=F32)


def _softmax_rows(sc):
    e = jnp.exp(sc - jnp.max(sc, axis=-1, keepdims=True))
    return e / jnp.sum(e, axis=-1, keepdims=True)


def _memkv_kernel(mem_ref, nw_ref, wk_ref, wv_ref, k_ref, v_ref, kb_ref, vb_ref):
    mh = _rms(mem_ref[0], nw_ref[...]).astype(BF16)
    k = _dot(mh, wk_ref[...])
    v = _dot(mh, wv_ref[...])
    for hd in range(ATT_HEADS):
        hsl = slice(hd * ATT_HEAD_DIM, (hd + 1) * ATT_HEAD_DIM)
        k_ref[0, :, hd, :] = k[:, hsl]
        v_ref[0, :, hd, :] = v[:, hsl]
        kb_ref[0, hd] = k[:, hsl].astype(BF16)
        vb_ref[0, hd] = v[:, hsl].astype(BF16)


def _memkv(mem, nw, wk, wv):
    b, m, d = mem.shape
    full = lambda shape: pl.BlockSpec(shape, lambda i: (0,) * len(shape))
    blk = pl.BlockSpec((1, m, d), lambda i: (i, 0, 0))
    oblk = pl.BlockSpec((1, m, ATT_HEADS, ATT_HEAD_DIM), lambda i: (i, 0, 0, 0))
    hblk = pl.BlockSpec((1, ATT_HEADS, m, ATT_HEAD_DIM), lambda i: (i, 0, 0, 0))
    return pl.pallas_call(
        _memkv_kernel,
        grid=(b,),
        in_specs=[blk, full((1, d)), full((d, d)), full((d, d))],
        out_specs=[oblk, oblk, hblk, hblk],
        out_shape=[jax.ShapeDtypeStruct((b, m, ATT_HEADS, ATT_HEAD_DIM), F32)] * 2
        + [jax.ShapeDtypeStruct((b, ATT_HEADS, m, ATT_HEAD_DIM), BF16)] * 2,
        compiler_params=pltpu.CompilerParams(dimension_semantics=("arbitrary",), vmem_limit_bytes=VMEM_LIMIT),
        name="memkv",
    )(mem, nw, wk, wv)


INPROJ_ROWS = 256
INPROJ_COL_CHUNK = 1024


def _inproj_kernel(x_ref, nw_ref, w_ref, wdt_ref, main_ref, dt_ref):
    h = _rms(x_ref[...], nw_ref[...]).astype(BF16)
    dt_ref[...] = _dot(h, wdt_ref[...])
    for c in range(MAIN_COLS // INPROJ_COL_CHUNK):
        cols = slice(c * INPROJ_COL_CHUNK, (c + 1) * INPROJ_COL_CHUNK)
        val = _dot(h, w_ref[:, cols])
        if COL_GATES[0] <= c * INPROJ_COL_CHUNK < COL_GATES[1]:
            val = _sigmoid(val)
        main_ref[:, cols] = val.astype(BF16)


def _inproj(x2d, nw, w_main, w_dt):
    m = x2d.shape[0]
    resident = lambda shape: pl.BlockSpec(shape, lambda i: (0,) * len(shape), pipeline_mode=pl.Buffered(1))
    return pl.pallas_call(
        _inproj_kernel,
        grid=(m // INPROJ_ROWS,),
        in_specs=[
            pl.BlockSpec((INPROJ_ROWS, D_MODEL), lambda i: (i, 0)),
            resident((1, D_MODEL)),
            resident((D_MODEL, MAIN_COLS)),
            resident((D_MODEL, LANES)),
        ],
        out_specs=[
            pl.BlockSpec((INPROJ_ROWS, MAIN_COLS), lambda i: (i, 0)),
            pl.BlockSpec((INPROJ_ROWS, LANES), lambda i: (i, 0)),
        ],
        out_shape=[jax.ShapeDtypeStruct((m, MAIN_COLS), BF16), jax.ShapeDtypeStruct((m, LANES), F32)],
        compiler_params=pltpu.CompilerParams(dimension_semantics=("arbitrary",), vmem_limit_bytes=VMEM_LIMIT),
        name="inproj",
    )(x2d, nw, w_main, w_dt)


def _pool_branch(pext, u, pos, wgrp_ref, pscale_ref, zp):
    nb, t, _ = u.shape
    ys = []
    for g, w in enumerate(POOL_WINDOWS):
        cols = slice(g * POOL_GROUP, (g + 1) * POOL_GROUP)
        win = pext[:, :, cols]
        for sh in [1 << e for e in range(g + 1)]:
            win = win + pltpu.roll(win, sh, axis=1)
        win = win[:, POOL_PAD:, :]
        cnt = jnp.minimum(w, pos + 1).astype(F32)
        d = (win / cnt - u[:, :, cols]).astype(BF16).reshape(nb * t, POOL_GROUP)
        ys.append(_dot(d, wgrp_ref[g]))
    return jnp.concatenate(ys, axis=1) * pscale_ref[...] * _silu(zp)


def _conv_branch(cext, convw_ref, convb_ref, store):
    for cc in range(CONV_DIM // GROUP_WIDTH):
        csl = slice(cc * GROUP_WIDTH, (cc + 1) * GROUP_WIDTH)
        ext = cext[:, :, csl]
        conv = convb_ref[:, csl].reshape(1, 1, GROUP_WIDTH)
        for kk in range(CONV_WIDTH):
            tap = ext if kk == CONV_WIDTH - 1 else pltpu.roll(ext, CONV_WIDTH - 1 - kk, axis=1)
            conv = conv + tap * convw_ref[kk:kk + 1, csl].reshape(1, 1, GROUP_WIDTH)
        store(cc, _silu(conv[:, CONV_PAD:, :]))


PROMPT_TILE = 256


def _seq_prompt_kernel(x_ref, nw_ref, w_ref, wdt_ref, k_ref, v_ref,
                       wgrp_ref, pscale_ref, convw_ref, convb_ref, dtb_ref, alog_ref, dexp_ref, ssdnw_ref,
                       gates_ref, ypool_ref, yssd_ref, yatt_ref, pool_o_ref, conv_o_ref, ssm_o_ref,
                       pext, cext, xs_scr, b_scr, c_scr, dt_scr, y_scr, h_scr, zs_scr, q_scr, za_scr):
    t, q = PROMPT_TILE, SSD_CHUNK
    s = pl.program_id(1)
    last = pl.num_programs(1) - 1
    hn = _rms(x_ref[...], nw_ref[...]).astype(BF16)

    def proj(piece, lo=0, hi=None):
        c0, c1 = piece
        return _dot(hn, w_ref[:, c0 + lo:(c1 if hi is None else c0 + hi)])

    @pl.when(s == 0)
    def _():
        pext[:, 0:POOL_PAD, :] = jnp.zeros((1, POOL_PAD, D_MODEL), F32)
        cext[:, 0:CONV_PAD, :] = jnp.zeros((1, CONV_PAD, CONV_DIM), F32)
        h_scr[...] = jnp.zeros(h_scr.shape, F32)

    @pl.when(s > 0)
    def _():
        carry_p = pext[:, t:t + POOL_PAD, :]
        carry_c = cext[:, t:t + CONV_PAD, :]
        pext[:, 0:POOL_PAD, :] = carry_p
        cext[:, 0:CONV_PAD, :] = carry_c

    def gates_piece(c):
        def run():
            gates_ref[:, c * D_MODEL:(c + 1) * D_MODEL] = _sigmoid(
                proj(COL_GATES, c * D_MODEL, (c + 1) * D_MODEL)).astype(BF16)
        return run

    def zs_piece(c):
        def run():
            zs_scr[:, c * D_MODEL:(c + 1) * D_MODEL] = _silu(proj(COL_ZS, c * D_MODEL, (c + 1) * D_MODEL)).astype(BF16)
        return run

    def q_piece():
        q_scr[...] = proj(COL_Q).astype(BF16)

    def za_piece():
        za_scr[...] = _silu(proj(COL_ZA)).astype(BF16)

    fillers = [gates_piece(0), gates_piece(1), gates_piece(2), zs_piece(0), zs_piece(1), q_piece, za_piece]

    def run_filler():
        if fillers:
            fillers.pop(0)()

    u = proj(COL_U).reshape(1, t, D_MODEL)
    pext[:, POOL_PAD:, :] = u
    pos = s * t + lax.broadcasted_iota(jnp.int32, (1, t, 1), 1)
    ypool_ref[...] = _pool_branch(pext, u, pos, wgrp_ref, pscale_ref, proj(COL_ZP)).astype(BF16)

    for c in range(CONV_DIM // D_MODEL):
        cext[:, CONV_PAD:, c * D_MODEL:(c + 1) * D_MODEL] = proj(
            COL_XBC, c * D_MODEL, (c + 1) * D_MODEL).reshape(1, t, D_MODEL)
    dt_scr[...] = _softplus(_dot(hn, wdt_ref[...]) + dtb_ref[...])

    def store_conv(cc, val):
        if cc < SSD_GROUPS:
            xs_scr[:, cc * GROUP_WIDTH:(cc + 1) * GROUP_WIDTH] = val[0]
        elif cc == SSD_GROUPS:
            b_scr[...] = val[0]
        else:
            c_scr[...] = val[0]
        run_filler()

    _conv_branch(cext, convw_ref, convb_ref, store_conv)

    a_neg = -jnp.exp(alog_ref[...])
    rq = lax.broadcasted_iota(jnp.int32, (q, q), 0)
    cq = lax.broadcasted_iota(jnp.int32, (q, q), 1)
    tril = rq >= cq
    tri_f = tril.astype(F32)
    lane_lo = lax.broadcasted_iota(jnp.int32, (1, LANES), 1) < SSD_HEAD_DIM
    pairs_per_group = SSD_HEADS // SSD_GROUPS // 2

    def chunk(c):
        rsl = slice(c * q, (c + 1) * q)
        dtc = dt_scr[rsl, :]
        acs = jnp.dot(tri_f, dtc * a_neg, precision=lax.Precision.HIGHEST, preferred_element_type=F32)
        acs_t = acs.T
        dt_t = dtc.T
        wdec_t = dt_t * jnp.exp(acs_t[:, q - 1:q] - acs_t)
        cdec = jnp.exp(acs[q - 1:q, :])
        for g in range(SSD_GROUPS):
            gsl = slice(g * SSD_STATE, (g + 1) * SSD_STATE)
            bg = b_scr[rsl, gsl]
            cg_b = c_scr[rsl, gsl].astype(BF16)
            cb = lax.dot_general(cg_b, bg.astype(BF16), _NT, preferred_element_type=F32)
            bg_t = bg.T
            hsl = slice(g * GROUP_WIDTH, (g + 1) * GROUP_WIDTH)
            z_g = _dot(cg_b, h_scr[:, hsl].astype(BF16))
            for jp in range(pairs_per_group):
                j = g * pairs_per_group + jp
                lsl = slice(j * LANES, (j + 1) * LANES)
                xp = xs_scr[rsl, lsl]
                x_bd = jnp.concatenate([jnp.where(lane_lo, xp, 0.0).astype(BF16),
                                        jnp.where(lane_lo, 0.0, xp).astype(BF16)], axis=0)
                ms, bws, cols = [], [], []
                for hh in range(2):
                    r = 2 * j + hh
                    cols.append(jnp.broadcast_to(acs[:, r:r + 1], (q, q)))
                    seg = cols[hh] - acs_t[r:r + 1, :]
                    ms.append(cb * jnp.exp(jnp.where(tril, seg, NEG_BIG)) * dt_t[r:r + 1, :])
                    bws.append(bg_t * wdec_t[r:r + 1, :])
                ea_pair = jnp.exp(jnp.where(lane_lo, cols[0], cols[1]))
                y = (_dot(jnp.concatenate(ms, axis=1).astype(BF16), x_bd)
                     + ea_pair * z_g[:, jp * LANES:(jp + 1) * LANES] + dexp_ref[:, lsl] * xp)
                y_scr[rsl, lsl] = y
                cd_pair = jnp.where(lane_lo, cdec[:, 2 * j:2 * j + 1], cdec[:, 2 * j + 1:2 * j + 2])
                h_scr[:, lsl] = h_scr[:, lsl] * cd_pair + _dot(jnp.concatenate(bws, axis=1).astype(BF16), x_bd)
            run_filler()

    for c in range(t // q):
        chunk(c)
    while fillers:
        run_filler()

    yz = y_scr[...] * zs_scr[...].astype(F32)
    yssd_ref[...] = _rms(yz, ssdnw_ref[...]).astype(BF16)

    scale = ATT_HEAD_DIM ** -0.5
    outs = []
    for hd in range(ATT_HEADS):
        hsl = slice(hd * ATT_HEAD_DIM, (hd + 1) * ATT_HEAD_DIM)
        p = _softmax_rows(lax.dot_general(q_scr[:, hsl], k_ref[0, hd], _NT, preferred_element_type=F32) * scale)
        outs.append(_dot(p.astype(BF16), v_ref[0, hd]))
    yatt_ref[...] = (jnp.concatenate(outs, axis=1) * za_scr[...].astype(F32)).astype(BF16)

    @pl.when(s == last)
    def _():
        pool_o_ref[...] = pext[:, t + POOL_PAD - POOL_HIST:t + POOL_PAD, :]
        conv_o_ref[...] = cext[:, t + CONV_PAD - (CONV_WIDTH - 1):t + CONV_PAD, :]
        for j in range(SSD_HEADS // 2):
            lsl = slice(j * LANES, (j + 1) * LANES)
            ssm_o_ref[0, lsl, :] = h_scr[:, lsl].T


def _seq_prompt(x2d, nw, w_main, w_dt, kb, vb, params, *, nseq, ntile):
    t = PROMPT_TILE
    m = x2d.shape[0]
    rowblk = lambda i, s: i * ntile + s
    row_spec = lambda width: pl.BlockSpec((t, width), lambda i, s: (rowblk(i, s), 0))
    seq_spec = lambda shape: pl.BlockSpec((1,) + shape, lambda i, s: (i,) + (0,) * len(shape))
    const_spec = lambda a: pl.BlockSpec(a.shape, lambda i, s: (0,) * a.ndim)
    resident = lambda a: pl.BlockSpec(a.shape, lambda i, s: (0,) * a.ndim, pipeline_mode=pl.Buffered(1))
    in_specs = [
        row_spec(D_MODEL), resident(nw), resident(w_main), resident(w_dt),
        seq_spec((ATT_HEADS, MEM_LEN, ATT_HEAD_DIM)),
        seq_spec((ATT_HEADS, MEM_LEN, ATT_HEAD_DIM)),
    ] + [const_spec(p) for p in params]
    out_specs = [
        row_spec(3 * D_MODEL), row_spec(D_MODEL), row_spec(SSD_WIDTH), row_spec(D_MODEL),
        seq_spec((POOL_HIST, D_MODEL)),
        seq_spec((CONV_WIDTH - 1, CONV_DIM)),
        seq_spec((SSD_WIDTH, SSD_STATE)),
    ]
    out_shape = [
        jax.ShapeDtypeStruct((m, 3 * D_MODEL), BF16),
        jax.ShapeDtypeStruct((m, D_MODEL), BF16),
        jax.ShapeDtypeStruct((m, SSD_WIDTH), BF16),
        jax.ShapeDtypeStruct((m, D_MODEL), BF16),
        jax.ShapeDtypeStruct((nseq, POOL_HIST, D_MODEL), F32),
        jax.ShapeDtypeStruct((nseq, CONV_WIDTH - 1, CONV_DIM), F32),
        jax.ShapeDtypeStruct((nseq, SSD_WIDTH, SSD_STATE), F32),
    ]
    scratch = [
        pltpu.VMEM((1, POOL_PAD + t, D_MODEL), F32),
        pltpu.VMEM((1, CONV_PAD + t, CONV_DIM), F32),
        pltpu.VMEM((t, SSD_WIDTH), F32),
        pltpu.VMEM((t, GROUP_WIDTH), F32),
        pltpu.VMEM((t, GROUP_WIDTH), F32),
        pltpu.VMEM((t, LANES), F32),
        pltpu.VMEM((t, SSD_WIDTH), F32),
        pltpu.VMEM((SSD_STATE, SSD_WIDTH), F32),
        pltpu.VMEM((t, SSD_WIDTH), BF16),
        pltpu.VMEM((t, D_MODEL), BF16),
        pltpu.VMEM((t, D_MODEL), BF16),
    ]
    return pl.pallas_call(
        _seq_prompt_kernel,
        grid=(nseq, ntile),
        in_specs=in_specs,
        out_specs=out_specs,
        out_shape=out_shape,
        scratch_shapes=scratch,
        compiler_params=pltpu.CompilerParams(dimension_semantics=("arbitrary", "arbitrary"),
                                             vmem_limit_bytes=SEQ_PROMPT_VMEM_LIMIT),
        name="seq_prompt",
    )(x2d, nw, w_main, w_dt, kb, vb, *params)


SAMPLE_BLOCK = 32


def _state_pre_kernel(u_ref, zp_ref, xbc_ref, dt_ref, ph_ref, ch_ref,
                      wgrp_ref, pscale_ref, convw_ref, convb_ref, dtb_ref, alog_ref,
                      ypool_ref, xs_ref, bc_ref, dts_ref, cd_ref, pool_o_ref, conv_o_ref, pext, cext):
    nb, t = SAMPLE_BLOCK, SUBLANES
    pext[:, 0:1, :] = jnp.zeros((nb, 1, D_MODEL), F32)
    pext[:, 1:POOL_PAD, :] = ph_ref[...]
    cext[:, 0:CONV_PAD - (CONV_WIDTH - 1), :] = jnp.zeros((nb, CONV_PAD - (CONV_WIDTH - 1), CONV_DIM), F32)
    cext[:, CONV_PAD - (CONV_WIDTH - 1):CONV_PAD, :] = ch_ref[...]

    u = u_ref[...].astype(F32).reshape(nb, t, D_MODEL)
    pext[:, POOL_PAD:, :] = u
    pos = PAST_LEN + lax.broadcasted_iota(jnp.int32, (1, t, 1), 1)
    ypool_ref[...] = _pool_branch(pext, u, pos, wgrp_ref, pscale_ref, zp_ref[...].astype(F32)).astype(BF16)

    cext[:, CONV_PAD:, :] = xbc_ref[...].astype(F32).reshape(nb, t, CONV_DIM)

    def store_conv(cc, val):
        val = val.reshape(nb * t, GROUP_WIDTH)
        if cc < SSD_GROUPS:
            xs_ref[:, cc * GROUP_WIDTH:(cc + 1) * GROUP_WIDTH] = val
        else:
            bc_ref[:, (cc - SSD_GROUPS) * GROUP_WIDTH:(cc - SSD_GROUPS + 1) * GROUP_WIDTH] = val

    _conv_branch(cext, convw_ref, convb_ref, store_conv)
    dt = _softplus(dt_ref[...] + dtb_ref[...])
    dts_ref[...] = dt
    a = dt * -jnp.exp(alog_ref[...])
    cd_ref[...] = jnp.exp(jnp.sum(a.reshape(nb, t, LANES), axis=1))
    pool_o_ref[...] = pext[:, t + POOL_PAD - POOL_HIST:t + POOL_PAD, :]
    conv_o_ref[...] = cext[:, t + CONV_PAD - (CONV_WIDTH - 1):t + CONV_PAD, :]


def _state_pre(main, dt, state_pool, state_conv, params):
    nb, t = SAMPLE_BLOCK, SUBLANES
    rows = nb * t
    m = main.shape[0]
    nseq = m // t
    col_spec = lambda width, idx: pl.BlockSpec((rows, width), lambda i: (i, idx))
    seq_spec = lambda shape: pl.BlockSpec((nb,) + shape, lambda i: (i,) + (0,) * len(shape))
    const_spec = lambda a: pl.BlockSpec(a.shape, lambda i: (0,) * a.ndim)
    return pl.pallas_call(
        _state_pre_kernel,
        grid=(nseq // nb,),
        in_specs=[col_spec(1024, 8), col_spec(1024, 9), col_spec(3072, 0), col_spec(LANES, 0),
                  seq_spec((POOL_HIST, D_MODEL)), seq_spec((CONV_WIDTH - 1, CONV_DIM))]
        + [const_spec(p) for p in params],
        out_specs=[col_spec(D_MODEL, 0), col_spec(SSD_WIDTH, 0), col_spec(2 * GROUP_WIDTH, 0), col_spec(LANES, 0),
                   pl.BlockSpec((nb, LANES), lambda i: (i, 0)),
                   seq_spec((POOL_HIST, D_MODEL)), seq_spec((CONV_WIDTH - 1, CONV_DIM))],
        out_shape=[jax.ShapeDtypeStruct((m, D_MODEL), BF16), jax.ShapeDtypeStruct((m, SSD_WIDTH), F32),
                   jax.ShapeDtypeStruct((m, 2 * GROUP_WIDTH), F32), jax.ShapeDtypeStruct((m, LANES), F32),
                   jax.ShapeDtypeStruct((nseq, LANES), F32),
                   jax.ShapeDtypeStruct((nseq, POOL_HIST, D_MODEL), F32),
                   jax.ShapeDtypeStruct((nseq, CONV_WIDTH - 1, CONV_DIM), F32)],
        scratch_shapes=[pltpu.VMEM((nb, POOL_PAD + t, D_MODEL), F32), pltpu.VMEM((nb, CONV_PAD + t, CONV_DIM), F32)],
        compiler_params=pltpu.CompilerParams(dimension_semantics=("arbitrary",), vmem_limit_bytes=VMEM_LIMIT),
        name="state_pre",
    )(main, main, main, dt, state_pool, state_conv, *params)


def _two_slot_pipeline(n, start_in, wait_in, compute, start_out=None, wait_out=None):
    start_in(0, 0)

    def body(bb, _):
        i0 = 2 * bb
        start_in(i0 + 1, 1)
        wait_in(i0, 0)
        if wait_out is not None:
            pl.when(bb > 0)(lambda: wait_out(i0 - 2, 0))
        compute(i0, 0)
        if start_out is not None:
            start_out(i0, 0)
        pl.when(bb + 1 < n // 2)(lambda: start_in(i0 + 2, 0))
        wait_in(i0 + 1, 1)
        if wait_out is not None:
            pl.when(bb > 0)(lambda: wait_out(i0 - 1, 1))
        compute(i0 + 1, 1)
        if start_out is not None:
            start_out(i0 + 1, 1)
        return 0

    lax.fori_loop(0, n // 2, body, 0)
    if wait_out is not None:
        wait_out(n - 2, 0)
        wait_out(n - 1, 1)


def _ssd_state_kernel(xs_ref, bc_ref, dt_ref, zs_ref, cd_ref, hin_hbm, expand_ref, segsum_ref, alog_ref, dexp_ref,
                      ssdnw_ref, yssd_ref, hout_hbm, hbuf, obuf, y_scr, sem_in, sem_out):
    nb, t = SAMPLE_BLOCK, SUBLANES
    base = pl.program_id(0) * nb
    a_neg = -jnp.exp(alog_ref[...])
    ridx = lax.broadcasted_iota(jnp.int32, (t, LANES), 0)

    def in_copy(b, slot):
        return pltpu.make_async_copy(hin_hbm.at[base + b], hbuf.at[slot], sem_in.at[slot])

    def out_copy(b, slot):
        return pltpu.make_async_copy(obuf.at[slot], hout_hbm.at[base + b], sem_out.at[slot])

    def compute(b, slot):
        rsl = pl.ds(pl.multiple_of(b * t, t), t)
        dtc = dt_ref[rsl, :]
        acs = dtc * a_neg
        for sh in (1, 2, 4):
            acs = acs + jnp.where(ridx >= sh, pltpu.roll(acs, sh, axis=0), 0.0)
        tot = acs[t - 1:t, :]
        x = xs_ref[rsl, :]
        bc = bc_ref[rsl, :]
        bm, cm = bc[:, :GROUP_WIDTH], bc[:, GROUP_WIDTH:]
        bm_r, cm_r = bm.astype(BF16).astype(F32), cm.astype(BF16).astype(F32)
        gs, ps = [], []
        for k in range(t):
            gs.append(jnp.exp(jnp.where(ridx >= k, acs - acs[k:k + 1, :], NEG_BIG)) * dtc[k:k + 1, :])
            ps.append(cm_r * bm_r[k:k + 1, :])
        cb_heads = _dot(jnp.concatenate(ps, axis=0).astype(BF16), segsum_ref[...])
        per_head = jnp.concatenate([jnp.concatenate(gs, axis=0) * cb_heads, jnp.exp(acs),
                                    dtc * jnp.exp(tot - acs)], axis=0)
        hi = per_head.astype(BF16)
        lo = (per_head - hi.astype(F32)).astype(BF16)
        wide = _dot(hi, expand_ref[...]) + _dot(lo, expand_ref[...])
        y = dexp_ref[...] * x
        for k in range(t):
            y = y + wide[k * t:(k + 1) * t, :] * x[k:k + 1, :]
        ea_wide = wide[t * t:t * t + t, :]
        xw = x * wide[t * t + t:, :]
        for g in range(SSD_GROUPS):
            gsl = slice(g * SSD_STATE, (g + 1) * SSD_STATE)
            wsl = slice(g * GROUP_WIDTH, (g + 1) * GROUP_WIDTH)
            hg = hbuf[slot, wsl, :]
            z_g = lax.dot_general(cm[:, gsl].astype(BF16), hg.astype(BF16), _NT, preferred_element_type=F32)
            y_scr[rsl, wsl] = y[:, wsl] + ea_wide[:, wsl] * z_g
            upd = lax.dot_general(xw[:, wsl].astype(BF16), bm[:, gsl].astype(BF16), _TN, preferred_element_type=F32)
            for r8 in range(SSD_HEADS // SSD_GROUPS):
                r = g * (SSD_HEADS // SSD_GROUPS) + r8
                rows_r = slice(r * SSD_HEAD_DIM, (r + 1) * SSD_HEAD_DIM)
                obuf[slot, rows_r, :] = (hbuf[slot, rows_r, :] * cd_ref[base + b, r]
                                         + upd[r8 * SSD_HEAD_DIM:(r8 + 1) * SSD_HEAD_DIM, :])

    _two_slot_pipeline(nb,
                       lambda b, slot: in_copy(b, slot).start(), lambda b, slot: in_copy(b, slot).wait(), compute,
                       lambda b, slot: out_copy(b, slot).start(), lambda b, slot: out_copy(b, slot).wait())
    yz = y_scr[...] * _silu(zs_ref[...].astype(F32))
    yssd_ref[...] = _rms(yz, ssdnw_ref[...]).astype(BF16)


def _ssd_state(xs, bc, dts, main, cd, hin, expand, segsum, alog, dexp, ssdnw):
    nb, t = SAMPLE_BLOCK, SUBLANES
    rows = nb * t
    m = xs.shape[0]
    nseq = m // t
    col_spec = lambda width, idx: pl.BlockSpec((rows, width), lambda i: (i, idx))
    const_spec = lambda a: pl.BlockSpec(a.shape, lambda i: (0,) * a.ndim)
    return pl.pallas_call(
        _ssd_state_kernel,
        grid=(nseq // nb,),
        in_specs=[col_spec(SSD_WIDTH, 0), col_spec(2 * GROUP_WIDTH, 0), col_spec(LANES, 0), col_spec(2048, 3),
                  pl.BlockSpec(memory_space=pltpu.SMEM), pl.BlockSpec(memory_space=pl.ANY),
                  const_spec(expand), const_spec(segsum), const_spec(alog), const_spec(dexp), const_spec(ssdnw)],
        out_specs=[col_spec(SSD_WIDTH, 0), pl.BlockSpec(memory_space=pl.ANY)],
        out_shape=[jax.ShapeDtypeStruct((m, SSD_WIDTH), BF16), jax.ShapeDtypeStruct(hin.shape, F32)],
        scratch_shapes=[pltpu.VMEM((2, SSD_WIDTH, SSD_STATE), F32), pltpu.VMEM((2, SSD_WIDTH, SSD_STATE), F32),
                        pltpu.VMEM((rows, SSD_WIDTH), F32),
                        pltpu.SemaphoreType.DMA((2,)), pltpu.SemaphoreType.DMA((2,))],
        compiler_params=pltpu.CompilerParams(dimension_semantics=("arbitrary",), vmem_limit_bytes=VMEM_LIMIT),
        name="ssd_state",
    )(xs, bc, dts, main, cd, hin, expand, segsum, alog, dexp, ssdnw)


def _att_state_kernel(q_ref, za_ref, k_hbm, v_hbm, yatt_ref, kbuf, vbuf, q_scr, att_scr, sem):
    nb, t = SAMPLE_BLOCK, SUBLANES
    base = pl.program_id(0) * nb
    scale = ATT_HEAD_DIM ** -0.5
    q_scr[...] = q_ref[...].astype(F32)

    def copies(b, slot):
        return [pltpu.make_async_copy(src.at[base + b, :, hd, :], buf.at[slot, hd], sem.at[slot, kv * ATT_HEADS + hd])
                for kv, (src, buf) in enumerate(((k_hbm, kbuf), (v_hbm, vbuf))) for hd in range(ATT_HEADS)]

    def start_in(b, slot):
        for c in copies(b, slot):
            c.start()

    def wait_in(b, slot):
        for c in copies(b, slot):
            c.wait()

    def compute(b, slot):
        rsl = pl.ds(pl.multiple_of(b * t, t), t)
        qb = q_scr[rsl, :].astype(BF16)
        outs = []
        for hd in range(ATT_HEADS):
            hsl = slice(hd * ATT_HEAD_DIM, (hd + 1) * ATT_HEAD_DIM)
            kh = kbuf[slot, hd].astype(BF16)
            p = _softmax_rows(lax.dot_general(qb[:, hsl], kh, _NT, preferred_element_type=F32) * scale)
            outs.append(_dot(p.astype(BF16), vbuf[slot, hd].astype(BF16)))
        att_scr[rsl, :] = jnp.concatenate(outs, axis=1)

    _two_slot_pipeline(nb, start_in, wait_in, compute)
    yatt_ref[...] = (att_scr[...] * _silu(za_ref[...].astype(F32))).astype(BF16)


def _att_state(main, k, v):
    nb, t = SAMPLE_BLOCK, SUBLANES
    rows = nb * t
    m = main.shape[0]
    col_spec = lambda width, idx: pl.BlockSpec((rows, width), lambda i: (i, idx))
    head_buf = pltpu.VMEM((2, ATT_HEADS, MEM_LEN, ATT_HEAD_DIM), F32)
    return pl.pallas_call(
        _att_state_kernel,
        grid=(m // rows,),
        in_specs=[col_spec(1024, 10), col_spec(1024, 11),
                  pl.BlockSpec(memory_space=pl.ANY), pl.BlockSpec(memory_space=pl.ANY)],
        out_specs=col_spec(D_MODEL, 0),
        out_shape=jax.ShapeDtypeStruct((m, D_MODEL), BF16),
        scratch_shapes=[head_buf, head_buf, pltpu.VMEM((rows, D_MODEL), F32), pltpu.VMEM((rows, D_MODEL), F32),
                        pltpu.SemaphoreType.DMA((2, 2 * ATT_HEADS))],
        compiler_params=pltpu.CompilerParams(dimension_semantics=("arbitrary",), vmem_limit_bytes=VMEM_LIMIT),
        name="att_state",
    )(main, main, k, v)


DENSE_ROWS = 256


def _dense_kernel(x_ref, gt_ref, yp_ref, ys_ref, ya_ref, wpo_ref, wso_ref, wao_ref, wo_ref, fnw_ref, y_ref):
    gates = gt_ref[...].astype(F32)
    merged = (gates[:, 0:D_MODEL] * _dot(yp_ref[...], wpo_ref[...])
              + gates[:, D_MODEL:2 * D_MODEL] * _dot(ys_ref[...], wso_ref[...])
              + gates[:, 2 * D_MODEL:] * _dot(ya_ref[...], wao_ref[...]))
    x_out = x_ref[...] + _dot(merged.astype(BF16), wo_ref[...])
    y_ref[...] = _rms(x_out, fnw_ref[...])


def _dense(x2d, gates, gate_idx, yp, ys, ya, wpo, wso, wao, wo, fnw):
    m = x2d.shape[0]
    row = lambda width, idx=0: pl.BlockSpec((DENSE_ROWS, width), lambda i: (i, idx))
    resident = lambda a: pl.BlockSpec(a.shape, lambda i: (0,) * a.ndim, pipeline_mode=pl.Buffered(1))
    return pl.pallas_call(
        _dense_kernel,
        grid=(m // DENSE_ROWS,),
        in_specs=[row(D_MODEL), row(3 * D_MODEL, gate_idx), row(D_MODEL), row(SSD_WIDTH), row(D_MODEL),
                  resident(wpo), resident(wso), resident(wao), resident(wo), resident(fnw)],
        out_specs=row(D_MODEL),
        out_shape=jax.ShapeDtypeStruct((m, D_MODEL), F32),
        compiler_params=pltpu.CompilerParams(dimension_semantics=("arbitrary",), vmem_limit_bytes=VMEM_LIMIT),
        name="dense",
    )(x2d, gates, yp, ys, ya, wpo, wso, wao, wo, fnw)


def kernel(x_prompt, x_sample, mem_prompt, state_pool, state_conv, state_ssm, cache_mem_k, cache_mem_v,
           norm_w, w_in, w_pool_grp, pool_scale, conv_w, conv_b, dt_bias, a_log, d_skip, ssd_norm_w,
           mem_norm_w, w_mem_k, w_mem_v, w_pool_out, w_ssd_out, w_att_out, w_out, final_norm_w):
    assert w_in.shape[0] == 1
    bp, sp, d = x_prompt.shape
    bs, ss, _ = x_sample.shape
    assert ss == SUBLANES and sp % PROMPT_TILE == 0 and bs % SAMPLE_BLOCK == 0

    w_u, w_zp, w_zs, w_xbc, w_dtc, w_q, w_za, w_gt = jnp.split(
        w_in[0], [1024, 2048, 4096, 7168, 7200, 8224, 9248], axis=1)
    w_main = jnp.concatenate([w_xbc, w_gt, w_zs, w_u, w_zp, w_q, w_za], axis=1).astype(BF16)
    w_dt = jnp.pad(w_dtc, ((0, 0), (0, LANES - SSD_HEADS))).astype(BF16)
    nw = norm_w[0].reshape(1, d)
    pad_heads = lambda a: jnp.pad(a.reshape(1, SSD_HEADS), ((0, 0), (0, LANES - SSD_HEADS)))
    wgrp = w_pool_grp[0].astype(BF16)
    pscale = pool_scale[0].reshape(1, d)
    convb = conv_b[0].reshape(1, CONV_DIM)
    dtb, alog = pad_heads(dt_bias[0]), pad_heads(a_log[0])
    dexp = jnp.repeat(d_skip[0], SSD_HEAD_DIM).reshape(1, SSD_WIDTH)
    ssdnw = ssd_norm_w[0].reshape(1, SSD_WIDTH)
    dense_w = (w_pool_out[0].astype(BF16), w_ssd_out[0].astype(BF16), w_att_out[0].astype(BF16),
               w_out[0].astype(BF16), final_norm_w.reshape(1, d))
    head_of_lane = jnp.arange(SSD_WIDTH) // SSD_HEAD_DIM
    expand = (jnp.arange(LANES)[:, None] == head_of_lane[None, :]).astype(BF16)
    group_of_head = jnp.where(jnp.arange(LANES) < SSD_HEADS, jnp.arange(LANES) // (SSD_HEADS // SSD_GROUPS), -1)
    segsum = ((jnp.arange(GROUP_WIDTH) // SSD_STATE)[:, None] == group_of_head[None, :]).astype(BF16)

    mk, mv, mkb, mvb = _memkv(mem_prompt, mem_norm_w[0].reshape(1, d), w_mem_k[0].astype(BF16),
                              w_mem_v[0].astype(BF16))
    xp2 = x_prompt.reshape(bp * sp, d)
    gates_p, yp, ysd, ya, pool_p, conv_p, ssm_p = _seq_prompt(
        xp2, nw, w_main, w_dt, mkb, mvb, (wgrp, pscale, conv_w[0], convb, dtb, alog, dexp, ssdnw),
        nseq=bp, ntile=sp // PROMPT_TILE)
    y_prompt = _dense(xp2, gates_p, 0, yp, ysd, ya, *dense_w).reshape(bp, sp, d)

    xs2 = x_sample.reshape(bs * ss, d)
    main_s, dt_s = _inproj(xs2, nw, w_main, w_dt)
    yp, xs, bc, dts, cd, pool_s, conv_s = _state_pre(
        main_s, dt_s, state_pool[0], state_conv[0], (wgrp, pscale, conv_w[0], convb, dtb, alog))
    ysd, ssm_s = _ssd_state(xs, bc, dts, main_s, cd, state_ssm[0].reshape(bs, SSD_WIDTH, SSD_STATE),
                            expand, segsum, alog, dexp, ssdnw)
    ya = _att_state(main_s, cache_mem_k[0], cache_mem_v[0])
    y_sample = _dense(xs2, main_s, COL_GATES[0] // (3 * D_MODEL), yp, ysd, ya, *dense_w).reshape(bs, ss, d)

    ssm_shape = (SSD_GROUPS, SSD_HEADS // SSD_GROUPS, SSD_HEAD_DIM, SSD_STATE)
    return (y_prompt, y_sample,
            pool_p[None], conv_p[None], ssm_p.reshape((1, bp) + ssm_shape),
            mk[None], mv[None],
            pool_s[None], conv_s[None], ssm_s.reshape((1, bs) + ssm_shape))
```

```python
import functools

import jax
import jax.numpy as jnp
from jax import lax
from jax.experimental import pallas as pl
from jax.experimental.pallas import tpu as pltpu

F32 = jnp.float32
BF16 = jnp.bfloat16

D_MODEL = 1024
POOL_WINDOWS = (2, 4, 8, 16)
POOL_GROUP = 256
POOL_HIST = 15
POOL_PAD = 16
SSD_WIDTH = 2048
SSD_HEADS = 32
SSD_HEAD_DIM = 64
SSD_GROUPS = 4
SSD_STATE = 128
GROUP_WIDTH = SSD_WIDTH // SSD_GROUPS
CONV_WIDTH = 4
CONV_DIM = 3072
CONV_PAD = 8
SSD_CHUNK = 128
MEM_LEN = 256
ATT_HEADS = 4
ATT_HEAD_DIM = 256
PAST_LEN = 16384
EPS = 1e-6
NEG_BIG = -1e30
SUBLANES = 8
LANES = 128
MAIN_COLS = 12288
COL_XBC, COL_GATES, COL_ZS = (0, 3072), (3072, 6144), (6144, 8192)
COL_U, COL_ZP, COL_Q, COL_ZA = (8192, 9216), (9216, 10240), (10240, 11264), (11264, 12288)
VMEM_LIMIT = 56 * 1024 * 1024
SEQ_PROMPT_VMEM_LIMIT = 60 * 1024 * 1024

_NT = (((1,), (1,)), ((), ()))
_TN = (((0,), (0,)), ((), ()))


def _sigmoid(x):
    return 1.0 / (1.0 + jnp.exp(-x))


def _silu(x):
    return x * _sigmoid(x)


def _softplus(x):
    return jnp.maximum(x, 0.0) + jnp.log1p(jnp.exp(-jnp.abs(x)))


def _rms(x, w):
    return x * lax.rsqrt(jnp.mean(x * x, axis=-1, keepdims=True) + EPS) * w


def _dot(a, b):
    return jnp.dot(a, b, preferred_element_type=F32)


def _softmax_rows(sc):
    e = jnp.exp(sc - jnp.max(sc, axis=-1, keepdims=True))
    return e / jnp.sum(e, axis=-1, keepdims=True)


def _memkv_kernel(mem_ref, nw_ref, wk_ref, wv_ref, k_ref, v_ref, kb_ref, vb_ref):
    mh = _rms(mem_ref[0], nw_ref[...]).astype(BF16)
    k = _dot(mh, wk_ref[...])
    v = _dot(mh, wv_ref[...])
    for hd in range(ATT_HEADS):
        hsl = slice(hd * ATT_HEAD_DIM, (hd + 1) * ATT_HEAD_DIM)
        k_ref[0, :, hd, :] = k[:, hsl]
        v_ref[0, :, hd, :] = v[:, hsl]
        kb_ref[0, hd] = k[:, hsl].astype(BF16)
        vb_ref[0, hd] = v[:, hsl].astype(BF16)


def _memkv(mem, nw, wk, wv):
    b, m, d = mem.shape
    full = lambda shape: pl.BlockSpec(shape, lambda i: (0,) * len(shape))
    blk = pl.BlockSpec((1, m, d), lambda i: (i, 0, 0))
    oblk = pl.BlockSpec((1, m, ATT_HEADS, ATT_HEAD_DIM), lambda i: (i, 0, 0, 0))
    hblk = pl.BlockSpec((1, ATT_HEADS, m, ATT_HEAD_DIM), lambda i: (i, 0, 0, 0))
    return pl.pallas_call(
        _memkv_kernel,
        grid=(b,),
        in_specs=[blk, full((1, d)), full((d, d)), full((d, d))],
        out_specs=[oblk, oblk, hblk, hblk],
        out_shape=[jax.ShapeDtypeStruct((b, m, ATT_HEADS, ATT_HEAD_DIM), F32)] * 2
        + [jax.ShapeDtypeStruct((b, ATT_HEADS, m, ATT_HEAD_DIM), BF16)] * 2,
        compiler_params=pltpu.CompilerParams(dimension_semantics=("arbitrary",), vmem_limit_bytes=VMEM_LIMIT),
        name="memkv",
    )(mem, nw, wk, wv)


INPROJ_ROWS = 256
INPROJ_COL_CHUNK = 1024


def _inproj_kernel(x_ref, nw_ref, w_ref, wdt_ref, main_ref, dt_ref):
    h = _rms(x_ref[...], nw_ref[...]).astype(BF16)
    dt_ref[...] = _dot(h, wdt_ref[...])
    for c in range(MAIN_COLS // INPROJ_COL_CHUNK):
        cols = slice(c * INPROJ_COL_CHUNK, (c + 1) * INPROJ_COL_CHUNK)
        val = _dot(h, w_ref[:, cols])
        if COL_GATES[0] <= c * INPROJ_COL_CHUNK < COL_GATES[1]:
            val = _sigmoid(val)
        main_ref[:, cols] = val.astype(BF16)


def _inproj(x2d, nw, w_main, w_dt):
    m = x2d.shape[0]
    resident = lambda shape: pl.BlockSpec(shape, lambda i: (0,) * len(shape), pipeline_mode=pl.Buffered(1))
    return pl.pallas_call(
        _inproj_kernel,
        grid=(m // INPROJ_ROWS,),
        in_specs=[
            pl.BlockSpec((INPROJ_ROWS, D_MODEL), lambda i: (i, 0)),
            resident((1, D_MODEL)),
            resident((D_MODEL, MAIN_COLS)),
            resident((D_MODEL, LANES)),
        ],
        out_specs=[
            pl.BlockSpec((INPROJ_ROWS, MAIN_COLS), lambda i: (i, 0)),
            pl.BlockSpec((INPROJ_ROWS, LANES), lambda i: (i, 0)),
        ],
        out_shape=[jax.ShapeDtypeStruct((m, MAIN_COLS), BF16), jax.ShapeDtypeStruct((m, LANES), F32)],
        compiler_params=pltpu.CompilerParams(dimension_semantics=("arbitrary",), vmem_limit_bytes=VMEM_LIMIT),
        name="inproj",
    )(x2d, nw, w_main, w_dt)


def _pool_branch(pext, u, pos, wgrp_ref, pscale_ref, zp):
    nb, t, _ = u.shape
    ys = []
    for g, w in enumerate(POOL_WINDOWS):
        cols = slice(g * POOL_GROUP, (g + 1) * POOL_GROUP)
        win = pext[:, :, cols]
        for sh in [1 << e for e in range(g + 1)]:
            win = win + pltpu.roll(win, sh, axis=1)
        win = win[:, POOL_PAD:, :]
        cnt = jnp.minimum(w, pos + 1).astype(F32)
        d = (win / cnt - u[:, :, cols]).astype(BF16).reshape(nb * t, POOL_GROUP)
        ys.append(_dot(d, wgrp_ref[g]))
    return jnp.concatenate(ys, axis=1) * pscale_ref[...] * _silu(zp)


def _conv_branch(cext, convw_ref, convb_ref, store):
    for cc in range(CONV_DIM // GROUP_WIDTH):
        csl = slice(cc * GROUP_WIDTH, (cc + 1) * GROUP_WIDTH)
        ext = cext[:, :, csl]
        conv = convb_ref[:, csl].reshape(1, 1, GROUP_WIDTH)
        for kk in range(CONV_WIDTH):
            tap = ext if kk == CONV_WIDTH - 1 else pltpu.roll(ext, CONV_WIDTH - 1 - kk, axis=1)
            conv = conv + tap * convw_ref[kk:kk + 1, csl].reshape(1, 1, GROUP_WIDTH)
        store(cc, _silu(conv[:, CONV_PAD:, :]))


PROMPT_TILE = 256


def _seq_prompt_kernel(x_ref, nw_ref, w_ref, wdt_ref, k_ref, v_ref,
                       wgrp_ref, pscale_ref, convw_ref, convb_ref, dtb_ref, alog_ref, dexp_ref, ssdnw_ref,
                       gates_ref, ypool_ref, yssd_ref, yatt_ref, pool_o_ref, conv_o_ref, ssm_o_ref,
                       pext, cext, xs_scr, b_scr, c_scr, dt_scr, y_scr, h_scr, zs_scr, q_scr, za_scr):
    t, q = PROMPT_TILE, SSD_CHUNK
    s = pl.program_id(1)
    last = pl.num_programs(1) - 1
    hn = _rms(x_ref[...], nw_ref[...]).astype(BF16)

    def proj(piece, lo=0, hi=None):
        c0, c1 = piece
        return _dot(hn, w_ref[:, c0 + lo:(c1 if hi is None else c0 + hi)])

    @pl.when(s == 0)
    def _():
        pext[:, 0:POOL_PAD, :] = jnp.zeros((1, POOL_PAD, D_MODEL), F32)
        cext[:, 0:CONV_PAD, :] = jnp.zeros((1, CONV_PAD, CONV_DIM), F32)
        h_scr[...] = jnp.zeros(h_scr.shape, F32)

    @pl.when(s > 0)
    def _():
        carry_p = pext[:, t:t + POOL_PAD, :]
        carry_c = cext[:, t:t + CONV_PAD, :]
        pext[:, 0:POOL_PAD, :] = carry_p
        cext[:, 0:CONV_PAD, :] = carry_c

    def gates_piece(c):
        def run():
            gates_ref[:, c * D_MODEL:(c + 1) * D_MODEL] = _sigmoid(
                proj(COL_GATES, c * D_MODEL, (c + 1) * D_MODEL)).astype(BF16)
        return run

    def zs_piece(c):
        def run():
            zs_scr[:, c * D_MODEL:(c + 1) * D_MODEL] = _silu(proj(COL_ZS, c * D_MODEL, (c + 1) * D_MODEL)).astype(BF16)
        return run

    def q_piece():
        q_scr[...] = proj(COL_Q).astype(BF16)

    def za_piece():
        za_scr[...] = _silu(proj(COL_ZA)).astype(BF16)

    fillers = [gates_piece(0), gates_piece(1), gates_piece(2), zs_piece(0), zs_piece(1), q_piece, za_piece]

    def run_filler():
        if fillers:
            fillers.pop(0)()

    u = proj(COL_U).reshape(1, t, D_MODEL)
    pext[:, POOL_PAD:, :] = u
    pos = s * t + lax.broadcasted_iota(jnp.int32, (1, t, 1), 1)
    ypool_ref[...] = _pool_branch(pext, u, pos, wgrp_ref, pscale_ref, proj(COL_ZP)).astype(BF16)

    for c in range(CONV_DIM // D_MODEL):
        cext[:, CONV_PAD:, c * D_MODEL:(c + 1) * D_MODEL] = proj(
            COL_XBC, c * D_MODEL, (c + 1) * D_MODEL).reshape(1, t, D_MODEL)
    dt_scr[...] = _softplus(_dot(hn, wdt_ref[...]) + dtb_ref[...])

    def store_conv(cc, val):
        if cc < SSD_GROUPS:
            xs_scr[:, cc * GROUP_WIDTH:(cc + 1) * GROUP_WIDTH] = val[0]
        elif cc == SSD_GROUPS:
            b_scr[...] = val[0]
        else:
            c_scr[...] = val[0]
        run_filler()

    _conv_branch(cext, convw_ref, convb_ref, store_conv)

    a_neg = -jnp.exp(alog_ref[...])
    rq = lax.broadcasted_iota(jnp.int32, (q, q), 0)
    cq = lax.broadcasted_iota(jnp.int32, (q, q), 1)
    tril = rq >= cq
    tri_f = tril.astype(F32)
    lane_lo = lax.broadcasted_iota(jnp.int32, (1, LANES), 1) < SSD_HEAD_DIM
    pairs_per_group = SSD_HEADS // SSD_GROUPS // 2

    def chunk(c):
        rsl = slice(c * q, (c + 1) * q)
        dtc = dt_scr[rsl, :]
        acs = jnp.dot(tri_f, dtc * a_neg, precision=lax.Precision.HIGHEST, preferred_element_type=F32)
        acs_t = acs.T
        dt_t = dtc.T
        wdec_t = dt_t * jnp.exp(acs_t[:, q - 1:q] - acs_t)
        cdec = jnp.exp(acs[q - 1:q, :])
        for g in range(SSD_GROUPS):
            gsl = slice(g * SSD_STATE, (g + 1) * SSD_STATE)
            bg = b_scr[rsl, gsl]
            cg_b = c_scr[rsl, gsl].astype(BF16)
            cb = lax.dot_general(cg_b, bg.astype(BF16), _NT, preferred_element_type=F32)
            bg_t = bg.T
            hsl = slice(g * GROUP_WIDTH, (g + 1) * GROUP_WIDTH)
            z_g = _dot(cg_b, h_scr[:, hsl].astype(BF16))
            for jp in range(pairs_per_group):
                j = g * pairs_per_group + jp
                lsl = slice(j * LANES, (j + 1) * LANES)
                xp = xs_scr[rsl, lsl]
                x_bd = jnp.concatenate([jnp.where(lane_lo, xp, 0.0).astype(BF16),
                                        jnp.where(lane_lo, 0.0, xp).astype(BF16)], axis=0)
                ms, bws, cols = [], [], []
                for hh in range(2):
                    r = 2 * j + hh
                    cols.append(jnp.broadcast_to(acs[:, r:r + 1], (q, q)))
                    seg = cols[hh] - acs_t[r:r + 1, :]
                    ms.append(cb * jnp.exp(jnp.where(tril, seg, NEG_BIG)) * dt_t[r:r + 1, :])
                    bws.append(bg_t * wdec_t[r:r + 1, :])
                ea_pair = jnp.exp(jnp.where(lane_lo, cols[0], cols[1]))
                y = (_dot(jnp.concatenate(ms, axis=1).astype(BF16), x_bd)
                     + ea_pair * z_g[:, jp * LANES:(jp + 1) * LANES] + dexp_ref[:, lsl] * xp)
                y_scr[rsl, lsl] = y
                cd_pair = jnp.where(lane_lo, cdec[:, 2 * j:2 * j + 1], cdec[:, 2 * j + 1:2 * j + 2])
                h_scr[:, lsl] = h_scr[:, lsl] * cd_pair + _dot(jnp.concatenate(bws, axis=1).astype(BF16), x_bd)
            run_filler()

    for c in range(t // q):
        chunk(c)
    while fillers:
        run_filler()

    yz = y_scr[...] * zs_scr[...].astype(F32)
    yssd_ref[...] = _rms(yz, ssdnw_ref[...]).astype(BF16)

    scale = ATT_HEAD_DIM ** -0.5
    outs = []
    for hd in range(ATT_HEADS):
        hsl = slice(hd * ATT_HEAD_DIM, (hd + 1) * ATT_HEAD_DIM)
        p = _softmax_rows(lax.dot_general(q_scr[:, hsl], k_ref[0, hd], _NT, preferred_element_type=F32) * scale)
        outs.append(_dot(p.astype(BF16), v_ref[0, hd]))
    yatt_ref[...] = (jnp.concatenate(outs, axis=1) * za_scr[...].astype(F32)).astype(BF16)

    @pl.when(s == last)
    def _():
        pool_o_ref[...] = pext[:, t + POOL_PAD - POOL_HIST:t + POOL_PAD, :]
        conv_o_ref[...] = cext[:, t + CONV_PAD - (CONV_WIDTH - 1):t + CONV_PAD, :]
        for j in range(SSD_HEADS // 2):
            lsl = slice(j * LANES, (j + 1) * LANES)
            ssm_o_ref[0, lsl, :] = h_scr[:, lsl].T


def _seq_prompt(x2d, nw, w_main, w_dt, kb, vb, params, *, nseq, ntile):
    t = PROMPT_TILE
    m = x2d.shape[0]
    rowblk = lambda i, s: i * ntile + s
    row_spec = lambda width: pl.BlockSpec((t, width), lambda i, s: (rowblk(i, s), 0))
    seq_spec = lambda shape: pl.BlockSpec((1,) + shape, lambda i, s: (i,) + (0,) * len(shape))
    const_spec = lambda a: pl.BlockSpec(a.shape, lambda i, s: (0,) * a.ndim)
    resident = lambda a: pl.BlockSpec(a.shape, lambda i, s: (0,) * a.ndim, pipeline_mode=pl.Buffered(1))
    in_specs = [
        row_spec(D_MODEL), resident(nw), resident(w_main), resident(w_dt),
        seq_spec((ATT_HEADS, MEM_LEN, ATT_HEAD_DIM)),
        seq_spec((ATT_HEADS, MEM_LEN, ATT_HEAD_DIM)),
    ] + [const_spec(p) for p in params]
    out_specs = [
        row_spec(3 * D_MODEL), row_spec(D_MODEL), row_spec(SSD_WIDTH), row_spec(D_MODEL),
        seq_spec((POOL_HIST, D_MODEL)),
        seq_spec((CONV_WIDTH - 1, CONV_DIM)),
        seq_spec((SSD_WIDTH, SSD_STATE)),
    ]
    out_shape = [
        jax.ShapeDtypeStruct((m, 3 * D_MODEL), BF16),
        jax.ShapeDtypeStruct((m, D_MODEL), BF16),
        jax.ShapeDtypeStruct((m, SSD_WIDTH), BF16),
        jax.ShapeDtypeStruct((m, D_MODEL), BF16),
        jax.ShapeDtypeStruct((nseq, POOL_HIST, D_MODEL), F32),
        jax.ShapeDtypeStruct((nseq, CONV_WIDTH - 1, CONV_DIM), F32),
        jax.ShapeDtypeStruct((nseq, SSD_WIDTH, SSD_STATE), F32),
    ]
    scratch = [
        pltpu.VMEM((1, POOL_PAD + t, D_MODEL), F32),
        pltpu.VMEM((1, CONV_PAD + t, CONV_DIM), F32),
        pltpu.VMEM((t, SSD_WIDTH), F32),
        pltpu.VMEM((t, GROUP_WIDTH), F32),
        pltpu.VMEM((t, GROUP_WIDTH), F32),
        pltpu.VMEM((t, LANES), F32),
        pltpu.VMEM((t, SSD_WIDTH), F32),
        pltpu.VMEM((SSD_STATE, SSD_WIDTH), F32),
        pltpu.VMEM((t, SSD_WIDTH), BF16),
        pltpu.VMEM((t, D_MODEL), BF16),
        pltpu.VMEM((t, D_MODEL), BF16),
    ]
    return pl.pallas_call(
        _seq_prompt_kernel,
        grid=(nseq, ntile),
        in_specs=in_specs,
        out_specs=out_specs,
        out_shape=out_shape,
        scratch_shapes=scratch,
        compiler_params=pltpu.CompilerParams(dimension_semantics=("arbitrary", "arbitrary"),
                                             vmem_limit_bytes=SEQ_PROMPT_VMEM_LIMIT),
        name="seq_prompt",
    )(x2d, nw, w_main, w_dt, kb, vb, *params)


SAMPLE_BLOCK = 32


def _state_pre_kernel(u_ref, zp_ref, xbc_ref, dt_ref, ph_ref, ch_ref,
                      wgrp_ref, pscale_ref, convw_ref, convb_ref, dtb_ref, alog_ref,
                      ypool_ref, xs_ref, bc_ref, dts_ref, cd_ref, pool_o_ref, conv_o_ref, pext, cext):
    nb, t = SAMPLE_BLOCK, SUBLANES
    pext[:, 0:1, :] = jnp.zeros((nb, 1, D_MODEL), F32)
    pext[:, 1:POOL_PAD, :] = ph_ref[...]
    cext[:, 0:CONV_PAD - (CONV_WIDTH - 1), :] = jnp.zeros((nb, CONV_PAD - (CONV_WIDTH - 1), CONV_DIM), F32)
    cext[:, CONV_PAD - (CONV_WIDTH - 1):CONV_PAD, :] = ch_ref[...]

    u = u_ref[...].astype(F32).reshape(nb, t, D_MODEL)
    pext[:, POOL_PAD:, :] = u
    pos = PAST_LEN + lax.broadcasted_iota(jnp.int32, (1, t, 1), 1)
    ypool_ref[...] = _pool_branch(pext, u, pos, wgrp_ref, pscale_ref, zp_ref[...].astype(F32)).astype(BF16)

    cext[:, CONV_PAD:, :] = xbc_ref[...].astype(F32).reshape(nb, t, CONV_DIM)

    def store_conv(cc, val):
        val = val.reshape(nb * t, GROUP_WIDTH)
        if cc < SSD_GROUPS:
            xs_ref[:, cc * GROUP_WIDTH:(cc + 1) * GROUP_WIDTH] = val
        else:
            bc_ref[:, (cc - SSD_GROUPS) * GROUP_WIDTH:(cc - SSD_GROUPS + 1) * GROUP_WIDTH] = val

    _conv_branch(cext, convw_ref, convb_ref, store_conv)
    dt = _softplus(dt_ref[...] + dtb_ref[...])
    dts_ref[...] = dt
    a = dt * -jnp.exp(alog_ref[...])
    cd_ref[...] = jnp.exp(jnp.sum(a.reshape(nb, t, LANES), axis=1))
    pool_o_ref[...] = pext[:, t + POOL_PAD - POOL_HIST:t + POOL_PAD, :]
    conv_o_ref[...] = cext[:, t + CONV_PAD - (CONV_WIDTH - 1):t + CONV_PAD, :]


def _state_pre(main, dt, state_pool, state_conv, params):
    nb, t = SAMPLE_BLOCK, SUBLANES
    rows = nb * t
    m = main.shape[0]
    nseq = m // t
    col_spec = lambda width, idx: pl.BlockSpec((rows, width), lambda i: (i, idx))
    seq_spec = lambda shape: pl.BlockSpec((nb,) + shape, lambda i: (i,) + (0,) * len(shape))
    const_spec = lambda a: pl.BlockSpec(a.shape, lambda i: (0,) * a.ndim)
    return pl.pallas_call(
        _state_pre_kernel,
        grid=(nseq // nb,),
        in_specs=[col_spec(1024, 8), col_spec(1024, 9), col_spec(3072, 0), col_spec(LANES, 0),
                  seq_spec((POOL_HIST, D_MODEL)), seq_spec((CONV_WIDTH - 1, CONV_DIM))]
        + [const_spec(p) for p in params],
        out_specs=[col_spec(D_MODEL, 0), col_spec(SSD_WIDTH, 0), col_spec(2 * GROUP_WIDTH, 0), col_spec(LANES, 0),
                   pl.BlockSpec((nb, LANES), lambda i: (i, 0)),
                   seq_spec((POOL_HIST, D_MODEL)), seq_spec((CONV_WIDTH - 1, CONV_DIM))],
        out_shape=[jax.ShapeDtypeStruct((m, D_MODEL), BF16), jax.ShapeDtypeStruct((m, SSD_WIDTH), F32),
                   jax.ShapeDtypeStruct((m, 2 * GROUP_WIDTH), F32), jax.ShapeDtypeStruct((m, LANES), F32),
                   jax.ShapeDtypeStruct((nseq, LANES), F32),
                   jax.ShapeDtypeStruct((nseq, POOL_HIST, D_MODEL), F32),
                   jax.ShapeDtypeStruct((nseq, CONV_WIDTH - 1, CONV_DIM), F32)],
        scratch_shapes=[pltpu.VMEM((nb, POOL_PAD + t, D_MODEL), F32), pltpu.VMEM((nb, CONV_PAD + t, CONV_DIM), F32)],
        compiler_params=pltpu.CompilerParams(dimension_semantics=("arbitrary",), vmem_limit_bytes=VMEM_LIMIT),
        name="state_pre",
    )(main, main, main, dt, state_pool, state_conv, *params)


def _two_slot_pipeline(n, start_in, wait_in, compute, start_out=None, wait_out=None):
    start_in(0, 0)

    def body(bb, _):
        i0 = 2 * bb
        start_in(i0 + 1, 1)
        wait_in(i0, 0)
        if wait_out is not None:
            pl.when(bb > 0)(lambda: wait_out(i0 - 2, 0))
        compute(i0, 0)
        if start_out is not None:
            start_out(i0, 0)
        pl.when(bb + 1 < n // 2)(lambda: start_in(i0 + 2, 0))
        wait_in(i0 + 1, 1)
        if wait_out is not None:
            pl.when(bb > 0)(lambda: wait_out(i0 - 1, 1))
        compute(i0 + 1, 1)
        if start_out is not None:
            start_out(i0 + 1, 1)
        return 0

    lax.fori_loop(0, n // 2, body, 0)
    if wait_out is not None:
        wait_out(n - 2, 0)
        wait_out(n - 1, 1)


def _ssd_state_kernel(xs_ref, bc_ref, dt_ref, zs_ref, cd_ref, hin_hbm, expand_ref, segsum_ref, alog_ref, dexp_ref,
                      ssdnw_ref, yssd_ref, hout_hbm, hbuf, obuf, y_scr, sem_in, sem_out):
    nb, t = SAMPLE_BLOCK, SUBLANES
    base = pl.program_id(0) * nb
    a_neg = -jnp.exp(alog_ref[...])
    ridx = lax.broadcasted_iota(jnp.int32, (t, LANES), 0)

    def in_copies(b, slot):
        return [pltpu.make_async_copy(hin_hbm.at[base + b, g * GROUP_WIDTH:(g + 1) * GROUP_WIDTH, :],
                                      hbuf.at[slot, g * GROUP_WIDTH:(g + 1) * GROUP_WIDTH, :], sem_in.at[slot, g])
                for g in range(SSD_GROUPS)]

    def out_copies(b, slot):
        return [pltpu.make_async_copy(obuf.at[slot, g * GROUP_WIDTH:(g + 1) * GROUP_WIDTH, :],
                                      hout_hbm.at[base + b, g * GROUP_WIDTH:(g + 1) * GROUP_WIDTH, :],
                                      sem_out.at[slot, g])
                for g in range(SSD_GROUPS)]

    def start_all(copies):
        for c in copies:
            c.start()

    def wait_all(copies):
        for c in copies:
            c.wait()

    def compute(b, slot):
        rsl = pl.ds(pl.multiple_of(b * t, t), t)
        dtc = dt_ref[rsl, :]
        acs = dtc * a_neg
        for sh in (1, 2, 4):
            acs = acs + jnp.where(ridx >= sh, pltpu.roll(acs, sh, axis=0), 0.0)
        tot = acs[t - 1:t, :]
        x = xs_ref[rsl, :]
        bc = bc_ref[rsl, :]
        bm, cm = bc[:, :GROUP_WIDTH], bc[:, GROUP_WIDTH:]
        bm_r, cm_r = bm.astype(BF16).astype(F32), cm.astype(BF16).astype(F32)
        gs, ps = [], []
        for k in range(t):
            gs.append(jnp.exp(jnp.where(ridx >= k, acs - acs[k:k + 1, :], NEG_BIG)) * dtc[k:k + 1, :])
            ps.append(cm_r * bm_r[k:k + 1, :])
        cb_heads = _dot(jnp.concatenate(ps, axis=0).astype(BF16), segsum_ref[...])
        per_head = jnp.concatenate([jnp.concatenate(gs, axis=0) * cb_heads, jnp.exp(acs),
                                    dtc * jnp.exp(tot - acs)], axis=0)
        hi = per_head.astype(BF16)
        lo = (per_head - hi.astype(F32)).astype(BF16)
        wide = _dot(hi, expand_ref[...]) + _dot(lo, expand_ref[...])
        y = dexp_ref[...] * x
        for k in range(t):
            y = y + wide[k * t:(k + 1) * t, :] * x[k:k + 1, :]
        ea_wide = wide[t * t:t * t + t, :]
        xw = x * wide[t * t + t:, :]
        for g in range(SSD_GROUPS):
            gsl = slice(g * SSD_STATE, (g + 1) * SSD_STATE)
            wsl = slice(g * GROUP_WIDTH, (g + 1) * GROUP_WIDTH)
            hg = hbuf[slot, wsl, :]
            z_g = lax.dot_general(cm[:, gsl].astype(BF16), hg.astype(BF16), _NT, preferred_element_type=F32)
            y_scr[rsl, wsl] = y[:, wsl] + ea_wide[:, wsl] * z_g
            upd = lax.dot_general(xw[:, wsl].astype(BF16), bm[:, gsl].astype(BF16), _TN, preferred_element_type=F32)
            for r8 in range(SSD_HEADS // SSD_GROUPS):
                r = g * (SSD_HEADS // SSD_GROUPS) + r8
                rows_r = slice(r * SSD_HEAD_DIM, (r + 1) * SSD_HEAD_DIM)
                obuf[slot, rows_r, :] = (hbuf[slot, rows_r, :] * cd_ref[base + b, r]
                                         + upd[r8 * SSD_HEAD_DIM:(r8 + 1) * SSD_HEAD_DIM, :])

    _two_slot_pipeline(nb,
                       lambda b, slot: start_all(in_copies(b, slot)), lambda b, slot: wait_all(in_copies(b, slot)),
                       compute,
                       lambda b, slot: start_all(out_copies(b, slot)), lambda b, slot: wait_all(out_copies(b, slot)))
    yz = y_scr[...] * _silu(zs_ref[...].astype(F32))
    yssd_ref[...] = _rms(yz, ssdnw_ref[...]).astype(BF16)


def _ssd_state(xs, bc, dts, main, cd, hin, expand, segsum, alog, dexp, ssdnw):
    nb, t = SAMPLE_BLOCK, SUBLANES
    rows = nb * t
    m = xs.shape[0]
    nseq = m // t
    col_spec = lambda width, idx: pl.BlockSpec((rows, width), lambda i: (i, idx))
    const_spec = lambda a: pl.BlockSpec(a.shape, lambda i: (0,) * a.ndim)
    return pl.pallas_call(
        _ssd_state_kernel,
        grid=(nseq // nb,),
        in_specs=[col_spec(SSD_WIDTH, 0), col_spec(2 * GROUP_WIDTH, 0), col_spec(LANES, 0), col_spec(2048, 3),
                  pl.BlockSpec(memory_space=pltpu.SMEM), pl.BlockSpec(memory_space=pl.ANY),
                  const_spec(expand), const_spec(segsum), const_spec(alog), const_spec(dexp), const_spec(ssdnw)],
        out_specs=[col_spec(SSD_WIDTH, 0), pl.BlockSpec(memory_space=pl.ANY)],
        out_shape=[jax.ShapeDtypeStruct((m, SSD_WIDTH), BF16), jax.ShapeDtypeStruct(hin.shape, F32)],
        scratch_shapes=[pltpu.VMEM((2, SSD_WIDTH, SSD_STATE), F32), pltpu.VMEM((2, SSD_WIDTH, SSD_STATE), F32),
                        pltpu.VMEM((rows, SSD_WIDTH), F32),
                        pltpu.SemaphoreType.DMA((2, SSD_GROUPS)), pltpu.SemaphoreType.DMA((2, SSD_GROUPS))],
        compiler_params=pltpu.CompilerParams(dimension_semantics=("arbitrary",), vmem_limit_bytes=VMEM_LIMIT),
        name="ssd_state",
    )(xs, bc, dts, main, cd, hin, expand, segsum, alog, dexp, ssdnw)


ATT_ITEM = 2


def _att_state_kernel(q_ref, za_ref, k_hbm, v_hbm, yatt_ref, kbuf, vbuf, q_scr, att_scr, sem):
    nb, t = SAMPLE_BLOCK, SUBLANES
    base = pl.program_id(0) * nb
    scale = ATT_HEAD_DIM ** -0.5
    q_scr[...] = q_ref[...].astype(F32)
    head_of_lane = lax.broadcasted_iota(jnp.int32, (1, D_MODEL), 1) // ATT_HEAD_DIM

    def copies(item, slot):
        out = []
        for j in range(ATT_ITEM):
            for kv, (src, buf) in enumerate(((k_hbm, kbuf), (v_hbm, vbuf))):
                for hd in range(ATT_HEADS):
                    hsl = slice(hd * ATT_HEAD_DIM, (hd + 1) * ATT_HEAD_DIM)
                    out.append(pltpu.make_async_copy(
                        src.at[base + item * ATT_ITEM + j, :, hd, :], buf.at[slot, j, :, hsl],
                        sem.at[slot, (j * 2 + kv) * ATT_HEADS + hd]))
        return out

    def start_in(item, slot):
        for c in copies(item, slot):
            c.start()

    def wait_in(item, slot):
        for c in copies(item, slot):
            c.wait()

    def compute(item, slot):
        for j in range(ATT_ITEM):
            rsl = pl.ds(pl.multiple_of((item * ATT_ITEM + j) * t, t), t)
            qf = q_scr[rsl, :]
            q_bd = jnp.concatenate([jnp.where(head_of_lane == hd, qf, 0.0) for hd in range(ATT_HEADS)],
                                   axis=0).astype(BF16)
            sc = lax.dot_general(q_bd, kbuf[slot, j].astype(BF16), _NT, preferred_element_type=F32) * scale
            o = _dot(_softmax_rows(sc).astype(BF16), vbuf[slot, j].astype(BF16))
            att_scr[rsl, :] = jnp.concatenate(
                [o[hd * t:(hd + 1) * t, hd * ATT_HEAD_DIM:(hd + 1) * ATT_HEAD_DIM] for hd in range(ATT_HEADS)], axis=1)

    _two_slot_pipeline(nb // ATT_ITEM, start_in, wait_in, compute)
    yatt_ref[...] = (att_scr[...] * _silu(za_ref[...].astype(F32))).astype(BF16)


def _att_state(main, k, v):
    nb, t = SAMPLE_BLOCK, SUBLANES
    rows = nb * t
    m = main.shape[0]
    col_spec = lambda width, idx: pl.BlockSpec((rows, width), lambda i: (i, idx))
    kv_buf = pltpu.VMEM((2, ATT_ITEM, MEM_LEN, D_MODEL), F32)
    return pl.pallas_call(
        _att_state_kernel,
        grid=(m // rows,),
        in_specs=[col_spec(1024, 10), col_spec(1024, 11),
                  pl.BlockSpec(memory_space=pl.ANY), pl.BlockSpec(memory_space=pl.ANY)],
        out_specs=col_spec(D_MODEL, 0),
        out_shape=jax.ShapeDtypeStruct((m, D_MODEL), BF16),
        scratch_shapes=[kv_buf, kv_buf, pltpu.VMEM((rows, D_MODEL), F32), pltpu.VMEM((rows, D_MODEL), F32),
                        pltpu.SemaphoreType.DMA((2, ATT_ITEM * 2 * ATT_HEADS))],
        compiler_params=pltpu.CompilerParams(dimension_semantics=("arbitrary",), vmem_limit_bytes=VMEM_LIMIT),
        name="att_state",
    )(main, main, k, v)


DENSE_ROWS = 256


def _dense_kernel(x_ref, gt_ref, yp_ref, ys_ref, ya_ref, wpo_ref, wso_ref, wao_ref, wo_ref, fnw_ref, y_ref):
    gates = gt_ref[...].astype(F32)
    merged = (gates[:, 0:D_MODEL] * _dot(yp_ref[...], wpo_ref[...])
              + gates[:, D_MODEL:2 * D_MODEL] * _dot(ys_ref[...], wso_ref[...])
              + gates[:, 2 * D_MODEL:] * _dot(ya_ref[...], wao_ref[...]))
    x_out = x_ref[...] + _dot(merged.astype(BF16), wo_ref[...])
    y_ref[...] = _rms(x_out, fnw_ref[...])


def _dense(x2d, gates, gate_idx, yp, ys, ya, wpo, wso, wao, wo, fnw):
    m = x2d.shape[0]
    row = lambda width, idx=0: pl.BlockSpec((DENSE_ROWS, width), lambda i: (i, idx))
    resident = lambda a: pl.BlockSpec(a.shape, lambda i: (0,) * a.ndim, pipeline_mode=pl.Buffered(1))
    return pl.pallas_call(
        _dense_kernel,
        grid=(m // DENSE_ROWS,),
        in_specs=[row(D_MODEL), row(3 * D_MODEL, gate_idx), row(D_MODEL), row(SSD_WIDTH), row(D_MODEL),
                  resident(wpo), resident(wso), resident(wao), resident(wo), resident(fnw)],
        out_specs=row(D_MODEL),
        out_shape=jax.ShapeDtypeStruct((m, D_MODEL), F32),
        compiler_params=pltpu.CompilerParams(dimension_semantics=("arbitrary",), vmem_limit_bytes=VMEM_LIMIT),
        name="dense",
    )(x2d, gates, yp, ys, ya, wpo, wso, wao, wo, fnw)


def kernel(x_prompt, x_sample, mem_prompt, state_pool, state_conv, state_ssm, cache_mem_k, cache_mem_v,
           norm_w, w_in, w_pool_grp, pool_scale, conv_w, conv_b, dt_bias, a_log, d_skip, ssd_norm_w,
           mem_norm_w, w_mem_k, w_mem_v, w_pool_out, w_ssd_out, w_att_out, w_out, final_norm_w):
    assert w_in.shape[0] == 1
    bp, sp, d = x_prompt.shape
    bs, ss, _ = x_sample.shape
    assert ss == SUBLANES and sp % PROMPT_TILE == 0 and bs % SAMPLE_BLOCK == 0

    w_u, w_zp, w_zs, w_xbc, w_dtc, w_q, w_za, w_gt = jnp.split(
        w_in[0], [1024, 2048, 4096, 7168, 7200, 8224, 9248], axis=1)
    w_main = jnp.concatenate([w_xbc, w_gt, w_zs, w_u, w_zp, w_q, w_za], axis=1).astype(BF16)
    w_dt = jnp.pad(w_dtc, ((0, 0), (0, LANES - SSD_HEADS))).astype(BF16)
    nw = norm_w[0].reshape(1, d)
    pad_heads = lambda a: jnp.pad(a.reshape(1, SSD_HEADS), ((0, 0), (0, LANES - SSD_HEADS)))
    wgrp = w_pool_grp[0].astype(BF16)
    pscale = pool_scale[0].reshape(1, d)
    convb = conv_b[0].reshape(1, CONV_DIM)
    dtb, alog = pad_heads(dt_bias[0]), pad_heads(a_log[0])
    dexp = jnp.repeat(d_skip[0], SSD_HEAD_DIM).reshape(1, SSD_WIDTH)
    ssdnw = ssd_norm_w[0].reshape(1, SSD_WIDTH)
    dense_w = (w_pool_out[0].astype(BF16), w_ssd_out[0].astype(BF16), w_att_out[0].astype(BF16),
               w_out[0].astype(BF16), final_norm_w.reshape(1, d))
    head_of_lane = jnp.arange(SSD_WIDTH) // SSD_HEAD_DIM
    expand = (jnp.arange(LANES)[:, None] == head_of_lane[None, :]).astype(BF16)
    group_of_head = jnp.where(jnp.arange(LANES) < SSD_HEADS, jnp.arange(LANES) // (SSD_HEADS // SSD_GROUPS), -1)
    segsum = ((jnp.arange(GROUP_WIDTH) // SSD_STATE)[:, None] == group_of_head[None, :]).astype(BF16)

    mk, mv, mkb, mvb = _memkv(mem_prompt, mem_norm_w[0].reshape(1, d), w_mem_k[0].astype(BF16),
                              w_mem_v[0].astype(BF16))
    xp2 = x_prompt.reshape(bp * sp, d)
    gates_p, yp, ysd, ya, pool_p, conv_p, ssm_p = _seq_prompt(
        xp2, nw, w_main, w_dt, mkb, mvb, (wgrp, pscale, conv_w[0], convb, dtb, alog, dexp, ssdnw),
        nseq=bp, ntile=sp // PROMPT_TILE)
    y_prompt = _dense(xp2, gates_p, 0, yp, ysd, ya, *dense_w).reshape(bp, sp, d)

    xs2 = x_sample.reshape(bs * ss, d)
    main_s, dt_s = _inproj(xs2, nw, w_main, w_dt)
    yp, xs, bc, dts, cd, pool_s, conv_s = _state_pre(
        main_s, dt_s, state_pool[0], state_conv[0], (wgrp, pscale, conv_w[0], convb, dtb, alog))
    ysd, ssm_s = _ssd_state(xs, bc, dts, main_s, cd, state_ssm[0].reshape(bs, SSD_WIDTH, SSD_STATE),
                            expand, segsum, alog, dexp, ssdnw)
    ya = _att_state(main_s, cache_mem_k[0], cache_mem_v[0])
    y_sample = _dense(xs2, main_s, COL_GATES[0] // (3 * D_MODEL), yp, ysd, ya, *dense_w).reshape(bs, ss, d)

    ssm_shape = (SSD_GROUPS, SSD_HEADS // SSD_GROUPS, SSD_HEAD_DIM, SSD_STATE)
    return (y_prompt, y_sample,
            pool_p[None], conv_p[None], ssm_p.reshape((1, bp) + ssm_shape),
            mk[None], mv[None],
            pool_s[None], conv_s[None], ssm_s.reshape((1, bs) + ssm_shape))
```

```python
import functools

import jax
import jax.numpy as jnp
from jax import lax
from jax.experimental import pallas as pl
from jax.experimental.pallas import tpu as pltpu

F32 = jnp.float32
BF16 = jnp.bfloat16

D_MODEL = 1024
POOL_WINDOWS = (2, 4, 8, 16)
POOL_GROUP = 256
POOL_HIST = 15
POOL_PAD = 16
SSD_WIDTH = 2048
SSD_HEADS = 32
SSD_HEAD_DIM = 64
SSD_GROUPS = 4
SSD_STATE = 128
GROUP_WIDTH = SSD_WIDTH // SSD_GROUPS
CONV_WIDTH = 4
CONV_DIM = 3072
CONV_PAD = 8
SSD_CHUNK = 128
MEM_LEN = 256
ATT_HEADS = 4
ATT_HEAD_DIM = 256
PAST_LEN = 16384
EPS = 1e-6
NEG_BIG = -1e30
SUBLANES = 8
LANES = 128
MAIN_COLS = 12288
COL_XBC, COL_GATES, COL_ZS = (0, 3072), (3072, 6144), (6144, 8192)
COL_U, COL_ZP, COL_Q, COL_ZA = (8192, 9216), (9216, 10240), (10240, 11264), (11264, 12288)
VMEM_LIMIT = 56 * 1024 * 1024
SEQ_PROMPT_VMEM_LIMIT = 60 * 1024 * 1024

_NT = (((1,), (1,)), ((), ()))
_TN = (((0,), (0,)), ((), ()))


def _sigmoid(x):
    return 1.0 / (1.0 + jnp.exp(-x))


def _silu(x):
    return x * _sigmoid(x)


def _softplus(x):
    return jnp.maximum(x, 0.0) + jnp.log1p(jnp.exp(-jnp.abs(x)))


def _rms(x, w):
    return x * lax.rsqrt(jnp.mean(x * x, axis=-1, keepdims=True) + EPS) * w


def _dot(a, b):
    return jnp.dot(a, b, preferred_element_type=F32)


def _softmax_rows(sc):
    e = jnp.exp(sc - jnp.max(sc, axis=-1, keepdims=True))
    return e / jnp.sum(e, axis=-1, keepdims=True)


def _memkv_kernel(mem_ref, nw_ref, wk_ref, wv_ref, k_ref, v_ref, kb_ref, vb_ref):
    mh = _rms(mem_ref[0], nw_ref[...]).astype(BF16)
    k = _dot(mh, wk_ref[...])
    v = _dot(mh, wv_ref[...])
    for hd in range(ATT_HEADS):
        hsl = slice(hd * ATT_HEAD_DIM, (hd + 1) * ATT_HEAD_DIM)
        k_ref[0, :, hd, :] = k[:, hsl]
        v_ref[0, :, hd, :] = v[:, hsl]
        kb_ref[0, hd] = k[:, hsl].astype(BF16)
        vb_ref[0, hd] = v[:, hsl].astype(BF16)


def _memkv(mem, nw, wk, wv):
    b, m, d = mem.shape
    full = lambda shape: pl.BlockSpec(shape, lambda i: (0,) * len(shape))
    blk = pl.BlockSpec((1, m, d), lambda i: (i, 0, 0))
    oblk = pl.BlockSpec((1, m, ATT_HEADS, ATT_HEAD_DIM), lambda i: (i, 0, 0, 0))
    hblk = pl.BlockSpec((1, ATT_HEADS, m, ATT_HEAD_DIM), lambda i: (i, 0, 0, 0))
    return pl.pallas_call(
        _memkv_kernel,
        grid=(b,),
        in_specs=[blk, full((1, d)), full((d, d)), full((d, d))],
        out_specs=[oblk, oblk, hblk, hblk],
        out_shape=[jax.ShapeDtypeStruct((b, m, ATT_HEADS, ATT_HEAD_DIM), F32)] * 2
        + [jax.ShapeDtypeStruct((b, ATT_HEADS, m, ATT_HEAD_DIM), BF16)] * 2,
        compiler_params=pltpu.CompilerParams(dimension_semantics=("arbitrary",), vmem_limit_bytes=VMEM_LIMIT),
        name="memkv",
    )(mem, nw, wk, wv)


INPROJ_ROWS = 256
INPROJ_COL_CHUNK = 1024


def _inproj_kernel(x_ref, nw_ref, w_ref, wdt_ref, main_ref, dt_ref):
    h = _rms(x_ref[...], nw_ref[...]).astype(BF16)
    dt_ref[...] = _dot(h, wdt_ref[...])
    for c in range(MAIN_COLS // INPROJ_COL_CHUNK):
        cols = slice(c * INPROJ_COL_CHUNK, (c + 1) * INPROJ_COL_CHUNK)
        val = _dot(h, w_ref[:, cols])
        if COL_GATES[0] <= c * INPROJ_COL_CHUNK < COL_GATES[1]:
            val = _sigmoid(val)
        main_ref[:, cols] = val.astype(BF16)


def _inproj(x2d, nw, w_main, w_dt):
    m = x2d.shape[0]
    resident = lambda shape: pl.BlockSpec(shape, lambda i: (0,) * len(shape), pipeline_mode=pl.Buffered(1))
    return pl.pallas_call(
        _inproj_kernel,
        grid=(m // INPROJ_ROWS,),
        in_specs=[
            pl.BlockSpec((INPROJ_ROWS, D_MODEL), lambda i: (i, 0)),
            resident((1, D_MODEL)),
            resident((D_MODEL, MAIN_COLS)),
            resident((D_MODEL, LANES)),
        ],
        out_specs=[
            pl.BlockSpec((INPROJ_ROWS, MAIN_COLS), lambda i: (i, 0)),
            pl.BlockSpec((INPROJ_ROWS, LANES), lambda i: (i, 0)),
        ],
        out_shape=[jax.ShapeDtypeStruct((m, MAIN_COLS), BF16), jax.ShapeDtypeStruct((m, LANES), F32)],
        compiler_params=pltpu.CompilerParams(dimension_semantics=("arbitrary",), vmem_limit_bytes=VMEM_LIMIT),
        name="inproj",
    )(x2d, nw, w_main, w_dt)


def _pool_branch(pext, u, pos, wgrp_ref, pscale_ref, zp):
    nb, t, _ = u.shape
    ys = []
    for g, w in enumerate(POOL_WINDOWS):
        cols = slice(g * POOL_GROUP, (g + 1) * POOL_GROUP)
        win = pext[:, :, cols]
        for sh in [1 << e for e in range(g + 1)]:
            win = win + pltpu.roll(win, sh, axis=1)
        win = win[:, POOL_PAD:, :]
        cnt = jnp.minimum(w, pos + 1).astype(F32)
        d = (win / cnt - u[:, :, cols]).astype(BF16).reshape(nb * t, POOL_GROUP)
        ys.append(_dot(d, wgrp_ref[g]))
    return jnp.concatenate(ys, axis=1) * pscale_ref[...] * _silu(zp)


def _conv_branch(cext, convw_ref, convb_ref, store):
    for cc in range(CONV_DIM // GROUP_WIDTH):
        csl = slice(cc * GROUP_WIDTH, (cc + 1) * GROUP_WIDTH)
        ext = cext[:, :, csl]
        conv = convb_ref[:, csl].reshape(1, 1, GROUP_WIDTH)
        for kk in range(CONV_WIDTH):
            tap = ext if kk == CONV_WIDTH - 1 else pltpu.roll(ext, CONV_WIDTH - 1 - kk, axis=1)
            conv = conv + tap * convw_ref[kk:kk + 1, csl].reshape(1, 1, GROUP_WIDTH)
        store(cc, _silu(conv[:, CONV_PAD:, :]))


PROMPT_TILE = 256


def _seq_prompt_kernel(x_ref, nw_ref, w_ref, wdt_ref, k_ref, v_ref,
                       wgrp_ref, pscale_ref, convw_ref, convb_ref, dtb_ref, alog_ref, dexp_ref, ssdnw_ref,
                       gates_ref, ypool_ref, yssd_ref, yatt_ref, pool_o_ref, conv_o_ref, ssm_o_ref,
                       pext, cext, xs_scr, b_scr, c_scr, dt_scr, y_scr, h_scr, zs_scr, q_scr, za_scr):
    t, q = PROMPT_TILE, SSD_CHUNK
    s = pl.program_id(1)
    last = pl.num_programs(1) - 1
    hn = _rms(x_ref[...], nw_ref[...]).astype(BF16)

    def proj(piece, lo=0, hi=None):
        c0, c1 = piece
        return _dot(hn, w_ref[:, c0 + lo:(c1 if hi is None else c0 + hi)])

    @pl.when(s == 0)
    def _():
        pext[:, 0:POOL_PAD, :] = jnp.zeros((1, POOL_PAD, D_MODEL), F32)
        cext[:, 0:CONV_PAD, :] = jnp.zeros((1, CONV_PAD, CONV_DIM), F32)
        h_scr[...] = jnp.zeros(h_scr.shape, F32)

    @pl.when(s > 0)
    def _():
        carry_p = pext[:, t:t + POOL_PAD, :]
        carry_c = cext[:, t:t + CONV_PAD, :]
        pext[:, 0:POOL_PAD, :] = carry_p
        cext[:, 0:CONV_PAD, :] = carry_c

    def gates_piece(c):
        def run():
            gates_ref[:, c * D_MODEL:(c + 1) * D_MODEL] = _sigmoid(
                proj(COL_GATES, c * D_MODEL, (c + 1) * D_MODEL).astype(BF16))
        return run

    def zs_piece(c):
        def run():
            zs_scr[:, c * D_MODEL:(c + 1) * D_MODEL] = _silu(proj(COL_ZS, c * D_MODEL, (c + 1) * D_MODEL).astype(BF16))
        return run

    def q_piece():
        q_scr[...] = proj(COL_Q).astype(BF16)

    def za_piece():
        za_scr[...] = _silu(proj(COL_ZA).astype(BF16))

    fillers = [gates_piece(0), gates_piece(1), gates_piece(2), zs_piece(0), zs_piece(1), q_piece, za_piece]

    def run_filler():
        if fillers:
            fillers.pop(0)()

    u = proj(COL_U).reshape(1, t, D_MODEL)
    pext[:, POOL_PAD:, :] = u
    pos = s * t + lax.broadcasted_iota(jnp.int32, (1, t, 1), 1)
    ypool_ref[...] = _pool_branch(pext, u, pos, wgrp_ref, pscale_ref, proj(COL_ZP)).astype(BF16)

    for c in range(CONV_DIM // D_MODEL):
        cext[:, CONV_PAD:, c * D_MODEL:(c + 1) * D_MODEL] = proj(
            COL_XBC, c * D_MODEL, (c + 1) * D_MODEL).reshape(1, t, D_MODEL)
    dt_scr[...] = _softplus(_dot(hn, wdt_ref[...]) + dtb_ref[...])

    def store_conv(cc, val):
        if cc < SSD_GROUPS:
            xs_scr[:, cc * GROUP_WIDTH:(cc + 1) * GROUP_WIDTH] = val[0]
        elif cc == SSD_GROUPS:
            b_scr[...] = val[0]
        else:
            c_scr[...] = val[0]
        run_filler()

    _conv_branch(cext, convw_ref, convb_ref, store_conv)

    a_neg = -jnp.exp(alog_ref[...])
    rq = lax.broadcasted_iota(jnp.int32, (q, q), 0)
    cq = lax.broadcasted_iota(jnp.int32, (q, q), 1)
    tril = rq >= cq
    tri_f = tril.astype(F32)
    lane_lo = lax.broadcasted_iota(jnp.int32, (1, LANES), 1) < SSD_HEAD_DIM
    pairs_per_group = SSD_HEADS // SSD_GROUPS // 2

    def chunk(c):
        rsl = slice(c * q, (c + 1) * q)
        dtc = dt_scr[rsl, :]
        acs = jnp.dot(tri_f, dtc * a_neg, precision=lax.Precision.HIGHEST, preferred_element_type=F32)
        acs_t = acs.T
        dt_t = dtc.T
        wdec_t = (dt_t * jnp.exp(acs_t[:, q - 1:q] - acs_t)).astype(BF16)
        row_t = acs_t - jnp.log(dt_t)
        cdec = jnp.exp(acs[q - 1:q, :])
        for g in range(SSD_GROUPS):
            gsl = slice(g * SSD_STATE, (g + 1) * SSD_STATE)
            bg = b_scr[rsl, gsl]
            cg_b = c_scr[rsl, gsl].astype(BF16)
            cb = lax.dot_general(cg_b, bg.astype(BF16), _NT, preferred_element_type=F32)
            bg_t = bg.T.astype(BF16)
            hsl = slice(g * GROUP_WIDTH, (g + 1) * GROUP_WIDTH)
            z_g = _dot(cg_b, h_scr[:, hsl].astype(BF16))
            for jp in range(pairs_per_group):
                j = g * pairs_per_group + jp
                lsl = slice(j * LANES, (j + 1) * LANES)
                xp = xs_scr[rsl, lsl]
                xp_b = xp.astype(BF16)
                zero_b = jnp.zeros_like(xp_b)
                x_bd = jnp.concatenate([jnp.where(lane_lo, xp_b, zero_b), jnp.where(lane_lo, zero_b, xp_b)],
                                       axis=0)
                ms, bws, cols = [], [], []
                for hh in range(2):
                    r = 2 * j + hh
                    cols.append(jnp.broadcast_to(acs[:, r:r + 1], (q, q)))
                    seg = cols[hh] - row_t[r:r + 1, :]
                    ms.append(cb * jnp.exp(jnp.where(tril, seg, NEG_BIG)))
                    bws.append(bg_t * wdec_t[r:r + 1, :])
                ea_pair = jnp.exp(jnp.where(lane_lo, cols[0], cols[1]))
                y = (_dot(jnp.concatenate(ms, axis=1).astype(BF16), x_bd)
                     + ea_pair * z_g[:, jp * LANES:(jp + 1) * LANES] + dexp_ref[:, lsl] * xp)
                y_scr[rsl, lsl] = y
                cd_pair = jnp.where(lane_lo, cdec[:, 2 * j:2 * j + 1], cdec[:, 2 * j + 1:2 * j + 2])
                h_scr[:, lsl] = h_scr[:, lsl] * cd_pair + _dot(jnp.concatenate(bws, axis=1), x_bd)
            run_filler()

    for c in range(t // q):
        chunk(c)
    while fillers:
        run_filler()

    yz = y_scr[...] * zs_scr[...].astype(F32)
    yssd_ref[...] = _rms(yz, ssdnw_ref[...]).astype(BF16)

    scale = ATT_HEAD_DIM ** -0.5
    outs = []
    for hd in range(ATT_HEADS):
        hsl = slice(hd * ATT_HEAD_DIM, (hd + 1) * ATT_HEAD_DIM)
        p = _softmax_rows(lax.dot_general(q_scr[:, hsl], k_ref[0, hd], _NT, preferred_element_type=F32) * scale)
        outs.append(_dot(p.astype(BF16), v_ref[0, hd]))
    yatt_ref[...] = (jnp.concatenate(outs, axis=1) * za_scr[...].astype(F32)).astype(BF16)

    @pl.when(s == last)
    def _():
        pool_o_ref[...] = pext[:, t + POOL_PAD - POOL_HIST:t + POOL_PAD, :]
        conv_o_ref[...] = cext[:, t + CONV_PAD - (CONV_WIDTH - 1):t + CONV_PAD, :]
        for j in range(SSD_HEADS // 2):
            lsl = slice(j * LANES, (j + 1) * LANES)
            ssm_o_ref[0, lsl, :] = h_scr[:, lsl].T


def _seq_prompt(x2d, nw, w_main, w_dt, kb, vb, params, *, nseq, ntile):
    t = PROMPT_TILE
    m = x2d.shape[0]
    rowblk = lambda i, s: i * ntile + s
    row_spec = lambda width: pl.BlockSpec((t, width), lambda i, s: (rowblk(i, s), 0))
    seq_spec = lambda shape: pl.BlockSpec((1,) + shape, lambda i, s: (i,) + (0,) * len(shape))
    const_spec = lambda a: pl.BlockSpec(a.shape, lambda i, s: (0,) * a.ndim)
    resident = lambda a: pl.BlockSpec(a.shape, lambda i, s: (0,) * a.ndim, pipeline_mode=pl.Buffered(1))
    in_specs = [
        row_spec(D_MODEL), resident(nw), resident(w_main), resident(w_dt),
        seq_spec((ATT_HEADS, MEM_LEN, ATT_HEAD_DIM)),
        seq_spec((ATT_HEADS, MEM_LEN, ATT_HEAD_DIM)),
    ] + [const_spec(p) for p in params]
    out_specs = [
        row_spec(3 * D_MODEL), row_spec(D_MODEL), row_spec(SSD_WIDTH), row_spec(D_MODEL),
        seq_spec((POOL_HIST, D_MODEL)),
        seq_spec((CONV_WIDTH - 1, CONV_DIM)),
        seq_spec((SSD_WIDTH, SSD_STATE)),
    ]
    out_shape = [
        jax.ShapeDtypeStruct((m, 3 * D_MODEL), BF16),
        jax.ShapeDtypeStruct((m, D_MODEL), BF16),
        jax.ShapeDtypeStruct((m, SSD_WIDTH), BF16),
        jax.ShapeDtypeStruct((m, D_MODEL), BF16),
        jax.ShapeDtypeStruct((nseq, POOL_HIST, D_MODEL), F32),
        jax.ShapeDtypeStruct((nseq, CONV_WIDTH - 1, CONV_DIM), F32),
        jax.ShapeDtypeStruct((nseq, SSD_WIDTH, SSD_STATE), F32),
    ]
    scratch = [
        pltpu.VMEM((1, POOL_PAD + t, D_MODEL), F32),
        pltpu.VMEM((1, CONV_PAD + t, CONV_DIM), F32),
        pltpu.VMEM((t, SSD_WIDTH), F32),
        pltpu.VMEM((t, GROUP_WIDTH), F32),
        pltpu.VMEM((t, GROUP_WIDTH), F32),
        pltpu.VMEM((t, LANES), F32),
        pltpu.VMEM((t, SSD_WIDTH), F32),
        pltpu.VMEM((SSD_STATE, SSD_WIDTH), F32),
        pltpu.VMEM((t, SSD_WIDTH), BF16),
        pltpu.VMEM((t, D_MODEL), BF16),
        pltpu.VMEM((t, D_MODEL), BF16),
    ]
    return pl.pallas_call(
        _seq_prompt_kernel,
        grid=(nseq, ntile),
        in_specs=in_specs,
        out_specs=out_specs,
        out_shape=out_shape,
        scratch_shapes=scratch,
        compiler_params=pltpu.CompilerParams(dimension_semantics=("arbitrary", "arbitrary"),
                                             vmem_limit_bytes=SEQ_PROMPT_VMEM_LIMIT),
        name="seq_prompt",
    )(x2d, nw, w_main, w_dt, kb, vb, *params)


SAMPLE_BLOCK = 32


def _state_pre_kernel(u_ref, zp_ref, xbc_ref, dt_ref, ph_ref, ch_ref,
                      wgrp_ref, pscale_ref, convw_ref, convb_ref, dtb_ref, alog_ref,
                      ypool_ref, xs_ref, bc_ref, dts_ref, cd_ref, pool_o_ref, conv_o_ref, pext, cext):
    nb, t = SAMPLE_BLOCK, SUBLANES
    pext[:, 0:1, :] = jnp.zeros((nb, 1, D_MODEL), F32)
    pext[:, 1:POOL_PAD, :] = ph_ref[...]
    cext[:, 0:CONV_PAD - (CONV_WIDTH - 1), :] = jnp.zeros((nb, CONV_PAD - (CONV_WIDTH - 1), CONV_DIM), F32)
    cext[:, CONV_PAD - (CONV_WIDTH - 1):CONV_PAD, :] = ch_ref[...]

    u = u_ref[...].astype(F32).reshape(nb, t, D_MODEL)
    pext[:, POOL_PAD:, :] = u
    pos = PAST_LEN + lax.broadcasted_iota(jnp.int32, (1, t, 1), 1)
    ypool_ref[...] = _pool_branch(pext, u, pos, wgrp_ref, pscale_ref, zp_ref[...].astype(F32)).astype(BF16)

    cext[:, CONV_PAD:, :] = xbc_ref[...].astype(F32).reshape(nb, t, CONV_DIM)

    def store_conv(cc, val):
        val = val.reshape(nb * t, GROUP_WIDTH)
        if cc < SSD_GROUPS:
            xs_ref[:, cc * GROUP_WIDTH:(cc + 1) * GROUP_WIDTH] = val
        else:
            bc_ref[:, (cc - SSD_GROUPS) * GROUP_WIDTH:(cc - SSD_GROUPS + 1) * GROUP_WIDTH] = val

    _conv_branch(cext, convw_ref, convb_ref, store_conv)
    dt = _softplus(dt_ref[...] + dtb_ref[...])
    dts_ref[...] = dt
    a = dt * -jnp.exp(alog_ref[...])
    cd_ref[...] = jnp.exp(jnp.sum(a.reshape(nb, t, LANES), axis=1))
    pool_o_ref[...] = pext[:, t + POOL_PAD - POOL_HIST:t + POOL_PAD, :]
    conv_o_ref[...] = cext[:, t + CONV_PAD - (CONV_WIDTH - 1):t + CONV_PAD, :]


def _state_pre(main, dt, state_pool, state_conv, params):
    nb, t = SAMPLE_BLOCK, SUBLANES
    rows = nb * t
    m = main.shape[0]
    nseq = m // t
    col_spec = lambda width, idx: pl.BlockSpec((rows, width), lambda i: (i, idx))
    seq_spec = lambda shape: pl.BlockSpec((nb,) + shape, lambda i: (i,) + (0,) * len(shape))
    const_spec = lambda a: pl.BlockSpec(a.shape, lambda i: (0,) * a.ndim)
    return pl.pallas_call(
        _state_pre_kernel,
        grid=(nseq // nb,),
        in_specs=[col_spec(1024, 8), col_spec(1024, 9), col_spec(3072, 0), col_spec(LANES, 0),
                  seq_spec((POOL_HIST, D_MODEL)), seq_spec((CONV_WIDTH - 1, CONV_DIM))]
        + [const_spec(p) for p in params],
        out_specs=[col_spec(D_MODEL, 0), col_spec(SSD_WIDTH, 0), col_spec(2 * GROUP_WIDTH, 0), col_spec(LANES, 0),
                   pl.BlockSpec((nb, LANES), lambda i: (i, 0)),
                   seq_spec((POOL_HIST, D_MODEL)), seq_spec((CONV_WIDTH - 1, CONV_DIM))],
        out_shape=[jax.ShapeDtypeStruct((m, D_MODEL), BF16), jax.ShapeDtypeStruct((m, SSD_WIDTH), F32),
                   jax.ShapeDtypeStruct((m, 2 * GROUP_WIDTH), F32), jax.ShapeDtypeStruct((m, LANES), F32),
                   jax.ShapeDtypeStruct((nseq, LANES), F32),
                   jax.ShapeDtypeStruct((nseq, POOL_HIST, D_MODEL), F32),
                   jax.ShapeDtypeStruct((nseq, CONV_WIDTH - 1, CONV_DIM), F32)],
        scratch_shapes=[pltpu.VMEM((nb, POOL_PAD + t, D_MODEL), F32), pltpu.VMEM((nb, CONV_PAD + t, CONV_DIM), F32)],
        compiler_params=pltpu.CompilerParams(dimension_semantics=("arbitrary",), vmem_limit_bytes=VMEM_LIMIT),
        name="state_pre",
    )(main, main, main, dt, state_pool, state_conv, *params)


PIPE_SLOTS = 4


def _ring_pipeline(n, start_in, wait_in, compute, start_out=None, wait_out=None):
    ns = PIPE_SLOTS
    for i in range(ns - 1):
        start_in(i, i)

    def body(bb, _):
        for k in range(ns):
            i = ns * bb + k
            nxt = i + ns - 1
            pl.when(nxt < n)(functools.partial(start_in, nxt, (k + ns - 1) % ns))
            wait_in(i, k)
            if wait_out is not None:
                pl.when(bb > 0)(functools.partial(wait_out, i - ns, k))
            compute(i, k)
            if start_out is not None:
                start_out(i, k)
        return 0

    lax.fori_loop(0, n // ns, body, 0)
    if wait_out is not None:
        for k in range(ns):
            wait_out(n - ns + k, k)


def _ssd_state_kernel(xs_ref, bc_ref, dt_ref, zs_ref, cd_ref, hin_hbm, expand_ref, segsum_ref, alog_ref, dexp_ref,
                      ssdnw_ref, yssd_ref, hout_hbm, hbuf, obuf, y_scr, sem_in, sem_out):
    nb, t = SAMPLE_BLOCK, SUBLANES
    base = pl.program_id(0) * nb
    a_neg = -jnp.exp(alog_ref[...])
    ridx = lax.broadcasted_iota(jnp.int32, (t, LANES), 0)

    def in_copies(b, slot):
        return [pltpu.make_async_copy(hin_hbm.at[base + b, g * GROUP_WIDTH:(g + 1) * GROUP_WIDTH, :],
                                      hbuf.at[slot, g * GROUP_WIDTH:(g + 1) * GROUP_WIDTH, :], sem_in.at[slot, g])
                for g in range(SSD_GROUPS)]

    def out_copies(b, slot):
        return [pltpu.make_async_copy(obuf.at[slot, g * GROUP_WIDTH:(g + 1) * GROUP_WIDTH, :],
                                      hout_hbm.at[base + b, g * GROUP_WIDTH:(g + 1) * GROUP_WIDTH, :],
                                      sem_out.at[slot, g])
                for g in range(SSD_GROUPS)]

    def start_all(copies):
        for c in copies:
            c.start()

    def wait_all(copies):
        for c in copies:
            c.wait()

    def compute(b, slot):
        rsl = pl.ds(pl.multiple_of(b * t, t), t)
        dtc = dt_ref[rsl, :]
        acs = dtc * a_neg
        for sh in (1, 2, 4):
            acs = acs + jnp.where(ridx >= sh, pltpu.roll(acs, sh, axis=0), 0.0)
        tot = acs[t - 1:t, :]
        x = xs_ref[rsl, :]
        bc = bc_ref[rsl, :]
        bm, cm = bc[:, :GROUP_WIDTH], bc[:, GROUP_WIDTH:]
        bm_r, cm_r = bm.astype(BF16).astype(F32), cm.astype(BF16).astype(F32)
        gs, ps = [], []
        for k in range(t):
            gs.append(jnp.exp(jnp.where(ridx >= k, acs - acs[k:k + 1, :], NEG_BIG)) * dtc[k:k + 1, :])
            ps.append(cm_r * bm_r[k:k + 1, :])
        cb_heads = _dot(jnp.concatenate(ps, axis=0).astype(BF16), segsum_ref[...])
        per_head = jnp.concatenate([jnp.concatenate(gs, axis=0) * cb_heads, jnp.exp(acs),
                                    dtc * jnp.exp(tot - acs)], axis=0)
        hi = per_head.astype(BF16)
        lo = (per_head - hi.astype(F32)).astype(BF16)
        wide = _dot(hi, expand_ref[...]) + _dot(lo, expand_ref[...])
        y = dexp_ref[...] * x
        for k in range(t):
            y = y + wide[k * t:(k + 1) * t, :] * x[k:k + 1, :]
        ea_wide = wide[t * t:t * t + t, :]
        xw = x * wide[t * t + t:, :]
        for g in range(SSD_GROUPS):
            gsl = slice(g * SSD_STATE, (g + 1) * SSD_STATE)
            wsl = slice(g * GROUP_WIDTH, (g + 1) * GROUP_WIDTH)
            hg = hbuf[slot, wsl, :]
            z_g = lax.dot_general(cm[:, gsl].astype(BF16), hg.astype(BF16), _NT, preferred_element_type=F32)
            y_scr[rsl, wsl] = y[:, wsl] + ea_wide[:, wsl] * z_g
            upd = lax.dot_general(xw[:, wsl].astype(BF16), bm[:, gsl].astype(BF16), _TN, preferred_element_type=F32)
            for r8 in range(SSD_HEADS // SSD_GROUPS):
                r = g * (SSD_HEADS // SSD_GROUPS) + r8
                rows_r = slice(r * SSD_HEAD_DIM, (r + 1) * SSD_HEAD_DIM)
                obuf[slot, rows_r, :] = (hbuf[slot, rows_r, :] * cd_ref[base + b, r]
                                         + upd[r8 * SSD_HEAD_DIM:(r8 + 1) * SSD_HEAD_DIM, :])

    _ring_pipeline(nb,
                       lambda b, slot: start_all(in_copies(b, slot)), lambda b, slot: wait_all(in_copies(b, slot)),
                       compute,
                       lambda b, slot: start_all(out_copies(b, slot)), lambda b, slot: wait_all(out_copies(b, slot)))
    yz = y_scr[...] * _silu(zs_ref[...].astype(F32))
    yssd_ref[...] = _rms(yz, ssdnw_ref[...]).astype(BF16)


def _ssd_state(xs, bc, dts, main, cd, hin, expand, segsum, alog, dexp, ssdnw):
    nb, t = SAMPLE_BLOCK, SUBLANES
    rows = nb * t
    m = xs.shape[0]
    nseq = m // t
    col_spec = lambda width, idx: pl.BlockSpec((rows, width), lambda i: (i, idx))
    const_spec = lambda a: pl.BlockSpec(a.shape, lambda i: (0,) * a.ndim)
    return pl.pallas_call(
        _ssd_state_kernel,
        grid=(nseq // nb,),
        in_specs=[col_spec(SSD_WIDTH, 0), col_spec(2 * GROUP_WIDTH, 0), col_spec(LANES, 0), col_spec(2048, 3),
                  pl.BlockSpec(memory_space=pltpu.SMEM), pl.BlockSpec(memory_space=pl.ANY),
                  const_spec(expand), const_spec(segsum), const_spec(alog), const_spec(dexp), const_spec(ssdnw)],
        out_specs=[col_spec(SSD_WIDTH, 0), pl.BlockSpec(memory_space=pl.ANY)],
        out_shape=[jax.ShapeDtypeStruct((m, SSD_WIDTH), BF16), jax.ShapeDtypeStruct(hin.shape, F32)],
        scratch_shapes=[pltpu.VMEM((PIPE_SLOTS, SSD_WIDTH, SSD_STATE), F32),
                        pltpu.VMEM((PIPE_SLOTS, SSD_WIDTH, SSD_STATE), F32),
                        pltpu.VMEM((rows, SSD_WIDTH), F32),
                        pltpu.SemaphoreType.DMA((PIPE_SLOTS, SSD_GROUPS)),
                        pltpu.SemaphoreType.DMA((PIPE_SLOTS, SSD_GROUPS))],
        compiler_params=pltpu.CompilerParams(dimension_semantics=("arbitrary",), vmem_limit_bytes=VMEM_LIMIT),
        name="ssd_state",
    )(xs, bc, dts, main, cd, hin, expand, segsum, alog, dexp, ssdnw)


ATT_ITEM = 2


def _att_state_kernel(q_ref, za_ref, k_hbm, v_hbm, yatt_ref, kbuf, vbuf, q_scr, att_scr, sem):
    nb, t = SAMPLE_BLOCK, SUBLANES
    base = pl.program_id(0) * nb
    scale = ATT_HEAD_DIM ** -0.5
    q_scr[...] = q_ref[...].astype(F32)
    head_of_lane = lax.broadcasted_iota(jnp.int32, (1, D_MODEL), 1) // ATT_HEAD_DIM

    def copies(item, slot):
        out = []
        for j in range(ATT_ITEM):
            for kv, (src, buf) in enumerate(((k_hbm, kbuf), (v_hbm, vbuf))):
                for hd in range(ATT_HEADS):
                    hsl = slice(hd * ATT_HEAD_DIM, (hd + 1) * ATT_HEAD_DIM)
                    out.append(pltpu.make_async_copy(
                        src.at[base + item * ATT_ITEM + j, :, hd, :], buf.at[slot, j, :, hsl],
                        sem.at[slot, (j * 2 + kv) * ATT_HEADS + hd]))
        return out

    def start_in(item, slot):
        for c in copies(item, slot):
            c.start()

    def wait_in(item, slot):
        for c in copies(item, slot):
            c.wait()

    def compute(item, slot):
        for j in range(ATT_ITEM):
            rsl = pl.ds(pl.multiple_of((item * ATT_ITEM + j) * t, t), t)
            qf = q_scr[rsl, :]
            q_bd = jnp.concatenate([jnp.where(head_of_lane == hd, qf, 0.0) for hd in range(ATT_HEADS)],
                                   axis=0).astype(BF16)
            sc = lax.dot_general(q_bd, kbuf[slot, j].astype(BF16), _NT, preferred_element_type=F32) * scale
            o = _dot(_softmax_rows(sc).astype(BF16), vbuf[slot, j].astype(BF16))
            att_scr[rsl, :] = jnp.concatenate(
                [o[hd * t:(hd + 1) * t, hd * ATT_HEAD_DIM:(hd + 1) * ATT_HEAD_DIM] for hd in range(ATT_HEADS)], axis=1)

    _ring_pipeline(nb // ATT_ITEM, start_in, wait_in, compute)
    yatt_ref[...] = (att_scr[...] * _silu(za_ref[...].astype(F32))).astype(BF16)


def _att_state(main, k, v):
    nb, t = SAMPLE_BLOCK, SUBLANES
    rows = nb * t
    m = main.shape[0]
    col_spec = lambda width, idx: pl.BlockSpec((rows, width), lambda i: (i, idx))
    kv_buf = pltpu.VMEM((PIPE_SLOTS, ATT_ITEM, MEM_LEN, D_MODEL), F32)
    return pl.pallas_call(
        _att_state_kernel,
        grid=(m // rows,),
        in_specs=[col_spec(1024, 10), col_spec(1024, 11),
                  pl.BlockSpec(memory_space=pl.ANY), pl.BlockSpec(memory_space=pl.ANY)],
        out_specs=col_spec(D_MODEL, 0),
        out_shape=jax.ShapeDtypeStruct((m, D_MODEL), BF16),
        scratch_shapes=[kv_buf, kv_buf, pltpu.VMEM((rows, D_MODEL), F32), pltpu.VMEM((rows, D_MODEL), F32),
                        pltpu.SemaphoreType.DMA((PIPE_SLOTS, ATT_ITEM * 2 * ATT_HEADS))],
        compiler_params=pltpu.CompilerParams(dimension_semantics=("arbitrary",), vmem_limit_bytes=VMEM_LIMIT),
        name="att_state",
    )(main, main, k, v)


DENSE_ROWS = 256


def _dense_kernel(x_ref, gt_ref, yp_ref, ys_ref, ya_ref, wpo_ref, wso_ref, wao_ref, wo_ref, fnw_ref, y_ref):
    gates = gt_ref[...].astype(F32)
    merged = (gates[:, 0:D_MODEL] * _dot(yp_ref[...], wpo_ref[...])
              + gates[:, D_MODEL:2 * D_MODEL] * _dot(ys_ref[...], wso_ref[...])
              + gates[:, 2 * D_MODEL:] * _dot(ya_ref[...], wao_ref[...]))
    x_out = x_ref[...] + _dot(merged.astype(BF16), wo_ref[...])
    y_ref[...] = _rms(x_out, fnw_ref[...])


def _dense(x2d, gates, gate_idx, yp, ys, ya, wpo, wso, wao, wo, fnw):
    m = x2d.shape[0]
    row = lambda width, idx=0: pl.BlockSpec((DENSE_ROWS, width), lambda i: (i, idx))
    resident = lambda a: pl.BlockSpec(a.shape, lambda i: (0,) * a.ndim, pipeline_mode=pl.Buffered(1))
    return pl.pallas_call(
        _dense_kernel,
        grid=(m // DENSE_ROWS,),
        in_specs=[row(D_MODEL), row(3 * D_MODEL, gate_idx), row(D_MODEL), row(SSD_WIDTH), row(D_MODEL),
                  resident(wpo), resident(wso), resident(wao), resident(wo), resident(fnw)],
        out_specs=row(D_MODEL),
        out_shape=jax.ShapeDtypeStruct((m, D_MODEL), F32),
        compiler_params=pltpu.CompilerParams(dimension_semantics=("arbitrary",), vmem_limit_bytes=VMEM_LIMIT),
        name="dense",
    )(x2d, gates, yp, ys, ya, wpo, wso, wao, wo, fnw)


def kernel(x_prompt, x_sample, mem_prompt, state_pool, state_conv, state_ssm, cache_mem_k, cache_mem_v,
           norm_w, w_in, w_pool_grp, pool_scale, conv_w, conv_b, dt_bias, a_log, d_skip, ssd_norm_w,
           mem_norm_w, w_mem_k, w_mem_v, w_pool_out, w_ssd_out, w_att_out, w_out, final_norm_w):
    assert w_in.shape[0] == 1
    bp, sp, d = x_prompt.shape
    bs, ss, _ = x_sample.shape
    assert ss == SUBLANES and sp % PROMPT_TILE == 0 and bs % SAMPLE_BLOCK == 0

    w_u, w_zp, w_zs, w_xbc, w_dtc, w_q, w_za, w_gt = jnp.split(
        w_in[0], [1024, 2048, 4096, 7168, 7200, 8224, 9248], axis=1)
    w_main = jnp.concatenate([w_xbc, w_gt, w_zs, w_u, w_zp, w_q, w_za], axis=1).astype(BF16)
    w_dt = jnp.pad(w_dtc, ((0, 0), (0, LANES - SSD_HEADS))).astype(BF16)
    nw = norm_w[0].reshape(1, d)
    pad_heads = lambda a: jnp.pad(a.reshape(1, SSD_HEADS), ((0, 0), (0, LANES - SSD_HEADS)))
    wgrp = w_pool_grp[0].astype(BF16)
    pscale = pool_scale[0].reshape(1, d)
    convb = conv_b[0].reshape(1, CONV_DIM)
    dtb, alog = pad_heads(dt_bias[0]), pad_heads(a_log[0])
    dexp = jnp.repeat(d_skip[0], SSD_HEAD_DIM).reshape(1, SSD_WIDTH)
    ssdnw = ssd_norm_w[0].reshape(1, SSD_WIDTH)
    dense_w = (w_pool_out[0].astype(BF16), w_ssd_out[0].astype(BF16), w_att_out[0].astype(BF16),
               w_out[0].astype(BF16), final_norm_w.reshape(1, d))
    head_of_lane = jnp.arange(SSD_WIDTH) // SSD_HEAD_DIM
    expand = (jnp.arange(LANES)[:, None] == head_of_lane[None, :]).astype(BF16)
    group_of_head = jnp.where(jnp.arange(LANES) < SSD_HEADS, jnp.arange(LANES) // (SSD_HEADS // SSD_GROUPS), -1)
    segsum = ((jnp.arange(GROUP_WIDTH) // SSD_STATE)[:, None] == group_of_head[None, :]).astype(BF16)

    mk, mv, mkb, mvb = _memkv(mem_prompt, mem_norm_w[0].reshape(1, d), w_mem_k[0].astype(BF16),
                              w_mem_v[0].astype(BF16))
    xp2 = x_prompt.reshape(bp * sp, d)
    gates_p, yp, ysd, ya, pool_p, conv_p, ssm_p = _seq_prompt(
        xp2, nw, w_main, w_dt, mkb, mvb, (wgrp, pscale, conv_w[0], convb, dtb, alog, dexp, ssdnw),
        nseq=bp, ntile=sp // PROMPT_TILE)
    y_prompt = _dense(xp2, gates_p, 0, yp, ysd, ya, *dense_w).reshape(bp, sp, d)

    xs2 = x_sample.reshape(bs * ss, d)
    main_s, dt_s = _inproj(xs2, nw, w_main, w_dt)
    yp, xs, bc, dts, cd, pool_s, conv_s = _state_pre(
        main_s, dt_s, state_pool[0], state_conv[0], (wgrp, pscale, conv_w[0], convb, dtb, alog))
    ysd, ssm_s = _ssd_state(xs, bc, dts, main_s, cd, state_ssm[0].reshape(bs, SSD_WIDTH, SSD_STATE),
                            expand, segsum, alog, dexp, ssdnw)
    ya = _att_state(main_s, cache_mem_k[0], cache_mem_v[0])
    y_sample = _dense(xs2, main_s, COL_GATES[0] // (3 * D_MODEL), yp, ysd, ya, *dense_w).reshape(bs, ss, d)

    ssm_shape = (SSD_GROUPS, SSD_HEADS // SSD_GROUPS, SSD_HEAD_DIM, SSD_STATE)
    return (y_prompt, y_sample,
            pool_p[None], conv_p[None], ssm_p.reshape((1, bp) + ssm_shape),
            mk[None], mv[None],
            pool_s[None], conv_s[None], ssm_s.reshape((1, bs) + ssm_shape))
```

```python
import functools

import jax
import jax.numpy as jnp
from jax import lax
from jax.experimental import pallas as pl
from jax.experimental.pallas import tpu as pltpu

F32 = jnp.float32
BF16 = jnp.bfloat16

D_MODEL = 1024
POOL_WINDOWS = (2, 4, 8, 16)
POOL_GROUP = 256
POOL_HIST = 15
POOL_PAD = 16
SSD_WIDTH = 2048
SSD_HEADS = 32
SSD_HEAD_DIM = 64
SSD_GROUPS = 4
SSD_STATE = 128
GROUP_WIDTH = SSD_WIDTH // SSD_GROUPS
CONV_WIDTH = 4
CONV_DIM = 3072
CONV_PAD = 8
SSD_CHUNK = 128
MEM_LEN = 256
ATT_HEADS = 4
ATT_HEAD_DIM = 256
PAST_LEN = 16384
EPS = 1e-6
NEG_BIG = -1e30
SUBLANES = 8
LANES = 128
MAIN_COLS = 12288
COL_XBC, COL_GATES, COL_ZS = (0, 3072), (3072, 6144), (6144, 8192)
COL_U, COL_ZP, COL_Q, COL_ZA = (8192, 9216), (9216, 10240), (10240, 11264), (11264, 12288)
W_SPLIT = (7168, 7200)
W_SRC = {COL_U: (0, 0), COL_ZP: (0, 1024), COL_ZS: (0, 2048), COL_XBC: (0, 4096),
         COL_Q: (1, 0), COL_ZA: (1, 1024), COL_GATES: (1, 2048)}
VMEM_LIMIT = 56 * 1024 * 1024
SEQ_PROMPT_VMEM_LIMIT = 60 * 1024 * 1024

_NT = (((1,), (1,)), ((), ()))
_TN = (((0,), (0,)), ((), ()))


def _sigmoid(x):
    return 1.0 / (1.0 + jnp.exp(-x))


def _silu(x):
    return x * _sigmoid(x)


def _softplus(x):
    return jnp.maximum(x, 0.0) + jnp.log1p(jnp.exp(-jnp.abs(x)))


def _rms(x, w):
    return x * lax.rsqrt(jnp.mean(x * x, axis=-1, keepdims=True) + EPS) * w


def _dot(a, b):
    return jnp.dot(a, b, preferred_element_type=F32)


def _softmax_rows(sc):
    e = jnp.exp(sc - jnp.max(sc, axis=-1, keepdims=True))
    return e / jnp.sum(e, axis=-1, keepdims=True)


def _weight_cols(w_refs, piece, lo, hi):
    idx, c0 = W_SRC[piece]
    return w_refs[idx][:, c0 + lo:c0 + hi]


def _memkv_kernel(mem_ref, nw_ref, wk_ref, wv_ref, k_ref, v_ref, kb_ref, vb_ref):
    mh = _rms(mem_ref[0], nw_ref[...]).astype(BF16)
    k = _dot(mh, wk_ref[...])
    v = _dot(mh, wv_ref[...])
    for hd in range(ATT_HEADS):
        hsl = slice(hd * ATT_HEAD_DIM, (hd + 1) * ATT_HEAD_DIM)
        k_ref[0, :, hd, :] = k[:, hsl]
        v_ref[0, :, hd, :] = v[:, hsl]
        kb_ref[0, hd] = k[:, hsl].astype(BF16)
        vb_ref[0, hd] = v[:, hsl].astype(BF16)


def _memkv(mem, nw, wk, wv):
    b, m, d = mem.shape
    full = lambda shape: pl.BlockSpec(shape, lambda i: (0,) * len(shape))
    blk = pl.BlockSpec((1, m, d), lambda i: (i, 0, 0))
    oblk = pl.BlockSpec((1, m, ATT_HEADS, ATT_HEAD_DIM), lambda i: (i, 0, 0, 0))
    hblk = pl.BlockSpec((1, ATT_HEADS, m, ATT_HEAD_DIM), lambda i: (i, 0, 0, 0))
    return pl.pallas_call(
        _memkv_kernel,
        grid=(b,),
        in_specs=[blk, full((1, d)), full((d, d)), full((d, d))],
        out_specs=[oblk, oblk, hblk, hblk],
        out_shape=[jax.ShapeDtypeStruct((b, m, ATT_HEADS, ATT_HEAD_DIM), F32)] * 2
        + [jax.ShapeDtypeStruct((b, ATT_HEADS, m, ATT_HEAD_DIM), BF16)] * 2,
        compiler_params=pltpu.CompilerParams(dimension_semantics=("arbitrary",), vmem_limit_bytes=VMEM_LIMIT),
        name="memkv",
    )(mem, nw, wk, wv)


INPROJ_ROWS = 256
INPROJ_COL_CHUNK = 1024


def _inproj_kernel(x_ref, nw_ref, wa_ref, wb_ref, wdt_ref, main_ref, dt_ref):
    h = _rms(x_ref[...], nw_ref[...]).astype(BF16)
    dt_ref[...] = _dot(h, wdt_ref[...])
    for piece in W_SRC:
        for lo in range(0, piece[1] - piece[0], INPROJ_COL_CHUNK):
            val = _dot(h, _weight_cols((wa_ref, wb_ref), piece, lo, lo + INPROJ_COL_CHUNK))
            if piece == COL_GATES:
                val = _sigmoid(val)
            main_ref[:, piece[0] + lo:piece[0] + lo + INPROJ_COL_CHUNK] = val.astype(BF16)


def _inproj(x2d, nw, w_a, w_b, w_dt):
    m = x2d.shape[0]
    resident = lambda a: pl.BlockSpec(a.shape, lambda i: (0,) * a.ndim, pipeline_mode=pl.Buffered(1))
    return pl.pallas_call(
        _inproj_kernel,
        grid=(m // INPROJ_ROWS,),
        in_specs=[pl.BlockSpec((INPROJ_ROWS, D_MODEL), lambda i: (i, 0)),
                  resident(nw), resident(w_a), resident(w_b), resident(w_dt)],
        out_specs=[
            pl.BlockSpec((INPROJ_ROWS, MAIN_COLS), lambda i: (i, 0)),
            pl.BlockSpec((INPROJ_ROWS, LANES), lambda i: (i, 0)),
        ],
        out_shape=[jax.ShapeDtypeStruct((m, MAIN_COLS), BF16), jax.ShapeDtypeStruct((m, LANES), F32)],
        compiler_params=pltpu.CompilerParams(dimension_semantics=("arbitrary",), vmem_limit_bytes=VMEM_LIMIT),
        name="inproj",
    )(x2d, nw, w_a, w_b, w_dt)


def _pool_branch(pext, u, pos, wgrp_ref, pscale_ref, zp):
    nb, t, _ = u.shape
    ys = []
    for g, w in enumerate(POOL_WINDOWS):
        cols = slice(g * POOL_GROUP, (g + 1) * POOL_GROUP)
        win = pext[:, :, cols]
        for sh in [1 << e for e in range(g + 1)]:
            win = win + pltpu.roll(win, sh, axis=1)
        win = win[:, POOL_PAD:, :]
        cnt = jnp.minimum(w, pos + 1).astype(F32)
        d = (win / cnt - u[:, :, cols]).astype(BF16).reshape(nb * t, POOL_GROUP)
        ys.append(_dot(d, wgrp_ref[g]))
    return jnp.concatenate(ys, axis=1) * pscale_ref[...] * _silu(zp)


def _conv_branch(cext, convw_ref, convb_ref, store):
    for cc in range(CONV_DIM // GROUP_WIDTH):
        csl = slice(cc * GROUP_WIDTH, (cc + 1) * GROUP_WIDTH)
        ext = cext[:, :, csl]
        conv = convb_ref[:, csl].reshape(1, 1, GROUP_WIDTH)
        for kk in range(CONV_WIDTH):
            tap = ext if kk == CONV_WIDTH - 1 else pltpu.roll(ext, CONV_WIDTH - 1 - kk, axis=1)
            conv = conv + tap * convw_ref[kk:kk + 1, csl].reshape(1, 1, GROUP_WIDTH)
        store(cc, _silu(conv[:, CONV_PAD:, :]))


PROMPT_TILE = 256


def _seq_prompt_kernel(x_ref, nw_ref, wa_ref, wb_ref, wdt_ref, k_ref, v_ref,
                       wgrp_ref, pscale_ref, convw_ref, convb_ref, dtb_ref, alog_ref, dexp_ref, ssdnw_ref,
                       gates_ref, ypool_ref, yssd_ref, yatt_ref, pool_o_ref, conv_o_ref, ssm_o_ref,
                       pext, cext, xs_scr, b_scr, c_scr, dt_scr, y_scr, h_scr, zs_scr, q_scr, za_scr):
    t, q = PROMPT_TILE, SSD_CHUNK
    s = pl.program_id(1)
    last = pl.num_programs(1) - 1
    hn = _rms(x_ref[...], nw_ref[...]).astype(BF16)

    def proj(piece, lo=0, hi=None):
        return _dot(hn, _weight_cols((wa_ref, wb_ref), piece, lo, piece[1] - piece[0] if hi is None else hi))

    @pl.when(s == 0)
    def _():
        pext[:, 0:POOL_PAD, :] = jnp.zeros((1, POOL_PAD, D_MODEL), F32)
        cext[:, 0:CONV_PAD, :] = jnp.zeros((1, CONV_PAD, CONV_DIM), F32)
        h_scr[...] = jnp.zeros(h_scr.shape, F32)

    @pl.when(s > 0)
    def _():
        carry_p = pext[:, t:t + POOL_PAD, :]
        carry_c = cext[:, t:t + CONV_PAD, :]
        pext[:, 0:POOL_PAD, :] = carry_p
        cext[:, 0:CONV_PAD, :] = carry_c

    def gates_piece(c):
        def run():
            gates_ref[:, c * D_MODEL:(c + 1) * D_MODEL] = _sigmoid(
                proj(COL_GATES, c * D_MODEL, (c + 1) * D_MODEL).astype(BF16))
        return run

    def zs_piece(c):
        def run():
            zs_scr[:, c * D_MODEL:(c + 1) * D_MODEL] = _silu(proj(COL_ZS, c * D_MODEL, (c + 1) * D_MODEL).astype(BF16))
        return run

    def q_piece():
        q_scr[...] = proj(COL_Q).astype(BF16)

    def za_piece():
        za_scr[...] = _silu(proj(COL_ZA).astype(BF16))

    fillers = [gates_piece(0), gates_piece(1), gates_piece(2), zs_piece(0), zs_piece(1), q_piece, za_piece]

    def run_filler():
        if fillers:
            fillers.pop(0)()

    u = proj(COL_U).reshape(1, t, D_MODEL)
    pext[:, POOL_PAD:, :] = u
    pos = s * t + lax.broadcasted_iota(jnp.int32, (1, t, 1), 1)
    ypool_ref[...] = _pool_branch(pext, u, pos, wgrp_ref, pscale_ref, proj(COL_ZP)).astype(BF16)

    for c in range(CONV_DIM // D_MODEL):
        cext[:, CONV_PAD:, c * D_MODEL:(c + 1) * D_MODEL] = proj(
            COL_XBC, c * D_MODEL, (c + 1) * D_MODEL).reshape(1, t, D_MODEL)
    dt_scr[...] = _softplus(_dot(hn, wdt_ref[...]) + dtb_ref[...])

    def store_conv(cc, val):
        if cc < SSD_GROUPS:
            xs_scr[:, cc * GROUP_WIDTH:(cc + 1) * GROUP_WIDTH] = val[0]
        elif cc == SSD_GROUPS:
            b_scr[...] = val[0]
        else:
            c_scr[...] = val[0]
        run_filler()

    _conv_branch(cext, convw_ref, convb_ref, store_conv)

    a_neg = -jnp.exp(alog_ref[...])
    rq = lax.broadcasted_iota(jnp.int32, (q, q), 0)
    cq = lax.broadcasted_iota(jnp.int32, (q, q), 1)
    tril = rq >= cq
    tri_f = tril.astype(F32)
    lane_lo = lax.broadcasted_iota(jnp.int32, (1, LANES), 1) < SSD_HEAD_DIM
    pairs_per_group = SSD_HEADS // SSD_GROUPS // 2

    def chunk(c):
        rsl = slice(c * q, (c + 1) * q)
        dtc = dt_scr[rsl, :]
        acs = jnp.dot(tri_f, dtc * a_neg, precision=lax.Precision.HIGHEST, preferred_element_type=F32)
        acs_t = acs.T
        dt_t = dtc.T
        wdec_t = (dt_t * jnp.exp(acs_t[:, q - 1:q] - acs_t)).astype(BF16)
        row_t = acs_t - jnp.log(dt_t)
        cdec = jnp.exp(acs[q - 1:q, :])
        for g in range(SSD_GROUPS):
            gsl = slice(g * SSD_STATE, (g + 1) * SSD_STATE)
            bg = b_scr[rsl, gsl]
            cg_b = c_scr[rsl, gsl].astype(BF16)
            cb = lax.dot_general(cg_b, bg.astype(BF16), _NT, preferred_element_type=F32)
            bg_t = bg.T.astype(BF16)
            hsl = slice(g * GROUP_WIDTH, (g + 1) * GROUP_WIDTH)
            z_g = _dot(cg_b, h_scr[:, hsl].astype(BF16))
            for jp in range(pairs_per_group):
                j = g * pairs_per_group + jp
                lsl = slice(j * LANES, (j + 1) * LANES)
                xp = xs_scr[rsl, lsl]
                xp_b = xp.astype(BF16)
                zero_b = jnp.zeros_like(xp_b)
                x_bd = jnp.concatenate([jnp.where(lane_lo, xp_b, zero_b), jnp.where(lane_lo, zero_b, xp_b)],
                                       axis=0)
                ms, bws, cols = [], [], []
                for hh in range(2):
                    r = 2 * j + hh
                    cols.append(jnp.broadcast_to(acs[:, r:r + 1], (q, q)))
                    seg = cols[hh] - row_t[r:r + 1, :]
                    ms.append(cb * jnp.exp(jnp.where(tril, seg, NEG_BIG)))
                    bws.append(bg_t * wdec_t[r:r + 1, :])
                ea_pair = jnp.exp(jnp.where(lane_lo, cols[0], cols[1]))
                y = (_dot(jnp.concatenate(ms, axis=1).astype(BF16), x_bd)
                     + ea_pair * z_g[:, jp * LANES:(jp + 1) * LANES] + dexp_ref[:, lsl] * xp)
                y_scr[rsl, lsl] = y
                cd_pair = jnp.where(lane_lo, cdec[:, 2 * j:2 * j + 1], cdec[:, 2 * j + 1:2 * j + 2])
                h_scr[:, lsl] = h_scr[:, lsl] * cd_pair + _dot(jnp.concatenate(bws, axis=1), x_bd)
            run_filler()

    for c in range(t // q):
        chunk(c)
    while fillers:
        run_filler()

    yz = y_scr[...] * zs_scr[...].astype(F32)
    yssd_ref[...] = _rms(yz, ssdnw_ref[...]).astype(BF16)

    scale = ATT_HEAD_DIM ** -0.5
    outs = []
    for hd in range(ATT_HEADS):
        hsl = slice(hd * ATT_HEAD_DIM, (hd + 1) * ATT_HEAD_DIM)
        p = _softmax_rows(lax.dot_general(q_scr[:, hsl], k_ref[0, hd], _NT, preferred_element_type=F32) * scale)
        outs.append(_dot(p.astype(BF16), v_ref[0, hd]))
    yatt_ref[...] = (jnp.concatenate(outs, axis=1) * za_scr[...].astype(F32)).astype(BF16)

    @pl.when(s == last)
    def _():
        pool_o_ref[...] = pext[:, t + POOL_PAD - POOL_HIST:t + POOL_PAD, :]
        conv_o_ref[...] = cext[:, t + CONV_PAD - (CONV_WIDTH - 1):t + CONV_PAD, :]
        for j in range(SSD_HEADS // 2):
            lsl = slice(j * LANES, (j + 1) * LANES)
            ssm_o_ref[0, lsl, :] = h_scr[:, lsl].T


def _seq_prompt(x2d, nw, w_a, w_b, w_dt, kb, vb, params, *, nseq, ntile):
    t = PROMPT_TILE
    m = x2d.shape[0]
    rowblk = lambda i, s: i * ntile + s
    row_spec = lambda width: pl.BlockSpec((t, width), lambda i, s: (rowblk(i, s), 0))
    seq_spec = lambda shape: pl.BlockSpec((1,) + shape, lambda i, s: (i,) + (0,) * len(shape))
    const_spec = lambda a: pl.BlockSpec(a.shape, lambda i, s: (0,) * a.ndim)
    resident = lambda a: pl.BlockSpec(a.shape, lambda i, s: (0,) * a.ndim, pipeline_mode=pl.Buffered(1))
    in_specs = [
        row_spec(D_MODEL), resident(nw), resident(w_a), resident(w_b), resident(w_dt),
        seq_spec((ATT_HEADS, MEM_LEN, ATT_HEAD_DIM)),
        seq_spec((ATT_HEADS, MEM_LEN, ATT_HEAD_DIM)),
    ] + [const_spec(p) for p in params]
    out_specs = [
        row_spec(3 * D_MODEL), row_spec(D_MODEL), row_spec(SSD_WIDTH), row_spec(D_MODEL),
        seq_spec((POOL_HIST, D_MODEL)),
        seq_spec((CONV_WIDTH - 1, CONV_DIM)),
        seq_spec((SSD_WIDTH, SSD_STATE)),
    ]
    out_shape = [
        jax.ShapeDtypeStruct((m, 3 * D_MODEL), BF16),
        jax.ShapeDtypeStruct((m, D_MODEL), BF16),
        jax.ShapeDtypeStruct((m, SSD_WIDTH), BF16),
        jax.ShapeDtypeStruct((m, D_MODEL), BF16),
        jax.ShapeDtypeStruct((nseq, POOL_HIST, D_MODEL), F32),
        jax.ShapeDtypeStruct((nseq, CONV_WIDTH - 1, CONV_DIM), F32),
        jax.ShapeDtypeStruct((nseq, SSD_WIDTH, SSD_STATE), F32),
    ]
    scratch = [
        pltpu.VMEM((1, POOL_PAD + t, D_MODEL), F32),
        pltpu.VMEM((1, CONV_PAD + t, CONV_DIM), F32),
        pltpu.VMEM((t, SSD_WIDTH), F32),
        pltpu.VMEM((t, GROUP_WIDTH), F32),
        pltpu.VMEM((t, GROUP_WIDTH), F32),
        pltpu.VMEM((t, LANES), F32),
        pltpu.VMEM((t, SSD_WIDTH), F32),
        pltpu.VMEM((SSD_STATE, SSD_WIDTH), F32),
        pltpu.VMEM((t, SSD_WIDTH), BF16),
        pltpu.VMEM((t, D_MODEL), BF16),
        pltpu.VMEM((t, D_MODEL), BF16),
    ]
    return pl.pallas_call(
        _seq_prompt_kernel,
        grid=(nseq, ntile),
        in_specs=in_specs,
        out_specs=out_specs,
        out_shape=out_shape,
        scratch_shapes=scratch,
        compiler_params=pltpu.CompilerParams(dimension_semantics=("arbitrary", "arbitrary"),
                                             vmem_limit_bytes=SEQ_PROMPT_VMEM_LIMIT),
        name="seq_prompt",
    )(x2d, nw, w_a, w_b, w_dt, kb, vb, *params)


SAMPLE_BLOCK = 32


def _state_pre_kernel(u_ref, zp_ref, xbc_ref, dt_ref, ph_ref, ch_ref,
                      wgrp_ref, pscale_ref, convw_ref, convb_ref, dtb_ref, alog_ref,
                      ypool_ref, xs_ref, bc_ref, dts_ref, cd_ref, pool_o_ref, conv_o_ref, pext, cext):
    nb, t = SAMPLE_BLOCK, SUBLANES
    pext[:, 0:1, :] = jnp.zeros((nb, 1, D_MODEL), F32)
    pext[:, 1:POOL_PAD, :] = ph_ref[...]
    cext[:, 0:CONV_PAD - (CONV_WIDTH - 1), :] = jnp.zeros((nb, CONV_PAD - (CONV_WIDTH - 1), CONV_DIM), F32)
    cext[:, CONV_PAD - (CONV_WIDTH - 1):CONV_PAD, :] = ch_ref[...]

    u = u_ref[...].astype(F32).reshape(nb, t, D_MODEL)
    pext[:, POOL_PAD:, :] = u
    pos = PAST_LEN + lax.broadcasted_iota(jnp.int32, (1, t, 1), 1)
    ypool_ref[...] = _pool_branch(pext, u, pos, wgrp_ref, pscale_ref, zp_ref[...].astype(F32)).astype(BF16)

    cext[:, CONV_PAD:, :] = xbc_ref[...].astype(F32).reshape(nb, t, CONV_DIM)

    def store_conv(cc, val):
        val = val.reshape(nb * t, GROUP_WIDTH)
        if cc < SSD_GROUPS:
            xs_ref[:, cc * GROUP_WIDTH:(cc + 1) * GROUP_WIDTH] = val
        else:
            bc_ref[:, (cc - SSD_GROUPS) * GROUP_WIDTH:(cc - SSD_GROUPS + 1) * GROUP_WIDTH] = val

    _conv_branch(cext, convw_ref, convb_ref, store_conv)
    dt = _softplus(dt_ref[...] + dtb_ref[...])
    dts_ref[...] = dt
    a = dt * -jnp.exp(alog_ref[...])
    cd_ref[...] = jnp.exp(jnp.sum(a.reshape(nb, t, LANES), axis=1))
    pool_o_ref[...] = pext[:, t + POOL_PAD - POOL_HIST:t + POOL_PAD, :]
    conv_o_ref[...] = cext[:, t + CONV_PAD - (CONV_WIDTH - 1):t + CONV_PAD, :]


def _col_block(piece):
    return piece[0] // (piece[1] - piece[0])


def _state_pre(main, dt, state_pool, state_conv, params):
    nb, t = SAMPLE_BLOCK, SUBLANES
    rows = nb * t
    m = main.shape[0]
    nseq = m // t
    col_spec = lambda width, idx: pl.BlockSpec((rows, width), lambda i: (i, idx))
    seq_spec = lambda shape: pl.BlockSpec((nb,) + shape, lambda i: (i,) + (0,) * len(shape))
    const_spec = lambda a: pl.BlockSpec(a.shape, lambda i: (0,) * a.ndim)
    return pl.pallas_call(
        _state_pre_kernel,
        grid=(nseq // nb,),
        in_specs=[col_spec(D_MODEL, _col_block(COL_U)), col_spec(D_MODEL, _col_block(COL_ZP)),
                  col_spec(CONV_DIM, _col_block(COL_XBC)), col_spec(LANES, 0),
                  seq_spec((POOL_HIST, D_MODEL)), seq_spec((CONV_WIDTH - 1, CONV_DIM))]
        + [const_spec(p) for p in params],
        out_specs=[col_spec(D_MODEL, 0), col_spec(SSD_WIDTH, 0), col_spec(2 * GROUP_WIDTH, 0), col_spec(LANES, 0),
                   pl.BlockSpec((nb, LANES), lambda i: (i, 0)),
                   seq_spec((POOL_HIST, D_MODEL)), seq_spec((CONV_WIDTH - 1, CONV_DIM))],
        out_shape=[jax.ShapeDtypeStruct((m, D_MODEL), BF16), jax.ShapeDtypeStruct((m, SSD_WIDTH), F32),
                   jax.ShapeDtypeStruct((m, 2 * GROUP_WIDTH), F32), jax.ShapeDtypeStruct((m, LANES), F32),
                   jax.ShapeDtypeStruct((nseq, LANES), F32),
                   jax.ShapeDtypeStruct((nseq, POOL_HIST, D_MODEL), F32),
                   jax.ShapeDtypeStruct((nseq, CONV_WIDTH - 1, CONV_DIM), F32)],
        scratch_shapes=[pltpu.VMEM((nb, POOL_PAD + t, D_MODEL), F32), pltpu.VMEM((nb, CONV_PAD + t, CONV_DIM), F32)],
        compiler_params=pltpu.CompilerParams(dimension_semantics=("arbitrary",), vmem_limit_bytes=VMEM_LIMIT),
        name="state_pre",
    )(main, main, main, dt, state_pool, state_conv, *params)


PIPE_SLOTS = 4


def _ring_pipeline(n, start_in, wait_in, compute, start_out=None, wait_out=None):
    ns = PIPE_SLOTS
    for i in range(ns - 1):
        start_in(i, i)

    def body(bb, _):
        for k in range(ns):
            i = ns * bb + k
            nxt = i + ns - 1
            pl.when(nxt < n)(functools.partial(start_in, nxt, (k + ns - 1) % ns))
            wait_in(i, k)
            if wait_out is not None:
                pl.when(bb > 0)(functools.partial(wait_out, i - ns, k))
            compute(i, k)
            if start_out is not None:
                start_out(i, k)
        return 0

    lax.fori_loop(0, n // ns, body, 0)
    if wait_out is not None:
        for k in range(ns):
            wait_out(n - ns + k, k)


def _ssd_state_kernel(xs_ref, bc_ref, dt_ref, zs_ref, cd_ref, hin_hbm, expand_ref, segsum_ref, alog_ref, dexp_ref,
                      ssdnw_ref, yssd_ref, hout_hbm, hbuf, obuf, y_scr, sem_in, sem_out):
    nb, t = SAMPLE_BLOCK, SUBLANES
    base = pl.program_id(0) * nb
    a_neg = -jnp.exp(alog_ref[...])
    ridx = lax.broadcasted_iota(jnp.int32, (t, LANES), 0)

    def in_copies(b, slot):
        return [pltpu.make_async_copy(hin_hbm.at[base + b, g * GROUP_WIDTH:(g + 1) * GROUP_WIDTH, :],
                                      hbuf.at[slot, g * GROUP_WIDTH:(g + 1) * GROUP_WIDTH, :], sem_in.at[slot, g])
                for g in range(SSD_GROUPS)]

    def out_copies(b, slot):
        return [pltpu.make_async_copy(obuf.at[slot, g * GROUP_WIDTH:(g + 1) * GROUP_WIDTH, :],
                                      hout_hbm.at[base + b, g * GROUP_WIDTH:(g + 1) * GROUP_WIDTH, :],
                                      sem_out.at[slot, g])
                for g in range(SSD_GROUPS)]

    def start_all(copies):
        for c in copies:
            c.start()

    def wait_all(copies):
        for c in copies:
            c.wait()

    def compute(b, slot):
        rsl = pl.ds(pl.multiple_of(b * t, t), t)
        dtc = dt_ref[rsl, :]
        acs = dtc * a_neg
        for sh in (1, 2, 4):
            acs = acs + jnp.where(ridx >= sh, pltpu.roll(acs, sh, axis=0), 0.0)
        tot = acs[t - 1:t, :]
        x = xs_ref[rsl, :]
        bc = bc_ref[rsl, :]
        bm, cm = bc[:, :GROUP_WIDTH], bc[:, GROUP_WIDTH:]
        bm_r, cm_r = bm.astype(BF16).astype(F32), cm.astype(BF16).astype(F32)
        gs, ps = [], []
        for k in range(t):
            gs.append(jnp.exp(jnp.where(ridx >= k, acs - acs[k:k + 1, :], NEG_BIG)) * dtc[k:k + 1, :])
            ps.append(cm_r * bm_r[k:k + 1, :])
        cb_heads = _dot(jnp.concatenate(ps, axis=0).astype(BF16), segsum_ref[...])
        per_head = jnp.concatenate([jnp.concatenate(gs, axis=0) * cb_heads, jnp.exp(acs),
                                    dtc * jnp.exp(tot - acs)], axis=0)
        hi = per_head.astype(BF16)
        lo = (per_head - hi.astype(F32)).astype(BF16)
        wide = _dot(hi, expand_ref[...]) + _dot(lo, expand_ref[...])
        y = dexp_ref[...] * x
        for k in range(t):
            y = y + wide[k * t:(k + 1) * t, :] * x[k:k + 1, :]
        ea_wide = wide[t * t:t * t + t, :]
        xw = x * wide[t * t + t:, :]
        for g in range(SSD_GROUPS):
            gsl = slice(g * SSD_STATE, (g + 1) * SSD_STATE)
            wsl = slice(g * GROUP_WIDTH, (g + 1) * GROUP_WIDTH)
            hg = hbuf[slot, wsl, :]
            z_g = lax.dot_general(cm[:, gsl].astype(BF16), hg.astype(BF16), _NT, preferred_element_type=F32)
            y_scr[rsl, wsl] = y[:, wsl] + ea_wide[:, wsl] * z_g
            upd = lax.dot_general(xw[:, wsl].astype(BF16), bm[:, gsl].astype(BF16), _TN, preferred_element_type=F32)
            for r8 in range(SSD_HEADS // SSD_GROUPS):
                r = g * (SSD_HEADS // SSD_GROUPS) + r8
                rows_r = slice(r * SSD_HEAD_DIM, (r + 1) * SSD_HEAD_DIM)
                obuf[slot, rows_r, :] = (hbuf[slot, rows_r, :] * cd_ref[base + b, r]
                                         + upd[r8 * SSD_HEAD_DIM:(r8 + 1) * SSD_HEAD_DIM, :])

    _ring_pipeline(nb,
                   lambda b, slot: start_all(in_copies(b, slot)), lambda b, slot: wait_all(in_copies(b, slot)),
                   compute,
                   lambda b, slot: start_all(out_copies(b, slot)), lambda b, slot: wait_all(out_copies(b, slot)))
    yz = y_scr[...] * _silu(zs_ref[...].astype(F32))
    yssd_ref[...] = _rms(yz, ssdnw_ref[...]).astype(BF16)


def _ssd_state(xs, bc, dts, main, cd, hin, expand, segsum, alog, dexp, ssdnw):
    nb, t = SAMPLE_BLOCK, SUBLANES
    rows = nb * t
    m = xs.shape[0]
    nseq = m // t
    col_spec = lambda width, idx: pl.BlockSpec((rows, width), lambda i: (i, idx))
    const_spec = lambda a: pl.BlockSpec(a.shape, lambda i: (0,) * a.ndim)
    return pl.pallas_call(
        _ssd_state_kernel,
        grid=(nseq // nb,),
        in_specs=[col_spec(SSD_WIDTH, 0), col_spec(2 * GROUP_WIDTH, 0), col_spec(LANES, 0),
                  col_spec(SSD_WIDTH, _col_block(COL_ZS)),
                  pl.BlockSpec(memory_space=pltpu.SMEM), pl.BlockSpec(memory_space=pl.ANY),
                  const_spec(expand), const_spec(segsum), const_spec(alog), const_spec(dexp), const_spec(ssdnw)],
        out_specs=[col_spec(SSD_WIDTH, 0), pl.BlockSpec(memory_space=pl.ANY)],
        out_shape=[jax.ShapeDtypeStruct((m, SSD_WIDTH), BF16), jax.ShapeDtypeStruct(hin.shape, F32)],
        scratch_shapes=[pltpu.VMEM((PIPE_SLOTS, SSD_WIDTH, SSD_STATE), F32),
                        pltpu.VMEM((PIPE_SLOTS, SSD_WIDTH, SSD_STATE), F32),
                        pltpu.VMEM((rows, SSD_WIDTH), F32),
                        pltpu.SemaphoreType.DMA((PIPE_SLOTS, SSD_GROUPS)),
                        pltpu.SemaphoreType.DMA((PIPE_SLOTS, SSD_GROUPS))],
        compiler_params=pltpu.CompilerParams(dimension_semantics=("arbitrary",), vmem_limit_bytes=VMEM_LIMIT),
        name="ssd_state",
    )(xs, bc, dts, main, cd, hin, expand, segsum, alog, dexp, ssdnw)


ATT_ITEM = 2


def _att_state_kernel(q_ref, za_ref, k_hbm, v_hbm, yatt_ref, kbuf, vbuf, q_scr, att_scr, sem):
    nb, t = SAMPLE_BLOCK, SUBLANES
    base = pl.program_id(0) * nb
    scale = ATT_HEAD_DIM ** -0.5
    q_scr[...] = q_ref[...].astype(F32)
    head_of_lane = lax.broadcasted_iota(jnp.int32, (1, D_MODEL), 1) // ATT_HEAD_DIM

    def copies(item, slot):
        out = []
        for j in range(ATT_ITEM):
            for kv, (src, buf) in enumerate(((k_hbm, kbuf), (v_hbm, vbuf))):
                for hd in range(ATT_HEADS):
                    hsl = slice(hd * ATT_HEAD_DIM, (hd + 1) * ATT_HEAD_DIM)
                    out.append(pltpu.make_async_copy(
                        src.at[base + item * ATT_ITEM + j, :, hd, :], buf.at[slot, j, :, hsl],
                        sem.at[slot, (j * 2 + kv) * ATT_HEADS + hd]))
        return out

    def start_in(item, slot):
        for c in copies(item, slot):
            c.start()

    def wait_in(item, slot):
        for c in copies(item, slot):
            c.wait()

    def compute(item, slot):
        for j in range(ATT_ITEM):
            rsl = pl.ds(pl.multiple_of((item * ATT_ITEM + j) * t, t), t)
            qf = q_scr[rsl, :]
            q_bd = jnp.concatenate([jnp.where(head_of_lane == hd, qf, 0.0) for hd in range(ATT_HEADS)],
                                   axis=0).astype(BF16)
            sc = lax.dot_general(q_bd, kbuf[slot, j].astype(BF16), _NT, preferred_element_type=F32) * scale
            o = _dot(_softmax_rows(sc).astype(BF16), vbuf[slot, j].astype(BF16))
            att_scr[rsl, :] = jnp.concatenate(
                [o[hd * t:(hd + 1) * t, hd * ATT_HEAD_DIM:(hd + 1) * ATT_HEAD_DIM] for hd in range(ATT_HEADS)], axis=1)

    _ring_pipeline(nb // ATT_ITEM, start_in, wait_in, compute)
    yatt_ref[...] = (att_scr[...] * _silu(za_ref[...].astype(F32))).astype(BF16)


def _att_state(main, k, v):
    nb, t = SAMPLE_BLOCK, SUBLANES
    rows = nb * t
    m = main.shape[0]
    col_spec = lambda width, idx: pl.BlockSpec((rows, width), lambda i: (i, idx))
    kv_buf = pltpu.VMEM((PIPE_SLOTS, ATT_ITEM, MEM_LEN, D_MODEL), F32)
    return pl.pallas_call(
        _att_state_kernel,
        grid=(m // rows,),
        in_specs=[col_spec(D_MODEL, _col_block(COL_Q)), col_spec(D_MODEL, _col_block(COL_ZA)),
                  pl.BlockSpec(memory_space=pl.ANY), pl.BlockSpec(memory_space=pl.ANY)],
        out_specs=col_spec(D_MODEL, 0),
        out_shape=jax.ShapeDtypeStruct((m, D_MODEL), BF16),
        scratch_shapes=[kv_buf, kv_buf, pltpu.VMEM((rows, D_MODEL), F32), pltpu.VMEM((rows, D_MODEL), F32),
                        pltpu.SemaphoreType.DMA((PIPE_SLOTS, ATT_ITEM * 2 * ATT_HEADS))],
        compiler_params=pltpu.CompilerParams(dimension_semantics=("arbitrary",), vmem_limit_bytes=VMEM_LIMIT),
        name="att_state",
    )(main, main, k, v)


DENSE_ROWS = 512


def _dense_kernel(x_ref, gt_ref, yp_ref, ys_ref, ya_ref, wpo_ref, wso_ref, wao_ref, wo_ref, fnw_ref, y_ref):
    gates = gt_ref[...].astype(F32)
    merged = (gates[:, 0:D_MODEL] * _dot(yp_ref[...], wpo_ref[...])
              + gates[:, D_MODEL:2 * D_MODEL] * _dot(ys_ref[...], wso_ref[...])
              + gates[:, 2 * D_MODEL:] * _dot(ya_ref[...], wao_ref[...]))
    x_out = x_ref[...] + _dot(merged.astype(BF16), wo_ref[...])
    y_ref[...] = _rms(x_out, fnw_ref[...])


def _dense(x2d, gates, gate_idx, yp, ys, ya, wpo, wso, wao, wo, fnw):
    m = x2d.shape[0]
    row = lambda width, idx=0: pl.BlockSpec((DENSE_ROWS, width), lambda i: (i, idx))
    resident = lambda a: pl.BlockSpec(a.shape, lambda i: (0,) * a.ndim, pipeline_mode=pl.Buffered(1))
    return pl.pallas_call(
        _dense_kernel,
        grid=(m // DENSE_ROWS,),
        in_specs=[row(D_MODEL), row(3 * D_MODEL, gate_idx), row(D_MODEL), row(SSD_WIDTH), row(D_MODEL),
                  resident(wpo), resident(wso), resident(wao), resident(wo), resident(fnw)],
        out_specs=row(D_MODEL),
        out_shape=jax.ShapeDtypeStruct((m, D_MODEL), F32),
        compiler_params=pltpu.CompilerParams(dimension_semantics=("arbitrary",), vmem_limit_bytes=VMEM_LIMIT),
        name="dense",
    )(x2d, gates, yp, ys, ya, wpo, wso, wao, wo, fnw)


def kernel(x_prompt, x_sample, mem_prompt, state_pool, state_conv, state_ssm, cache_mem_k, cache_mem_v,
           norm_w, w_in, w_pool_grp, pool_scale, conv_w, conv_b, dt_bias, a_log, d_skip, ssd_norm_w,
           mem_norm_w, w_mem_k, w_mem_v, w_pool_out, w_ssd_out, w_att_out, w_out, final_norm_w):
    assert w_in.shape[0] == 1
    bp, sp, d = x_prompt.shape
    bs, ss, _ = x_sample.shape
    assert ss == SUBLANES and sp % PROMPT_TILE == 0 and bs % SAMPLE_BLOCK == 0

    w_a = w_in[0][:, :W_SPLIT[0]].astype(BF16)
    w_b = w_in[0][:, W_SPLIT[1]:].astype(BF16)
    w_dt = jnp.pad(w_in[0][:, W_SPLIT[0]:W_SPLIT[1]], ((0, 0), (0, LANES - SSD_HEADS))).astype(BF16)
    nw = norm_w[0].reshape(1, d)
    pad_heads = lambda a: jnp.pad(a.reshape(1, SSD_HEADS), ((0, 0), (0, LANES - SSD_HEADS)))
    wgrp = w_pool_grp[0].astype(BF16)
    pscale = pool_scale[0].reshape(1, d)
    convb = conv_b[0].reshape(1, CONV_DIM)
    dtb, alog = pad_heads(dt_bias[0]), pad_heads(a_log[0])
    dexp = jnp.repeat(d_skip[0], SSD_HEAD_DIM).reshape(1, SSD_WIDTH)
    ssdnw = ssd_norm_w[0].reshape(1, SSD_WIDTH)
    dense_w = (w_pool_out[0].astype(BF16), w_ssd_out[0].astype(BF16), w_att_out[0].astype(BF16),
               w_out[0].astype(BF16), final_norm_w.reshape(1, d))
    head_of_lane = jnp.arange(SSD_WIDTH) // SSD_HEAD_DIM
    expand = (jnp.arange(LANES)[:, None] == head_of_lane[None, :]).astype(BF16)
    group_of_head = jnp.where(jnp.arange(LANES) < SSD_HEADS, jnp.arange(LANES) // (SSD_HEADS // SSD_GROUPS), -1)
    segsum = ((jnp.arange(GROUP_WIDTH) // SSD_STATE)[:, None] == group_of_head[None, :]).astype(BF16)

    mk, mv, mkb, mvb = _memkv(mem_prompt, mem_norm_w[0].reshape(1, d), w_mem_k[0].astype(BF16),
                              w_mem_v[0].astype(BF16))
    xp2 = x_prompt.reshape(bp * sp, d)
    gates_p, yp, ysd, ya, pool_p, conv_p, ssm_p = _seq_prompt(
        xp2, nw, w_a, w_b, w_dt, mkb, mvb, (wgrp, pscale, conv_w[0], convb, dtb, alog, dexp, ssdnw),
        nseq=bp, ntile=sp // PROMPT_TILE)
    y_prompt = _dense(xp2, gates_p, 0, yp, ysd, ya, *dense_w).reshape(bp, sp, d)

    xs2 = x_sample.reshape(bs * ss, d)
    main_s, dt_s = _inproj(xs2, nw, w_a, w_b, w_dt)
    yp, xs, bc, dts, cd, pool_s, conv_s = _state_pre(
        main_s, dt_s, state_pool[0], state_conv[0], (wgrp, pscale, conv_w[0], convb, dtb, alog))
    ysd, ssm_s = _ssd_state(xs, bc, dts, main_s, cd, state_ssm[0].reshape(bs, SSD_WIDTH, SSD_STATE),
                            expand, segsum, alog, dexp, ssdnw)
    ya = _att_state(main_s, cache_mem_k[0], cache_mem_v[0])
    y_sample = _dense(xs2, main_s, _col_block(COL_GATES), yp, ysd, ya, *dense_w).reshape(bs, ss, d)

    ssm_shape = (SSD_GROUPS, SSD_HEADS // SSD_GROUPS, SSD_HEAD_DIM, SSD_STATE)
    return (y_prompt, y_sample,
            pool_p[None], conv_p[None], ssm_p.reshape((1, bp) + ssm_shape),
            mk[None], mv[None],
            pool_s[None], conv_s[None], ssm_s.reshape((1, bs) + ssm_shape))
```

```python
import functools

import jax
import jax.numpy as jnp
from jax import lax
from jax.experimental import pallas as pl
from jax.experimental.pallas import tpu as pltpu

F32 = jnp.float32
BF16 = jnp.bfloat16

D_MODEL = 1024
POOL_WINDOWS = (2, 4, 8, 16)
POOL_GROUP = 256
POOL_HIST = 15
POOL_PAD = 16
SSD_WIDTH = 2048
SSD_HEADS = 32
SSD_HEAD_DIM = 64
SSD_GROUPS = 4
SSD_STATE = 128
GROUP_WIDTH = SSD_WIDTH // SSD_GROUPS
CONV_WIDTH = 4
CONV_DIM = 3072
CONV_PAD = 8
SSD_CHUNK = 128
MEM_LEN = 256
ATT_HEADS = 4
ATT_HEAD_DIM = 256
PAST_LEN = 16384
EPS = 1e-6
NEG_BIG = -1e30
SUBLANES = 8
LANES = 128
MAIN_COLS = 12288
COL_XBC, COL_GATES, COL_ZS = (0, 3072), (3072, 6144), (6144, 8192)
COL_U, COL_ZP, COL_Q, COL_ZA = (8192, 9216), (9216, 10240), (10240, 11264), (11264, 12288)
W_SPLIT = (7168, 7200)
W_SRC = {COL_U: (0, 0), COL_ZP: (0, 1024), COL_ZS: (0, 2048), COL_XBC: (0, 4096),
         COL_Q: (1, 0), COL_ZA: (1, 1024), COL_GATES: (1, 2048)}
VMEM_LIMIT = 56 * 1024 * 1024
SEQ_PROMPT_VMEM_LIMIT = 60 * 1024 * 1024

_NT = (((1,), (1,)), ((), ()))
_TN = (((0,), (0,)), ((), ()))


def _sigmoid(x):
    return 1.0 / (1.0 + jnp.exp(-x))


def _silu(x):
    return x * _sigmoid(x)


def _softplus(x):
    return jnp.maximum(x, 0.0) + jnp.log1p(jnp.exp(-jnp.abs(x)))


def _rms(x, w):
    return x * lax.rsqrt(jnp.mean(x * x, axis=-1, keepdims=True) + EPS) * w


def _dot(a, b):
    return jnp.dot(a, b, preferred_element_type=F32)


def _softmax_rows(sc):
    e = jnp.exp(sc - jnp.max(sc, axis=-1, keepdims=True))
    return e / jnp.sum(e, axis=-1, keepdims=True)


def _weight_cols(w_refs, piece, lo, hi):
    idx, c0 = W_SRC[piece]
    return w_refs[idx][:, c0 + lo:c0 + hi]


def _memkv_kernel(mem_ref, nw_ref, wk_ref, wv_ref, k_ref, v_ref, kb_ref, vb_ref):
    mh = _rms(mem_ref[0], nw_ref[...]).astype(BF16)
    k = _dot(mh, wk_ref[...])
    v = _dot(mh, wv_ref[...])
    for hd in range(ATT_HEADS):
        hsl = slice(hd * ATT_HEAD_DIM, (hd + 1) * ATT_HEAD_DIM)
        k_ref[0, :, hd, :] = k[:, hsl]
        v_ref[0, :, hd, :] = v[:, hsl]
        kb_ref[0, hd] = k[:, hsl].astype(BF16)
        vb_ref[0, hd] = v[:, hsl].astype(BF16)


def _memkv(mem, nw, wk, wv):
    b, m, d = mem.shape
    full = lambda shape: pl.BlockSpec(shape, lambda i: (0,) * len(shape))
    blk = pl.BlockSpec((1, m, d), lambda i: (i, 0, 0))
    oblk = pl.BlockSpec((1, m, ATT_HEADS, ATT_HEAD_DIM), lambda i: (i, 0, 0, 0))
    hblk = pl.BlockSpec((1, ATT_HEADS, m, ATT_HEAD_DIM), lambda i: (i, 0, 0, 0))
    return pl.pallas_call(
        _memkv_kernel,
        grid=(b,),
        in_specs=[blk, full((1, d)), full((d, d)), full((d, d))],
        out_specs=[oblk, oblk, hblk, hblk],
        out_shape=[jax.ShapeDtypeStruct((b, m, ATT_HEADS, ATT_HEAD_DIM), F32)] * 2
        + [jax.ShapeDtypeStruct((b, ATT_HEADS, m, ATT_HEAD_DIM), BF16)] * 2,
        compiler_params=pltpu.CompilerParams(dimension_semantics=("arbitrary",), vmem_limit_bytes=VMEM_LIMIT),
        name="memkv",
    )(mem, nw, wk, wv)


INPROJ_ROWS = 256
INPROJ_COL_CHUNK = 1024


def _inproj_kernel(x_ref, nw_ref, wa_ref, wb_ref, wdt_ref, main_ref, dt_ref):
    h = _rms(x_ref[...], nw_ref[...]).astype(BF16)
    dt_ref[...] = _dot(h, wdt_ref[...])
    for piece in W_SRC:
        for lo in range(0, piece[1] - piece[0], INPROJ_COL_CHUNK):
            val = _dot(h, _weight_cols((wa_ref, wb_ref), piece, lo, lo + INPROJ_COL_CHUNK))
            if piece == COL_GATES:
                val = _sigmoid(val)
            main_ref[:, piece[0] + lo:piece[0] + lo + INPROJ_COL_CHUNK] = val.astype(BF16)


def _inproj(x2d, nw, w_a, w_b, w_dt):
    m = x2d.shape[0]
    resident = lambda a: pl.BlockSpec(a.shape, lambda i: (0,) * a.ndim, pipeline_mode=pl.Buffered(1))
    return pl.pallas_call(
        _inproj_kernel,
        grid=(m // INPROJ_ROWS,),
        in_specs=[pl.BlockSpec((INPROJ_ROWS, D_MODEL), lambda i: (i, 0)),
                  resident(nw), resident(w_a), resident(w_b), resident(w_dt)],
        out_specs=[
            pl.BlockSpec((INPROJ_ROWS, MAIN_COLS), lambda i: (i, 0)),
            pl.BlockSpec((INPROJ_ROWS, LANES), lambda i: (i, 0)),
        ],
        out_shape=[jax.ShapeDtypeStruct((m, MAIN_COLS), BF16), jax.ShapeDtypeStruct((m, LANES), F32)],
        compiler_params=pltpu.CompilerParams(dimension_semantics=("arbitrary",), vmem_limit_bytes=VMEM_LIMIT),
        name="inproj",
    )(x2d, nw, w_a, w_b, w_dt)


def _pool_branch(pext, u, pos, wgrp_ref, pscale_ref, zp):
    nb, t, _ = u.shape
    ys = []
    for g, w in enumerate(POOL_WINDOWS):
        cols = slice(g * POOL_GROUP, (g + 1) * POOL_GROUP)
        win = pext[:, :, cols]
        for sh in [1 << e for e in range(g + 1)]:
            win = win + pltpu.roll(win, sh, axis=1)
        win = win[:, POOL_PAD:, :]
        cnt = jnp.minimum(w, pos + 1).astype(F32)
        d = (win / cnt - u[:, :, cols]).astype(BF16).reshape(nb * t, POOL_GROUP)
        ys.append(_dot(d, wgrp_ref[g]))
    return jnp.concatenate(ys, axis=1) * pscale_ref[...] * _silu(zp)


def _conv_branch(cext, convw_ref, convb_ref, store):
    for cc in range(CONV_DIM // GROUP_WIDTH):
        csl = slice(cc * GROUP_WIDTH, (cc + 1) * GROUP_WIDTH)
        ext = cext[:, :, csl]
        conv = convb_ref[:, csl].reshape(1, 1, GROUP_WIDTH)
        for kk in range(CONV_WIDTH):
            tap = ext if kk == CONV_WIDTH - 1 else pltpu.roll(ext, CONV_WIDTH - 1 - kk, axis=1)
            conv = conv + tap * convw_ref[kk:kk + 1, csl].reshape(1, 1, GROUP_WIDTH)
        store(cc, _silu(conv[:, CONV_PAD:, :]))


PROMPT_TILE = 256


def _seq_prompt_kernel(ntile, x_ref, nw_ref, wa_ref, wb_ref, wdt_ref, k_ref, v_ref,
                       wgrp_ref, pscale_ref, convw_ref, convb_ref, dtb_ref, alog_ref, dexp_ref, ssdnw_ref,
                       gates_ref, ypool_ref, yssd_ref, yatt_ref, pool_o_ref, conv_o_ref, ssm_o_ref,
                       pext, cext, xs_scr, b_scr, c_scr, dt_scr, y_scr, h_scr, zs_scr, q_scr, za_scr):
    t, q = PROMPT_TILE, SSD_CHUNK
    k = pl.program_id(0)
    s = k % ntile
    last = ntile - 1

    @pl.when(k == 0)
    def _():
        y_scr[...] = jnp.zeros(y_scr.shape, F32)
        zs_scr[...] = jnp.zeros(zs_scr.shape, BF16)
        q_scr[...] = jnp.zeros(q_scr.shape, BF16)
        za_scr[...] = jnp.zeros(za_scr.shape, BF16)

    @pl.when(s == 0)
    def _():
        pext[:, 0:POOL_PAD, :] = jnp.zeros((1, POOL_PAD, D_MODEL), F32)
        cext[:, 0:CONV_PAD, :] = jnp.zeros((1, CONV_PAD, CONV_DIM), F32)
        h_scr[...] = jnp.zeros(h_scr.shape, F32)

    @pl.when(s > 0)
    def _():
        carry_p = pext[:, t:t + POOL_PAD, :]
        carry_c = cext[:, t:t + CONV_PAD, :]
        pext[:, 0:POOL_PAD, :] = carry_p
        cext[:, 0:CONV_PAD, :] = carry_c

    yz = y_scr[...] * zs_scr[...].astype(F32)
    yssd_ref[...] = _rms(yz, ssdnw_ref[...]).astype(BF16)
    scale = ATT_HEAD_DIM ** -0.5
    outs = []
    for hd in range(ATT_HEADS):
        hsl = slice(hd * ATT_HEAD_DIM, (hd + 1) * ATT_HEAD_DIM)
        p = _softmax_rows(lax.dot_general(q_scr[:, hsl], k_ref[0, hd], _NT, preferred_element_type=F32) * scale)
        outs.append(_dot(p.astype(BF16), v_ref[0, hd]))
    yatt_ref[...] = (jnp.concatenate(outs, axis=1) * za_scr[...].astype(F32)).astype(BF16)

    hn = _rms(x_ref[...], nw_ref[...]).astype(BF16)

    def proj(piece, lo=0, hi=None):
        return _dot(hn, _weight_cols((wa_ref, wb_ref), piece, lo, piece[1] - piece[0] if hi is None else hi))

    def gates_piece(c):
        def run():
            gates_ref[:, c * D_MODEL:(c + 1) * D_MODEL] = _sigmoid(
                proj(COL_GATES, c * D_MODEL, (c + 1) * D_MODEL).astype(BF16))
        return run

    def zs_piece(c):
        def run():
            zs_scr[:, c * D_MODEL:(c + 1) * D_MODEL] = _silu(proj(COL_ZS, c * D_MODEL, (c + 1) * D_MODEL).astype(BF16))
        return run

    def q_piece():
        q_scr[...] = proj(COL_Q).astype(BF16)

    def za_piece():
        za_scr[...] = _silu(proj(COL_ZA).astype(BF16))

    fillers = [gates_piece(0), gates_piece(1), gates_piece(2), zs_piece(0), zs_piece(1), q_piece, za_piece]

    def run_filler():
        if fillers:
            fillers.pop(0)()

    u = proj(COL_U).reshape(1, t, D_MODEL)
    pext[:, POOL_PAD:, :] = u
    pos = s * t + lax.broadcasted_iota(jnp.int32, (1, t, 1), 1)
    ypool_ref[...] = _pool_branch(pext, u, pos, wgrp_ref, pscale_ref, proj(COL_ZP)).astype(BF16)

    for c in range(CONV_DIM // D_MODEL):
        cext[:, CONV_PAD:, c * D_MODEL:(c + 1) * D_MODEL] = proj(
            COL_XBC, c * D_MODEL, (c + 1) * D_MODEL).reshape(1, t, D_MODEL)
    dt_scr[...] = _softplus(_dot(hn, wdt_ref[...]) + dtb_ref[...])

    def store_conv(cc, val):
        if cc < SSD_GROUPS:
            xs_scr[:, cc * GROUP_WIDTH:(cc + 1) * GROUP_WIDTH] = val[0]
        elif cc == SSD_GROUPS:
            b_scr[...] = val[0]
        else:
            c_scr[...] = val[0]
        run_filler()

    _conv_branch(cext, convw_ref, convb_ref, store_conv)

    a_neg = -jnp.exp(alog_ref[...])
    rq = lax.broadcasted_iota(jnp.int32, (q, q), 0)
    cq = lax.broadcasted_iota(jnp.int32, (q, q), 1)
    tril = rq >= cq
    tri_f = tril.astype(F32)
    lane_lo = lax.broadcasted_iota(jnp.int32, (1, LANES), 1) < SSD_HEAD_DIM
    pairs_per_group = SSD_HEADS // SSD_GROUPS // 2

    def chunk(c):
        rsl = slice(c * q, (c + 1) * q)
        dtc = dt_scr[rsl, :]
        acs = jnp.dot(tri_f, dtc * a_neg, precision=lax.Precision.HIGHEST, preferred_element_type=F32)
        acs_t = acs.T
        dt_t = dtc.T
        wdec_t = (dt_t * jnp.exp(acs_t[:, q - 1:q] - acs_t)).astype(BF16)
        row_t = acs_t - jnp.log(dt_t)
        cdec = jnp.exp(acs[q - 1:q, :])
        for g in range(SSD_GROUPS):
            gsl = slice(g * SSD_STATE, (g + 1) * SSD_STATE)
            bg = b_scr[rsl, gsl]
            cg_b = c_scr[rsl, gsl].astype(BF16)
            cb = lax.dot_general(cg_b, bg.astype(BF16), _NT, preferred_element_type=F32)
            bg_t = bg.T.astype(BF16)
            hsl = slice(g * GROUP_WIDTH, (g + 1) * GROUP_WIDTH)
            z_g = _dot(cg_b, h_scr[:, hsl].astype(BF16))
            for jp in range(pairs_per_group):
                j = g * pairs_per_group + jp
                lsl = slice(j * LANES, (j + 1) * LANES)
                xp = xs_scr[rsl, lsl]
                xp_b = xp.astype(BF16)
                zero_b = jnp.zeros_like(xp_b)
                x_bd = jnp.concatenate([jnp.where(lane_lo, xp_b, zero_b), jnp.where(lane_lo, zero_b, xp_b)],
                                       axis=0)
                ms, bws, cols = [], [], []
                for hh in range(2):
                    r = 2 * j + hh
                    cols.append(jnp.broadcast_to(acs[:, r:r + 1], (q, q)))
                    seg = cols[hh] - row_t[r:r + 1, :]
                    ms.append(cb * jnp.exp(jnp.where(tril, seg, NEG_BIG)))
                    bws.append(bg_t * wdec_t[r:r + 1, :])
                ea_pair = jnp.exp(jnp.where(lane_lo, cols[0], cols[1]))
                y = (_dot(jnp.concatenate(ms, axis=1).astype(BF16), x_bd)
                     + ea_pair * z_g[:, jp * LANES:(jp + 1) * LANES] + dexp_ref[:, lsl] * xp)
                y_scr[rsl, lsl] = y
                cd_pair = jnp.where(lane_lo, cdec[:, 2 * j:2 * j + 1], cdec[:, 2 * j + 1:2 * j + 2])
                h_scr[:, lsl] = h_scr[:, lsl] * cd_pair + _dot(jnp.concatenate(bws, axis=1), x_bd)
            run_filler()

    for c in range(t // q):
        chunk(c)
    while fillers:
        run_filler()

    @pl.when(jnp.logical_and(s == last, k < pl.num_programs(0) - 1))
    def _():
        pool_o_ref[...] = pext[:, t + POOL_PAD - POOL_HIST:t + POOL_PAD, :]
        conv_o_ref[...] = cext[:, t + CONV_PAD - (CONV_WIDTH - 1):t + CONV_PAD, :]
        for j in range(SSD_HEADS // 2):
            lsl = slice(j * LANES, (j + 1) * LANES)
            ssm_o_ref[0, lsl, :] = h_scr[:, lsl].T


def _seq_prompt(x2d, nw, w_a, w_b, w_dt, kb, vb, params, *, nseq, ntile):
    t = PROMPT_TILE
    m = x2d.shape[0]
    ntiles = nseq * ntile
    tile = lambda k: jnp.minimum(k, ntiles - 1)
    closed = lambda k: jnp.maximum(k - 1, 0)
    own_rows = lambda width: pl.BlockSpec((t, width), lambda k: (k, 0))
    closed_rows = lambda width: pl.BlockSpec((t, width), lambda k: (closed(k), 0))
    seq_spec = lambda shape, which: pl.BlockSpec((1,) + shape, lambda k: (which(k) // ntile,) + (0,) * len(shape))
    const_spec = lambda a: pl.BlockSpec(a.shape, lambda k: (0,) * a.ndim)
    resident = lambda a: pl.BlockSpec(a.shape, lambda k: (0,) * a.ndim, pipeline_mode=pl.Buffered(1))
    in_specs = [
        pl.BlockSpec((t, D_MODEL), lambda k: (tile(k), 0)),
        resident(nw), resident(w_a), resident(w_b), resident(w_dt),
        seq_spec((ATT_HEADS, MEM_LEN, ATT_HEAD_DIM), closed),
        seq_spec((ATT_HEADS, MEM_LEN, ATT_HEAD_DIM), closed),
    ] + [const_spec(p) for p in params]
    out_specs = [
        own_rows(3 * D_MODEL), own_rows(D_MODEL), closed_rows(SSD_WIDTH), closed_rows(D_MODEL),
        seq_spec((POOL_HIST, D_MODEL), tile),
        seq_spec((CONV_WIDTH - 1, CONV_DIM), tile),
        seq_spec((SSD_WIDTH, SSD_STATE), tile),
    ]
    out_shape = [
        jax.ShapeDtypeStruct((m + t, 3 * D_MODEL), BF16),
        jax.ShapeDtypeStruct((m + t, D_MODEL), BF16),
        jax.ShapeDtypeStruct((m, SSD_WIDTH), BF16),
        jax.ShapeDtypeStruct((m, D_MODEL), BF16),
        jax.ShapeDtypeStruct((nseq, POOL_HIST, D_MODEL), F32),
        jax.ShapeDtypeStruct((nseq, CONV_WIDTH - 1, CONV_DIM), F32),
        jax.ShapeDtypeStruct((nseq, SSD_WIDTH, SSD_STATE), F32),
    ]
    scratch = [
        pltpu.VMEM((1, POOL_PAD + t, D_MODEL), F32),
        pltpu.VMEM((1, CONV_PAD + t, CONV_DIM), F32),
        pltpu.VMEM((t, SSD_WIDTH), F32),
        pltpu.VMEM((t, GROUP_WIDTH), F32),
        pltpu.VMEM((t, GROUP_WIDTH), F32),
        pltpu.VMEM((t, LANES), F32),
        pltpu.VMEM((t, SSD_WIDTH), F32),
        pltpu.VMEM((SSD_STATE, SSD_WIDTH), F32),
        pltpu.VMEM((t, SSD_WIDTH), BF16),
        pltpu.VMEM((t, D_MODEL), BF16),
        pltpu.VMEM((t, D_MODEL), BF16),
    ]
    return pl.pallas_call(
        functools.partial(_seq_prompt_kernel, ntile),
        grid=(ntiles + 1,),
        in_specs=in_specs,
        out_specs=out_specs,
        out_shape=out_shape,
        scratch_shapes=scratch,
        compiler_params=pltpu.CompilerParams(dimension_semantics=("arbitrary",),
                                             vmem_limit_bytes=SEQ_PROMPT_VMEM_LIMIT),
        name="seq_prompt",
    )(x2d, nw, w_a, w_b, w_dt, kb, vb, *params)


SAMPLE_BLOCK = 32


def _state_pre_kernel(u_ref, zp_ref, xbc_ref, dt_ref, ph_ref, ch_ref,
                      wgrp_ref, pscale_ref, convw_ref, convb_ref, dtb_ref, alog_ref,
                      ypool_ref, xs_ref, bc_ref, dts_ref, cd_ref, pool_o_ref, conv_o_ref, pext, cext):
    nb, t = SAMPLE_BLOCK, SUBLANES
    pext[:, 0:1, :] = jnp.zeros((nb, 1, D_MODEL), F32)
    pext[:, 1:POOL_PAD, :] = ph_ref[...]
    cext[:, 0:CONV_PAD - (CONV_WIDTH - 1), :] = jnp.zeros((nb, CONV_PAD - (CONV_WIDTH - 1), CONV_DIM), F32)
    cext[:, CONV_PAD - (CONV_WIDTH - 1):CONV_PAD, :] = ch_ref[...]

    u = u_ref[...].astype(F32).reshape(nb, t, D_MODEL)
    pext[:, POOL_PAD:, :] = u
    pos = PAST_LEN + lax.broadcasted_iota(jnp.int32, (1, t, 1), 1)
    ypool_ref[...] = _pool_branch(pext, u, pos, wgrp_ref, pscale_ref, zp_ref[...].astype(F32)).astype(BF16)

    cext[:, CONV_PAD:, :] = xbc_ref[...].astype(F32).reshape(nb, t, CONV_DIM)

    def store_conv(cc, val):
        val = val.reshape(nb * t, GROUP_WIDTH)
        if cc < SSD_GROUPS:
            xs_ref[:, cc * GROUP_WIDTH:(cc + 1) * GROUP_WIDTH] = val
        else:
            bc_ref[:, (cc - SSD_GROUPS) * GROUP_WIDTH:(cc - SSD_GROUPS + 1) * GROUP_WIDTH] = val

    _conv_branch(cext, convw_ref, convb_ref, store_conv)
    dt = _softplus(dt_ref[...] + dtb_ref[...])
    dts_ref[...] = dt
    a = dt * -jnp.exp(alog_ref[...])
    cd_ref[...] = jnp.exp(jnp.sum(a.reshape(nb, t, LANES), axis=1))
    pool_o_ref[...] = pext[:, t + POOL_PAD - POOL_HIST:t + POOL_PAD, :]
    conv_o_ref[...] = cext[:, t + CONV_PAD - (CONV_WIDTH - 1):t + CONV_PAD, :]


def _col_block(piece):
    return piece[0] // (piece[1] - piece[0])


def _state_pre(main, dt, state_pool, state_conv, params):
    nb, t = SAMPLE_BLOCK, SUBLANES
    rows = nb * t
    m = main.shape[0]
    nseq = m // t
    col_spec = lambda width, idx: pl.BlockSpec((rows, width), lambda i: (i, idx))
    seq_spec = lambda shape: pl.BlockSpec((nb,) + shape, lambda i: (i,) + (0,) * len(shape))
    const_spec = lambda a: pl.BlockSpec(a.shape, lambda i: (0,) * a.ndim)
    return pl.pallas_call(
        _state_pre_kernel,
        grid=(nseq // nb,),
        in_specs=[col_spec(D_MODEL, _col_block(COL_U)), col_spec(D_MODEL, _col_block(COL_ZP)),
                  col_spec(CONV_DIM, _col_block(COL_XBC)), col_spec(LANES, 0),
                  seq_spec((POOL_HIST, D_MODEL)), seq_spec((CONV_WIDTH - 1, CONV_DIM))]
        + [const_spec(p) for p in params],
        out_specs=[col_spec(D_MODEL, 0), col_spec(SSD_WIDTH, 0), col_spec(2 * GROUP_WIDTH, 0), col_spec(LANES, 0),
                   pl.BlockSpec((nb, LANES), lambda i: (i, 0)),
                   seq_spec((POOL_HIST, D_MODEL)), seq_spec((CONV_WIDTH - 1, CONV_DIM))],
        out_shape=[jax.ShapeDtypeStruct((m, D_MODEL), BF16), jax.ShapeDtypeStruct((m, SSD_WIDTH), F32),
                   jax.ShapeDtypeStruct((m, 2 * GROUP_WIDTH), F32), jax.ShapeDtypeStruct((m, LANES), F32),
                   jax.ShapeDtypeStruct((nseq, LANES), F32),
                   jax.ShapeDtypeStruct((nseq, POOL_HIST, D_MODEL), F32),
                   jax.ShapeDtypeStruct((nseq, CONV_WIDTH - 1, CONV_DIM), F32)],
        scratch_shapes=[pltpu.VMEM((nb, POOL_PAD + t, D_MODEL), F32), pltpu.VMEM((nb, CONV_PAD + t, CONV_DIM), F32)],
        compiler_params=pltpu.CompilerParams(dimension_semantics=("arbitrary",), vmem_limit_bytes=VMEM_LIMIT),
        name="state_pre",
    )(main, main, main, dt, state_pool, state_conv, *params)


PIPE_SLOTS = 4


def _ring_pipeline(n, start_in, wait_in, compute, start_out=None, wait_out=None):
    ns = PIPE_SLOTS
    for i in range(ns - 1):
        start_in(i, i)

    def body(bb, _):
        for k in range(ns):
            i = ns * bb + k
            nxt = i + ns - 1
            pl.when(nxt < n)(functools.partial(start_in, nxt, (k + ns - 1) % ns))
            wait_in(i, k)
            if wait_out is not None:
                pl.when(bb > 0)(functools.partial(wait_out, i - ns, k))
            compute(i, k)
            if start_out is not None:
                start_out(i, k)
        return 0

    lax.fori_loop(0, n // ns, body, 0)
    if wait_out is not None:
        for k in range(ns):
            wait_out(n - ns + k, k)


def _ssd_state_kernel(xs_ref, bc_ref, dt_ref, zs_ref, cd_ref, hin_hbm, expand_ref, segsum_ref, alog_ref, dexp_ref,
                      ssdnw_ref, yssd_ref, hout_hbm, hbuf, obuf, y_scr, sem_in, sem_out):
    nb, t = SAMPLE_BLOCK, SUBLANES
    base = pl.program_id(0) * nb
    a_neg = -jnp.exp(alog_ref[...])
    ridx = lax.broadcasted_iota(jnp.int32, (t, LANES), 0)

    def in_copies(b, slot):
        return [pltpu.make_async_copy(hin_hbm.at[base + b, g * GROUP_WIDTH:(g + 1) * GROUP_WIDTH, :],
                                      hbuf.at[slot, g * GROUP_WIDTH:(g + 1) * GROUP_WIDTH, :], sem_in.at[slot, g])
                for g in range(SSD_GROUPS)]

    def out_copies(b, slot):
        return [pltpu.make_async_copy(obuf.at[slot, g * GROUP_WIDTH:(g + 1) * GROUP_WIDTH, :],
                                      hout_hbm.at[base + b, g * GROUP_WIDTH:(g + 1) * GROUP_WIDTH, :],
                                      sem_out.at[slot, g])
                for g in range(SSD_GROUPS)]

    def start_all(copies):
        for c in copies:
            c.start()

    def wait_all(copies):
        for c in copies:
            c.wait()

    def compute(b, slot):
        rsl = pl.ds(pl.multiple_of(b * t, t), t)
        dtc = dt_ref[rsl, :]
        acs = dtc * a_neg
        for sh in (1, 2, 4):
            acs = acs + jnp.where(ridx >= sh, pltpu.roll(acs, sh, axis=0), 0.0)
        tot = acs[t - 1:t, :]
        x = xs_ref[rsl, :]
        bc = bc_ref[rsl, :]
        bm, cm = bc[:, :GROUP_WIDTH], bc[:, GROUP_WIDTH:]
        bm_r, cm_r = bm.astype(BF16).astype(F32), cm.astype(BF16).astype(F32)
        gs, ps = [], []
        for k in range(t):
            gs.append(jnp.exp(jnp.where(ridx >= k, acs - acs[k:k + 1, :], NEG_BIG)) * dtc[k:k + 1, :])
            ps.append(cm_r * bm_r[k:k + 1, :])
        cb_heads = _dot(jnp.concatenate(ps, axis=0).astype(BF16), segsum_ref[...])
        per_head = jnp.concatenate([jnp.concatenate(gs, axis=0) * cb_heads, jnp.exp(acs),
                                    dtc * jnp.exp(tot - acs)], axis=0)
        hi = per_head.astype(BF16)
        lo = (per_head - hi.astype(F32)).astype(BF16)
        wide = _dot(hi, expand_ref[...]) + _dot(lo, expand_ref[...])
        y = dexp_ref[...] * x
        for k in range(t):
            y = y + wide[k * t:(k + 1) * t, :] * x[k:k + 1, :]
        ea_wide = wide[t * t:t * t + t, :]
        xw = x * wide[t * t + t:, :]
        for g in range(SSD_GROUPS):
            gsl = slice(g * SSD_STATE, (g + 1) * SSD_STATE)
            wsl = slice(g * GROUP_WIDTH, (g + 1) * GROUP_WIDTH)
            hg = hbuf[slot, wsl, :]
            z_g = lax.dot_general(cm[:, gsl].astype(BF16), hg.astype(BF16), _NT, preferred_element_type=F32)
            y_scr[rsl, wsl] = y[:, wsl] + ea_wide[:, wsl] * z_g
            upd = lax.dot_general(xw[:, wsl].astype(BF16), bm[:, gsl].astype(BF16), _TN, preferred_element_type=F32)
            for r8 in range(SSD_HEADS // SSD_GROUPS):
                r = g * (SSD_HEADS // SSD_GROUPS) + r8
                rows_r = slice(r * SSD_HEAD_DIM, (r + 1) * SSD_HEAD_DIM)
                obuf[slot, rows_r, :] = (hbuf[slot, rows_r, :] * cd_ref[base + b, r]
                                         + upd[r8 * SSD_HEAD_DIM:(r8 + 1) * SSD_HEAD_DIM, :])

    _ring_pipeline(nb,
                   lambda b, slot: start_all(in_copies(b, slot)), lambda b, slot: wait_all(in_copies(b, slot)),
                   compute,
                   lambda b, slot: start_all(out_copies(b, slot)), lambda b, slot: wait_all(out_copies(b, slot)))
    yz = y_scr[...] * _silu(zs_ref[...].astype(F32))
    yssd_ref[...] = _rms(yz, ssdnw_ref[...]).astype(BF16)


def _ssd_state(xs, bc, dts, main, cd, hin, expand, segsum, alog, dexp, ssdnw):
    nb, t = SAMPLE_BLOCK, SUBLANES
    rows = nb * t
    m = xs.shape[0]
    nseq = m // t
    col_spec = lambda width, idx: pl.BlockSpec((rows, width), lambda i: (i, idx))
    const_spec = lambda a: pl.BlockSpec(a.shape, lambda i: (0,) * a.ndim)
    return pl.pallas_call(
        _ssd_state_kernel,
        grid=(nseq // nb,),
        in_specs=[col_spec(SSD_WIDTH, 0), col_spec(2 * GROUP_WIDTH, 0), col_spec(LANES, 0),
                  col_spec(SSD_WIDTH, _col_block(COL_ZS)),
                  pl.BlockSpec(memory_space=pltpu.SMEM), pl.BlockSpec(memory_space=pl.ANY),
                  const_spec(expand), const_spec(segsum), const_spec(alog), const_spec(dexp), const_spec(ssdnw)],
        out_specs=[col_spec(SSD_WIDTH, 0), pl.BlockSpec(memory_space=pl.ANY)],
        out_shape=[jax.ShapeDtypeStruct((m, SSD_WIDTH), BF16), jax.ShapeDtypeStruct(hin.shape, F32)],
        scratch_shapes=[pltpu.VMEM((PIPE_SLOTS, SSD_WIDTH, SSD_STATE), F32),
                        pltpu.VMEM((PIPE_SLOTS, SSD_WIDTH, SSD_STATE), F32),
                        pltpu.VMEM((rows, SSD_WIDTH), F32),
                        pltpu.SemaphoreType.DMA((PIPE_SLOTS, SSD_GROUPS)),
                        pltpu.SemaphoreType.DMA((PIPE_SLOTS, SSD_GROUPS))],
        compiler_params=pltpu.CompilerParams(dimension_semantics=("arbitrary",), vmem_limit_bytes=VMEM_LIMIT),
        name="ssd_state",
    )(xs, bc, dts, main, cd, hin, expand, segsum, alog, dexp, ssdnw)


ATT_ITEM = 2


def _att_state_kernel(q_ref, za_ref, k_hbm, v_hbm, yatt_ref, kbuf, vbuf, q_scr, att_scr, sem):
    nb, t = SAMPLE_BLOCK, SUBLANES
    base = pl.program_id(0) * nb
    scale = ATT_HEAD_DIM ** -0.5
    q_scr[...] = q_ref[...].astype(F32)
    head_of_lane = lax.broadcasted_iota(jnp.int32, (1, D_MODEL), 1) // ATT_HEAD_DIM

    def copies(item, slot):
        out = []
        for j in range(ATT_ITEM):
            for kv, (src, buf) in enumerate(((k_hbm, kbuf), (v_hbm, vbuf))):
                for hd in range(ATT_HEADS):
                    hsl = slice(hd * ATT_HEAD_DIM, (hd + 1) * ATT_HEAD_DIM)
                    out.append(pltpu.make_async_copy(
                        src.at[base + item * ATT_ITEM + j, :, hd, :], buf.at[slot, j, :, hsl],
                        sem.at[slot, (j * 2 + kv) * ATT_HEADS + hd]))
        return out

    def start_in(item, slot):
        for c in copies(item, slot):
            c.start()

    def wait_in(item, slot):
        for c in copies(item, slot):
            c.wait()

    def compute(item, slot):
        for j in range(ATT_ITEM):
            rsl = pl.ds(pl.multiple_of((item * ATT_ITEM + j) * t, t), t)
            qf = q_scr[rsl, :]
            q_bd = jnp.concatenate([jnp.where(head_of_lane == hd, qf, 0.0) for hd in range(ATT_HEADS)],
                                   axis=0).astype(BF16)
            sc = lax.dot_general(q_bd, kbuf[slot, j].astype(BF16), _NT, preferred_element_type=F32) * scale
            o = _dot(_softmax_rows(sc).astype(BF16), vbuf[slot, j].astype(BF16))
            att_scr[rsl, :] = jnp.concatenate(
                [o[hd * t:(hd + 1) * t, hd * ATT_HEAD_DIM:(hd + 1) * ATT_HEAD_DIM] for hd in range(ATT_HEADS)], axis=1)

    _ring_pipeline(nb // ATT_ITEM, start_in, wait_in, compute)
    yatt_ref[...] = (att_scr[...] * _silu(za_ref[...].astype(F32))).astype(BF16)


def _att_state(main, k, v):
    nb, t = SAMPLE_BLOCK, SUBLANES
    rows = nb * t
    m = main.shape[0]
    col_spec = lambda width, idx: pl.BlockSpec((rows, width), lambda i: (i, idx))
    kv_buf = pltpu.VMEM((PIPE_SLOTS, ATT_ITEM, MEM_LEN, D_MODEL), F32)
    return pl.pallas_call(
        _att_state_kernel,
        grid=(m // rows,),
        in_specs=[col_spec(D_MODEL, _col_block(COL_Q)), col_spec(D_MODEL, _col_block(COL_ZA)),
                  pl.BlockSpec(memory_space=pl.ANY), pl.BlockSpec(memory_space=pl.ANY)],
        out_specs=col_spec(D_MODEL, 0),
        out_shape=jax.ShapeDtypeStruct((m, D_MODEL), BF16),
        scratch_shapes=[kv_buf, kv_buf, pltpu.VMEM((rows, D_MODEL), F32), pltpu.VMEM((rows, D_MODEL), F32),
                        pltpu.SemaphoreType.DMA((PIPE_SLOTS, ATT_ITEM * 2 * ATT_HEADS))],
        compiler_params=pltpu.CompilerParams(dimension_semantics=("arbitrary",), vmem_limit_bytes=VMEM_LIMIT),
        name="att_state",
    )(main, main, k, v)


DENSE_ROWS = 512


def _dense_kernel(x_ref, gt_ref, yp_ref, ys_ref, ya_ref, wpo_ref, wso_ref, wao_ref, wo_ref, fnw_ref, y_ref):
    gates = gt_ref[...].astype(F32)
    merged = (gates[:, 0:D_MODEL] * _dot(yp_ref[...], wpo_ref[...])
              + gates[:, D_MODEL:2 * D_MODEL] * _dot(ys_ref[...], wso_ref[...])
              + gates[:, 2 * D_MODEL:] * _dot(ya_ref[...], wao_ref[...]))
    x_out = x_ref[...] + _dot(merged.astype(BF16), wo_ref[...])
    y_ref[...] = _rms(x_out, fnw_ref[...])


def _dense(x2d, gates, gate_idx, yp, ys, ya, wpo, wso, wao, wo, fnw):
    m = x2d.shape[0]
    row = lambda width, idx=0: pl.BlockSpec((DENSE_ROWS, width), lambda i: (i, idx))
    resident = lambda a: pl.BlockSpec(a.shape, lambda i: (0,) * a.ndim, pipeline_mode=pl.Buffered(1))
    return pl.pallas_call(
        _dense_kernel,
        grid=(m // DENSE_ROWS,),
        in_specs=[row(D_MODEL), row(3 * D_MODEL, gate_idx), row(D_MODEL), row(SSD_WIDTH), row(D_MODEL),
                  resident(wpo), resident(wso), resident(wao), resident(wo), resident(fnw)],
        out_specs=row(D_MODEL),
        out_shape=jax.ShapeDtypeStruct((m, D_MODEL), F32),
        compiler_params=pltpu.CompilerParams(dimension_semantics=("arbitrary",), vmem_limit_bytes=VMEM_LIMIT),
        name="dense",
    )(x2d, gates, yp, ys, ya, wpo, wso, wao, wo, fnw)


def kernel(x_prompt, x_sample, mem_prompt, state_pool, state_conv, state_ssm, cache_mem_k, cache_mem_v,
           norm_w, w_in, w_pool_grp, pool_scale, conv_w, conv_b, dt_bias, a_log, d_skip, ssd_norm_w,
           mem_norm_w, w_mem_k, w_mem_v, w_pool_out, w_ssd_out, w_att_out, w_out, final_norm_w):
    assert w_in.shape[0] == 1
    bp, sp, d = x_prompt.shape
    bs, ss, _ = x_sample.shape
    assert ss == SUBLANES and sp % PROMPT_TILE == 0 and bs % SAMPLE_BLOCK == 0

    w_a = w_in[0][:, :W_SPLIT[0]].astype(BF16)
    w_b = w_in[0][:, W_SPLIT[1]:].astype(BF16)
    w_dt = jnp.pad(w_in[0][:, W_SPLIT[0]:W_SPLIT[1]], ((0, 0), (0, LANES - SSD_HEADS))).astype(BF16)
    nw = norm_w[0].reshape(1, d)
    pad_heads = lambda a: jnp.pad(a.reshape(1, SSD_HEADS), ((0, 0), (0, LANES - SSD_HEADS)))
    wgrp = w_pool_grp[0].astype(BF16)
    pscale = pool_scale[0].reshape(1, d)
    convb = conv_b[0].reshape(1, CONV_DIM)
    dtb, alog = pad_heads(dt_bias[0]), pad_heads(a_log[0])
    dexp = jnp.repeat(d_skip[0], SSD_HEAD_DIM).reshape(1, SSD_WIDTH)
    ssdnw = ssd_norm_w[0].reshape(1, SSD_WIDTH)
    dense_w = (w_pool_out[0].astype(BF16), w_ssd_out[0].astype(BF16), w_att_out[0].astype(BF16),
               w_out[0].astype(BF16), final_norm_w.reshape(1, d))
    head_of_lane = jnp.arange(SSD_WIDTH) // SSD_HEAD_DIM
    expand = (jnp.arange(LANES)[:, None] == head_of_lane[None, :]).astype(BF16)
    group_of_head = jnp.where(jnp.arange(LANES) < SSD_HEADS, jnp.arange(LANES) // (SSD_HEADS // SSD_GROUPS), -1)
    segsum = ((jnp.arange(GROUP_WIDTH) // SSD_STATE)[:, None] == group_of_head[None, :]).astype(BF16)

    mk, mv, mkb, mvb = _memkv(mem_prompt, mem_norm_w[0].reshape(1, d), w_mem_k[0].astype(BF16),
                              w_mem_v[0].astype(BF16))
    xp2 = x_prompt.reshape(bp * sp, d)
    gates_p, yp, ysd, ya, pool_p, conv_p, ssm_p = _seq_prompt(
        xp2, nw, w_a, w_b, w_dt, mkb, mvb, (wgrp, pscale, conv_w[0], convb, dtb, alog, dexp, ssdnw),
        nseq=bp, ntile=sp // PROMPT_TILE)
    y_prompt = _dense(xp2, gates_p, 0, yp, ysd, ya, *dense_w).reshape(bp, sp, d)

    xs2 = x_sample.reshape(bs * ss, d)
    main_s, dt_s = _inproj(xs2, nw, w_a, w_b, w_dt)
    yp, xs, bc, dts, cd, pool_s, conv_s = _state_pre(
        main_s, dt_s, state_pool[0], state_conv[0], (wgrp, pscale, conv_w[0], convb, dtb, alog))
    ysd, ssm_s = _ssd_state(xs, bc, dts, main_s, cd, state_ssm[0].reshape(bs, SSD_WIDTH, SSD_STATE),
                            expand, segsum, alog, dexp, ssdnw)
    ya = _att_state(main_s, cache_mem_k[0], cache_mem_v[0])
    y_sample = _dense(xs2, main_s, _col_block(COL_GATES), yp, ysd, ya, *dense_w).reshape(bs, ss, d)

    ssm_shape = (SSD_GROUPS, SSD_HEADS // SSD_GROUPS, SSD_HEAD_DIM, SSD_STATE)
    return (y_prompt, y_sample,
            pool_p[None], conv_p[None], ssm_p.reshape((1, bp) + ssm_shape),
            mk[None], mv[None],
            pool_s[None], conv_s[None], ssm_s.reshape((1, bs) + ssm_shape))
```

```python
import functools

import jax
import jax.numpy as jnp
from jax import lax
from jax.experimental import pallas as pl
from jax.experimental.pallas import tpu as pltpu

F32 = jnp.float32
BF16 = jnp.bfloat16

D_MODEL = 1024
POOL_WINDOWS = (2, 4, 8, 16)
POOL_GROUP = 256
POOL_HIST = 15
POOL_PAD = 16
SSD_WIDTH = 2048
SSD_HEADS = 32
SSD_HEAD_DIM = 64
SSD_GROUPS = 4
SSD_STATE = 128
GROUP_WIDTH = SSD_WIDTH // SSD_GROUPS
CONV_WIDTH = 4
CONV_DIM = 3072
CONV_PAD = 8
SSD_CHUNK = 128
MEM_LEN = 256
ATT_HEADS = 4
ATT_HEAD_DIM = 256
PAST_LEN = 16384
EPS = 1e-6
NEG_BIG = -1e30
SUBLANES = 8
LANES = 128
MAIN_COLS = 12288
COL_XBC, COL_GATES, COL_ZS = (0, 3072), (3072, 6144), (6144, 8192)
COL_U, COL_ZP, COL_Q, COL_ZA = (8192, 9216), (9216, 10240), (10240, 11264), (11264, 12288)
W_SPLIT = (7168, 7200)
W_SRC = {COL_U: (0, 0), COL_ZP: (0, 1024), COL_ZS: (0, 2048), COL_XBC: (0, 4096),
         COL_Q: (1, 0), COL_ZA: (1, 1024), COL_GATES: (1, 2048)}
VMEM_LIMIT = 56 * 1024 * 1024
SEQ_PROMPT_VMEM_LIMIT = 60 * 1024 * 1024

_NT = (((1,), (1,)), ((), ()))
_TN = (((0,), (0,)), ((), ()))


def _sigmoid(x):
    return 1.0 / (1.0 + jnp.exp(-x))


def _silu(x):
    return x * _sigmoid(x)


def _softplus(x):
    return jnp.maximum(x, 0.0) + jnp.log1p(jnp.exp(-jnp.abs(x)))


def _rms(x, w):
    return x * lax.rsqrt(jnp.mean(x * x, axis=-1, keepdims=True) + EPS) * w


def _dot(a, b):
    return jnp.dot(a, b, preferred_element_type=F32)


def _softmax_rows(sc):
    e = jnp.exp(sc - jnp.max(sc, axis=-1, keepdims=True))
    return e / jnp.sum(e, axis=-1, keepdims=True)


def _weight_cols(w_refs, piece, lo, hi):
    idx, c0 = W_SRC[piece]
    return w_refs[idx][:, c0 + lo:c0 + hi]


WPREP_ROWS = 128


def _wprep_kernel(w_ref, a_ref, b_ref, dt_ref):
    a_ref[...] = w_ref[0, :, 0:W_SPLIT[0]].astype(BF16)
    b_ref[...] = w_ref[0, :, W_SPLIT[1]:].astype(BF16)
    dt = w_ref[0, :, W_SPLIT[0]:W_SPLIT[1]]
    dt_ref[...] = jnp.concatenate([dt, jnp.zeros((dt.shape[0], LANES - SSD_HEADS), F32)], axis=1).astype(BF16)


def _wprep(w_in):
    _, d, cols = w_in.shape
    wb_cols = cols - W_SPLIT[1]
    return pl.pallas_call(
        _wprep_kernel,
        grid=(d // WPREP_ROWS,),
        in_specs=[pl.BlockSpec((1, WPREP_ROWS, cols), lambda i: (0, i, 0))],
        out_specs=[pl.BlockSpec((WPREP_ROWS, W_SPLIT[0]), lambda i: (i, 0)),
                   pl.BlockSpec((WPREP_ROWS, wb_cols), lambda i: (i, 0)),
                   pl.BlockSpec((WPREP_ROWS, LANES), lambda i: (i, 0))],
        out_shape=[jax.ShapeDtypeStruct((d, W_SPLIT[0]), BF16), jax.ShapeDtypeStruct((d, wb_cols), BF16),
                   jax.ShapeDtypeStruct((d, LANES), BF16)],
        compiler_params=pltpu.CompilerParams(dimension_semantics=("arbitrary",), vmem_limit_bytes=VMEM_LIMIT),
        name="wprep",
    )(w_in)


def _memkv_kernel(mem_ref, nw_ref, wk_ref, wv_ref, k_ref, v_ref, kb_ref, vb_ref):
    mh = _rms(mem_ref[0], nw_ref[...]).astype(BF16)
    k = _dot(mh, wk_ref[...])
    v = _dot(mh, wv_ref[...])
    for hd in range(ATT_HEADS):
        hsl = slice(hd * ATT_HEAD_DIM, (hd + 1) * ATT_HEAD_DIM)
        k_ref[0, :, hd, :] = k[:, hsl]
        v_ref[0, :, hd, :] = v[:, hsl]
        kb_ref[0, hd] = k[:, hsl].astype(BF16)
        vb_ref[0, hd] = v[:, hsl].astype(BF16)


def _memkv(mem, nw, wk, wv):
    b, m, d = mem.shape
    full = lambda shape: pl.BlockSpec(shape, lambda i: (0,) * len(shape))
    blk = pl.BlockSpec((1, m, d), lambda i: (i, 0, 0))
    oblk = pl.BlockSpec((1, m, ATT_HEADS, ATT_HEAD_DIM), lambda i: (i, 0, 0, 0))
    hblk = pl.BlockSpec((1, ATT_HEADS, m, ATT_HEAD_DIM), lambda i: (i, 0, 0, 0))
    return pl.pallas_call(
        _memkv_kernel,
        grid=(b,),
        in_specs=[blk, full((1, d)), full((d, d)), full((d, d))],
        out_specs=[oblk, oblk, hblk, hblk],
        out_shape=[jax.ShapeDtypeStruct((b, m, ATT_HEADS, ATT_HEAD_DIM), F32)] * 2
        + [jax.ShapeDtypeStruct((b, ATT_HEADS, m, ATT_HEAD_DIM), BF16)] * 2,
        compiler_params=pltpu.CompilerParams(dimension_semantics=("arbitrary",), vmem_limit_bytes=VMEM_LIMIT),
        name="memkv",
    )(mem, nw, wk, wv)


INPROJ_ROWS = 256
INPROJ_COL_CHUNK = 1024


def _inproj_kernel(x_ref, nw_ref, wa_ref, wb_ref, wdt_ref, main_ref, dt_ref):
    h = _rms(x_ref[...], nw_ref[...]).astype(BF16)
    dt_ref[...] = _dot(h, wdt_ref[...])
    for piece in W_SRC:
        for lo in range(0, piece[1] - piece[0], INPROJ_COL_CHUNK):
            val = _dot(h, _weight_cols((wa_ref, wb_ref), piece, lo, lo + INPROJ_COL_CHUNK))
            if piece == COL_GATES:
                val = _sigmoid(val)
            main_ref[:, piece[0] + lo:piece[0] + lo + INPROJ_COL_CHUNK] = val.astype(BF16)


def _inproj(x2d, nw, w_a, w_b, w_dt):
    m = x2d.shape[0]
    resident = lambda a: pl.BlockSpec(a.shape, lambda i: (0,) * a.ndim, pipeline_mode=pl.Buffered(1))
    return pl.pallas_call(
        _inproj_kernel,
        grid=(m // INPROJ_ROWS,),
        in_specs=[pl.BlockSpec((INPROJ_ROWS, D_MODEL), lambda i: (i, 0)),
                  resident(nw), resident(w_a), resident(w_b), resident(w_dt)],
        out_specs=[
            pl.BlockSpec((INPROJ_ROWS, MAIN_COLS), lambda i: (i, 0)),
            pl.BlockSpec((INPROJ_ROWS, LANES), lambda i: (i, 0)),
        ],
        out_shape=[jax.ShapeDtypeStruct((m, MAIN_COLS), BF16), jax.ShapeDtypeStruct((m, LANES), F32)],
        compiler_params=pltpu.CompilerParams(dimension_semantics=("arbitrary",), vmem_limit_bytes=VMEM_LIMIT),
        name="inproj",
    )(x2d, nw, w_a, w_b, w_dt)


def _pool_branch(pext, u, pos, wgrp_ref, pscale_ref, zp):
    nb, t, _ = u.shape
    ys = []
    for g, w in enumerate(POOL_WINDOWS):
        cols = slice(g * POOL_GROUP, (g + 1) * POOL_GROUP)
        win = pext[:, :, cols]
        for sh in [1 << e for e in range(g + 1)]:
            win = win + pltpu.roll(win, sh, axis=1)
        win = win[:, POOL_PAD:, :]
        cnt = jnp.minimum(w, pos + 1).astype(F32)
        d = (win / cnt - u[:, :, cols]).astype(BF16).reshape(nb * t, POOL_GROUP)
        ys.append(_dot(d, wgrp_ref[g]))
    return jnp.concatenate(ys, axis=1) * pscale_ref[...] * _silu(zp)


def _conv_branch(cext, convw_ref, convb_ref, store):
    for cc in range(CONV_DIM // GROUP_WIDTH):
        csl = slice(cc * GROUP_WIDTH, (cc + 1) * GROUP_WIDTH)
        ext = cext[:, :, csl]
        conv = convb_ref[:, csl].reshape(1, 1, GROUP_WIDTH)
        for kk in range(CONV_WIDTH):
            tap = ext if kk == CONV_WIDTH - 1 else pltpu.roll(ext, CONV_WIDTH - 1 - kk, axis=1)
            conv = conv + tap * convw_ref[kk:kk + 1, csl].reshape(1, 1, GROUP_WIDTH)
        store(cc, _silu(conv[:, CONV_PAD:, :]))


PROMPT_TILE = 256


def _seq_prompt_kernel(ntile, x_ref, nw_ref, wa_ref, wb_ref, wdt_ref, k_ref, v_ref,
                       wgrp_ref, pscale_ref, convw_ref, convb_ref, dtb_ref, alog_ref, dexp_ref, ssdnw_ref,
                       gates_ref, ypool_ref, yssd_ref, yatt_ref, pool_o_ref, conv_o_ref, ssm_o_ref,
                       pext, cext, xs_scr, b_scr, c_scr, dt_scr, y_scr, h_scr, zs_scr, q_scr, za_scr):
    t, q = PROMPT_TILE, SSD_CHUNK
    k = pl.program_id(0)
    s = k % ntile
    last = ntile - 1

    @pl.when(k == 0)
    def _():
        y_scr[...] = jnp.zeros(y_scr.shape, F32)
        zs_scr[...] = jnp.zeros(zs_scr.shape, BF16)
        q_scr[...] = jnp.zeros(q_scr.shape, BF16)
        za_scr[...] = jnp.zeros(za_scr.shape, BF16)

    @pl.when(s == 0)
    def _():
        pext[:, 0:POOL_PAD, :] = jnp.zeros((1, POOL_PAD, D_MODEL), F32)
        cext[:, 0:CONV_PAD, :] = jnp.zeros((1, CONV_PAD, CONV_DIM), F32)
        h_scr[...] = jnp.zeros(h_scr.shape, F32)

    @pl.when(s > 0)
    def _():
        carry_p = pext[:, t:t + POOL_PAD, :]
        carry_c = cext[:, t:t + CONV_PAD, :]
        pext[:, 0:POOL_PAD, :] = carry_p
        cext[:, 0:CONV_PAD, :] = carry_c

    yz = y_scr[...] * zs_scr[...].astype(F32)
    yssd_ref[...] = _rms(yz, ssdnw_ref[...]).astype(BF16)
    scale = ATT_HEAD_DIM ** -0.5
    outs = []
    for hd in range(ATT_HEADS):
        hsl = slice(hd * ATT_HEAD_DIM, (hd + 1) * ATT_HEAD_DIM)
        p = _softmax_rows(lax.dot_general(q_scr[:, hsl], k_ref[0, hd], _NT, preferred_element_type=F32) * scale)
        outs.append(_dot(p.astype(BF16), v_ref[0, hd]))
    yatt_ref[...] = (jnp.concatenate(outs, axis=1) * za_scr[...].astype(F32)).astype(BF16)

    hn = _rms(x_ref[...], nw_ref[...]).astype(BF16)

    def proj(piece, lo=0, hi=None):
        return _dot(hn, _weight_cols((wa_ref, wb_ref), piece, lo, piece[1] - piece[0] if hi is None else hi))

    def gates_piece(c):
        def run():
            gates_ref[:, c * D_MODEL:(c + 1) * D_MODEL] = _sigmoid(
                proj(COL_GATES, c * D_MODEL, (c + 1) * D_MODEL).astype(BF16))
        return run

    def zs_piece(c):
        def run():
            zs_scr[:, c * D_MODEL:(c + 1) * D_MODEL] = _silu(proj(COL_ZS, c * D_MODEL, (c + 1) * D_MODEL).astype(BF16))
        return run

    def q_piece():
        q_scr[...] = proj(COL_Q).astype(BF16)

    def za_piece():
        za_scr[...] = _silu(proj(COL_ZA).astype(BF16))

    fillers = [gates_piece(0), gates_piece(1), gates_piece(2), zs_piece(0), zs_piece(1), q_piece, za_piece]

    def run_filler():
        if fillers:
            fillers.pop(0)()

    u = proj(COL_U).reshape(1, t, D_MODEL)
    pext[:, POOL_PAD:, :] = u
    pos = s * t + lax.broadcasted_iota(jnp.int32, (1, t, 1), 1)
    ypool_ref[...] = _pool_branch(pext, u, pos, wgrp_ref, pscale_ref, proj(COL_ZP)).astype(BF16)

    for c in range(CONV_DIM // D_MODEL):
        cext[:, CONV_PAD:, c * D_MODEL:(c + 1) * D_MODEL] = proj(
            COL_XBC, c * D_MODEL, (c + 1) * D_MODEL).reshape(1, t, D_MODEL)
    dt_scr[...] = _softplus(_dot(hn, wdt_ref[...]) + dtb_ref[...])

    def store_conv(cc, val):
        if cc < SSD_GROUPS:
            xs_scr[:, cc * GROUP_WIDTH:(cc + 1) * GROUP_WIDTH] = val[0]
        elif cc == SSD_GROUPS:
            b_scr[...] = val[0]
        else:
            c_scr[...] = val[0]
        run_filler()

    _conv_branch(cext, convw_ref, convb_ref, store_conv)

    a_neg = -jnp.exp(alog_ref[...])
    rq = lax.broadcasted_iota(jnp.int32, (q, q), 0)
    cq = lax.broadcasted_iota(jnp.int32, (q, q), 1)
    tril = rq >= cq
    tri_f = tril.astype(F32)
    lane_lo = lax.broadcasted_iota(jnp.int32, (1, LANES), 1) < SSD_HEAD_DIM
    pairs_per_group = SSD_HEADS // SSD_GROUPS // 2

    def chunk(c):
        rsl = slice(c * q, (c + 1) * q)
        dtc = dt_scr[rsl, :]
        acs = jnp.dot(tri_f, dtc * a_neg, precision=lax.Precision.HIGHEST, preferred_element_type=F32)
        acs_t = acs.T
        dt_t = dtc.T
        wdec_t = (dt_t * jnp.exp(acs_t[:, q - 1:q] - acs_t)).astype(BF16)
        row_t = acs_t - jnp.log(dt_t)
        cdec = jnp.exp(acs[q - 1:q, :])
        for g in range(SSD_GROUPS):
            gsl = slice(g * SSD_STATE, (g + 1) * SSD_STATE)
            bg = b_scr[rsl, gsl]
            cg_b = c_scr[rsl, gsl].astype(BF16)
            cb = lax.dot_general(cg_b, bg.astype(BF16), _NT, preferred_element_type=F32)
            bg_t = bg.T.astype(BF16)
            hsl = slice(g * GROUP_WIDTH, (g + 1) * GROUP_WIDTH)
            z_g = _dot(cg_b, h_scr[:, hsl].astype(BF16))
            for jp in range(pairs_per_group):
                j = g * pairs_per_group + jp
                lsl = slice(j * LANES, (j + 1) * LANES)
                xp = xs_scr[rsl, lsl]
                xp_b = xp.astype(BF16)
                zero_b = jnp.zeros_like(xp_b)
                x_bd = jnp.concatenate([jnp.where(lane_lo, xp_b, zero_b), jnp.where(lane_lo, zero_b, xp_b)],
                                       axis=0)
                ms, bws, cols = [], [], []
                for hh in range(2):
                    r = 2 * j + hh
                    cols.append(jnp.broadcast_to(acs[:, r:r + 1], (q, q)))
                    seg = cols[hh] - row_t[r:r + 1, :]
                    ms.append(cb * jnp.exp(jnp.where(tril, seg, NEG_BIG)))
                    bws.append(bg_t * wdec_t[r:r + 1, :])
                ea_pair = jnp.exp(jnp.where(lane_lo, cols[0], cols[1]))
                y = (_dot(jnp.concatenate(ms, axis=1).astype(BF16), x_bd)
                     + ea_pair * z_g[:, jp * LANES:(jp + 1) * LANES] + dexp_ref[:, lsl] * xp)
                y_scr[rsl, lsl] = y
                cd_pair = jnp.where(lane_lo, cdec[:, 2 * j:2 * j + 1], cdec[:, 2 * j + 1:2 * j + 2])
                h_scr[:, lsl] = h_scr[:, lsl] * cd_pair + _dot(jnp.concatenate(bws, axis=1), x_bd)
            run_filler()

    for c in range(t // q):
        chunk(c)
    while fillers:
        run_filler()

    @pl.when(jnp.logical_and(s == last, k < pl.num_programs(0) - 1))
    def _():
        pool_o_ref[...] = pext[:, t + POOL_PAD - POOL_HIST:t + POOL_PAD, :]
        conv_o_ref[...] = cext[:, t + CONV_PAD - (CONV_WIDTH - 1):t + CONV_PAD, :]
        for j in range(SSD_HEADS // 2):
            lsl = slice(j * LANES, (j + 1) * LANES)
            ssm_o_ref[0, lsl, :] = h_scr[:, lsl].T


def _seq_prompt(x2d, nw, w_a, w_b, w_dt, kb, vb, params, *, nseq, ntile):
    t = PROMPT_TILE
    m = x2d.shape[0]
    ntiles = nseq * ntile
    tile = lambda k: jnp.minimum(k, ntiles - 1)
    closed = lambda k: jnp.maximum(k - 1, 0)
    own_rows = lambda width: pl.BlockSpec((t, width), lambda k: (k, 0))
    closed_rows = lambda width: pl.BlockSpec((t, width), lambda k: (closed(k), 0))
    seq_spec = lambda shape, which: pl.BlockSpec((1,) + shape, lambda k: (which(k) // ntile,) + (0,) * len(shape))
    const_spec = lambda a: pl.BlockSpec(a.shape, lambda k: (0,) * a.ndim)
    resident = lambda a: pl.BlockSpec(a.shape, lambda k: (0,) * a.ndim, pipeline_mode=pl.Buffered(1))
    in_specs = [
        pl.BlockSpec((t, D_MODEL), lambda k: (tile(k), 0)),
        resident(nw), resident(w_a), resident(w_b), resident(w_dt),
        seq_spec((ATT_HEADS, MEM_LEN, ATT_HEAD_DIM), closed),
        seq_spec((ATT_HEADS, MEM_LEN, ATT_HEAD_DIM), closed),
    ] + [const_spec(p) for p in params]
    out_specs = [
        own_rows(3 * D_MODEL), own_rows(D_MODEL), closed_rows(SSD_WIDTH), closed_rows(D_MODEL),
        seq_spec((POOL_HIST, D_MODEL), tile),
        seq_spec((CONV_WIDTH - 1, CONV_DIM), tile),
        seq_spec((SSD_WIDTH, SSD_STATE), tile),
    ]
    out_shape = [
        jax.ShapeDtypeStruct((m + t, 3 * D_MODEL), BF16),
        jax.ShapeDtypeStruct((m + t, D_MODEL), BF16),
        jax.ShapeDtypeStruct((m, SSD_WIDTH), BF16),
        jax.ShapeDtypeStruct((m, D_MODEL), BF16),
        jax.ShapeDtypeStruct((nseq, POOL_HIST, D_MODEL), F32),
        jax.ShapeDtypeStruct((nseq, CONV_WIDTH - 1, CONV_DIM), F32),
        jax.ShapeDtypeStruct((nseq, SSD_WIDTH, SSD_STATE), F32),
    ]
    scratch = [
        pltpu.VMEM((1, POOL_PAD + t, D_MODEL), F32),
        pltpu.VMEM((1, CONV_PAD + t, CONV_DIM), F32),
        pltpu.VMEM((t, SSD_WIDTH), F32),
        pltpu.VMEM((t, GROUP_WIDTH), F32),
        pltpu.VMEM((t, GROUP_WIDTH), F32),
        pltpu.VMEM((t, LANES), F32),
        pltpu.VMEM((t, SSD_WIDTH), F32),
        pltpu.VMEM((SSD_STATE, SSD_WIDTH), F32),
        pltpu.VMEM((t, SSD_WIDTH), BF16),
        pltpu.VMEM((t, D_MODEL), BF16),
        pltpu.VMEM((t, D_MODEL), BF16),
    ]
    return pl.pallas_call(
        functools.partial(_seq_prompt_kernel, ntile),
        grid=(ntiles + 1,),
        in_specs=in_specs,
        out_specs=out_specs,
        out_shape=out_shape,
        scratch_shapes=scratch,
        compiler_params=pltpu.CompilerParams(dimension_semantics=("arbitrary",),
                                             vmem_limit_bytes=SEQ_PROMPT_VMEM_LIMIT),
        name="seq_prompt",
    )(x2d, nw, w_a, w_b, w_dt, kb, vb, *params)


SAMPLE_BLOCK = 32


def _state_pre_kernel(u_ref, zp_ref, xbc_ref, dt_ref, ph_ref, ch_ref,
                      wgrp_ref, pscale_ref, convw_ref, convb_ref, dtb_ref, alog_ref,
                      ypool_ref, xs_ref, bc_ref, dts_ref, cd_ref, pool_o_ref, conv_o_ref, pext, cext):
    nb, t = SAMPLE_BLOCK, SUBLANES
    pext[:, 0:1, :] = jnp.zeros((nb, 1, D_MODEL), F32)
    pext[:, 1:POOL_PAD, :] = ph_ref[...]
    cext[:, 0:CONV_PAD - (CONV_WIDTH - 1), :] = jnp.zeros((nb, CONV_PAD - (CONV_WIDTH - 1), CONV_DIM), F32)
    cext[:, CONV_PAD - (CONV_WIDTH - 1):CONV_PAD, :] = ch_ref[...]

    u = u_ref[...].astype(F32).reshape(nb, t, D_MODEL)
    pext[:, POOL_PAD:, :] = u
    pos = PAST_LEN + lax.broadcasted_iota(jnp.int32, (1, t, 1), 1)
    ypool_ref[...] = _pool_branch(pext, u, pos, wgrp_ref, pscale_ref, zp_ref[...].astype(F32)).astype(BF16)

    cext[:, CONV_PAD:, :] = xbc_ref[...].astype(F32).reshape(nb, t, CONV_DIM)

    def store_conv(cc, val):
        val = val.reshape(nb * t, GROUP_WIDTH)
        if cc < SSD_GROUPS:
            xs_ref[:, cc * GROUP_WIDTH:(cc + 1) * GROUP_WIDTH] = val
        else:
            bc_ref[:, (cc - SSD_GROUPS) * GROUP_WIDTH:(cc - SSD_GROUPS + 1) * GROUP_WIDTH] = val

    _conv_branch(cext, convw_ref, convb_ref, store_conv)
    dt = _softplus(dt_ref[...] + dtb_ref[...])
    dts_ref[...] = dt
    a = dt * -jnp.exp(alog_ref[...])
    cd_ref[...] = jnp.exp(jnp.sum(a.reshape(nb, t, LANES), axis=1))
    pool_o_ref[...] = pext[:, t + POOL_PAD - POOL_HIST:t + POOL_PAD, :]
    conv_o_ref[...] = cext[:, t + CONV_PAD - (CONV_WIDTH - 1):t + CONV_PAD, :]


def _col_block(piece):
    return piece[0] // (piece[1] - piece[0])


def _state_pre(main, dt, state_pool, state_conv, params):
    nb, t = SAMPLE_BLOCK, SUBLANES
    rows = nb * t
    m = main.shape[0]
    nseq = m // t
    col_spec = lambda width, idx: pl.BlockSpec((rows, width), lambda i: (i, idx))
    seq_spec = lambda shape: pl.BlockSpec((nb,) + shape, lambda i: (i,) + (0,) * len(shape))
    const_spec = lambda a: pl.BlockSpec(a.shape, lambda i: (0,) * a.ndim)
    return pl.pallas_call(
        _state_pre_kernel,
        grid=(nseq // nb,),
        in_specs=[col_spec(D_MODEL, _col_block(COL_U)), col_spec(D_MODEL, _col_block(COL_ZP)),
                  col_spec(CONV_DIM, _col_block(COL_XBC)), col_spec(LANES, 0),
                  seq_spec((POOL_HIST, D_MODEL)), seq_spec((CONV_WIDTH - 1, CONV_DIM))]
        + [const_spec(p) for p in params],
        out_specs=[col_spec(D_MODEL, 0), col_spec(SSD_WIDTH, 0), col_spec(2 * GROUP_WIDTH, 0), col_spec(LANES, 0),
                   pl.BlockSpec((nb, LANES), lambda i: (i, 0)),
                   seq_spec((POOL_HIST, D_MODEL)), seq_spec((CONV_WIDTH - 1, CONV_DIM))],
        out_shape=[jax.ShapeDtypeStruct((m, D_MODEL), BF16), jax.ShapeDtypeStruct((m, SSD_WIDTH), F32),
                   jax.ShapeDtypeStruct((m, 2 * GROUP_WIDTH), F32), jax.ShapeDtypeStruct((m, LANES), F32),
                   jax.ShapeDtypeStruct((nseq, LANES), F32),
                   jax.ShapeDtypeStruct((nseq, POOL_HIST, D_MODEL), F32),
                   jax.ShapeDtypeStruct((nseq, CONV_WIDTH - 1, CONV_DIM), F32)],
        scratch_shapes=[pltpu.VMEM((nb, POOL_PAD + t, D_MODEL), F32), pltpu.VMEM((nb, CONV_PAD + t, CONV_DIM), F32)],
        compiler_params=pltpu.CompilerParams(dimension_semantics=("arbitrary",), vmem_limit_bytes=VMEM_LIMIT),
        name="state_pre",
    )(main, main, main, dt, state_pool, state_conv, *params)


SSD_PIPE_SLOTS = 8
ATT_PIPE_SLOTS = 4


def _ring_pipeline(n, ns, start_in, wait_in, compute, start_out=None, wait_out=None):
    for i in range(ns - 1):
        start_in(i, i)

    def body(bb, _):
        for k in range(ns):
            i = ns * bb + k
            nxt = i + ns - 1
            pl.when(nxt < n)(functools.partial(start_in, nxt, (k + ns - 1) % ns))
            wait_in(i, k)
            if wait_out is not None:
                pl.when(bb > 0)(functools.partial(wait_out, i - ns, k))
            compute(i, k)
            if start_out is not None:
                start_out(i, k)
        return 0

    lax.fori_loop(0, n // ns, body, 0)
    if wait_out is not None:
        for k in range(ns):
            wait_out(n - ns + k, k)


def _ssd_state_kernel(xs_ref, bc_ref, dt_ref, zs_ref, cd_ref, hin_hbm, expand_ref, segsum_ref, alog_ref, dexp_ref,
                      ssdnw_ref, yssd_ref, hout_hbm, hbuf, obuf, y_scr, sem_in, sem_out):
    nb, t = SAMPLE_BLOCK, SUBLANES
    base = pl.program_id(0) * nb
    a_neg = -jnp.exp(alog_ref[...])
    ridx = lax.broadcasted_iota(jnp.int32, (t, LANES), 0)

    def in_copies(b, slot):
        return [pltpu.make_async_copy(hin_hbm.at[base + b, g * GROUP_WIDTH:(g + 1) * GROUP_WIDTH, :],
                                      hbuf.at[slot, g * GROUP_WIDTH:(g + 1) * GROUP_WIDTH, :], sem_in.at[slot, g])
                for g in range(SSD_GROUPS)]

    def out_copies(b, slot):
        return [pltpu.make_async_copy(obuf.at[slot, g * GROUP_WIDTH:(g + 1) * GROUP_WIDTH, :],
                                      hout_hbm.at[base + b, g * GROUP_WIDTH:(g + 1) * GROUP_WIDTH, :],
                                      sem_out.at[slot, g])
                for g in range(SSD_GROUPS)]

    def start_all(copies):
        for c in copies:
            c.start()

    def wait_all(copies):
        for c in copies:
            c.wait()

    def compute(b, slot):
        rsl = pl.ds(pl.multiple_of(b * t, t), t)
        dtc = dt_ref[rsl, :]
        acs = dtc * a_neg
        for sh in (1, 2, 4):
            acs = acs + jnp.where(ridx >= sh, pltpu.roll(acs, sh, axis=0), 0.0)
        tot = acs[t - 1:t, :]
        x = xs_ref[rsl, :]
        bc = bc_ref[rsl, :]
        bm, cm = bc[:, :GROUP_WIDTH], bc[:, GROUP_WIDTH:]
        bm_r, cm_r = bm.astype(BF16).astype(F32), cm.astype(BF16).astype(F32)
        gs, ps = [], []
        for k in range(t):
            gs.append(jnp.exp(jnp.where(ridx >= k, acs - acs[k:k + 1, :], NEG_BIG)) * dtc[k:k + 1, :])
            ps.append(cm_r * bm_r[k:k + 1, :])
        cb_heads = _dot(jnp.concatenate(ps, axis=0).astype(BF16), segsum_ref[...])
        per_head = jnp.concatenate([jnp.concatenate(gs, axis=0) * cb_heads, jnp.exp(acs),
                                    dtc * jnp.exp(tot - acs)], axis=0)
        hi = per_head.astype(BF16)
        lo = (per_head - hi.astype(F32)).astype(BF16)
        wide = _dot(hi, expand_ref[...]) + _dot(lo, expand_ref[...])
        y = dexp_ref[...] * x
        for k in range(t):
            y = y + wide[k * t:(k + 1) * t, :] * x[k:k + 1, :]
        ea_wide = wide[t * t:t * t + t, :]
        xw = x * wide[t * t + t:, :]
        for g in range(SSD_GROUPS):
            gsl = slice(g * SSD_STATE, (g + 1) * SSD_STATE)
            wsl = slice(g * GROUP_WIDTH, (g + 1) * GROUP_WIDTH)
            hg = hbuf[slot, wsl, :]
            z_g = lax.dot_general(cm[:, gsl].astype(BF16), hg.astype(BF16), _NT, preferred_element_type=F32)
            y_scr[rsl, wsl] = y[:, wsl] + ea_wide[:, wsl] * z_g
            upd = lax.dot_general(xw[:, wsl].astype(BF16), bm[:, gsl].astype(BF16), _TN, preferred_element_type=F32)
            for r8 in range(SSD_HEADS // SSD_GROUPS):
                r = g * (SSD_HEADS // SSD_GROUPS) + r8
                rows_r = slice(r * SSD_HEAD_DIM, (r + 1) * SSD_HEAD_DIM)
                obuf[slot, rows_r, :] = (hbuf[slot, rows_r, :] * cd_ref[base + b, r]
                                         + upd[r8 * SSD_HEAD_DIM:(r8 + 1) * SSD_HEAD_DIM, :])

    _ring_pipeline(nb, SSD_PIPE_SLOTS,
                   lambda b, slot: start_all(in_copies(b, slot)), lambda b, slot: wait_all(in_copies(b, slot)),
                   compute,
                   lambda b, slot: start_all(out_copies(b, slot)), lambda b, slot: wait_all(out_copies(b, slot)))
    yz = y_scr[...] * _silu(zs_ref[...].astype(F32))
    yssd_ref[...] = _rms(yz, ssdnw_ref[...]).astype(BF16)


def _ssd_state(xs, bc, dts, main, cd, hin, expand, segsum, alog, dexp, ssdnw):
    nb, t = SAMPLE_BLOCK, SUBLANES
    rows = nb * t
    m = xs.shape[0]
    nseq = m // t
    col_spec = lambda width, idx: pl.BlockSpec((rows, width), lambda i: (i, idx))
    const_spec = lambda a: pl.BlockSpec(a.shape, lambda i: (0,) * a.ndim)
    return pl.pallas_call(
        _ssd_state_kernel,
        grid=(nseq // nb,),
        in_specs=[col_spec(SSD_WIDTH, 0), col_spec(2 * GROUP_WIDTH, 0), col_spec(LANES, 0),
                  col_spec(SSD_WIDTH, _col_block(COL_ZS)),
                  pl.BlockSpec(memory_space=pltpu.SMEM), pl.BlockSpec(memory_space=pl.ANY),
                  const_spec(expand), const_spec(segsum), const_spec(alog), const_spec(dexp), const_spec(ssdnw)],
        out_specs=[col_spec(SSD_WIDTH, 0), pl.BlockSpec(memory_space=pl.ANY)],
        out_shape=[jax.ShapeDtypeStruct((m, SSD_WIDTH), BF16), jax.ShapeDtypeStruct(hin.shape, F32)],
        scratch_shapes=[pltpu.VMEM((SSD_PIPE_SLOTS, SSD_WIDTH, SSD_STATE), F32),
                        pltpu.VMEM((SSD_PIPE_SLOTS, SSD_WIDTH, SSD_STATE), F32),
                        pltpu.VMEM((rows, SSD_WIDTH), F32),
                        pltpu.SemaphoreType.DMA((SSD_PIPE_SLOTS, SSD_GROUPS)),
                        pltpu.SemaphoreType.DMA((SSD_PIPE_SLOTS, SSD_GROUPS))],
        compiler_params=pltpu.CompilerParams(dimension_semantics=("arbitrary",), vmem_limit_bytes=VMEM_LIMIT),
        name="ssd_state",
    )(xs, bc, dts, main, cd, hin, expand, segsum, alog, dexp, ssdnw)


ATT_ITEM = 2


def _att_state_kernel(q_ref, za_ref, k_hbm, v_hbm, yatt_ref, kbuf, vbuf, q_scr, att_scr, sem):
    nb, t = SAMPLE_BLOCK, SUBLANES
    base = pl.program_id(0) * nb
    scale = ATT_HEAD_DIM ** -0.5
    q_scr[...] = q_ref[...].astype(F32)
    head_of_lane = lax.broadcasted_iota(jnp.int32, (1, D_MODEL), 1) // ATT_HEAD_DIM

    def copies(item, slot):
        out = []
        for j in range(ATT_ITEM):
            for kv, (src, buf) in enumerate(((k_hbm, kbuf), (v_hbm, vbuf))):
                for hd in range(ATT_HEADS):
                    hsl = slice(hd * ATT_HEAD_DIM, (hd + 1) * ATT_HEAD_DIM)
                    out.append(pltpu.make_async_copy(
                        src.at[base + item * ATT_ITEM + j, :, hd, :], buf.at[slot, j, :, hsl],
                        sem.at[slot, (j * 2 + kv) * ATT_HEADS + hd]))
        return out

    def start_in(item, slot):
        for c in copies(item, slot):
            c.start()

    def wait_in(item, slot):
        for c in copies(item, slot):
            c.wait()

    def compute(item, slot):
        for j in range(ATT_ITEM):
            rsl = pl.ds(pl.multiple_of((item * ATT_ITEM + j) * t, t), t)
            qf = q_scr[rsl, :]
            q_bd = jnp.concatenate([jnp.where(head_of_lane == hd, qf, 0.0) for hd in range(ATT_HEADS)],
                                   axis=0).astype(BF16)
            sc = lax.dot_general(q_bd, kbuf[slot, j].astype(BF16), _NT, preferred_element_type=F32) * scale
            o = _dot(_softmax_rows(sc).astype(BF16), vbuf[slot, j].astype(BF16))
            att_scr[rsl, :] = jnp.concatenate(
                [o[hd * t:(hd + 1) * t, hd * ATT_HEAD_DIM:(hd + 1) * ATT_HEAD_DIM] for hd in range(ATT_HEADS)], axis=1)

    _ring_pipeline(nb // ATT_ITEM, ATT_PIPE_SLOTS, start_in, wait_in, compute)
    yatt_ref[...] = (att_scr[...] * _silu(za_ref[...].astype(F32))).astype(BF16)


def _att_state(main, k, v):
    nb, t = SAMPLE_BLOCK, SUBLANES
    rows = nb * t
    m = main.shape[0]
    col_spec = lambda width, idx: pl.BlockSpec((rows, width), lambda i: (i, idx))
    kv_buf = pltpu.VMEM((ATT_PIPE_SLOTS, ATT_ITEM, MEM_LEN, D_MODEL), F32)
    return pl.pallas_call(
        _att_state_kernel,
        grid=(m // rows,),
        in_specs=[col_spec(D_MODEL, _col_block(COL_Q)), col_spec(D_MODEL, _col_block(COL_ZA)),
                  pl.BlockSpec(memory_space=pl.ANY), pl.BlockSpec(memory_space=pl.ANY)],
        out_specs=col_spec(D_MODEL, 0),
        out_shape=jax.ShapeDtypeStruct((m, D_MODEL), BF16),
        scratch_shapes=[kv_buf, kv_buf, pltpu.VMEM((rows, D_MODEL), F32), pltpu.VMEM((rows, D_MODEL), F32),
                        pltpu.SemaphoreType.DMA((ATT_PIPE_SLOTS, ATT_ITEM * 2 * ATT_HEADS))],
        compiler_params=pltpu.CompilerParams(dimension_semantics=("arbitrary",), vmem_limit_bytes=VMEM_LIMIT),
        name="att_state",
    )(main, main, k, v)


DENSE_ROWS = 512


def _dense_kernel(x_ref, gt_ref, yp_ref, ys_ref, ya_ref, wpo_ref, wso_ref, wao_ref, wo_ref, fnw_ref, y_ref):
    gates = gt_ref[...].astype(F32)
    merged = (gates[:, 0:D_MODEL] * _dot(yp_ref[...], wpo_ref[...])
              + gates[:, D_MODEL:2 * D_MODEL] * _dot(ys_ref[...], wso_ref[...])
              + gates[:, 2 * D_MODEL:] * _dot(ya_ref[...], wao_ref[...]))
    x_out = x_ref[...] + _dot(merged.astype(BF16), wo_ref[...])
    y_ref[...] = _rms(x_out, fnw_ref[...])


def _dense(x2d, gates, gate_idx, yp, ys, ya, wpo, wso, wao, wo, fnw):
    m = x2d.shape[0]
    row = lambda width, idx=0: pl.BlockSpec((DENSE_ROWS, width), lambda i: (i, idx))
    resident = lambda a: pl.BlockSpec(a.shape, lambda i: (0,) * a.ndim, pipeline_mode=pl.Buffered(1))
    return pl.pallas_call(
        _dense_kernel,
        grid=(m // DENSE_ROWS,),
        in_specs=[row(D_MODEL), row(3 * D_MODEL, gate_idx), row(D_MODEL), row(SSD_WIDTH), row(D_MODEL),
                  resident(wpo), resident(wso), resident(wao), resident(wo), resident(fnw)],
        out_specs=row(D_MODEL),
        out_shape=jax.ShapeDtypeStruct((m, D_MODEL), F32),
        compiler_params=pltpu.CompilerParams(dimension_semantics=("arbitrary",), vmem_limit_bytes=VMEM_LIMIT),
        name="dense",
    )(x2d, gates, yp, ys, ya, wpo, wso, wao, wo, fnw)


def kernel(x_prompt, x_sample, mem_prompt, state_pool, state_conv, state_ssm, cache_mem_k, cache_mem_v,
           norm_w, w_in, w_pool_grp, pool_scale, conv_w, conv_b, dt_bias, a_log, d_skip, ssd_norm_w,
           mem_norm_w, w_mem_k, w_mem_v, w_pool_out, w_ssd_out, w_att_out, w_out, final_norm_w):
    assert w_in.shape[0] == 1
    bp, sp, d = x_prompt.shape
    bs, ss, _ = x_sample.shape
    assert ss == SUBLANES and sp % PROMPT_TILE == 0 and bs % SAMPLE_BLOCK == 0

    w_a, w_b, w_dt = _wprep(w_in)
    nw = norm_w[0].reshape(1, d)
    pad_heads = lambda a: jnp.pad(a.reshape(1, SSD_HEADS), ((0, 0), (0, LANES - SSD_HEADS)))
    wgrp = w_pool_grp[0].astype(BF16)
    pscale = pool_scale[0].reshape(1, d)
    convb = conv_b[0].reshape(1, CONV_DIM)
    dtb, alog = pad_heads(dt_bias[0]), pad_heads(a_log[0])
    dexp = jnp.repeat(d_skip[0], SSD_HEAD_DIM).reshape(1, SSD_WIDTH)
    ssdnw = ssd_norm_w[0].reshape(1, SSD_WIDTH)
    dense_w = (w_pool_out[0].astype(BF16), w_ssd_out[0].astype(BF16), w_att_out[0].astype(BF16),
               w_out[0].astype(BF16), final_norm_w.reshape(1, d))
    head_of_lane = jnp.arange(SSD_WIDTH) // SSD_HEAD_DIM
    expand = (jnp.arange(LANES)[:, None] == head_of_lane[None, :]).astype(BF16)
    group_of_head = jnp.where(jnp.arange(LANES) < SSD_HEADS, jnp.arange(LANES) // (SSD_HEADS // SSD_GROUPS), -1)
    segsum = ((jnp.arange(GROUP_WIDTH) // SSD_STATE)[:, None] == group_of_head[None, :]).astype(BF16)

    mk, mv, mkb, mvb = _memkv(mem_prompt, mem_norm_w[0].reshape(1, d), w_mem_k[0].astype(BF16),
                              w_mem_v[0].astype(BF16))
    xp2 = x_prompt.reshape(bp * sp, d)
    gates_p, yp, ysd, ya, pool_p, conv_p, ssm_p = _seq_prompt(
        xp2, nw, w_a, w_b, w_dt, mkb, mvb, (wgrp, pscale, conv_w[0], convb, dtb, alog, dexp, ssdnw),
        nseq=bp, ntile=sp // PROMPT_TILE)
    y_prompt = _dense(xp2, gates_p, 0, yp, ysd, ya, *dense_w).reshape(bp, sp, d)

    xs2 = x_sample.reshape(bs * ss, d)
    main_s, dt_s = _inproj(xs2, nw, w_a, w_b, w_dt)
    yp, xs, bc, dts, cd, pool_s, conv_s = _state_pre(
        main_s, dt_s, state_pool[0], state_conv[0], (wgrp, pscale, conv_w[0], convb, dtb, alog))
    ysd, ssm_s = _ssd_state(xs, bc, dts, main_s, cd, state_ssm[0].reshape(bs, SSD_WIDTH, SSD_STATE),
                            expand, segsum, alog, dexp, ssdnw)
    ya = _att_state(main_s, cache_mem_k[0], cache_mem_v[0])
    y_sample = _dense(xs2, main_s, _col_block(COL_GATES), yp, ysd, ya, *dense_w).reshape(bs, ss, d)

    ssm_shape = (SSD_GROUPS, SSD_HEADS // SSD_GROUPS, SSD_HEAD_DIM, SSD_STATE)
    return (y_prompt, y_sample,
            pool_p[None], conv_p[None], ssm_p.reshape((1, bp) + ssm_shape),
            mk[None], mv[None],
            pool_s[None], conv_s[None], ssm_s.reshape((1, bs) + ssm_shape))
```

```python
import functools

import jax
import jax.numpy as jnp
from jax import lax
from jax.experimental import pallas as pl
from jax.experimental.pallas import tpu as pltpu

F32 = jnp.float32
BF16 = jnp.bfloat16

D_MODEL = 1024
POOL_WINDOWS = (2, 4, 8, 16)
POOL_GROUP = 256
POOL_HIST = 15
POOL_PAD = 16
SSD_WIDTH = 2048
SSD_HEADS = 32
SSD_HEAD_DIM = 64
SSD_GROUPS = 4
SSD_STATE = 128
GROUP_WIDTH = SSD_WIDTH // SSD_GROUPS
CONV_WIDTH = 4
CONV_DIM = 3072
CONV_PAD = 8
SSD_CHUNK = 128
MEM_LEN = 256
ATT_HEADS = 4
ATT_HEAD_DIM = 256
PAST_LEN = 16384
EPS = 1e-6
NEG_BIG = -1e30
SUBLANES = 8
LANES = 128
MAIN_COLS = 12288
COL_XBC, COL_GATES, COL_ZS = (0, 3072), (3072, 6144), (6144, 8192)
COL_U, COL_ZP, COL_Q, COL_ZA = (8192, 9216), (9216, 10240), (10240, 11264), (11264, 12288)
W_SPLIT = (7168, 7200)
W_SRC = {COL_U: (0, 0), COL_ZP: (0, 1024), COL_ZS: (0, 2048), COL_XBC: (0, 4096),
         COL_Q: (1, 0), COL_ZA: (1, 1024), COL_GATES: (1, 2048)}
VMEM_LIMIT = 56 * 1024 * 1024
SEQ_PROMPT_VMEM_LIMIT = 60 * 1024 * 1024

_NT = (((1,), (1,)), ((), ()))
_TN = (((0,), (0,)), ((), ()))


def _sigmoid(x):
    return 1.0 / (1.0 + jnp.exp(-x))


def _silu(x):
    return x * _sigmoid(x)


def _softplus(x):
    return jnp.maximum(x, 0.0) + jnp.log1p(jnp.exp(-jnp.abs(x)))


def _rms(x, w):
    return x * lax.rsqrt(jnp.mean(x * x, axis=-1, keepdims=True) + EPS) * w


def _dot(a, b):
    return jnp.dot(a, b, preferred_element_type=F32)


def _softmax_rows(sc):
    e = jnp.exp(sc - jnp.max(sc, axis=-1, keepdims=True))
    return e / jnp.sum(e, axis=-1, keepdims=True)


def _weight_cols(w_refs, piece, lo, hi):
    idx, c0 = W_SRC[piece]
    return w_refs[idx][:, c0 + lo:c0 + hi]


def _memkv_kernel(mem_ref, nw_ref, wk_ref, wv_ref, k_ref, v_ref, kb_ref, vb_ref):
    mh = _rms(mem_ref[0], nw_ref[...]).astype(BF16)
    k = _dot(mh, wk_ref[...])
    v = _dot(mh, wv_ref[...])
    for hd in range(ATT_HEADS):
        hsl = slice(hd * ATT_HEAD_DIM, (hd + 1) * ATT_HEAD_DIM)
        k_ref[0, :, hd, :] = k[:, hsl]
        v_ref[0, :, hd, :] = v[:, hsl]
        kb_ref[0, hd] = k[:, hsl].astype(BF16)
        vb_ref[0, hd] = v[:, hsl].astype(BF16)


def _memkv(mem, nw, wk, wv):
    b, m, d = mem.shape
    full = lambda shape: pl.BlockSpec(shape, lambda i: (0,) * len(shape))
    blk = pl.BlockSpec((1, m, d), lambda i: (i, 0, 0))
    oblk = pl.BlockSpec((1, m, ATT_HEADS, ATT_HEAD_DIM), lambda i: (i, 0, 0, 0))
    hblk = pl.BlockSpec((1, ATT_HEADS, m, ATT_HEAD_DIM), lambda i: (i, 0, 0, 0))
    return pl.pallas_call(
        _memkv_kernel,
        grid=(b,),
        in_specs=[blk, full((1, d)), full((d, d)), full((d, d))],
        out_specs=[oblk, oblk, hblk, hblk],
        out_shape=[jax.ShapeDtypeStruct((b, m, ATT_HEADS, ATT_HEAD_DIM), F32)] * 2
        + [jax.ShapeDtypeStruct((b, ATT_HEADS, m, ATT_HEAD_DIM), BF16)] * 2,
        compiler_params=pltpu.CompilerParams(dimension_semantics=("arbitrary",), vmem_limit_bytes=VMEM_LIMIT),
        name="memkv",
    )(mem, nw, wk, wv)


INPROJ_ROWS = 256
INPROJ_COL_CHUNK = 1024


def _inproj_kernel(x_ref, nw_ref, wa_ref, wb_ref, wdt_ref, main_ref, dt_ref):
    h = _rms(x_ref[...], nw_ref[...]).astype(BF16)
    dt_ref[...] = _dot(h, wdt_ref[...])
    for piece in W_SRC:
        for lo in range(0, piece[1] - piece[0], INPROJ_COL_CHUNK):
            val = _dot(h, _weight_cols((wa_ref, wb_ref), piece, lo, lo + INPROJ_COL_CHUNK))
            if piece == COL_GATES:
                val = _sigmoid(val)
            main_ref[:, piece[0] + lo:piece[0] + lo + INPROJ_COL_CHUNK] = val.astype(BF16)


def _inproj(x2d, nw, w_a, w_b, w_dt):
    m = x2d.shape[0]
    resident = lambda a: pl.BlockSpec(a.shape, lambda i: (0,) * a.ndim, pipeline_mode=pl.Buffered(1))
    return pl.pallas_call(
        _inproj_kernel,
        grid=(m // INPROJ_ROWS,),
        in_specs=[pl.BlockSpec((INPROJ_ROWS, D_MODEL), lambda i: (i, 0)),
                  resident(nw), resident(w_a), resident(w_b), resident(w_dt)],
        out_specs=[
            pl.BlockSpec((INPROJ_ROWS, MAIN_COLS), lambda i: (i, 0)),
            pl.BlockSpec((INPROJ_ROWS, LANES), lambda i: (i, 0)),
        ],
        out_shape=[jax.ShapeDtypeStruct((m, MAIN_COLS), BF16), jax.ShapeDtypeStruct((m, LANES), F32)],
        compiler_params=pltpu.CompilerParams(dimension_semantics=("arbitrary",), vmem_limit_bytes=VMEM_LIMIT),
        name="inproj",
    )(x2d, nw, w_a, w_b, w_dt)


def _pool_branch(pext, u, pos, wgrp_ref, pscale_ref, zp):
    nb, t, _ = u.shape
    ys = []
    for g, w in enumerate(POOL_WINDOWS):
        cols = slice(g * POOL_GROUP, (g + 1) * POOL_GROUP)
        win = pext[:, :, cols]
        for sh in [1 << e for e in range(g + 1)]:
            win = win + pltpu.roll(win, sh, axis=1)
        win = win[:, POOL_PAD:, :]
        inv_cnt = 1.0 / jnp.minimum(w, pos + 1).astype(F32)
        d = (win * inv_cnt - u[:, :, cols]).astype(BF16).reshape(nb * t, POOL_GROUP)
        ys.append(_dot(d, wgrp_ref[g]))
    return jnp.concatenate(ys, axis=1) * pscale_ref[...] * _silu(zp)


def _conv_branch(cext, convw_ref, convb_ref, store):
    for cc in range(CONV_DIM // GROUP_WIDTH):
        csl = slice(cc * GROUP_WIDTH, (cc + 1) * GROUP_WIDTH)
        ext = cext[:, :, csl]
        conv = convb_ref[:, csl].reshape(1, 1, GROUP_WIDTH)
        for kk in range(CONV_WIDTH):
            tap = ext if kk == CONV_WIDTH - 1 else pltpu.roll(ext, CONV_WIDTH - 1 - kk, axis=1)
            conv = conv + tap * convw_ref[kk:kk + 1, csl].reshape(1, 1, GROUP_WIDTH)
        store(cc, _silu(conv[:, CONV_PAD:, :]))


PROMPT_TILE = 256


def _seq_prompt_kernel(ntile, x_ref, nw_ref, wa_ref, wb_ref, wdt_ref, k_ref, v_ref,
                       wgrp_ref, pscale_ref, convw_ref, convb_ref, dtb_ref, alog_ref, dexp_ref, ssdnw_ref,
                       gates_ref, ypool_ref, yssd_ref, yatt_ref, pool_o_ref, conv_o_ref, ssm_o_ref,
                       pext, cext, xs_scr, b_scr, c_scr, dt_scr, y_scr, h_scr, zs_scr, q_scr, za_scr):
    t, q = PROMPT_TILE, SSD_CHUNK
    k = pl.program_id(0)
    s = k % ntile
    last = ntile - 1

    @pl.when(k == 0)
    def _():
        y_scr[...] = jnp.zeros(y_scr.shape, F32)
        zs_scr[...] = jnp.zeros(zs_scr.shape, BF16)
        q_scr[...] = jnp.zeros(q_scr.shape, BF16)
        za_scr[...] = jnp.zeros(za_scr.shape, BF16)

    @pl.when(s == 0)
    def _():
        pext[:, 0:POOL_PAD, :] = jnp.zeros((1, POOL_PAD, D_MODEL), F32)
        cext[:, 0:CONV_PAD, :] = jnp.zeros((1, CONV_PAD, CONV_DIM), F32)
        h_scr[...] = jnp.zeros(h_scr.shape, F32)

    @pl.when(s > 0)
    def _():
        carry_p = pext[:, t:t + POOL_PAD, :]
        carry_c = cext[:, t:t + CONV_PAD, :]
        pext[:, 0:POOL_PAD, :] = carry_p
        cext[:, 0:CONV_PAD, :] = carry_c

    yz = y_scr[...] * zs_scr[...].astype(F32)
    yssd_ref[...] = _rms(yz, ssdnw_ref[...]).astype(BF16)
    scale = ATT_HEAD_DIM ** -0.5
    outs = []
    for hd in range(ATT_HEADS):
        hsl = slice(hd * ATT_HEAD_DIM, (hd + 1) * ATT_HEAD_DIM)
        p = _softmax_rows(lax.dot_general(q_scr[:, hsl], k_ref[0, hd], _NT, preferred_element_type=F32) * scale)
        outs.append(_dot(p.astype(BF16), v_ref[0, hd]))
    yatt_ref[...] = (jnp.concatenate(outs, axis=1) * za_scr[...].astype(F32)).astype(BF16)

    hn = _rms(x_ref[...], nw_ref[...]).astype(BF16)

    def proj(piece, lo=0, hi=None):
        return _dot(hn, _weight_cols((wa_ref, wb_ref), piece, lo, piece[1] - piece[0] if hi is None else hi))

    def gates_piece(c):
        def run():
            gates_ref[:, c * D_MODEL:(c + 1) * D_MODEL] = _sigmoid(
                proj(COL_GATES, c * D_MODEL, (c + 1) * D_MODEL).astype(BF16))
        return run

    def zs_piece(c):
        def run():
            zs_scr[:, c * D_MODEL:(c + 1) * D_MODEL] = _silu(proj(COL_ZS, c * D_MODEL, (c + 1) * D_MODEL).astype(BF16))
        return run

    def q_piece():
        q_scr[...] = proj(COL_Q).astype(BF16)

    def za_piece():
        za_scr[...] = _silu(proj(COL_ZA).astype(BF16))

    fillers = [gates_piece(0), gates_piece(1), gates_piece(2), zs_piece(0), zs_piece(1), q_piece, za_piece]

    def run_filler():
        if fillers:
            fillers.pop(0)()

    u = proj(COL_U).reshape(1, t, D_MODEL)
    pext[:, POOL_PAD:, :] = u
    pos = s * t + lax.broadcasted_iota(jnp.int32, (1, t, 1), 1)
    ypool_ref[...] = _pool_branch(pext, u, pos, wgrp_ref, pscale_ref, proj(COL_ZP)).astype(BF16)

    for c in range(CONV_DIM // D_MODEL):
        cext[:, CONV_PAD:, c * D_MODEL:(c + 1) * D_MODEL] = proj(
            COL_XBC, c * D_MODEL, (c + 1) * D_MODEL).reshape(1, t, D_MODEL)
    dt_scr[...] = _softplus(_dot(hn, wdt_ref[...]) + dtb_ref[...])

    def store_conv(cc, val):
        if cc < SSD_GROUPS:
            xs_scr[:, cc * GROUP_WIDTH:(cc + 1) * GROUP_WIDTH] = val[0]
        elif cc == SSD_GROUPS:
            b_scr[...] = val[0]
        else:
            c_scr[...] = val[0]
        run_filler()

    _conv_branch(cext, convw_ref, convb_ref, store_conv)

    a_neg = -jnp.exp(alog_ref[...])
    rq = lax.broadcasted_iota(jnp.int32, (q, q), 0)
    cq = lax.broadcasted_iota(jnp.int32, (q, q), 1)
    tril = rq >= cq
    tri_f = tril.astype(F32)
    lane_lo = lax.broadcasted_iota(jnp.int32, (1, LANES), 1) < SSD_HEAD_DIM
    pairs_per_group = SSD_HEADS // SSD_GROUPS // 2

    def chunk(c):
        rsl = slice(c * q, (c + 1) * q)
        dtc = dt_scr[rsl, :]
        acs = jnp.dot(tri_f, dtc * a_neg, precision=lax.Precision.HIGHEST, preferred_element_type=F32)
        acs_t = acs.T
        dt_t = dtc.T
        wdec_t = (dt_t * jnp.exp(acs_t[:, q - 1:q] - acs_t)).astype(BF16)
        row_t = acs_t - jnp.log(dt_t)
        cdec = jnp.exp(acs[q - 1:q, :])
        for g in range(SSD_GROUPS):
            gsl = slice(g * SSD_STATE, (g + 1) * SSD_STATE)
            bg = b_scr[rsl, gsl]
            cg_b = c_scr[rsl, gsl].astype(BF16)
            cb = lax.dot_general(cg_b, bg.astype(BF16), _NT, preferred_element_type=F32).astype(BF16)
            bg_t = bg.T.astype(BF16)
            hsl = slice(g * GROUP_WIDTH, (g + 1) * GROUP_WIDTH)
            z_g = _dot(cg_b, h_scr[:, hsl].astype(BF16))
            for jp in range(pairs_per_group):
                j = g * pairs_per_group + jp
                lsl = slice(j * LANES, (j + 1) * LANES)
                xp = xs_scr[rsl, lsl]
                xp_b = xp.astype(BF16)
                zero_b = jnp.zeros_like(xp_b)
                x_bd = jnp.concatenate([jnp.where(lane_lo, xp_b, zero_b), jnp.where(lane_lo, zero_b, xp_b)],
                                       axis=0)
                ms, bws, cols = [], [], []
                for hh in range(2):
                    r = 2 * j + hh
                    cols.append(jnp.broadcast_to(acs[:, r:r + 1], (q, q)))
                    seg = cols[hh] - row_t[r:r + 1, :]
                    ms.append(cb * jnp.exp(jnp.where(tril, seg, NEG_BIG)).astype(BF16))
                    bws.append(bg_t * wdec_t[r:r + 1, :])
                ea_pair = jnp.exp(jnp.where(lane_lo, cols[0], cols[1]))
                y = (_dot(jnp.concatenate(ms, axis=1), x_bd)
                     + ea_pair * z_g[:, jp * LANES:(jp + 1) * LANES] + dexp_ref[:, lsl] * xp)
                y_scr[rsl, lsl] = y
                cd_pair = jnp.where(lane_lo, cdec[:, 2 * j:2 * j + 1], cdec[:, 2 * j + 1:2 * j + 2])
                h_scr[:, lsl] = h_scr[:, lsl] * cd_pair + _dot(jnp.concatenate(bws, axis=1), x_bd)
            run_filler()

    for c in range(t // q):
        chunk(c)
    while fillers:
        run_filler()

    @pl.when(jnp.logical_and(s == last, k < pl.num_programs(0) - 1))
    def _():
        pool_o_ref[...] = pext[:, t + POOL_PAD - POOL_HIST:t + POOL_PAD, :]
        conv_o_ref[...] = cext[:, t + CONV_PAD - (CONV_WIDTH - 1):t + CONV_PAD, :]
        for j in range(SSD_HEADS // 2):
            lsl = slice(j * LANES, (j + 1) * LANES)
            ssm_o_ref[0, lsl, :] = h_scr[:, lsl].T


def _seq_prompt(x2d, nw, w_a, w_b, w_dt, kb, vb, params, *, nseq, ntile):
    t = PROMPT_TILE
    m = x2d.shape[0]
    ntiles = nseq * ntile
    tile = lambda k: jnp.minimum(k, ntiles - 1)
    closed = lambda k: jnp.maximum(k - 1, 0)
    own_rows = lambda width: pl.BlockSpec((t, width), lambda k: (k, 0))
    closed_rows = lambda width: pl.BlockSpec((t, width), lambda k: (closed(k), 0))
    seq_spec = lambda shape, which: pl.BlockSpec((1,) + shape, lambda k: (which(k) // ntile,) + (0,) * len(shape))
    const_spec = lambda a: pl.BlockSpec(a.shape, lambda k: (0,) * a.ndim)
    resident = lambda a: pl.BlockSpec(a.shape, lambda k: (0,) * a.ndim, pipeline_mode=pl.Buffered(1))
    in_specs = [
        pl.BlockSpec((t, D_MODEL), lambda k: (tile(k), 0)),
        resident(nw), resident(w_a), resident(w_b), resident(w_dt),
        seq_spec((ATT_HEADS, MEM_LEN, ATT_HEAD_DIM), closed),
        seq_spec((ATT_HEADS, MEM_LEN, ATT_HEAD_DIM), closed),
    ] + [const_spec(p) for p in params]
    out_specs = [
        own_rows(3 * D_MODEL), own_rows(D_MODEL), closed_rows(SSD_WIDTH), closed_rows(D_MODEL),
        seq_spec((POOL_HIST, D_MODEL), tile),
        seq_spec((CONV_WIDTH - 1, CONV_DIM), tile),
        seq_spec((SSD_WIDTH, SSD_STATE), tile),
    ]
    out_shape = [
        jax.ShapeDtypeStruct((m + t, 3 * D_MODEL), BF16),
        jax.ShapeDtypeStruct((m + t, D_MODEL), BF16),
        jax.ShapeDtypeStruct((m, SSD_WIDTH), BF16),
        jax.ShapeDtypeStruct((m, D_MODEL), BF16),
        jax.ShapeDtypeStruct((nseq, POOL_HIST, D_MODEL), F32),
        jax.ShapeDtypeStruct((nseq, CONV_WIDTH - 1, CONV_DIM), F32),
        jax.ShapeDtypeStruct((nseq, SSD_WIDTH, SSD_STATE), F32),
    ]
    scratch = [
        pltpu.VMEM((1, POOL_PAD + t, D_MODEL), F32),
        pltpu.VMEM((1, CONV_PAD + t, CONV_DIM), F32),
        pltpu.VMEM((t, SSD_WIDTH), F32),
        pltpu.VMEM((t, GROUP_WIDTH), F32),
        pltpu.VMEM((t, GROUP_WIDTH), F32),
        pltpu.VMEM((t, LANES), F32),
        pltpu.VMEM((t, SSD_WIDTH), F32),
        pltpu.VMEM((SSD_STATE, SSD_WIDTH), F32),
        pltpu.VMEM((t, SSD_WIDTH), BF16),
        pltpu.VMEM((t, D_MODEL), BF16),
        pltpu.VMEM((t, D_MODEL), BF16),
    ]
    return pl.pallas_call(
        functools.partial(_seq_prompt_kernel, ntile),
        grid=(ntiles + 1,),
        in_specs=in_specs,
        out_specs=out_specs,
        out_shape=out_shape,
        scratch_shapes=scratch,
        compiler_params=pltpu.CompilerParams(dimension_semantics=("arbitrary",),
                                             vmem_limit_bytes=SEQ_PROMPT_VMEM_LIMIT),
        name="seq_prompt",
    )(x2d, nw, w_a, w_b, w_dt, kb, vb, *params)


SAMPLE_BLOCK = 32
STATE_PRE_SUB = 4


def _state_pre_kernel(u_ref, zp_ref, xbc_ref, dt_ref, ph_ref, ch_ref,
                      wgrp_ref, pscale_ref, convw_ref, convb_ref, dtb_ref, alog_ref,
                      ypool_ref, xs_ref, bc_ref, dts_ref, cd_ref, pool_o_ref, conv_o_ref, pext, cext):
    nb, t = SAMPLE_BLOCK, SUBLANES
    pext[:, 0:1, :] = jnp.zeros((nb, 1, D_MODEL), F32)
    pext[:, 1:POOL_PAD, :] = ph_ref[...]
    cext[:, 0:CONV_PAD - (CONV_WIDTH - 1), :] = jnp.zeros((nb, CONV_PAD - (CONV_WIDTH - 1), CONV_DIM), F32)
    cext[:, CONV_PAD - (CONV_WIDTH - 1):CONV_PAD, :] = ch_ref[...]

    pext[:, POOL_PAD:, :] = u_ref[...].astype(F32).reshape(nb, t, D_MODEL)
    cext[:, CONV_PAD:, :] = xbc_ref[...].astype(F32).reshape(nb, t, CONV_DIM)
    pos = PAST_LEN + lax.broadcasted_iota(jnp.int32, (1, t, 1), 1)
    for sb in range(nb // STATE_PRE_SUB):
        seqs = slice(sb * STATE_PRE_SUB, (sb + 1) * STATE_PRE_SUB)
        rsl = slice(sb * STATE_PRE_SUB * t, (sb + 1) * STATE_PRE_SUB * t)
        pv, cv = pext.at[seqs], cext.at[seqs]
        ypool_ref[rsl, :] = _pool_branch(pv, pv[:, POOL_PAD:, :], pos, wgrp_ref, pscale_ref,
                                         zp_ref[rsl, :].astype(F32)).astype(BF16)

        def store_conv(cc, val, rsl=rsl):
            val = val.reshape(STATE_PRE_SUB * t, GROUP_WIDTH)
            if cc < SSD_GROUPS:
                xs_ref[rsl, cc * GROUP_WIDTH:(cc + 1) * GROUP_WIDTH] = val
            else:
                bc_ref[rsl, (cc - SSD_GROUPS) * GROUP_WIDTH:(cc - SSD_GROUPS + 1) * GROUP_WIDTH] = val

        _conv_branch(cv, convw_ref, convb_ref, store_conv)
    dt = _softplus(dt_ref[...] + dtb_ref[...])
    dts_ref[...] = dt
    a = dt * -jnp.exp(alog_ref[...])
    cd_ref[...] = jnp.exp(jnp.sum(a.reshape(nb, t, LANES), axis=1))
    pool_o_ref[...] = pext[:, t + POOL_PAD - POOL_HIST:t + POOL_PAD, :]
    conv_o_ref[...] = cext[:, t + CONV_PAD - (CONV_WIDTH - 1):t + CONV_PAD, :]


def _col_block(piece):
    return piece[0] // (piece[1] - piece[0])


def _state_pre(main, dt, state_pool, state_conv, params):
    nb, t = SAMPLE_BLOCK, SUBLANES
    rows = nb * t
    m = main.shape[0]
    nseq = m // t
    col_spec = lambda width, idx: pl.BlockSpec((rows, width), lambda i: (i, idx))
    seq_spec = lambda shape: pl.BlockSpec((nb,) + shape, lambda i: (i,) + (0,) * len(shape))
    const_spec = lambda a: pl.BlockSpec(a.shape, lambda i: (0,) * a.ndim)
    return pl.pallas_call(
        _state_pre_kernel,
        grid=(nseq // nb,),
        in_specs=[col_spec(D_MODEL, _col_block(COL_U)), col_spec(D_MODEL, _col_block(COL_ZP)),
                  col_spec(CONV_DIM, _col_block(COL_XBC)), col_spec(LANES, 0),
                  seq_spec((POOL_HIST, D_MODEL)), seq_spec((CONV_WIDTH - 1, CONV_DIM))]
        + [const_spec(p) for p in params],
        out_specs=[col_spec(D_MODEL, 0), col_spec(SSD_WIDTH, 0), col_spec(2 * GROUP_WIDTH, 0), col_spec(LANES, 0),
                   pl.BlockSpec((nb, LANES), lambda i: (i, 0)),
                   seq_spec((POOL_HIST, D_MODEL)), seq_spec((CONV_WIDTH - 1, CONV_DIM))],
        out_shape=[jax.ShapeDtypeStruct((m, D_MODEL), BF16), jax.ShapeDtypeStruct((m, SSD_WIDTH), F32),
                   jax.ShapeDtypeStruct((m, 2 * GROUP_WIDTH), F32), jax.ShapeDtypeStruct((m, LANES), F32),
                   jax.ShapeDtypeStruct((nseq, LANES), F32),
                   jax.ShapeDtypeStruct((nseq, POOL_HIST, D_MODEL), F32),
                   jax.ShapeDtypeStruct((nseq, CONV_WIDTH - 1, CONV_DIM), F32)],
        scratch_shapes=[pltpu.VMEM((nb, POOL_PAD + t, D_MODEL), F32), pltpu.VMEM((nb, CONV_PAD + t, CONV_DIM), F32)],
        compiler_params=pltpu.CompilerParams(dimension_semantics=("arbitrary",), vmem_limit_bytes=VMEM_LIMIT),
        name="state_pre",
    )(main, main, main, dt, state_pool, state_conv, *params)


PIPE_SLOTS = 4


def _ring_pipeline(n, start_in, wait_in, compute, start_out=None, wait_out=None):
    ns = PIPE_SLOTS
    for i in range(ns - 1):
        start_in(i, i)

    def body(bb, _):
        for k in range(ns):
            i = ns * bb + k
            nxt = i + ns - 1
            pl.when(nxt < n)(functools.partial(start_in, nxt, (k + ns - 1) % ns))
            wait_in(i, k)
            if wait_out is not None:
                pl.when(bb > 0)(functools.partial(wait_out, i - ns, k))
            compute(i, k)
            if start_out is not None:
                start_out(i, k)
        return 0

    lax.fori_loop(0, n // ns, body, 0)
    if wait_out is not None:
        for k in range(ns):
            wait_out(n - ns + k, k)


def _ssd_state_kernel(xs_ref, bc_ref, dt_ref, zs_ref, cd_ref, hin_hbm, expand_ref, segsum_ref, alog_ref, dexp_ref,
                      ssdnw_ref, yssd_ref, hout_hbm, hbuf, obuf, y_scr, sem_in, sem_out):
    nb, t = SAMPLE_BLOCK, SUBLANES
    base = pl.program_id(0) * nb
    a_neg = -jnp.exp(alog_ref[...])
    ridx = lax.broadcasted_iota(jnp.int32, (t, LANES), 0)

    def in_copies(b, slot):
        return [pltpu.make_async_copy(hin_hbm.at[base + b, g * GROUP_WIDTH:(g + 1) * GROUP_WIDTH, :],
                                      hbuf.at[slot, g * GROUP_WIDTH:(g + 1) * GROUP_WIDTH, :], sem_in.at[slot, g])
                for g in range(SSD_GROUPS)]

    def out_copies(b, slot):
        return [pltpu.make_async_copy(obuf.at[slot, g * GROUP_WIDTH:(g + 1) * GROUP_WIDTH, :],
                                      hout_hbm.at[base + b, g * GROUP_WIDTH:(g + 1) * GROUP_WIDTH, :],
                                      sem_out.at[slot, g])
                for g in range(SSD_GROUPS)]

    def start_all(copies):
        for c in copies:
            c.start()

    def wait_all(copies):
        for c in copies:
            c.wait()

    def compute(b, slot):
        rsl = pl.ds(pl.multiple_of(b * t, t), t)
        dtc = dt_ref[rsl, :]
        acs = dtc * a_neg
        for sh in (1, 2, 4):
            acs = acs + jnp.where(ridx >= sh, pltpu.roll(acs, sh, axis=0), 0.0)
        tot = acs[t - 1:t, :]
        x = xs_ref[rsl, :]
        bc = bc_ref[rsl, :]
        bm, cm = bc[:, :GROUP_WIDTH], bc[:, GROUP_WIDTH:]
        bm_r, cm_r = bm.astype(BF16).astype(F32), cm.astype(BF16).astype(F32)
        gs, ps = [], []
        for k in range(t):
            gs.append(jnp.exp(jnp.where(ridx >= k, acs - acs[k:k + 1, :], NEG_BIG)) * dtc[k:k + 1, :])
            ps.append(cm_r * bm_r[k:k + 1, :])
        cb_heads = _dot(jnp.concatenate(ps, axis=0).astype(BF16), segsum_ref[...])
        per_head = jnp.concatenate([jnp.concatenate(gs, axis=0) * cb_heads, jnp.exp(acs),
                                    dtc * jnp.exp(tot - acs)], axis=0)
        hi = per_head.astype(BF16)
        lo = (per_head - hi.astype(F32)).astype(BF16)
        wide = _dot(hi, expand_ref[...]) + _dot(lo, expand_ref[...])
        y = dexp_ref[...] * x
        for k in range(t):
            y = y + wide[k * t:(k + 1) * t, :] * x[k:k + 1, :]
        ea_wide = wide[t * t:t * t + t, :]
        xw = x * wide[t * t + t:, :]
        for g in range(SSD_GROUPS):
            gsl = slice(g * SSD_STATE, (g + 1) * SSD_STATE)
            wsl = slice(g * GROUP_WIDTH, (g + 1) * GROUP_WIDTH)
            hg = hbuf[slot, wsl, :]
            z_g = lax.dot_general(cm[:, gsl].astype(BF16), hg.astype(BF16), _NT, preferred_element_type=F32)
            y_scr[rsl, wsl] = y[:, wsl] + ea_wide[:, wsl] * z_g
            upd = lax.dot_general(xw[:, wsl].astype(BF16), bm[:, gsl].astype(BF16), _TN, preferred_element_type=F32)
            for r8 in range(SSD_HEADS // SSD_GROUPS):
                r = g * (SSD_HEADS // SSD_GROUPS) + r8
                rows_r = slice(r * SSD_HEAD_DIM, (r + 1) * SSD_HEAD_DIM)
                obuf[slot, rows_r, :] = (hbuf[slot, rows_r, :] * cd_ref[base + b, r]
                                         + upd[r8 * SSD_HEAD_DIM:(r8 + 1) * SSD_HEAD_DIM, :])

    _ring_pipeline(nb,
                   lambda b, slot: start_all(in_copies(b, slot)), lambda b, slot: wait_all(in_copies(b, slot)),
                   compute,
                   lambda b, slot: start_all(out_copies(b, slot)), lambda b, slot: wait_all(out_copies(b, slot)))
    yz = y_scr[...] * _silu(zs_ref[...].astype(F32))
    yssd_ref[...] = _rms(yz, ssdnw_ref[...]).astype(BF16)


def _ssd_state(xs, bc, dts, main, cd, hin, expand, segsum, alog, dexp, ssdnw):
    nb, t = SAMPLE_BLOCK, SUBLANES
    rows = nb * t
    m = xs.shape[0]
    nseq = m // t
    col_spec = lambda width, idx: pl.BlockSpec((rows, width), lambda i: (i, idx))
    const_spec = lambda a: pl.BlockSpec(a.shape, lambda i: (0,) * a.ndim)
    return pl.pallas_call(
        _ssd_state_kernel,
        grid=(nseq // nb,),
        in_specs=[col_spec(SSD_WIDTH, 0), col_spec(2 * GROUP_WIDTH, 0), col_spec(LANES, 0),
                  col_spec(SSD_WIDTH, _col_block(COL_ZS)),
                  pl.BlockSpec(memory_space=pltpu.SMEM), pl.BlockSpec(memory_space=pl.ANY),
                  const_spec(expand), const_spec(segsum), const_spec(alog), const_spec(dexp), const_spec(ssdnw)],
        out_specs=[col_spec(SSD_WIDTH, 0), pl.BlockSpec(memory_space=pl.ANY)],
        out_shape=[jax.ShapeDtypeStruct((m, SSD_WIDTH), BF16), jax.ShapeDtypeStruct(hin.shape, F32)],
        scratch_shapes=[pltpu.VMEM((PIPE_SLOTS, SSD_WIDTH, SSD_STATE), F32),
                        pltpu.VMEM((PIPE_SLOTS, SSD_WIDTH, SSD_STATE), F32),
                        pltpu.VMEM((rows, SSD_WIDTH), F32),
                        pltpu.SemaphoreType.DMA((PIPE_SLOTS, SSD_GROUPS)),
                        pltpu.SemaphoreType.DMA((PIPE_SLOTS, SSD_GROUPS))],
        compiler_params=pltpu.CompilerParams(dimension_semantics=("arbitrary",), vmem_limit_bytes=VMEM_LIMIT),
        name="ssd_state",
    )(xs, bc, dts, main, cd, hin, expand, segsum, alog, dexp, ssdnw)


ATT_ITEM = 2


def _att_state_kernel(q_ref, za_ref, k_hbm, v_hbm, yatt_ref, kbuf, vbuf, q_scr, att_scr, sem):
    nb, t = SAMPLE_BLOCK, SUBLANES
    base = pl.program_id(0) * nb
    scale = ATT_HEAD_DIM ** -0.5
    q_scr[...] = q_ref[...].astype(F32)
    head_of_lane = lax.broadcasted_iota(jnp.int32, (1, D_MODEL), 1) // ATT_HEAD_DIM

    def copies(item, slot):
        out = []
        for j in range(ATT_ITEM):
            for kv, (src, buf) in enumerate(((k_hbm, kbuf), (v_hbm, vbuf))):
                for hd in range(ATT_HEADS):
                    hsl = slice(hd * ATT_HEAD_DIM, (hd + 1) * ATT_HEAD_DIM)
                    out.append(pltpu.make_async_copy(
                        src.at[base + item * ATT_ITEM + j, :, hd, :], buf.at[slot, j, :, hsl],
                        sem.at[slot, (j * 2 + kv) * ATT_HEADS + hd]))
        return out

    def start_in(item, slot):
        for c in copies(item, slot):
            c.start()

    def wait_in(item, slot):
        for c in copies(item, slot):
            c.wait()

    def compute(item, slot):
        for j in range(ATT_ITEM):
            rsl = pl.ds(pl.multiple_of((item * ATT_ITEM + j) * t, t), t)
            qf = q_scr[rsl, :]
            q_bd = jnp.concatenate([jnp.where(head_of_lane == hd, qf, 0.0) for hd in range(ATT_HEADS)],
                                   axis=0).astype(BF16)
            sc = lax.dot_general(q_bd, kbuf[slot, j].astype(BF16), _NT, preferred_element_type=F32) * scale
            o = _dot(_softmax_rows(sc).astype(BF16), vbuf[slot, j].astype(BF16))
            att_scr[rsl, :] = jnp.concatenate(
                [o[hd * t:(hd + 1) * t, hd * ATT_HEAD_DIM:(hd + 1) * ATT_HEAD_DIM] for hd in range(ATT_HEADS)], axis=1)

    _ring_pipeline(nb // ATT_ITEM, start_in, wait_in, compute)
    yatt_ref[...] = (att_scr[...] * _silu(za_ref[...].astype(F32))).astype(BF16)


def _att_state(main, k, v):
    nb, t = SAMPLE_BLOCK, SUBLANES
    rows = nb * t
    m = main.shape[0]
    col_spec = lambda width, idx: pl.BlockSpec((rows, width), lambda i: (i, idx))
    kv_buf = pltpu.VMEM((PIPE_SLOTS, ATT_ITEM, MEM_LEN, D_MODEL), F32)
    return pl.pallas_call(
        _att_state_kernel,
        grid=(m // rows,),
        in_specs=[col_spec(D_MODEL, _col_block(COL_Q)), col_spec(D_MODEL, _col_block(COL_ZA)),
                  pl.BlockSpec(memory_space=pl.ANY), pl.BlockSpec(memory_space=pl.ANY)],
        out_specs=col_spec(D_MODEL, 0),
        out_shape=jax.ShapeDtypeStruct((m, D_MODEL), BF16),
        scratch_shapes=[kv_buf, kv_buf, pltpu.VMEM((rows, D_MODEL), F32), pltpu.VMEM((rows, D_MODEL), F32),
                        pltpu.SemaphoreType.DMA((PIPE_SLOTS, ATT_ITEM * 2 * ATT_HEADS))],
        compiler_params=pltpu.CompilerParams(dimension_semantics=("arbitrary",), vmem_limit_bytes=VMEM_LIMIT),
        name="att_state",
    )(main, main, k, v)


DENSE_ROWS = 512


def _dense_kernel(x_ref, gt_ref, yp_ref, ys_ref, ya_ref, wpo_ref, wso_ref, wao_ref, wo_ref, fnw_ref, y_ref):
    gates = gt_ref[...].astype(F32)
    merged = (gates[:, 0:D_MODEL] * _dot(yp_ref[...], wpo_ref[...])
              + gates[:, D_MODEL:2 * D_MODEL] * _dot(ys_ref[...], wso_ref[...])
              + gates[:, 2 * D_MODEL:] * _dot(ya_ref[...], wao_ref[...]))
    x_out = x_ref[...] + _dot(merged.astype(BF16), wo_ref[...])
    y_ref[...] = _rms(x_out, fnw_ref[...])


def _dense(x2d, gates, gate_idx, yp, ys, ya, wpo, wso, wao, wo, fnw):
    m = x2d.shape[0]
    row = lambda width, idx=0: pl.BlockSpec((DENSE_ROWS, width), lambda i: (i, idx))
    resident = lambda a: pl.BlockSpec(a.shape, lambda i: (0,) * a.ndim, pipeline_mode=pl.Buffered(1))
    return pl.pallas_call(
        _dense_kernel,
        grid=(m // DENSE_ROWS,),
        in_specs=[row(D_MODEL), row(3 * D_MODEL, gate_idx), row(D_MODEL), row(SSD_WIDTH), row(D_MODEL),
                  resident(wpo), resident(wso), resident(wao), resident(wo), resident(fnw)],
        out_specs=row(D_MODEL),
        out_shape=jax.ShapeDtypeStruct((m, D_MODEL), F32),
        compiler_params=pltpu.CompilerParams(dimension_semantics=("arbitrary",), vmem_limit_bytes=VMEM_LIMIT),
        name="dense",
    )(x2d, gates, yp, ys, ya, wpo, wso, wao, wo, fnw)


def kernel(x_prompt, x_sample, mem_prompt, state_pool, state_conv, state_ssm, cache_mem_k, cache_mem_v,
           norm_w, w_in, w_pool_grp, pool_scale, conv_w, conv_b, dt_bias, a_log, d_skip, ssd_norm_w,
           mem_norm_w, w_mem_k, w_mem_v, w_pool_out, w_ssd_out, w_att_out, w_out, final_norm_w):
    assert w_in.shape[0] == 1
    bp, sp, d = x_prompt.shape
    bs, ss, _ = x_sample.shape
    assert ss == SUBLANES and sp % PROMPT_TILE == 0 and bs % SAMPLE_BLOCK == 0

    w_a = w_in[0][:, :W_SPLIT[0]].astype(BF16)
    w_b = w_in[0][:, W_SPLIT[1]:].astype(BF16)
    w_dt = jnp.pad(w_in[0][:, W_SPLIT[0]:W_SPLIT[1]], ((0, 0), (0, LANES - SSD_HEADS))).astype(BF16)
    nw = norm_w[0].reshape(1, d)
    pad_heads = lambda a: jnp.pad(a.reshape(1, SSD_HEADS), ((0, 0), (0, LANES - SSD_HEADS)))
    wgrp = w_pool_grp[0].astype(BF16)
    pscale = pool_scale[0].reshape(1, d)
    convb = conv_b[0].reshape(1, CONV_DIM)
    dtb, alog = pad_heads(dt_bias[0]), pad_heads(a_log[0])
    dexp = jnp.repeat(d_skip[0], SSD_HEAD_DIM).reshape(1, SSD_WIDTH)
    ssdnw = ssd_norm_w[0].reshape(1, SSD_WIDTH)
    dense_w = (w_pool_out[0].astype(BF16), w_ssd_out[0].astype(BF16), w_att_out[0].astype(BF16),
               w_out[0].astype(BF16), final_norm_w.reshape(1, d))
    head_of_lane = jnp.arange(SSD_WIDTH) // SSD_HEAD_DIM
    expand = (jnp.arange(LANES)[:, None] == head_of_lane[None, :]).astype(BF16)
    group_of_head = jnp.where(jnp.arange(LANES) < SSD_HEADS, jnp.arange(LANES) // (SSD_HEADS // SSD_GROUPS), -1)
    segsum = ((jnp.arange(GROUP_WIDTH) // SSD_STATE)[:, None] == group_of_head[None, :]).astype(BF16)

    mk, mv, mkb, mvb = _memkv(mem_prompt, mem_norm_w[0].reshape(1, d), w_mem_k[0].astype(BF16),
                              w_mem_v[0].astype(BF16))
    xp2 = x_prompt.reshape(bp * sp, d)
    gates_p, yp, ysd, ya, pool_p, conv_p, ssm_p = _seq_prompt(
        xp2, nw, w_a, w_b, w_dt, mkb, mvb, (wgrp, pscale, conv_w[0], convb, dtb, alog, dexp, ssdnw),
        nseq=bp, ntile=sp // PROMPT_TILE)
    y_prompt = _dense(xp2, gates_p, 0, yp, ysd, ya, *dense_w).reshape(bp, sp, d)

    xs2 = x_sample.reshape(bs * ss, d)
    main_s, dt_s = _inproj(xs2, nw, w_a, w_b, w_dt)
    yp, xs, bc, dts, cd, pool_s, conv_s = _state_pre(
        main_s, dt_s, state_pool[0], state_conv[0], (wgrp, pscale, conv_w[0], convb, dtb, alog))
    ysd, ssm_s = _ssd_state(xs, bc, dts, main_s, cd, state_ssm[0].reshape(bs, SSD_WIDTH, SSD_STATE),
                            expand, segsum, alog, dexp, ssdnw)
    ya = _att_state(main_s, cache_mem_k[0], cache_mem_v[0])
    y_sample = _dense(xs2, main_s, _col_block(COL_GATES), yp, ysd, ya, *dense_w).reshape(bs, ss, d)

    ssm_shape = (SSD_GROUPS, SSD_HEADS // SSD_GROUPS, SSD_HEAD_DIM, SSD_STATE)
    return (y_prompt, y_sample,
            pool_p[None], conv_p[None], ssm_p.reshape((1, bp) + ssm_shape),
            mk[None], mv[None],
            pool_s[None], conv_s[None], ssm_s.reshape((1, bs) + ssm_shape))
```

```python
import functools

import jax
import jax.numpy as jnp
from jax import lax
from jax.experimental import pallas as pl
from jax.experimental.pallas import tpu as pltpu

F32 = jnp.float32
BF16 = jnp.bfloat16

D_MODEL = 1024
POOL_WINDOWS = (2, 4, 8, 16)
POOL_GROUP = 256
POOL_HIST = 15
POOL_PAD = 16
SSD_WIDTH = 2048
SSD_HEADS = 32
SSD_HEAD_DIM = 64
SSD_GROUPS = 4
SSD_STATE = 128
GROUP_WIDTH = SSD_WIDTH // SSD_GROUPS
CONV_WIDTH = 4
CONV_DIM = 3072
CONV_PAD = 8
SSD_CHUNK = 128
MEM_LEN = 256
ATT_HEADS = 4
ATT_HEAD_DIM = 256
PAST_LEN = 16384
EPS = 1e-6
NEG_BIG = -1e30
SUBLANES = 8
LANES = 128
MAIN_COLS = 12288
COL_XBC, COL_GATES, COL_ZS = (0, 3072), (3072, 6144), (6144, 8192)
COL_U, COL_ZP, COL_Q, COL_ZA = (8192, 9216), (9216, 10240), (10240, 11264), (11264, 12288)
W_SPLIT = (7168, 7200)
W_SRC = {COL_U: (0, 0), COL_ZP: (0, 1024), COL_ZS: (0, 2048), COL_XBC: (0, 4096),
         COL_Q: (1, 0), COL_ZA: (1, 1024), COL_GATES: (1, 2048)}
VMEM_LIMIT = 56 * 1024 * 1024
SEQ_PROMPT_VMEM_LIMIT = 60 * 1024 * 1024

_NT = (((1,), (1,)), ((), ()))
_TN = (((0,), (0,)), ((), ()))


def _sigmoid(x):
    return 1.0 / (1.0 + jnp.exp(-x))


def _silu(x):
    return x * _sigmoid(x)


def _softplus(x):
    return jnp.maximum(x, 0.0) + jnp.log1p(jnp.exp(-jnp.abs(x)))


def _rms(x, w):
    return x * lax.rsqrt(jnp.mean(x * x, axis=-1, keepdims=True) + EPS) * w


def _dot(a, b):
    return jnp.dot(a, b, preferred_element_type=F32)


def _softmax_rows(sc):
    e = jnp.exp(sc - jnp.max(sc, axis=-1, keepdims=True))
    return e / jnp.sum(e, axis=-1, keepdims=True)


def _weight_cols(w_refs, piece, lo, hi):
    idx, c0 = W_SRC[piece]
    return w_refs[idx][:, c0 + lo:c0 + hi]


def _memkv_kernel(mem_ref, nw_ref, wk_ref, wv_ref, k_ref, v_ref, kb_ref, vb_ref):
    mh = _rms(mem_ref[0], nw_ref[...]).astype(BF16)
    k = _dot(mh, wk_ref[...])
    v = _dot(mh, wv_ref[...])
    for hd in range(ATT_HEADS):
        hsl = slice(hd * ATT_HEAD_DIM, (hd + 1) * ATT_HEAD_DIM)
        k_ref[0, :, hd, :] = k[:, hsl]
        v_ref[0, :, hd, :] = v[:, hsl]
        kb_ref[0, hd] = k[:, hsl].astype(BF16)
        vb_ref[0, hd] = v[:, hsl].astype(BF16)


def _memkv(mem, nw, wk, wv):
    b, m, d = mem.shape
    full = lambda shape: pl.BlockSpec(shape, lambda i: (0,) * len(shape))
    blk = pl.BlockSpec((1, m, d), lambda i: (i, 0, 0))
    oblk = pl.BlockSpec((1, m, ATT_HEADS, ATT_HEAD_DIM), lambda i: (i, 0, 0, 0))
    hblk = pl.BlockSpec((1, ATT_HEADS, m, ATT_HEAD_DIM), lambda i: (i, 0, 0, 0))
    return pl.pallas_call(
        _memkv_kernel,
        grid=(b,),
        in_specs=[blk, full((1, d)), full((d, d)), full((d, d))],
        out_specs=[oblk, oblk, hblk, hblk],
        out_shape=[jax.ShapeDtypeStruct((b, m, ATT_HEADS, ATT_HEAD_DIM), F32)] * 2
        + [jax.ShapeDtypeStruct((b, ATT_HEADS, m, ATT_HEAD_DIM), BF16)] * 2,
        compiler_params=pltpu.CompilerParams(dimension_semantics=("arbitrary",), vmem_limit_bytes=VMEM_LIMIT),
        name="memkv",
    )(mem, nw, wk, wv)


INPROJ_ROWS = 256
INPROJ_COL_CHUNK = 1024


def _inproj_kernel(x_ref, nw_ref, wa_ref, wb_ref, wdt_ref, main_ref, dt_ref):
    h = _rms(x_ref[...], nw_ref[...]).astype(BF16)
    dt_ref[...] = _dot(h, wdt_ref[...])
    for piece in W_SRC:
        for lo in range(0, piece[1] - piece[0], INPROJ_COL_CHUNK):
            val = _dot(h, _weight_cols((wa_ref, wb_ref), piece, lo, lo + INPROJ_COL_CHUNK))
            if piece == COL_GATES:
                val = _sigmoid(val)
            main_ref[:, piece[0] + lo:piece[0] + lo + INPROJ_COL_CHUNK] = val.astype(BF16)


def _inproj(x2d, nw, w_a, w_b, w_dt):
    m = x2d.shape[0]
    resident = lambda a: pl.BlockSpec(a.shape, lambda i: (0,) * a.ndim, pipeline_mode=pl.Buffered(1))
    return pl.pallas_call(
        _inproj_kernel,
        grid=(m // INPROJ_ROWS,),
        in_specs=[pl.BlockSpec((INPROJ_ROWS, D_MODEL), lambda i: (i, 0)),
                  resident(nw), resident(w_a), resident(w_b), resident(w_dt)],
        out_specs=[
            pl.BlockSpec((INPROJ_ROWS, MAIN_COLS), lambda i: (i, 0)),
            pl.BlockSpec((INPROJ_ROWS, LANES), lambda i: (i, 0)),
        ],
        out_shape=[jax.ShapeDtypeStruct((m, MAIN_COLS), BF16), jax.ShapeDtypeStruct((m, LANES), F32)],
        compiler_params=pltpu.CompilerParams(dimension_semantics=("arbitrary",), vmem_limit_bytes=VMEM_LIMIT),
        name="inproj",
    )(x2d, nw, w_a, w_b, w_dt)


def _pool_branch(pext, u, pos, wgrp_ref, pscale_ref, zp):
    nb, t, _ = u.shape
    ys = []
    for g, w in enumerate(POOL_WINDOWS):
        cols = slice(g * POOL_GROUP, (g + 1) * POOL_GROUP)
        win = pext[:, :, cols]
        for sh in [1 << e for e in range(g + 1)]:
            win = win + pltpu.roll(win, sh, axis=1)
        win = win[:, POOL_PAD:, :]
        inv_cnt = 1.0 / jnp.minimum(w, pos + 1).astype(F32)
        d = (win * inv_cnt - u[:, :, cols]).astype(BF16).reshape(nb * t, POOL_GROUP)
        ys.append(_dot(d, wgrp_ref[g]))
    return jnp.concatenate(ys, axis=1) * pscale_ref[...] * _silu(zp)


def _conv_branch(cext, convw_ref, convb_ref, store):
    for cc in range(CONV_DIM // GROUP_WIDTH):
        csl = slice(cc * GROUP_WIDTH, (cc + 1) * GROUP_WIDTH)
        ext = cext[:, :, csl]
        conv = convb_ref[:, csl].reshape(1, 1, GROUP_WIDTH)
        for kk in range(CONV_WIDTH):
            tap = ext if kk == CONV_WIDTH - 1 else pltpu.roll(ext, CONV_WIDTH - 1 - kk, axis=1)
            conv = conv + tap * convw_ref[kk:kk + 1, csl].reshape(1, 1, GROUP_WIDTH)
        store(cc, _silu(conv[:, CONV_PAD:, :]))


PROMPT_TILE = 256


def _seq_prompt_kernel(ntile, x_ref, nw_ref, wa_ref, wb_ref, wdt_ref, k_ref, v_ref,
                       wgrp_ref, pscale_ref, convw_ref, convb_ref, dtb_ref, alog_ref, dexp_ref, ssdnw_ref,
                       gates_ref, ypool_ref, yssd_ref, yatt_ref, pool_o_ref, conv_o_ref, ssm_o_ref,
                       pext, cext, xs_scr, b_scr, c_scr, dt_scr, y_scr, h_scr, zs_scr, q_scr, za_scr):
    t, q = PROMPT_TILE, SSD_CHUNK
    k = pl.program_id(0)
    s = k % ntile
    last = ntile - 1

    @pl.when(k == 0)
    def _():
        y_scr[...] = jnp.zeros(y_scr.shape, F32)
        zs_scr[...] = jnp.zeros(zs_scr.shape, BF16)
        q_scr[...] = jnp.zeros(q_scr.shape, BF16)
        za_scr[...] = jnp.zeros(za_scr.shape, BF16)

    @pl.when(s == 0)
    def _():
        pext[:, 0:POOL_PAD, :] = jnp.zeros((1, POOL_PAD, D_MODEL), F32)
        cext[:, 0:CONV_PAD, :] = jnp.zeros((1, CONV_PAD, CONV_DIM), F32)
        h_scr[...] = jnp.zeros(h_scr.shape, F32)

    @pl.when(s > 0)
    def _():
        carry_p = pext[:, t:t + POOL_PAD, :]
        carry_c = cext[:, t:t + CONV_PAD, :]
        pext[:, 0:POOL_PAD, :] = carry_p
        cext[:, 0:CONV_PAD, :] = carry_c

    yz = y_scr[...] * zs_scr[...].astype(F32)
    yssd_ref[...] = _rms(yz, ssdnw_ref[...]).astype(BF16)
    scale = ATT_HEAD_DIM ** -0.5
    outs = []
    for hd in range(ATT_HEADS):
        hsl = slice(hd * ATT_HEAD_DIM, (hd + 1) * ATT_HEAD_DIM)
        p = _softmax_rows(lax.dot_general(q_scr[:, hsl], k_ref[0, hd], _NT, preferred_element_type=F32) * scale)
        outs.append(_dot(p.astype(BF16), v_ref[0, hd]))
    yatt_ref[...] = (jnp.concatenate(outs, axis=1) * za_scr[...].astype(F32)).astype(BF16)

    hn = _rms(x_ref[...], nw_ref[...]).astype(BF16)

    def proj(piece, lo=0, hi=None):
        return _dot(hn, _weight_cols((wa_ref, wb_ref), piece, lo, piece[1] - piece[0] if hi is None else hi))

    def gates_piece(c):
        def run():
            gates_ref[:, c * D_MODEL:(c + 1) * D_MODEL] = _sigmoid(
                proj(COL_GATES, c * D_MODEL, (c + 1) * D_MODEL).astype(BF16))
        return run

    def zs_piece(c):
        def run():
            zs_scr[:, c * D_MODEL:(c + 1) * D_MODEL] = _silu(proj(COL_ZS, c * D_MODEL, (c + 1) * D_MODEL).astype(BF16))
        return run

    def q_piece():
        q_scr[...] = proj(COL_Q).astype(BF16)

    def za_piece():
        za_scr[...] = _silu(proj(COL_ZA).astype(BF16))

    fillers = [gates_piece(0), gates_piece(1), gates_piece(2), zs_piece(0), zs_piece(1), q_piece, za_piece]

    def run_filler():
        if fillers:
            fillers.pop(0)()

    u = proj(COL_U).reshape(1, t, D_MODEL)
    pext[:, POOL_PAD:, :] = u
    pos = s * t + lax.broadcasted_iota(jnp.int32, (1, t, 1), 1)
    ypool_ref[...] = _pool_branch(pext, u, pos, wgrp_ref, pscale_ref, proj(COL_ZP)).astype(BF16)

    for c in range(CONV_DIM // D_MODEL):
        cext[:, CONV_PAD:, c * D_MODEL:(c + 1) * D_MODEL] = proj(
            COL_XBC, c * D_MODEL, (c + 1) * D_MODEL).reshape(1, t, D_MODEL)
    dt_scr[...] = _softplus(_dot(hn, wdt_ref[...]) + dtb_ref[...])

    def store_conv(cc, val):
        if cc < SSD_GROUPS:
            xs_scr[:, cc * GROUP_WIDTH:(cc + 1) * GROUP_WIDTH] = val[0]
        elif cc == SSD_GROUPS:
            b_scr[...] = val[0]
        else:
            c_scr[...] = val[0]
        run_filler()

    _conv_branch(cext, convw_ref, convb_ref, store_conv)

    a_neg = -jnp.exp(alog_ref[...])
    rq = lax.broadcasted_iota(jnp.int32, (q, q), 0)
    cq = lax.broadcasted_iota(jnp.int32, (q, q), 1)
    tril = rq >= cq
    tri_f = tril.astype(F32)
    lane_lo = lax.broadcasted_iota(jnp.int32, (1, LANES), 1) < SSD_HEAD_DIM
    pairs_per_group = SSD_HEADS // SSD_GROUPS // 2

    def chunk(c):
        rsl = slice(c * q, (c + 1) * q)
        dtc = dt_scr[rsl, :]
        acs = jnp.dot(tri_f, dtc * a_neg, precision=lax.Precision.HIGHEST, preferred_element_type=F32)
        acs_t = acs.T
        dt_t = dtc.T
        wdec_t = (dt_t * jnp.exp(acs_t[:, q - 1:q] - acs_t)).astype(BF16)
        row_t = acs_t - jnp.log(dt_t)
        cdec = jnp.exp(acs[q - 1:q, :])
        for g in range(SSD_GROUPS):
            gsl = slice(g * SSD_STATE, (g + 1) * SSD_STATE)
            bg = b_scr[rsl, gsl]
            cg_b = c_scr[rsl, gsl].astype(BF16)
            cb = lax.dot_general(cg_b, bg.astype(BF16), _NT, preferred_element_type=F32).astype(BF16)
            bg_t = bg.T.astype(BF16)
            hsl = slice(g * GROUP_WIDTH, (g + 1) * GROUP_WIDTH)
            z_g = _dot(cg_b, h_scr[:, hsl].astype(BF16))
            for jp in range(pairs_per_group):
                j = g * pairs_per_group + jp
                lsl = slice(j * LANES, (j + 1) * LANES)
                xp = xs_scr[rsl, lsl]
                xp_b = xp.astype(BF16)
                zero_b = jnp.zeros_like(xp_b)
                x_bd = jnp.concatenate([jnp.where(lane_lo, xp_b, zero_b), jnp.where(lane_lo, zero_b, xp_b)],
                                       axis=0)
                ms, bws, cols = [], [], []
                for hh in range(2):
                    r = 2 * j + hh
                    cols.append(jnp.broadcast_to(acs[:, r:r + 1], (q, q)))
                    seg = cols[hh] - row_t[r:r + 1, :]
                    ms.append(cb * jnp.exp(jnp.where(tril, seg, NEG_BIG)).astype(BF16))
                    bws.append(bg_t * wdec_t[r:r + 1, :])
                ea_pair = jnp.exp(jnp.where(lane_lo, cols[0], cols[1]))
                y = (_dot(jnp.concatenate(ms, axis=1), x_bd)
                     + ea_pair * z_g[:, jp * LANES:(jp + 1) * LANES] + dexp_ref[:, lsl] * xp)
                y_scr[rsl, lsl] = y
                cd_pair = jnp.where(lane_lo, cdec[:, 2 * j:2 * j + 1], cdec[:, 2 * j + 1:2 * j + 2])
                h_scr[:, lsl] = h_scr[:, lsl] * cd_pair + _dot(jnp.concatenate(bws, axis=1), x_bd)
            run_filler()

    for c in range(t // q):
        chunk(c)
    while fillers:
        run_filler()

    @pl.when(jnp.logical_and(s == last, k < pl.num_programs(0) - 1))
    def _():
        pool_o_ref[...] = pext[:, t + POOL_PAD - POOL_HIST:t + POOL_PAD, :]
        conv_o_ref[...] = cext[:, t + CONV_PAD - (CONV_WIDTH - 1):t + CONV_PAD, :]
        for j in range(SSD_HEADS // 2):
            lsl = slice(j * LANES, (j + 1) * LANES)
            ssm_o_ref[0, lsl, :] = h_scr[:, lsl].T


def _seq_prompt(x2d, nw, w_a, w_b, w_dt, kb, vb, params, *, nseq, ntile):
    t = PROMPT_TILE
    m = x2d.shape[0]
    ntiles = nseq * ntile
    tile = lambda k: jnp.minimum(k, ntiles - 1)
    closed = lambda k: jnp.maximum(k - 1, 0)
    own_rows = lambda width: pl.BlockSpec((t, width), lambda k: (k, 0))
    closed_rows = lambda width: pl.BlockSpec((t, width), lambda k: (closed(k), 0))
    seq_spec = lambda shape, which: pl.BlockSpec((1,) + shape, lambda k: (which(k) // ntile,) + (0,) * len(shape))
    const_spec = lambda a: pl.BlockSpec(a.shape, lambda k: (0,) * a.ndim)
    resident = lambda a: pl.BlockSpec(a.shape, lambda k: (0,) * a.ndim, pipeline_mode=pl.Buffered(1))
    in_specs = [
        pl.BlockSpec((t, D_MODEL), lambda k: (tile(k), 0)),
        resident(nw), resident(w_a), resident(w_b), resident(w_dt),
        seq_spec((ATT_HEADS, MEM_LEN, ATT_HEAD_DIM), closed),
        seq_spec((ATT_HEADS, MEM_LEN, ATT_HEAD_DIM), closed),
    ] + [const_spec(p) for p in params]
    out_specs = [
        own_rows(3 * D_MODEL), own_rows(D_MODEL), closed_rows(SSD_WIDTH), closed_rows(D_MODEL),
        seq_spec((POOL_HIST, D_MODEL), tile),
        seq_spec((CONV_WIDTH - 1, CONV_DIM), tile),
        seq_spec((SSD_WIDTH, SSD_STATE), tile),
    ]
    out_shape = [
        jax.ShapeDtypeStruct((m + t, 3 * D_MODEL), BF16),
        jax.ShapeDtypeStruct((m + t, D_MODEL), BF16),
        jax.ShapeDtypeStruct((m, SSD_WIDTH), BF16),
        jax.ShapeDtypeStruct((m, D_MODEL), BF16),
        jax.ShapeDtypeStruct((nseq, POOL_HIST, D_MODEL), F32),
        jax.ShapeDtypeStruct((nseq, CONV_WIDTH - 1, CONV_DIM), F32),
        jax.ShapeDtypeStruct((nseq, SSD_WIDTH, SSD_STATE), F32),
    ]
    scratch = [
        pltpu.VMEM((1, POOL_PAD + t, D_MODEL), F32),
        pltpu.VMEM((1, CONV_PAD + t, CONV_DIM), F32),
        pltpu.VMEM((t, SSD_WIDTH), F32),
        pltpu.VMEM((t, GROUP_WIDTH), F32),
        pltpu.VMEM((t, GROUP_WIDTH), F32),
        pltpu.VMEM((t, LANES), F32),
        pltpu.VMEM((t, SSD_WIDTH), F32),
        pltpu.VMEM((SSD_STATE, SSD_WIDTH), F32),
        pltpu.VMEM((t, SSD_WIDTH), BF16),
        pltpu.VMEM((t, D_MODEL), BF16),
        pltpu.VMEM((t, D_MODEL), BF16),
    ]
    return pl.pallas_call(
        functools.partial(_seq_prompt_kernel, ntile),
        grid=(ntiles + 1,),
        in_specs=in_specs,
        out_specs=out_specs,
        out_shape=out_shape,
        scratch_shapes=scratch,
        compiler_params=pltpu.CompilerParams(dimension_semantics=("arbitrary",),
                                             vmem_limit_bytes=SEQ_PROMPT_VMEM_LIMIT),
        name="seq_prompt",
    )(x2d, nw, w_a, w_b, w_dt, kb, vb, *params)


SAMPLE_BLOCK = 32
STATE_PRE_SUB = 4


def _state_pre_kernel(u_ref, zp_ref, xbc_ref, dt_ref, ph_ref, ch_ref,
                      wgrp_ref, pscale_ref, convw_ref, convb_ref, dtb_ref, alog_ref,
                      ypool_ref, xs_ref, bc_ref, dts_ref, cd_ref, pool_o_ref, conv_o_ref, pext, cext):
    nb, t = SAMPLE_BLOCK, SUBLANES
    pext[:, 0:1, :] = jnp.zeros((nb, 1, D_MODEL), F32)
    pext[:, 1:POOL_PAD, :] = ph_ref[...]
    cext[:, 0:CONV_PAD - (CONV_WIDTH - 1), :] = jnp.zeros((nb, CONV_PAD - (CONV_WIDTH - 1), CONV_DIM), F32)
    cext[:, CONV_PAD - (CONV_WIDTH - 1):CONV_PAD, :] = ch_ref[...]

    pext[:, POOL_PAD:, :] = u_ref[...].astype(F32).reshape(nb, t, D_MODEL)
    cext[:, CONV_PAD:, :] = xbc_ref[...].astype(F32).reshape(nb, t, CONV_DIM)
    pos = PAST_LEN + lax.broadcasted_iota(jnp.int32, (1, t, 1), 1)
    for sb in range(nb // STATE_PRE_SUB):
        seqs = slice(sb * STATE_PRE_SUB, (sb + 1) * STATE_PRE_SUB)
        rsl = slice(sb * STATE_PRE_SUB * t, (sb + 1) * STATE_PRE_SUB * t)
        pv, cv = pext.at[seqs], cext.at[seqs]
        ypool_ref[rsl, :] = _pool_branch(pv, pv[:, POOL_PAD:, :], pos, wgrp_ref, pscale_ref,
                                         zp_ref[rsl, :].astype(F32)).astype(BF16)

        def store_conv(cc, val, rsl=rsl):
            val = val.reshape(STATE_PRE_SUB * t, GROUP_WIDTH)
            if cc < SSD_GROUPS:
                xs_ref[rsl, cc * GROUP_WIDTH:(cc + 1) * GROUP_WIDTH] = val
            else:
                bc_ref[rsl, (cc - SSD_GROUPS) * GROUP_WIDTH:(cc - SSD_GROUPS + 1) * GROUP_WIDTH] = val

        _conv_branch(cv, convw_ref, convb_ref, store_conv)
    dt = _softplus(dt_ref[...] + dtb_ref[...])
    dts_ref[...] = dt
    a = dt * -jnp.exp(alog_ref[...])
    cd_ref[...] = jnp.exp(jnp.sum(a.reshape(nb, t, LANES), axis=1))
    pool_o_ref[...] = pext[:, t + POOL_PAD - POOL_HIST:t + POOL_PAD, :]
    conv_o_ref[...] = cext[:, t + CONV_PAD - (CONV_WIDTH - 1):t + CONV_PAD, :]


def _col_block(piece):
    return piece[0] // (piece[1] - piece[0])


def _state_pre(main, dt, state_pool, state_conv, params):
    nb, t = SAMPLE_BLOCK, SUBLANES
    rows = nb * t
    m = main.shape[0]
    nseq = m // t
    col_spec = lambda width, idx: pl.BlockSpec((rows, width), lambda i: (i, idx))
    seq_spec = lambda shape: pl.BlockSpec((nb,) + shape, lambda i: (i,) + (0,) * len(shape))
    const_spec = lambda a: pl.BlockSpec(a.shape, lambda i: (0,) * a.ndim)
    return pl.pallas_call(
        _state_pre_kernel,
        grid=(nseq // nb,),
        in_specs=[col_spec(D_MODEL, _col_block(COL_U)), col_spec(D_MODEL, _col_block(COL_ZP)),
                  col_spec(CONV_DIM, _col_block(COL_XBC)), col_spec(LANES, 0),
                  seq_spec((POOL_HIST, D_MODEL)), seq_spec((CONV_WIDTH - 1, CONV_DIM))]
        + [const_spec(p) for p in params],
        out_specs=[col_spec(D_MODEL, 0), col_spec(SSD_WIDTH, 0), col_spec(2 * GROUP_WIDTH, 0), col_spec(LANES, 0),
                   pl.BlockSpec((nb, LANES), lambda i: (i, 0)),
                   seq_spec((POOL_HIST, D_MODEL)), seq_spec((CONV_WIDTH - 1, CONV_DIM))],
        out_shape=[jax.ShapeDtypeStruct((m, D_MODEL), BF16), jax.ShapeDtypeStruct((m, SSD_WIDTH), F32),
                   jax.ShapeDtypeStruct((m, 2 * GROUP_WIDTH), F32), jax.ShapeDtypeStruct((m, LANES), F32),
                   jax.ShapeDtypeStruct((nseq, LANES), F32),
                   jax.ShapeDtypeStruct((nseq, POOL_HIST, D_MODEL), F32),
                   jax.ShapeDtypeStruct((nseq, CONV_WIDTH - 1, CONV_DIM), F32)],
        scratch_shapes=[pltpu.VMEM((nb, POOL_PAD + t, D_MODEL), F32), pltpu.VMEM((nb, CONV_PAD + t, CONV_DIM), F32)],
        compiler_params=pltpu.CompilerParams(dimension_semantics=("arbitrary",), vmem_limit_bytes=VMEM_LIMIT),
        name="state_pre",
    )(main, main, main, dt, state_pool, state_conv, *params)


PIPE_SLOTS = 4


def _ring_pipeline(n, start_in, wait_in, compute, start_out=None, wait_out=None):
    ns = PIPE_SLOTS
    for i in range(ns - 1):
        start_in(i, i)

    def body(bb, _):
        for k in range(ns):
            i = ns * bb + k
            nxt = i + ns - 1
            pl.when(nxt < n)(functools.partial(start_in, nxt, (k + ns - 1) % ns))
            wait_in(i, k)
            if wait_out is not None:
                pl.when(bb > 0)(functools.partial(wait_out, i - ns, k))
            compute(i, k)
            if start_out is not None:
                start_out(i, k)
        return 0

    lax.fori_loop(0, n // ns, body, 0)
    if wait_out is not None:
        for k in range(ns):
            wait_out(n - ns + k, k)


def _ssd_state_kernel(xs_ref, bc_ref, dt_ref, zs_ref, cd_ref, hin_hbm, expand_ref, segsum_ref, alog_ref, dexp_ref,
                      ssdnw_ref, yssd_ref, hout_hbm, hbuf, obuf, y_scr, sem_in, sem_out):
    nb, t = SAMPLE_BLOCK, SUBLANES
    base = pl.program_id(0) * nb
    a_neg = -jnp.exp(alog_ref[...])
    ridx = lax.broadcasted_iota(jnp.int32, (t, LANES), 0)

    def in_copies(b, slot):
        return [pltpu.make_async_copy(hin_hbm.at[base + b, g * GROUP_WIDTH:(g + 1) * GROUP_WIDTH, :],
                                      hbuf.at[slot, g * GROUP_WIDTH:(g + 1) * GROUP_WIDTH, :], sem_in.at[slot, g])
                for g in range(SSD_GROUPS)]

    def out_copies(b, slot):
        return [pltpu.make_async_copy(obuf.at[slot, g * GROUP_WIDTH:(g + 1) * GROUP_WIDTH, :],
                                      hout_hbm.at[base + b, g * GROUP_WIDTH:(g + 1) * GROUP_WIDTH, :],
                                      sem_out.at[slot, g])
                for g in range(SSD_GROUPS)]

    def start_all(copies):
        for c in copies:
            c.start()

    def wait_all(copies):
        for c in copies:
            c.wait()

    def compute(b, slot):
        rsl = pl.ds(pl.multiple_of(b * t, t), t)
        dtc = dt_ref[rsl, :]
        acs = dtc * a_neg
        for sh in (1, 2, 4):
            acs = acs + jnp.where(ridx >= sh, pltpu.roll(acs, sh, axis=0), 0.0)
        tot = acs[t - 1:t, :]
        x = xs_ref[rsl, :]
        bc = bc_ref[rsl, :]
        bm, cm = bc[:, :GROUP_WIDTH], bc[:, GROUP_WIDTH:]
        bm_r, cm_r = bm.astype(BF16).astype(F32), cm.astype(BF16).astype(F32)
        gs, ps = [], []
        for k in range(t):
            gs.append(jnp.exp(jnp.where(ridx >= k, acs - acs[k:k + 1, :], NEG_BIG)) * dtc[k:k + 1, :])
            ps.append(cm_r * bm_r[k:k + 1, :])
        cb_heads = _dot(jnp.concatenate(ps, axis=0).astype(BF16), segsum_ref[...])
        per_head = jnp.concatenate([jnp.concatenate(gs, axis=0) * cb_heads, jnp.exp(acs),
                                    dtc * jnp.exp(tot - acs)], axis=0)
        hi = per_head.astype(BF16)
        lo = (per_head - hi.astype(F32)).astype(BF16)
        wide = _dot(hi, expand_ref[...]) + _dot(lo, expand_ref[...])
        y = dexp_ref[...] * x
        for k in range(t):
            y = y + wide[k * t:(k + 1) * t, :] * x[k:k + 1, :]
        ea_wide = wide[t * t:t * t + t, :]
        xw = x * wide[t * t + t:, :]
        for g in range(SSD_GROUPS):
            gsl = slice(g * SSD_STATE, (g + 1) * SSD_STATE)
            wsl = slice(g * GROUP_WIDTH, (g + 1) * GROUP_WIDTH)
            hg = hbuf[slot, wsl, :]
            z_g = lax.dot_general(cm[:, gsl].astype(BF16), hg.astype(BF16), _NT, preferred_element_type=F32)
            y_scr[rsl, wsl] = y[:, wsl] + ea_wide[:, wsl] * z_g
            upd = lax.dot_general(xw[:, wsl].astype(BF16), bm[:, gsl].astype(BF16), _TN, preferred_element_type=F32)
            for r8 in range(SSD_HEADS // SSD_GROUPS):
                r = g * (SSD_HEADS // SSD_GROUPS) + r8
                rows_r = slice(r * SSD_HEAD_DIM, (r + 1) * SSD_HEAD_DIM)
                obuf[slot, rows_r, :] = (hbuf[slot, rows_r, :] * cd_ref[base + b, r]
                                         + upd[r8 * SSD_HEAD_DIM:(r8 + 1) * SSD_HEAD_DIM, :])

    _ring_pipeline(nb,
                   lambda b, slot: start_all(in_copies(b, slot)), lambda b, slot: wait_all(in_copies(b, slot)),
                   compute,
                   lambda b, slot: start_all(out_copies(b, slot)), lambda b, slot: wait_all(out_copies(b, slot)))
    yz = y_scr[...] * _silu(zs_ref[...].astype(F32))
    yssd_ref[...] = _rms(yz, ssdnw_ref[...]).astype(BF16)


def _ssd_state(xs, bc, dts, main, cd, hin, expand, segsum, alog, dexp, ssdnw):
    nb, t = SAMPLE_BLOCK, SUBLANES
    rows = nb * t
    m = xs.shape[0]
    nseq = m // t
    col_spec = lambda width, idx: pl.BlockSpec((rows, width), lambda i: (i, idx))
    const_spec = lambda a: pl.BlockSpec(a.shape, lambda i: (0,) * a.ndim)
    return pl.pallas_call(
        _ssd_state_kernel,
        grid=(nseq // nb,),
        in_specs=[col_spec(SSD_WIDTH, 0), col_spec(2 * GROUP_WIDTH, 0), col_spec(LANES, 0),
                  col_spec(SSD_WIDTH, _col_block(COL_ZS)),
                  pl.BlockSpec(memory_space=pltpu.SMEM), pl.BlockSpec(memory_space=pl.ANY),
                  const_spec(expand), const_spec(segsum), const_spec(alog), const_spec(dexp), const_spec(ssdnw)],
        out_specs=[col_spec(SSD_WIDTH, 0), pl.BlockSpec(memory_space=pl.ANY)],
        out_shape=[jax.ShapeDtypeStruct((m, SSD_WIDTH), BF16), jax.ShapeDtypeStruct(hin.shape, F32)],
        scratch_shapes=[pltpu.VMEM((PIPE_SLOTS, SSD_WIDTH, SSD_STATE), F32),
                        pltpu.VMEM((PIPE_SLOTS, SSD_WIDTH, SSD_STATE), F32),
                        pltpu.VMEM((rows, SSD_WIDTH), F32),
                        pltpu.SemaphoreType.DMA((PIPE_SLOTS, SSD_GROUPS)),
                        pltpu.SemaphoreType.DMA((PIPE_SLOTS, SSD_GROUPS))],
        compiler_params=pltpu.CompilerParams(dimension_semantics=("arbitrary",), vmem_limit_bytes=VMEM_LIMIT),
        name="ssd_state",
    )(xs, bc, dts, main, cd, hin, expand, segsum, alog, dexp, ssdnw)


ATT_ITEM = 4


def _att_state_kernel(q_ref, za_ref, k_hbm, v_hbm, yatt_ref, kbuf, vbuf, q_scr, att_scr, sem):
    nb, t = SAMPLE_BLOCK, SUBLANES
    base = pl.program_id(0) * nb
    scale = ATT_HEAD_DIM ** -0.5
    q_scr[...] = q_ref[...].astype(F32)
    head_of_lane = lax.broadcasted_iota(jnp.int32, (1, D_MODEL), 1) // ATT_HEAD_DIM

    def copies(item, slot):
        out = []
        for j in range(ATT_ITEM):
            for kv, (src, buf) in enumerate(((k_hbm, kbuf), (v_hbm, vbuf))):
                for hd in range(ATT_HEADS):
                    hsl = slice(hd * ATT_HEAD_DIM, (hd + 1) * ATT_HEAD_DIM)
                    out.append(pltpu.make_async_copy(
                        src.at[base + item * ATT_ITEM + j, :, hd, :], buf.at[slot, j, :, hsl],
                        sem.at[slot, (j * 2 + kv) * ATT_HEADS + hd]))
        return out

    def start_in(item, slot):
        for c in copies(item, slot):
            c.start()

    def wait_in(item, slot):
        for c in copies(item, slot):
            c.wait()

    def compute(item, slot):
        seqs = range(ATT_ITEM)
        rsl = [pl.ds(pl.multiple_of((item * ATT_ITEM + j) * t, t), t) for j in seqs]
        scs = []
        for j in seqs:
            qf = q_scr[rsl[j], :]
            q_bd = jnp.concatenate([jnp.where(head_of_lane == hd, qf, 0.0) for hd in range(ATT_HEADS)],
                                   axis=0).astype(BF16)
            scs.append(lax.dot_general(q_bd, kbuf[slot, j].astype(BF16), _NT, preferred_element_type=F32) * scale)
        ps = [_softmax_rows(sc).astype(BF16) for sc in scs]
        outs = [_dot(ps[j], vbuf[slot, j].astype(BF16)) for j in seqs]
        for j in seqs:
            att_scr[rsl[j], :] = jnp.concatenate(
                [outs[j][hd * t:(hd + 1) * t, hd * ATT_HEAD_DIM:(hd + 1) * ATT_HEAD_DIM] for hd in range(ATT_HEADS)],
                axis=1)

    _ring_pipeline(nb // ATT_ITEM, start_in, wait_in, compute)
    yatt_ref[...] = (att_scr[...] * _silu(za_ref[...].astype(F32))).astype(BF16)


def _att_state(main, k, v):
    nb, t = SAMPLE_BLOCK, SUBLANES
    rows = nb * t
    m = main.shape[0]
    col_spec = lambda width, idx: pl.BlockSpec((rows, width), lambda i: (i, idx))
    kv_buf = pltpu.VMEM((PIPE_SLOTS, ATT_ITEM, MEM_LEN, D_MODEL), F32)
    return pl.pallas_call(
        _att_state_kernel,
        grid=(m // rows,),
        in_specs=[col_spec(D_MODEL, _col_block(COL_Q)), col_spec(D_MODEL, _col_block(COL_ZA)),
                  pl.BlockSpec(memory_space=pl.ANY), pl.BlockSpec(memory_space=pl.ANY)],
        out_specs=col_spec(D_MODEL, 0),
        out_shape=jax.ShapeDtypeStruct((m, D_MODEL), BF16),
        scratch_shapes=[kv_buf, kv_buf, pltpu.VMEM((rows, D_MODEL), F32), pltpu.VMEM((rows, D_MODEL), F32),
                        pltpu.SemaphoreType.DMA((PIPE_SLOTS, ATT_ITEM * 2 * ATT_HEADS))],
        compiler_params=pltpu.CompilerParams(dimension_semantics=("arbitrary",), vmem_limit_bytes=VMEM_LIMIT),
        name="att_state",
    )(main, main, k, v)


DENSE_ROWS = 512


def _dense_kernel(x_ref, gt_ref, yp_ref, ys_ref, ya_ref, wpo_ref, wso_ref, wao_ref, wo_ref, fnw_ref, y_ref):
    gates = gt_ref[...].astype(F32)
    merged = (gates[:, 0:D_MODEL] * _dot(yp_ref[...], wpo_ref[...])
              + gates[:, D_MODEL:2 * D_MODEL] * _dot(ys_ref[...], wso_ref[...])
              + gates[:, 2 * D_MODEL:] * _dot(ya_ref[...], wao_ref[...]))
    x_out = x_ref[...] + _dot(merged.astype(BF16), wo_ref[...])
    y_ref[...] = _rms(x_out, fnw_ref[...])


def _dense(x2d, gates, gate_idx, yp, ys, ya, wpo, wso, wao, wo, fnw):
    m = x2d.shape[0]
    row = lambda width, idx=0: pl.BlockSpec((DENSE_ROWS, width), lambda i: (i, idx))
    resident = lambda a: pl.BlockSpec(a.shape, lambda i: (0,) * a.ndim, pipeline_mode=pl.Buffered(1))
    return pl.pallas_call(
        _dense_kernel,
        grid=(m // DENSE_ROWS,),
        in_specs=[row(D_MODEL), row(3 * D_MODEL, gate_idx), row(D_MODEL), row(SSD_WIDTH), row(D_MODEL),
                  resident(wpo), resident(wso), resident(wao), resident(wo), resident(fnw)],
        out_specs=row(D_MODEL),
        out_shape=jax.ShapeDtypeStruct((m, D_MODEL), F32),
        compiler_params=pltpu.CompilerParams(dimension_semantics=("arbitrary",), vmem_limit_bytes=VMEM_LIMIT),
        name="dense",
    )(x2d, gates, yp, ys, ya, wpo, wso, wao, wo, fnw)


def kernel(x_prompt, x_sample, mem_prompt, state_pool, state_conv, state_ssm, cache_mem_k, cache_mem_v,
           norm_w, w_in, w_pool_grp, pool_scale, conv_w, conv_b, dt_bias, a_log, d_skip, ssd_norm_w,
           mem_norm_w, w_mem_k, w_mem_v, w_pool_out, w_ssd_out, w_att_out, w_out, final_norm_w):
    assert w_in.shape[0] == 1
    bp, sp, d = x_prompt.shape
    bs, ss, _ = x_sample.shape
    assert ss == SUBLANES and sp % PROMPT_TILE == 0 and bs % SAMPLE_BLOCK == 0

    w_a = w_in[0][:, :W_SPLIT[0]].astype(BF16)
    w_b = w_in[0][:, W_SPLIT[1]:].astype(BF16)
    w_dt = jnp.pad(w_in[0][:, W_SPLIT[0]:W_SPLIT[1]], ((0, 0), (0, LANES - SSD_HEADS))).astype(BF16)
    nw = norm_w[0].reshape(1, d)
    pad_heads = lambda a: jnp.pad(a.reshape(1, SSD_HEADS), ((0, 0), (0, LANES - SSD_HEADS)))
    wgrp = w_pool_grp[0].astype(BF16)
    pscale = pool_scale[0].reshape(1, d)
    convb = conv_b[0].reshape(1, CONV_DIM)
    dtb, alog = pad_heads(dt_bias[0]), pad_heads(a_log[0])
    dexp = jnp.repeat(d_skip[0], SSD_HEAD_DIM).reshape(1, SSD_WIDTH)
    ssdnw = ssd_norm_w[0].reshape(1, SSD_WIDTH)
    dense_w = (w_pool_out[0].astype(BF16), w_ssd_out[0].astype(BF16), w_att_out[0].astype(BF16),
               w_out[0].astype(BF16), final_norm_w.reshape(1, d))
    head_of_lane = jnp.arange(SSD_WIDTH) // SSD_HEAD_DIM
    expand = (jnp.arange(LANES)[:, None] == head_of_lane[None, :]).astype(BF16)
    group_of_head = jnp.where(jnp.arange(LANES) < SSD_HEADS, jnp.arange(LANES) // (SSD_HEADS // SSD_GROUPS), -1)
    segsum = ((jnp.arange(GROUP_WIDTH) // SSD_STATE)[:, None] == group_of_head[None, :]).astype(BF16)

    mk, mv, mkb, mvb = _memkv(mem_prompt, mem_norm_w[0].reshape(1, d), w_mem_k[0].astype(BF16),
                              w_mem_v[0].astype(BF16))
    xp2 = x_prompt.reshape(bp * sp, d)
    gates_p, yp, ysd, ya, pool_p, conv_p, ssm_p = _seq_prompt(
        xp2, nw, w_a, w_b, w_dt, mkb, mvb, (wgrp, pscale, conv_w[0], convb, dtb, alog, dexp, ssdnw),
        nseq=bp, ntile=sp // PROMPT_TILE)
    y_prompt = _dense(xp2, gates_p, 0, yp, ysd, ya, *dense_w).reshape(bp, sp, d)

    xs2 = x_sample.reshape(bs * ss, d)
    main_s, dt_s = _inproj(xs2, nw, w_a, w_b, w_dt)
    yp, xs, bc, dts, cd, pool_s, conv_s = _state_pre(
        main_s, dt_s, state_pool[0], state_conv[0], (wgrp, pscale, conv_w[0], convb, dtb, alog))
    ysd, ssm_s = _ssd_state(xs, bc, dts, main_s, cd, state_ssm[0].reshape(bs, SSD_WIDTH, SSD_STATE),
                            expand, segsum, alog, dexp, ssdnw)
    ya = _att_state(main_s, cache_mem_k[0], cache_mem_v[0])
    y_sample = _dense(xs2, main_s, _col_block(COL_GATES), yp, ysd, ya, *dense_w).reshape(bs, ss, d)

    ssm_shape = (SSD_GROUPS, SSD_HEADS // SSD_GROUPS, SSD_HEAD_DIM, SSD_STATE)
    return (y_prompt, y_sample,
            pool_p[None], conv_p[None], ssm_p.reshape((1, bp) + ssm_shape),
            mk[None], mv[None],
            pool_s[None], conv_s[None], ssm_s.reshape((1, bs) + ssm_shape))
```

```python
import functools

import jax
import jax.numpy as jnp
from jax import lax
from jax.experimental import pallas as pl
from jax.experimental.pallas import tpu as pltpu

F32 = jnp.float32
BF16 = jnp.bfloat16

D_MODEL = 1024
POOL_WINDOWS = (2, 4, 8, 16)
POOL_GROUP = 256
POOL_HIST = 15
POOL_PAD = 16
SSD_WIDTH = 2048
SSD_HEADS = 32
SSD_HEAD_DIM = 64
SSD_GROUPS = 4
SSD_STATE = 128
GROUP_WIDTH = SSD_WIDTH // SSD_GROUPS
CONV_WIDTH = 4
CONV_DIM = 3072
CONV_PAD = 8
SSD_CHUNK = 128
MEM_LEN = 256
ATT_HEADS = 4
ATT_HEAD_DIM = 256
PAST_LEN = 16384
EPS = 1e-6
NEG_BIG = -1e30
SUBLANES = 8
LANES = 128
MAIN_COLS = 12288
COL_XBC, COL_GATES, COL_ZS = (0, 3072), (3072, 6144), (6144, 8192)
COL_U, COL_ZP, COL_Q, COL_ZA = (8192, 9216), (9216, 10240), (10240, 11264), (11264, 12288)
W_SPLIT = (7168, 7200)
W_SRC = {COL_U: (0, 0), COL_ZP: (0, 1024), COL_ZS: (0, 2048), COL_XBC: (0, 4096),
         COL_Q: (1, 0), COL_ZA: (1, 1024), COL_GATES: (1, 2048)}
VMEM_LIMIT = 56 * 1024 * 1024
SEQ_PROMPT_VMEM_LIMIT = 60 * 1024 * 1024

_NT = (((1,), (1,)), ((), ()))
_TN = (((0,), (0,)), ((), ()))


def _sigmoid(x):
    return 1.0 / (1.0 + jnp.exp(-x))


def _silu(x):
    return x * _sigmoid(x)


def _softplus(x):
    return jnp.maximum(x, 0.0) + jnp.log1p(jnp.exp(-jnp.abs(x)))


def _rms(x, w):
    return x * lax.rsqrt(jnp.mean(x * x, axis=-1, keepdims=True) + EPS) * w


def _dot(a, b):
    return jnp.dot(a, b, preferred_element_type=F32)


def _softmax_rows(sc):
    e = jnp.exp(sc - jnp.max(sc, axis=-1, keepdims=True))
    return e / jnp.sum(e, axis=-1, keepdims=True)


def _weight_cols(w_refs, piece, lo, hi):
    idx, c0 = W_SRC[piece]
    return w_refs[idx][:, c0 + lo:c0 + hi]


def _memkv_kernel(mem_ref, nw_ref, wk_ref, wv_ref, k_ref, v_ref, kb_ref, vb_ref):
    mh = _rms(mem_ref[0], nw_ref[...]).astype(BF16)
    k = _dot(mh, wk_ref[...])
    v = _dot(mh, wv_ref[...])
    for hd in range(ATT_HEADS):
        hsl = slice(hd * ATT_HEAD_DIM, (hd + 1) * ATT_HEAD_DIM)
        k_ref[0, :, hd, :] = k[:, hsl]
        v_ref[0, :, hd, :] = v[:, hsl]
        kb_ref[0, hd] = k[:, hsl].astype(BF16)
        vb_ref[0, hd] = v[:, hsl].astype(BF16)


def _memkv(mem, nw, wk, wv):
    b, m, d = mem.shape
    full = lambda shape: pl.BlockSpec(shape, lambda i: (0,) * len(shape))
    blk = pl.BlockSpec((1, m, d), lambda i: (i, 0, 0))
    oblk = pl.BlockSpec((1, m, ATT_HEADS, ATT_HEAD_DIM), lambda i: (i, 0, 0, 0))
    hblk = pl.BlockSpec((1, ATT_HEADS, m, ATT_HEAD_DIM), lambda i: (i, 0, 0, 0))
    return pl.pallas_call(
        _memkv_kernel,
        grid=(b,),
        in_specs=[blk, full((1, d)), full((d, d)), full((d, d))],
        out_specs=[oblk, oblk, hblk, hblk],
        out_shape=[jax.ShapeDtypeStruct((b, m, ATT_HEADS, ATT_HEAD_DIM), F32)] * 2
        + [jax.ShapeDtypeStruct((b, ATT_HEADS, m, ATT_HEAD_DIM), BF16)] * 2,
        compiler_params=pltpu.CompilerParams(dimension_semantics=("arbitrary",), vmem_limit_bytes=VMEM_LIMIT),
        name="memkv",
    )(mem, nw, wk, wv)


INPROJ_ROWS = 256
INPROJ_COL_CHUNK = 1024


def _inproj_kernel(x_ref, nw_ref, wa_ref, wb_ref, wdt_ref, main_ref, dt_ref):
    h = _rms(x_ref[...], nw_ref[...]).astype(BF16)
    dt_ref[...] = _dot(h, wdt_ref[...])
    for piece in W_SRC:
        for lo in range(0, piece[1] - piece[0], INPROJ_COL_CHUNK):
            val = _dot(h, _weight_cols((wa_ref, wb_ref), piece, lo, lo + INPROJ_COL_CHUNK))
            if piece == COL_GATES:
                val = _sigmoid(val)
            main_ref[:, piece[0] + lo:piece[0] + lo + INPROJ_COL_CHUNK] = val.astype(BF16)


def _inproj(x2d, nw, w_a, w_b, w_dt):
    m = x2d.shape[0]
    resident = lambda a: pl.BlockSpec(a.shape, lambda i: (0,) * a.ndim, pipeline_mode=pl.Buffered(1))
    return pl.pallas_call(
        _inproj_kernel,
        grid=(m // INPROJ_ROWS,),
        in_specs=[pl.BlockSpec((INPROJ_ROWS, D_MODEL), lambda i: (i, 0)),
                  resident(nw), resident(w_a), resident(w_b), resident(w_dt)],
        out_specs=[
            pl.BlockSpec((INPROJ_ROWS, MAIN_COLS), lambda i: (i, 0)),
            pl.BlockSpec((INPROJ_ROWS, LANES), lambda i: (i, 0)),
        ],
        out_shape=[jax.ShapeDtypeStruct((m, MAIN_COLS), BF16), jax.ShapeDtypeStruct((m, LANES), F32)],
        compiler_params=pltpu.CompilerParams(dimension_semantics=("arbitrary",), vmem_limit_bytes=VMEM_LIMIT),
        name="inproj",
    )(x2d, nw, w_a, w_b, w_dt)


def _pool_branch(pext, u, pos, wgrp_ref, pscale_ref, zp):
    nb, t, _ = u.shape
    ys = []
    for g, w in enumerate(POOL_WINDOWS):
        cols = slice(g * POOL_GROUP, (g + 1) * POOL_GROUP)
        win = pext[:, :, cols]
        for sh in [1 << e for e in range(g + 1)]:
            win = win + pltpu.roll(win, sh, axis=1)
        win = win[:, POOL_PAD:, :]
        inv_cnt = 1.0 / jnp.minimum(w, pos + 1).astype(F32)
        d = (win * inv_cnt - u[:, :, cols]).astype(BF16).reshape(nb * t, POOL_GROUP)
        ys.append(_dot(d, wgrp_ref[g]))
    return jnp.concatenate(ys, axis=1) * pscale_ref[...] * _silu(zp)


def _conv_branch(cext, convw_ref, convb_ref, store):
    for cc in range(CONV_DIM // GROUP_WIDTH):
        csl = slice(cc * GROUP_WIDTH, (cc + 1) * GROUP_WIDTH)
        ext = cext[:, :, csl]
        conv = convb_ref[:, csl].reshape(1, 1, GROUP_WIDTH)
        for kk in range(CONV_WIDTH):
            tap = ext if kk == CONV_WIDTH - 1 else pltpu.roll(ext, CONV_WIDTH - 1 - kk, axis=1)
            conv = conv + tap * convw_ref[kk:kk + 1, csl].reshape(1, 1, GROUP_WIDTH)
        store(cc, _silu(conv[:, CONV_PAD:, :]))


PROMPT_TILE = 256


def _seq_prompt_kernel(ntile, x_ref, nw_ref, wa_ref, wb_ref, wdt_ref, k_ref, v_ref,
                       wgrp_ref, pscale_ref, convw_ref, convb_ref, dtb_ref, alog_ref, dexp_ref, ssdnw_ref,
                       gates_ref, ypool_ref, yssd_ref, yatt_ref, pool_o_ref, conv_o_ref, ssm_o_ref,
                       pext, cext, xs_scr, b_scr, c_scr, dt_scr, y_scr, h_scr, zs_scr, q_scr, za_scr):
    t, q = PROMPT_TILE, SSD_CHUNK
    k = pl.program_id(0)
    s = k % ntile
    last = ntile - 1

    @pl.when(k == 0)
    def _():
        y_scr[...] = jnp.zeros(y_scr.shape, F32)
        zs_scr[...] = jnp.zeros(zs_scr.shape, BF16)
        q_scr[...] = jnp.zeros(q_scr.shape, BF16)
        za_scr[...] = jnp.zeros(za_scr.shape, BF16)

    @pl.when(s == 0)
    def _():
        pext[:, 0:POOL_PAD, :] = jnp.zeros((1, POOL_PAD, D_MODEL), F32)
        cext[:, 0:CONV_PAD, :] = jnp.zeros((1, CONV_PAD, CONV_DIM), F32)
        h_scr[...] = jnp.zeros(h_scr.shape, F32)

    @pl.when(s > 0)
    def _():
        carry_p = pext[:, t:t + POOL_PAD, :]
        carry_c = cext[:, t:t + CONV_PAD, :]
        pext[:, 0:POOL_PAD, :] = carry_p
        cext[:, 0:CONV_PAD, :] = carry_c

    yz = y_scr[...] * zs_scr[...].astype(F32)
    yssd_ref[...] = _rms(yz, ssdnw_ref[...]).astype(BF16)
    scale = ATT_HEAD_DIM ** -0.5
    outs = []
    for hd in range(ATT_HEADS):
        hsl = slice(hd * ATT_HEAD_DIM, (hd + 1) * ATT_HEAD_DIM)
        p = _softmax_rows(lax.dot_general(q_scr[:, hsl], k_ref[0, hd], _NT, preferred_element_type=F32) * scale)
        outs.append(_dot(p.astype(BF16), v_ref[0, hd]))
    yatt_ref[...] = (jnp.concatenate(outs, axis=1) * za_scr[...].astype(F32)).astype(BF16)

    hn = _rms(x_ref[...], nw_ref[...]).astype(BF16)

    def proj(piece, lo=0, hi=None):
        return _dot(hn, _weight_cols((wa_ref, wb_ref), piece, lo, piece[1] - piece[0] if hi is None else hi))

    def gates_piece(c):
        def run():
            gates_ref[:, c * D_MODEL:(c + 1) * D_MODEL] = _sigmoid(
                proj(COL_GATES, c * D_MODEL, (c + 1) * D_MODEL).astype(BF16))
        return run

    def zs_piece(c):
        def run():
            zs_scr[:, c * D_MODEL:(c + 1) * D_MODEL] = _silu(proj(COL_ZS, c * D_MODEL, (c + 1) * D_MODEL).astype(BF16))
        return run

    def q_piece():
        q_scr[...] = proj(COL_Q).astype(BF16)

    def za_piece():
        za_scr[...] = _silu(proj(COL_ZA).astype(BF16))

    fillers = [gates_piece(0), gates_piece(1), gates_piece(2), zs_piece(0), zs_piece(1), q_piece, za_piece]

    def run_filler():
        if fillers:
            fillers.pop(0)()

    u = proj(COL_U).reshape(1, t, D_MODEL)
    pext[:, POOL_PAD:, :] = u
    pos = s * t + lax.broadcasted_iota(jnp.int32, (1, t, 1), 1)
    ypool_ref[...] = _pool_branch(pext, u, pos, wgrp_ref, pscale_ref, proj(COL_ZP)).astype(BF16)

    for c in range(CONV_DIM // D_MODEL):
        cext[:, CONV_PAD:, c * D_MODEL:(c + 1) * D_MODEL] = proj(
            COL_XBC, c * D_MODEL, (c + 1) * D_MODEL).reshape(1, t, D_MODEL)
    dt_scr[...] = _softplus(_dot(hn, wdt_ref[...]) + dtb_ref[...])

    def store_conv(cc, val):
        if cc < SSD_GROUPS:
            xs_scr[:, cc * GROUP_WIDTH:(cc + 1) * GROUP_WIDTH] = val[0]
        elif cc == SSD_GROUPS:
            b_scr[...] = val[0]
        else:
            c_scr[...] = val[0]
        run_filler()

    _conv_branch(cext, convw_ref, convb_ref, store_conv)

    a_neg = -jnp.exp(alog_ref[...])
    rq = lax.broadcasted_iota(jnp.int32, (q, q), 0)
    cq = lax.broadcasted_iota(jnp.int32, (q, q), 1)
    tril = rq >= cq
    tri_f = tril.astype(F32)
    lane_lo = lax.broadcasted_iota(jnp.int32, (1, LANES), 1) < SSD_HEAD_DIM
    pairs_per_group = SSD_HEADS // SSD_GROUPS // 2

    def chunk(c):
        rsl = slice(c * q, (c + 1) * q)
        dtc = dt_scr[rsl, :]
        acs = jnp.dot(tri_f, dtc * a_neg, precision=lax.Precision.HIGHEST, preferred_element_type=F32)
        acs_t = acs.T
        dt_t = dtc.T
        wdec_t = (dt_t * jnp.exp(acs_t[:, q - 1:q] - acs_t)).astype(BF16)
        row_t = acs_t - jnp.log(dt_t)
        cdec = jnp.exp(acs[q - 1:q, :])
        for g in range(SSD_GROUPS):
            gsl = slice(g * SSD_STATE, (g + 1) * SSD_STATE)
            bg = b_scr[rsl, gsl]
            cg_b = c_scr[rsl, gsl].astype(BF16)
            cb = lax.dot_general(cg_b, bg.astype(BF16), _NT, preferred_element_type=F32).astype(BF16)
            bg_t = bg.T.astype(BF16)
            hsl = slice(g * GROUP_WIDTH, (g + 1) * GROUP_WIDTH)
            z_g = _dot(cg_b, h_scr[:, hsl].astype(BF16))
            for jp in range(pairs_per_group):
                j = g * pairs_per_group + jp
                lsl = slice(j * LANES, (j + 1) * LANES)
                xp = xs_scr[rsl, lsl]
                xp_b = xp.astype(BF16)
                zero_b = jnp.zeros_like(xp_b)
                x_bd = jnp.concatenate([jnp.where(lane_lo, xp_b, zero_b), jnp.where(lane_lo, zero_b, xp_b)],
                                       axis=0)
                ms, bws, cols = [], [], []
                for hh in range(2):
                    r = 2 * j + hh
                    cols.append(jnp.broadcast_to(acs[:, r:r + 1], (q, q)))
                    seg = cols[hh] - row_t[r:r + 1, :]
                    ms.append(cb * jnp.exp(jnp.where(tril, seg, NEG_BIG)).astype(BF16))
                    bws.append(bg_t * wdec_t[r:r + 1, :])
                ea_pair = jnp.exp(jnp.where(lane_lo, cols[0], cols[1]))
                y = (_dot(jnp.concatenate(ms, axis=1), x_bd)
                     + ea_pair * z_g[:, jp * LANES:(jp + 1) * LANES] + dexp_ref[:, lsl] * xp)
                y_scr[rsl, lsl] = y
                cd_pair = jnp.where(lane_lo, cdec[:, 2 * j:2 * j + 1], cdec[:, 2 * j + 1:2 * j + 2])
                h_scr[:, lsl] = h_scr[:, lsl] * cd_pair + _dot(jnp.concatenate(bws, axis=1), x_bd)
            run_filler()

    for c in range(t // q):
        chunk(c)
    while fillers:
        run_filler()

    @pl.when(jnp.logical_and(s == last, k < pl.num_programs(0) - 1))
    def _():
        pool_o_ref[...] = pext[:, t + POOL_PAD - POOL_HIST:t + POOL_PAD, :]
        conv_o_ref[...] = cext[:, t + CONV_PAD - (CONV_WIDTH - 1):t + CONV_PAD, :]
        for j in range(SSD_HEADS // 2):
            lsl = slice(j * LANES, (j + 1) * LANES)
            ssm_o_ref[0, lsl, :] = h_scr[:, lsl].T


def _seq_prompt(x2d, nw, w_a, w_b, w_dt, kb, vb, params, *, nseq, ntile):
    t = PROMPT_TILE
    m = x2d.shape[0]
    ntiles = nseq * ntile
    tile = lambda k: jnp.minimum(k, ntiles - 1)
    closed = lambda k: jnp.maximum(k - 1, 0)
    own_rows = lambda width: pl.BlockSpec((t, width), lambda k: (k, 0))
    closed_rows = lambda width: pl.BlockSpec((t, width), lambda k: (closed(k), 0))
    seq_spec = lambda shape, which: pl.BlockSpec((1,) + shape, lambda k: (which(k) // ntile,) + (0,) * len(shape))
    const_spec = lambda a: pl.BlockSpec(a.shape, lambda k: (0,) * a.ndim)
    resident = lambda a: pl.BlockSpec(a.shape, lambda k: (0,) * a.ndim, pipeline_mode=pl.Buffered(1))
    in_specs = [
        pl.BlockSpec((t, D_MODEL), lambda k: (tile(k), 0)),
        resident(nw), resident(w_a), resident(w_b), resident(w_dt),
        seq_spec((ATT_HEADS, MEM_LEN, ATT_HEAD_DIM), closed),
        seq_spec((ATT_HEADS, MEM_LEN, ATT_HEAD_DIM), closed),
    ] + [const_spec(p) for p in params]
    out_specs = [
        own_rows(3 * D_MODEL), own_rows(D_MODEL), closed_rows(SSD_WIDTH), closed_rows(D_MODEL),
        seq_spec((POOL_HIST, D_MODEL), tile),
        seq_spec((CONV_WIDTH - 1, CONV_DIM), tile),
        seq_spec((SSD_WIDTH, SSD_STATE), tile),
    ]
    out_shape = [
        jax.ShapeDtypeStruct((m + t, 3 * D_MODEL), BF16),
        jax.ShapeDtypeStruct((m + t, D_MODEL), BF16),
        jax.ShapeDtypeStruct((m, SSD_WIDTH), BF16),
        jax.ShapeDtypeStruct((m, D_MODEL), BF16),
        jax.ShapeDtypeStruct((nseq, POOL_HIST, D_MODEL), F32),
        jax.ShapeDtypeStruct((nseq, CONV_WIDTH - 1, CONV_DIM), F32),
        jax.ShapeDtypeStruct((nseq, SSD_WIDTH, SSD_STATE), F32),
    ]
    scratch = [
        pltpu.VMEM((1, POOL_PAD + t, D_MODEL), F32),
        pltpu.VMEM((1, CONV_PAD + t, CONV_DIM), F32),
        pltpu.VMEM((t, SSD_WIDTH), F32),
        pltpu.VMEM((t, GROUP_WIDTH), F32),
        pltpu.VMEM((t, GROUP_WIDTH), F32),
        pltpu.VMEM((t, LANES), F32),
        pltpu.VMEM((t, SSD_WIDTH), F32),
        pltpu.VMEM((SSD_STATE, SSD_WIDTH), F32),
        pltpu.VMEM((t, SSD_WIDTH), BF16),
        pltpu.VMEM((t, D_MODEL), BF16),
        pltpu.VMEM((t, D_MODEL), BF16),
    ]
    return pl.pallas_call(
        functools.partial(_seq_prompt_kernel, ntile),
        grid=(ntiles + 1,),
        in_specs=in_specs,
        out_specs=out_specs,
        out_shape=out_shape,
        scratch_shapes=scratch,
        compiler_params=pltpu.CompilerParams(dimension_semantics=("arbitrary",),
                                             vmem_limit_bytes=SEQ_PROMPT_VMEM_LIMIT),
        name="seq_prompt",
    )(x2d, nw, w_a, w_b, w_dt, kb, vb, *params)


SAMPLE_BLOCK = 32
STATE_PRE_SUB = 4


def _state_pre_kernel(u_ref, zp_ref, xbc_ref, dt_ref, ph_ref, ch_ref,
                      wgrp_ref, pscale_ref, convw_ref, convb_ref, dtb_ref, alog_ref,
                      ypool_ref, xs_ref, bc_ref, dts_ref, cd_ref, pool_o_ref, conv_o_ref, pext, cext):
    nb, t = SAMPLE_BLOCK, SUBLANES
    pext[:, 0:1, :] = jnp.zeros((nb, 1, D_MODEL), F32)
    pext[:, 1:POOL_PAD, :] = ph_ref[...]
    cext[:, 0:CONV_PAD - (CONV_WIDTH - 1), :] = jnp.zeros((nb, CONV_PAD - (CONV_WIDTH - 1), CONV_DIM), F32)
    cext[:, CONV_PAD - (CONV_WIDTH - 1):CONV_PAD, :] = ch_ref[...]

    pext[:, POOL_PAD:, :] = u_ref[...].astype(F32).reshape(nb, t, D_MODEL)
    cext[:, CONV_PAD:, :] = xbc_ref[...].astype(F32).reshape(nb, t, CONV_DIM)
    pos = PAST_LEN + lax.broadcasted_iota(jnp.int32, (1, t, 1), 1)
    for sb in range(nb // STATE_PRE_SUB):
        seqs = slice(sb * STATE_PRE_SUB, (sb + 1) * STATE_PRE_SUB)
        rsl = slice(sb * STATE_PRE_SUB * t, (sb + 1) * STATE_PRE_SUB * t)
        pv, cv = pext.at[seqs], cext.at[seqs]
        ypool_ref[rsl, :] = _pool_branch(pv, pv[:, POOL_PAD:, :], pos, wgrp_ref, pscale_ref,
                                         zp_ref[rsl, :].astype(F32)).astype(BF16)

        def store_conv(cc, val, rsl=rsl):
            val = val.reshape(STATE_PRE_SUB * t, GROUP_WIDTH)
            if cc < SSD_GROUPS:
                xs_ref[rsl, cc * GROUP_WIDTH:(cc + 1) * GROUP_WIDTH] = val
            else:
                bc_ref[rsl, (cc - SSD_GROUPS) * GROUP_WIDTH:(cc - SSD_GROUPS + 1) * GROUP_WIDTH] = val

        _conv_branch(cv, convw_ref, convb_ref, store_conv)
    dt = _softplus(dt_ref[...] + dtb_ref[...])
    dts_ref[...] = dt
    a = dt * -jnp.exp(alog_ref[...])
    cd_ref[...] = jnp.exp(jnp.sum(a.reshape(nb, t, LANES), axis=1))
    pool_o_ref[...] = pext[:, t + POOL_PAD - POOL_HIST:t + POOL_PAD, :]
    conv_o_ref[...] = cext[:, t + CONV_PAD - (CONV_WIDTH - 1):t + CONV_PAD, :]


def _col_block(piece):
    return piece[0] // (piece[1] - piece[0])


def _state_pre(main, dt, state_pool, state_conv, params):
    nb, t = SAMPLE_BLOCK, SUBLANES
    rows = nb * t
    m = main.shape[0]
    nseq = m // t
    col_spec = lambda width, idx: pl.BlockSpec((rows, width), lambda i: (i, idx))
    seq_spec = lambda shape: pl.BlockSpec((nb,) + shape, lambda i: (i,) + (0,) * len(shape))
    const_spec = lambda a: pl.BlockSpec(a.shape, lambda i: (0,) * a.ndim)
    return pl.pallas_call(
        _state_pre_kernel,
        grid=(nseq // nb,),
        in_specs=[col_spec(D_MODEL, _col_block(COL_U)), col_spec(D_MODEL, _col_block(COL_ZP)),
                  col_spec(CONV_DIM, _col_block(COL_XBC)), col_spec(LANES, 0),
                  seq_spec((POOL_HIST, D_MODEL)), seq_spec((CONV_WIDTH - 1, CONV_DIM))]
        + [const_spec(p) for p in params],
        out_specs=[col_spec(D_MODEL, 0), col_spec(SSD_WIDTH, 0), col_spec(2 * GROUP_WIDTH, 0), col_spec(LANES, 0),
                   pl.BlockSpec((nb, LANES), lambda i: (i, 0)),
                   seq_spec((POOL_HIST, D_MODEL)), seq_spec((CONV_WIDTH - 1, CONV_DIM))],
        out_shape=[jax.ShapeDtypeStruct((m, D_MODEL), BF16), jax.ShapeDtypeStruct((m, SSD_WIDTH), F32),
                   jax.ShapeDtypeStruct((m, 2 * GROUP_WIDTH), F32), jax.ShapeDtypeStruct((m, LANES), F32),
                   jax.ShapeDtypeStruct((nseq, LANES), F32),
                   jax.ShapeDtypeStruct((nseq, POOL_HIST, D_MODEL), F32),
                   jax.ShapeDtypeStruct((nseq, CONV_WIDTH - 1, CONV_DIM), F32)],
        scratch_shapes=[pltpu.VMEM((nb, POOL_PAD + t, D_MODEL), F32), pltpu.VMEM((nb, CONV_PAD + t, CONV_DIM), F32)],
        compiler_params=pltpu.CompilerParams(dimension_semantics=("arbitrary",), vmem_limit_bytes=VMEM_LIMIT),
        name="state_pre",
    )(main, main, main, dt, state_pool, state_conv, *params)


PIPE_SLOTS = 4


def _ring_pipeline(n, start_in, wait_in, compute, start_out=None, wait_out=None):
    ns = PIPE_SLOTS
    for i in range(ns - 1):
        start_in(i, i)

    def body(bb, _):
        for k in range(ns):
            i = ns * bb + k
            nxt = i + ns - 1
            pl.when(nxt < n)(functools.partial(start_in, nxt, (k + ns - 1) % ns))
            wait_in(i, k)
            if wait_out is not None:
                pl.when(bb > 0)(functools.partial(wait_out, i - ns, k))
            compute(i, k)
            if start_out is not None:
                start_out(i, k)
        return 0

    lax.fori_loop(0, n // ns, body, 0)
    if wait_out is not None:
        for k in range(ns):
            wait_out(n - ns + k, k)


SSD_ITEM = 4


def _ssd_state_kernel(xs_ref, bc_ref, dt_ref, zs_ref, cd_ref, hin_hbm, expand_ref, segsum_ref, alog_ref, dexp_ref,
                      ssdnw_ref, yssd_ref, hout_hbm, hbuf, obuf, y_scr, sem_in, sem_out):
    nb, t = SAMPLE_BLOCK, SUBLANES
    base = pl.program_id(0) * nb
    a_neg = -jnp.exp(alog_ref[...])
    ridx = lax.broadcasted_iota(jnp.int32, (t, LANES), 0)

    def in_copies(item, slot):
        return [pltpu.make_async_copy(hin_hbm.at[base + item * SSD_ITEM + j, g * GROUP_WIDTH:(g + 1) * GROUP_WIDTH, :],
                                      hbuf.at[slot, j, g * GROUP_WIDTH:(g + 1) * GROUP_WIDTH, :],
                                      sem_in.at[slot, j * SSD_GROUPS + g])
                for j in range(SSD_ITEM) for g in range(SSD_GROUPS)]

    def out_copies(item, slot):
        return [pltpu.make_async_copy(obuf.at[slot, j, g * GROUP_WIDTH:(g + 1) * GROUP_WIDTH, :],
                                      hout_hbm.at[base + item * SSD_ITEM + j, g * GROUP_WIDTH:(g + 1) * GROUP_WIDTH, :],
                                      sem_out.at[slot, j * SSD_GROUPS + g])
                for j in range(SSD_ITEM) for g in range(SSD_GROUPS)]

    def start_all(copies):
        for c in copies:
            c.start()

    def wait_all(copies):
        for c in copies:
            c.wait()

    def per_head_factors(b):
        rsl = pl.ds(pl.multiple_of(b * t, t), t)
        dtc = dt_ref[rsl, :]
        acs = dtc * a_neg
        for sh in (1, 2, 4):
            acs = acs + jnp.where(ridx >= sh, pltpu.roll(acs, sh, axis=0), 0.0)
        tot = acs[t - 1:t, :]
        bc = bc_ref[rsl, :]
        bm, cm = bc[:, :GROUP_WIDTH], bc[:, GROUP_WIDTH:]
        bm_r, cm_r = bm.astype(BF16).astype(F32), cm.astype(BF16).astype(F32)
        gs, ps = [], []
        for k in range(t):
            gs.append(jnp.exp(jnp.where(ridx >= k, acs - acs[k:k + 1, :], NEG_BIG)) * dtc[k:k + 1, :])
            ps.append(cm_r * bm_r[k:k + 1, :])
        cb_heads = _dot(jnp.concatenate(ps, axis=0).astype(BF16), segsum_ref[...])
        per_head = jnp.concatenate([jnp.concatenate(gs, axis=0) * cb_heads, jnp.exp(acs),
                                    dtc * jnp.exp(tot - acs)], axis=0)
        return dict(rsl=rsl, b=b, bm=bm, cm=cm, per_head=per_head)

    def widen(st):
        hi = st["per_head"].astype(BF16)
        lo = (st["per_head"] - hi.astype(F32)).astype(BF16)
        st["wide"] = _dot(hi, expand_ref[...]) + _dot(lo, expand_ref[...])

    def intra(st):
        x = xs_ref[st["rsl"], :]
        wide = st["wide"]
        y = dexp_ref[...] * x
        for k in range(t):
            y = y + wide[k * t:(k + 1) * t, :] * x[k:k + 1, :]
        st["y"] = y
        st["ea_wide"] = wide[t * t:t * t + t, :]
        st["xw"] = x * wide[t * t + t:, :]

    def state_group(st, slot, j, g):
        gsl = slice(g * SSD_STATE, (g + 1) * SSD_STATE)
        wsl = slice(g * GROUP_WIDTH, (g + 1) * GROUP_WIDTH)
        hg = hbuf[slot, j, wsl, :]
        z_g = lax.dot_general(st["cm"][:, gsl].astype(BF16), hg.astype(BF16), _NT, preferred_element_type=F32)
        y_scr[st["rsl"], wsl] = st["y"][:, wsl] + st["ea_wide"][:, wsl] * z_g
        upd = lax.dot_general(st["xw"][:, wsl].astype(BF16), st["bm"][:, gsl].astype(BF16), _TN,
                              preferred_element_type=F32)
        for r8 in range(SSD_HEADS // SSD_GROUPS):
            r = g * (SSD_HEADS // SSD_GROUPS) + r8
            rows_r = slice(r * SSD_HEAD_DIM, (r + 1) * SSD_HEAD_DIM)
            obuf[slot, j, rows_r, :] = (hbuf[slot, j, rows_r, :] * cd_ref[base + st["b"], r]
                                        + upd[r8 * SSD_HEAD_DIM:(r8 + 1) * SSD_HEAD_DIM, :])

    def compute(item, slot):
        sts = [per_head_factors(item * SSD_ITEM + j) for j in range(SSD_ITEM)]
        for st in sts:
            widen(st)
        for st in sts:
            intra(st)
        for g in range(SSD_GROUPS):
            for j, st in enumerate(sts):
                state_group(st, slot, j, g)

    _ring_pipeline(nb // SSD_ITEM,
                   lambda i, slot: start_all(in_copies(i, slot)), lambda i, slot: wait_all(in_copies(i, slot)),
                   compute,
                   lambda i, slot: start_all(out_copies(i, slot)), lambda i, slot: wait_all(out_copies(i, slot)))
    yz = y_scr[...] * _silu(zs_ref[...].astype(F32))
    yssd_ref[...] = _rms(yz, ssdnw_ref[...]).astype(BF16)


def _ssd_state(xs, bc, dts, main, cd, hin, expand, segsum, alog, dexp, ssdnw):
    nb, t = SAMPLE_BLOCK, SUBLANES
    rows = nb * t
    m = xs.shape[0]
    nseq = m // t
    col_spec = lambda width, idx: pl.BlockSpec((rows, width), lambda i: (i, idx))
    const_spec = lambda a: pl.BlockSpec(a.shape, lambda i: (0,) * a.ndim)
    return pl.pallas_call(
        _ssd_state_kernel,
        grid=(nseq // nb,),
        in_specs=[col_spec(SSD_WIDTH, 0), col_spec(2 * GROUP_WIDTH, 0), col_spec(LANES, 0),
                  col_spec(SSD_WIDTH, _col_block(COL_ZS)),
                  pl.BlockSpec(memory_space=pltpu.SMEM), pl.BlockSpec(memory_space=pl.ANY),
                  const_spec(expand), const_spec(segsum), const_spec(alog), const_spec(dexp), const_spec(ssdnw)],
        out_specs=[col_spec(SSD_WIDTH, 0), pl.BlockSpec(memory_space=pl.ANY)],
        out_shape=[jax.ShapeDtypeStruct((m, SSD_WIDTH), BF16), jax.ShapeDtypeStruct(hin.shape, F32)],
        scratch_shapes=[pltpu.VMEM((PIPE_SLOTS, SSD_ITEM, SSD_WIDTH, SSD_STATE), F32),
                        pltpu.VMEM((PIPE_SLOTS, SSD_ITEM, SSD_WIDTH, SSD_STATE), F32),
                        pltpu.VMEM((rows, SSD_WIDTH), F32),
                        pltpu.SemaphoreType.DMA((PIPE_SLOTS, SSD_ITEM * SSD_GROUPS)),
                        pltpu.SemaphoreType.DMA((PIPE_SLOTS, SSD_ITEM * SSD_GROUPS))],
        compiler_params=pltpu.CompilerParams(dimension_semantics=("arbitrary",), vmem_limit_bytes=VMEM_LIMIT),
        name="ssd_state",
    )(xs, bc, dts, main, cd, hin, expand, segsum, alog, dexp, ssdnw)


ATT_ITEM = 4


def _att_state_kernel(q_ref, za_ref, k_hbm, v_hbm, yatt_ref, kbuf, vbuf, q_scr, att_scr, sem):
    nb, t = SAMPLE_BLOCK, SUBLANES
    base = pl.program_id(0) * nb
    scale = ATT_HEAD_DIM ** -0.5
    q_scr[...] = q_ref[...].astype(F32)
    head_of_lane = lax.broadcasted_iota(jnp.int32, (1, D_MODEL), 1) // ATT_HEAD_DIM

    def copies(item, slot):
        out = []
        for j in range(ATT_ITEM):
            for kv, (src, buf) in enumerate(((k_hbm, kbuf), (v_hbm, vbuf))):
                for hd in range(ATT_HEADS):
                    hsl = slice(hd * ATT_HEAD_DIM, (hd + 1) * ATT_HEAD_DIM)
                    out.append(pltpu.make_async_copy(
                        src.at[base + item * ATT_ITEM + j, :, hd, :], buf.at[slot, j, :, hsl],
                        sem.at[slot, (j * 2 + kv) * ATT_HEADS + hd]))
        return out

    def start_in(item, slot):
        for c in copies(item, slot):
            c.start()

    def wait_in(item, slot):
        for c in copies(item, slot):
            c.wait()

    def compute(item, slot):
        seqs = range(ATT_ITEM)
        rsl = [pl.ds(pl.multiple_of((item * ATT_ITEM + j) * t, t), t) for j in seqs]
        scs = []
        for j in seqs:
            qf = q_scr[rsl[j], :]
            q_bd = jnp.concatenate([jnp.where(head_of_lane == hd, qf, 0.0) for hd in range(ATT_HEADS)],
                                   axis=0).astype(BF16)
            scs.append(lax.dot_general(q_bd, kbuf[slot, j].astype(BF16), _NT, preferred_element_type=F32) * scale)
        ps = [_softmax_rows(sc).astype(BF16) for sc in scs]
        outs = [_dot(ps[j], vbuf[slot, j].astype(BF16)) for j in seqs]
        for j in seqs:
            att_scr[rsl[j], :] = jnp.concatenate(
                [outs[j][hd * t:(hd + 1) * t, hd * ATT_HEAD_DIM:(hd + 1) * ATT_HEAD_DIM] for hd in range(ATT_HEADS)],
                axis=1)

    _ring_pipeline(nb // ATT_ITEM, start_in, wait_in, compute)
    yatt_ref[...] = (att_scr[...] * _silu(za_ref[...].astype(F32))).astype(BF16)


def _att_state(main, k, v):
    nb, t = SAMPLE_BLOCK, SUBLANES
    rows = nb * t
    m = main.shape[0]
    col_spec = lambda width, idx: pl.BlockSpec((rows, width), lambda i: (i, idx))
    kv_buf = pltpu.VMEM((PIPE_SLOTS, ATT_ITEM, MEM_LEN, D_MODEL), F32)
    return pl.pallas_call(
        _att_state_kernel,
        grid=(m // rows,),
        in_specs=[col_spec(D_MODEL, _col_block(COL_Q)), col_spec(D_MODEL, _col_block(COL_ZA)),
                  pl.BlockSpec(memory_space=pl.ANY), pl.BlockSpec(memory_space=pl.ANY)],
        out_specs=col_spec(D_MODEL, 0),
        out_shape=jax.ShapeDtypeStruct((m, D_MODEL), BF16),
        scratch_shapes=[kv_buf, kv_buf, pltpu.VMEM((rows, D_MODEL), F32), pltpu.VMEM((rows, D_MODEL), F32),
                        pltpu.SemaphoreType.DMA((PIPE_SLOTS, ATT_ITEM * 2 * ATT_HEADS))],
        compiler_params=pltpu.CompilerParams(dimension_semantics=("arbitrary",), vmem_limit_bytes=VMEM_LIMIT),
        name="att_state",
    )(main, main, k, v)


DENSE_ROWS = 512


def _dense_kernel(x_ref, gt_ref, yp_ref, ys_ref, ya_ref, wpo_ref, wso_ref, wao_ref, wo_ref, fnw_ref, y_ref):
    gates = gt_ref[...].astype(F32)
    merged = (gates[:, 0:D_MODEL] * _dot(yp_ref[...], wpo_ref[...])
              + gates[:, D_MODEL:2 * D_MODEL] * _dot(ys_ref[...], wso_ref[...])
              + gates[:, 2 * D_MODEL:] * _dot(ya_ref[...], wao_ref[...]))
    x_out = x_ref[...] + _dot(merged.astype(BF16), wo_ref[...])
    y_ref[...] = _rms(x_out, fnw_ref[...])


def _dense(x2d, gates, gate_idx, yp, ys, ya, wpo, wso, wao, wo, fnw):
    m = x2d.shape[0]
    row = lambda width, idx=0: pl.BlockSpec((DENSE_ROWS, width), lambda i: (i, idx))
    resident = lambda a: pl.BlockSpec(a.shape, lambda i: (0,) * a.ndim, pipeline_mode=pl.Buffered(1))
    return pl.pallas_call(
        _dense_kernel,
        grid=(m // DENSE_ROWS,),
        in_specs=[row(D_MODEL), row(3 * D_MODEL, gate_idx), row(D_MODEL), row(SSD_WIDTH), row(D_MODEL),
                  resident(wpo), resident(wso), resident(wao), resident(wo), resident(fnw)],
        out_specs=row(D_MODEL),
        out_shape=jax.ShapeDtypeStruct((m, D_MODEL), F32),
        compiler_params=pltpu.CompilerParams(dimension_semantics=("arbitrary",), vmem_limit_bytes=VMEM_LIMIT),
        name="dense",
    )(x2d, gates, yp, ys, ya, wpo, wso, wao, wo, fnw)


def kernel(x_prompt, x_sample, mem_prompt, state_pool, state_conv, state_ssm, cache_mem_k, cache_mem_v,
           norm_w, w_in, w_pool_grp, pool_scale, conv_w, conv_b, dt_bias, a_log, d_skip, ssd_norm_w,
           mem_norm_w, w_mem_k, w_mem_v, w_pool_out, w_ssd_out, w_att_out, w_out, final_norm_w):
    assert w_in.shape[0] == 1
    bp, sp, d = x_prompt.shape
    bs, ss, _ = x_sample.shape
    assert ss == SUBLANES and sp % PROMPT_TILE == 0 and bs % SAMPLE_BLOCK == 0

    w_a = w_in[0][:, :W_SPLIT[0]].astype(BF16)
    w_b = w_in[0][:, W_SPLIT[1]:].astype(BF16)
    w_dt = jnp.pad(w_in[0][:, W_SPLIT[0]:W_SPLIT[1]], ((0, 0), (0, LANES - SSD_HEADS))).astype(BF16)
    nw = norm_w[0].reshape(1, d)
    pad_heads = lambda a: jnp.pad(a.reshape(1, SSD_HEADS), ((0, 0), (0, LANES - SSD_HEADS)))
    wgrp = w_pool_grp[0].astype(BF16)
    pscale = pool_scale[0].reshape(1, d)
    convb = conv_b[0].reshape(1, CONV_DIM)
    dtb, alog = pad_heads(dt_bias[0]), pad_heads(a_log[0])
    dexp = jnp.repeat(d_skip[0], SSD_HEAD_DIM).reshape(1, SSD_WIDTH)
    ssdnw = ssd_norm_w[0].reshape(1, SSD_WIDTH)
    dense_w = (w_pool_out[0].astype(BF16), w_ssd_out[0].astype(BF16), w_att_out[0].astype(BF16),
               w_out[0].astype(BF16), final_norm_w.reshape(1, d))
    head_of_lane = jnp.arange(SSD_WIDTH) // SSD_HEAD_DIM
    expand = (jnp.arange(LANES)[:, None] == head_of_lane[None, :]).astype(BF16)
    group_of_head = jnp.where(jnp.arange(LANES) < SSD_HEADS, jnp.arange(LANES) // (SSD_HEADS // SSD_GROUPS), -1)
    segsum = ((jnp.arange(GROUP_WIDTH) // SSD_STATE)[:, None] == group_of_head[None, :]).astype(BF16)

    mk, mv, mkb, mvb = _memkv(mem_prompt, mem_norm_w[0].reshape(1, d), w_mem_k[0].astype(BF16),
                              w_mem_v[0].astype(BF16))
    xp2 = x_prompt.reshape(bp * sp, d)
    gates_p, yp, ysd, ya, pool_p, conv_p, ssm_p = _seq_prompt(
        xp2, nw, w_a, w_b, w_dt, mkb, mvb, (wgrp, pscale, conv_w[0], convb, dtb, alog, dexp, ssdnw),
        nseq=bp, ntile=sp // PROMPT_TILE)
    y_prompt = _dense(xp2, gates_p, 0, yp, ysd, ya, *dense_w).reshape(bp, sp, d)

    xs2 = x_sample.reshape(bs * ss, d)
    main_s, dt_s = _inproj(xs2, nw, w_a, w_b, w_dt)
    yp, xs, bc, dts, cd, pool_s, conv_s = _state_pre(
        main_s, dt_s, state_pool[0], state_conv[0], (wgrp, pscale, conv_w[0], convb, dtb, alog))
    ysd, ssm_s = _ssd_state(xs, bc, dts, main_s, cd, state_ssm[0].reshape(bs, SSD_WIDTH, SSD_STATE),
                            expand, segsum, alog, dexp, ssdnw)
    ya = _att_state(main_s, cache_mem_k[0], cache_mem_v[0])
    y_sample = _dense(xs2, main_s, _col_block(COL_GATES), yp, ysd, ya, *dense_w).reshape(bs, ss, d)

    ssm_shape = (SSD_GROUPS, SSD_HEADS // SSD_GROUPS, SSD_HEAD_DIM, SSD_STATE)
    return (y_prompt, y_sample,
            pool_p[None], conv_p[None], ssm_p.reshape((1, bp) + ssm_shape),
            mk[None], mv[None],
            pool_s[None], conv_s[None], ssm_s.reshape((1, bs) + ssm_shape))
```

```python
import functools

import jax
import jax.numpy as jnp
from jax import lax
from jax.experimental import pallas as pl
from jax.experimental.pallas import tpu as pltpu

F32 = jnp.float32
BF16 = jnp.bfloat16

D_MODEL = 1024
POOL_WINDOWS = (2, 4, 8, 16)
POOL_GROUP = 256
POOL_HIST = 15
POOL_PAD = 16
SSD_WIDTH = 2048
SSD_HEADS = 32
SSD_HEAD_DIM = 64
SSD_GROUPS = 4
SSD_STATE = 128
GROUP_WIDTH = SSD_WIDTH // SSD_GROUPS
CONV_WIDTH = 4
CONV_DIM = 3072
CONV_PAD = 8
SSD_CHUNK = 128
MEM_LEN = 256
ATT_HEADS = 4
ATT_HEAD_DIM = 256
PAST_LEN = 16384
EPS = 1e-6
NEG_BIG = -1e30
SUBLANES = 8
LANES = 128
MAIN_COLS = 12288
COL_XBC, COL_GATES, COL_ZS = (0, 3072), (3072, 6144), (6144, 8192)
COL_U, COL_ZP, COL_Q, COL_ZA = (8192, 9216), (9216, 10240), (10240, 11264), (11264, 12288)
W_SPLIT = (7168, 7200)
W_SRC = {COL_U: (0, 0), COL_ZP: (0, 1024), COL_ZS: (0, 2048), COL_XBC: (0, 4096),
         COL_Q: (1, 0), COL_ZA: (1, 1024), COL_GATES: (1, 2048)}
VMEM_LIMIT = 56 * 1024 * 1024
SEQ_PROMPT_VMEM_LIMIT = 60 * 1024 * 1024

_NT = (((1,), (1,)), ((), ()))
_TN = (((0,), (0,)), ((), ()))


def _sigmoid(x):
    return 1.0 / (1.0 + jnp.exp(-x))


def _silu(x):
    return x * _sigmoid(x)


def _softplus(x):
    return jnp.maximum(x, 0.0) + jnp.log1p(jnp.exp(-jnp.abs(x)))


def _rms(x, w):
    return x * lax.rsqrt(jnp.mean(x * x, axis=-1, keepdims=True) + EPS) * w


def _dot(a, b):
    return jnp.dot(a, b, preferred_element_type=F32)


def _softmax_rows(sc):
    e = jnp.exp(sc - jnp.max(sc, axis=-1, keepdims=True))
    return e / jnp.sum(e, axis=-1, keepdims=True)


def _weight_cols(w_refs, piece, lo, hi):
    idx, c0 = W_SRC[piece]
    return w_refs[idx][:, c0 + lo:c0 + hi]


def _memkv_kernel(mem_ref, nw_ref, wk_ref, wv_ref, k_ref, v_ref, kb_ref, vb_ref):
    mh = _rms(mem_ref[0], nw_ref[...]).astype(BF16)
    k = _dot(mh, wk_ref[...])
    v = _dot(mh, wv_ref[...])
    for hd in range(ATT_HEADS):
        hsl = slice(hd * ATT_HEAD_DIM, (hd + 1) * ATT_HEAD_DIM)
        k_ref[0, :, hd, :] = k[:, hsl]
        v_ref[0, :, hd, :] = v[:, hsl]
        kb_ref[0, hd] = k[:, hsl].astype(BF16)
        vb_ref[0, hd] = v[:, hsl].astype(BF16)


def _memkv(mem, nw, wk, wv):
    b, m, d = mem.shape
    full = lambda shape: pl.BlockSpec(shape, lambda i: (0,) * len(shape))
    blk = pl.BlockSpec((1, m, d), lambda i: (i, 0, 0))
    oblk = pl.BlockSpec((1, m, ATT_HEADS, ATT_HEAD_DIM), lambda i: (i, 0, 0, 0))
    hblk = pl.BlockSpec((1, ATT_HEADS, m, ATT_HEAD_DIM), lambda i: (i, 0, 0, 0))
    return pl.pallas_call(
        _memkv_kernel,
        grid=(b,),
        in_specs=[blk, full((1, d)), full((d, d)), full((d, d))],
        out_specs=[oblk, oblk, hblk, hblk],
        out_shape=[jax.ShapeDtypeStruct((b, m, ATT_HEADS, ATT_HEAD_DIM), F32)] * 2
        + [jax.ShapeDtypeStruct((b, ATT_HEADS, m, ATT_HEAD_DIM), BF16)] * 2,
        compiler_params=pltpu.CompilerParams(dimension_semantics=("arbitrary",), vmem_limit_bytes=VMEM_LIMIT),
        name="memkv",
    )(mem, nw, wk, wv)


INPROJ_COL_CHUNK = 1024


def _inproj_plan():
    plan = []
    for piece in sorted(W_SRC):
        idx, c0 = W_SRC[piece]
        for lo in range(0, piece[1] - piece[0], INPROJ_COL_CHUNK):
            plan.append((idx, (c0 + lo) // INPROJ_COL_CHUNK, piece == COL_GATES))
    return plan


def _inproj_kernel(plan, x_ref, nw_ref, wa_ref, wb_ref, wdt_ref, main_ref, dt_ref, h_scr):
    c = pl.program_id(0)

    @pl.when(c == 0)
    def _():
        h = _rms(x_ref[...], nw_ref[...]).astype(BF16)
        h_scr[...] = h
        dt_ref[...] = _dot(h, wdt_ref[...])

    def emit(w_ref, gate):
        def run():
            val = _dot(h_scr[...], w_ref[...])
            if gate:
                val = _sigmoid(val)
            main_ref[...] = val.astype(BF16)
        return run

    for idx, gate in ((0, False), (1, False), (1, True)):
        steps = [i for i, (pi, _, pg) in enumerate(plan) if (pi, pg) == (idx, gate)]
        cond = functools.reduce(jnp.logical_or, [c == i for i in steps])
        pl.when(cond)(emit((wa_ref, wb_ref)[idx], gate))


def _inproj(x2d, nw, w_a, w_b, w_dt):
    m = x2d.shape[0]
    plan = _inproj_plan()

    def block_of(idx):
        tbl, nxt = [0] * len(plan), None
        for i in reversed(range(len(plan))):
            if plan[i][0] == idx:
                nxt = plan[i][1]
            tbl[i] = nxt
        last = next(b for b in reversed(tbl) if b is not None)
        tbl = [last if b is None else b for b in tbl]

        def index_map(c):
            blk = tbl[0]
            for i in range(1, len(tbl)):
                blk = jnp.where(c >= i, tbl[i], blk)
            return (0, blk)
        return index_map

    const = lambda a: pl.BlockSpec(a.shape, lambda c: (0,) * a.ndim)
    return pl.pallas_call(
        functools.partial(_inproj_kernel, plan),
        grid=(len(plan),),
        in_specs=[const(x2d), const(nw),
                  pl.BlockSpec((D_MODEL, INPROJ_COL_CHUNK), block_of(0)),
                  pl.BlockSpec((D_MODEL, INPROJ_COL_CHUNK), block_of(1)),
                  const(w_dt)],
        out_specs=[pl.BlockSpec((m, INPROJ_COL_CHUNK), lambda c: (0, c)),
                   pl.BlockSpec((m, LANES), lambda c: (0, 0))],
        out_shape=[jax.ShapeDtypeStruct((m, MAIN_COLS), BF16), jax.ShapeDtypeStruct((m, LANES), F32)],
        scratch_shapes=[pltpu.VMEM((m, D_MODEL), BF16)],
        compiler_params=pltpu.CompilerParams(dimension_semantics=("arbitrary",), vmem_limit_bytes=VMEM_LIMIT),
        name="inproj",
    )(x2d, nw, w_a, w_b, w_dt)


def _pool_branch(pext, u, pos, wgrp_ref, pscale_ref, zp):
    nb, t, _ = u.shape
    ys = []
    for g, w in enumerate(POOL_WINDOWS):
        cols = slice(g * POOL_GROUP, (g + 1) * POOL_GROUP)
        win = pext[:, :, cols]
        for sh in [1 << e for e in range(g + 1)]:
            win = win + pltpu.roll(win, sh, axis=1)
        win = win[:, POOL_PAD:, :]
        inv_cnt = 1.0 / jnp.minimum(w, pos + 1).astype(F32)
        d = (win * inv_cnt - u[:, :, cols]).astype(BF16).reshape(nb * t, POOL_GROUP)
        ys.append(_dot(d, wgrp_ref[g]))
    return jnp.concatenate(ys, axis=1) * pscale_ref[...] * _silu(zp)


def _conv_branch(cext, convw_ref, convb_ref, store):
    for cc in range(CONV_DIM // GROUP_WIDTH):
        csl = slice(cc * GROUP_WIDTH, (cc + 1) * GROUP_WIDTH)
        ext = cext[:, :, csl]
        conv = convb_ref[:, csl].reshape(1, 1, GROUP_WIDTH)
        for kk in range(CONV_WIDTH):
            tap = ext if kk == CONV_WIDTH - 1 else pltpu.roll(ext, CONV_WIDTH - 1 - kk, axis=1)
            conv = conv + tap * convw_ref[kk:kk + 1, csl].reshape(1, 1, GROUP_WIDTH)
        store(cc, _silu(conv[:, CONV_PAD:, :]))


PROMPT_TILE = 256


def _seq_prompt_kernel(ntile, x_ref, nw_ref, wa_ref, wb_ref, wdt_ref, k_ref, v_ref,
                       wgrp_ref, pscale_ref, convw_ref, convb_ref, dtb_ref, alog_ref, dexp_ref, ssdnw_ref,
                       gates_ref, ypool_ref, yssd_ref, yatt_ref, pool_o_ref, conv_o_ref, ssm_o_ref,
                       pext, cext, xs_scr, b_scr, c_scr, dt_scr, y_scr, h_scr, zs_scr, q_scr, za_scr):
    t, q = PROMPT_TILE, SSD_CHUNK
    k = pl.program_id(0)
    s = k % ntile
    last = ntile - 1

    @pl.when(k == 0)
    def _():
        y_scr[...] = jnp.zeros(y_scr.shape, F32)
        zs_scr[...] = jnp.zeros(zs_scr.shape, BF16)
        q_scr[...] = jnp.zeros(q_scr.shape, BF16)
        za_scr[...] = jnp.zeros(za_scr.shape, BF16)

    @pl.when(s == 0)
    def _():
        pext[:, 0:POOL_PAD, :] = jnp.zeros((1, POOL_PAD, D_MODEL), F32)
        cext[:, 0:CONV_PAD, :] = jnp.zeros((1, CONV_PAD, CONV_DIM), F32)
        h_scr[...] = jnp.zeros(h_scr.shape, F32)

    @pl.when(s > 0)
    def _():
        carry_p = pext[:, t:t + POOL_PAD, :]
        carry_c = cext[:, t:t + CONV_PAD, :]
        pext[:, 0:POOL_PAD, :] = carry_p
        cext[:, 0:CONV_PAD, :] = carry_c

    yz = y_scr[...] * zs_scr[...].astype(F32)
    yssd_ref[...] = _rms(yz, ssdnw_ref[...]).astype(BF16)
    scale = ATT_HEAD_DIM ** -0.5
    outs = []
    for hd in range(ATT_HEADS):
        hsl = slice(hd * ATT_HEAD_DIM, (hd + 1) * ATT_HEAD_DIM)
        p = _softmax_rows(lax.dot_general(q_scr[:, hsl], k_ref[0, hd], _NT, preferred_element_type=F32) * scale)
        outs.append(_dot(p.astype(BF16), v_ref[0, hd]))
    yatt_ref[...] = (jnp.concatenate(outs, axis=1) * za_scr[...].astype(F32)).astype(BF16)

    hn = _rms(x_ref[...], nw_ref[...]).astype(BF16)

    def proj(piece, lo=0, hi=None):
        return _dot(hn, _weight_cols((wa_ref, wb_ref), piece, lo, piece[1] - piece[0] if hi is None else hi))

    def gates_piece(c):
        def run():
            gates_ref[:, c * D_MODEL:(c + 1) * D_MODEL] = _sigmoid(
                proj(COL_GATES, c * D_MODEL, (c + 1) * D_MODEL).astype(BF16))
        return run

    def zs_piece(c):
        def run():
            zs_scr[:, c * D_MODEL:(c + 1) * D_MODEL] = _silu(proj(COL_ZS, c * D_MODEL, (c + 1) * D_MODEL).astype(BF16))
        return run

    def q_piece():
        q_scr[...] = proj(COL_Q).astype(BF16)

    def za_piece():
        za_scr[...] = _silu(proj(COL_ZA).astype(BF16))

    fillers = [gates_piece(0), gates_piece(1), gates_piece(2), zs_piece(0), zs_piece(1), q_piece, za_piece]

    def run_filler():
        if fillers:
            fillers.pop(0)()

    u = proj(COL_U).reshape(1, t, D_MODEL)
    pext[:, POOL_PAD:, :] = u
    pos = s * t + lax.broadcasted_iota(jnp.int32, (1, t, 1), 1)
    ypool_ref[...] = _pool_branch(pext, u, pos, wgrp_ref, pscale_ref, proj(COL_ZP)).astype(BF16)

    for c in range(CONV_DIM // D_MODEL):
        cext[:, CONV_PAD:, c * D_MODEL:(c + 1) * D_MODEL] = proj(
            COL_XBC, c * D_MODEL, (c + 1) * D_MODEL).reshape(1, t, D_MODEL)
    dt_scr[...] = _softplus(_dot(hn, wdt_ref[...]) + dtb_ref[...])

    def store_conv(cc, val):
        if cc < SSD_GROUPS:
            xs_scr[:, cc * GROUP_WIDTH:(cc + 1) * GROUP_WIDTH] = val[0]
        elif cc == SSD_GROUPS:
            b_scr[...] = val[0]
        else:
            c_scr[...] = val[0]
        run_filler()

    _conv_branch(cext, convw_ref, convb_ref, store_conv)

    a_neg = -jnp.exp(alog_ref[...])
    rq = lax.broadcasted_iota(jnp.int32, (q, q), 0)
    cq = lax.broadcasted_iota(jnp.int32, (q, q), 1)
    tril = rq >= cq
    tri_f = tril.astype(F32)
    lane_lo = lax.broadcasted_iota(jnp.int32, (1, LANES), 1) < SSD_HEAD_DIM
    pairs_per_group = SSD_HEADS // SSD_GROUPS // 2

    def chunk(c):
        rsl = slice(c * q, (c + 1) * q)
        dtc = dt_scr[rsl, :]
        acs = jnp.dot(tri_f, dtc * a_neg, precision=lax.Precision.HIGHEST, preferred_element_type=F32)
        acs_t = acs.T
        dt_t = dtc.T
        wdec_t = (dt_t * jnp.exp(acs_t[:, q - 1:q] - acs_t)).astype(BF16)
        row_t = acs_t - jnp.log(dt_t)
        cdec = jnp.exp(acs[q - 1:q, :])
        for g in range(SSD_GROUPS):
            gsl = slice(g * SSD_STATE, (g + 1) * SSD_STATE)
            bg = b_scr[rsl, gsl]
            cg_b = c_scr[rsl, gsl].astype(BF16)
            cb = lax.dot_general(cg_b, bg.astype(BF16), _NT, preferred_element_type=F32).astype(BF16)
            bg_t = bg.T.astype(BF16)
            hsl = slice(g * GROUP_WIDTH, (g + 1) * GROUP_WIDTH)
            z_g = _dot(cg_b, h_scr[:, hsl].astype(BF16))
            for jp in range(pairs_per_group):
                j = g * pairs_per_group + jp
                lsl = slice(j * LANES, (j + 1) * LANES)
                xp = xs_scr[rsl, lsl]
                xp_b = xp.astype(BF16)
                zero_b = jnp.zeros_like(xp_b)
                x_bd = jnp.concatenate([jnp.where(lane_lo, xp_b, zero_b), jnp.where(lane_lo, zero_b, xp_b)],
                                       axis=0)
                ms, bws, cols = [], [], []
                for hh in range(2):
                    r = 2 * j + hh
                    cols.append(jnp.broadcast_to(acs[:, r:r + 1], (q, q)))
                    seg = cols[hh] - row_t[r:r + 1, :]
                    ms.append(cb * jnp.exp(jnp.where(tril, seg, NEG_BIG)).astype(BF16))
                    bws.append(bg_t * wdec_t[r:r + 1, :])
                ea_pair = jnp.exp(jnp.where(lane_lo, cols[0], cols[1]))
                y = (_dot(jnp.concatenate(ms, axis=1), x_bd)
                     + ea_pair * z_g[:, jp * LANES:(jp + 1) * LANES] + dexp_ref[:, lsl] * xp)
                y_scr[rsl, lsl] = y
                cd_pair = jnp.where(lane_lo, cdec[:, 2 * j:2 * j + 1], cdec[:, 2 * j + 1:2 * j + 2])
                h_scr[:, lsl] = h_scr[:, lsl] * cd_pair + _dot(jnp.concatenate(bws, axis=1), x_bd)
            run_filler()

    for c in range(t // q):
        chunk(c)
    while fillers:
        run_filler()

    @pl.when(jnp.logical_and(s == last, k < pl.num_programs(0) - 1))
    def _():
        pool_o_ref[...] = pext[:, t + POOL_PAD - POOL_HIST:t + POOL_PAD, :]
        conv_o_ref[...] = cext[:, t + CONV_PAD - (CONV_WIDTH - 1):t + CONV_PAD, :]
        for j in range(SSD_HEADS // 2):
            lsl = slice(j * LANES, (j + 1) * LANES)
            ssm_o_ref[0, lsl, :] = h_scr[:, lsl].T


def _seq_prompt(x2d, nw, w_a, w_b, w_dt, kb, vb, params, *, nseq, ntile):
    t = PROMPT_TILE
    m = x2d.shape[0]
    ntiles = nseq * ntile
    tile = lambda k: jnp.minimum(k, ntiles - 1)
    closed = lambda k: jnp.maximum(k - 1, 0)
    own_rows = lambda width: pl.BlockSpec((t, width), lambda k: (k, 0))
    closed_rows = lambda width: pl.BlockSpec((t, width), lambda k: (closed(k), 0))
    seq_spec = lambda shape, which: pl.BlockSpec((1,) + shape, lambda k: (which(k) // ntile,) + (0,) * len(shape))
    const_spec = lambda a: pl.BlockSpec(a.shape, lambda k: (0,) * a.ndim)
    resident = lambda a: pl.BlockSpec(a.shape, lambda k: (0,) * a.ndim, pipeline_mode=pl.Buffered(1))
    in_specs = [
        pl.BlockSpec((t, D_MODEL), lambda k: (tile(k), 0)),
        resident(nw), resident(w_a), resident(w_b), resident(w_dt),
        seq_spec((ATT_HEADS, MEM_LEN, ATT_HEAD_DIM), closed),
        seq_spec((ATT_HEADS, MEM_LEN, ATT_HEAD_DIM), closed),
    ] + [const_spec(p) for p in params]
    out_specs = [
        own_rows(3 * D_MODEL), own_rows(D_MODEL), closed_rows(SSD_WIDTH), closed_rows(D_MODEL),
        seq_spec((POOL_HIST, D_MODEL), tile),
        seq_spec((CONV_WIDTH - 1, CONV_DIM), tile),
        seq_spec((SSD_WIDTH, SSD_STATE), tile),
    ]
    out_shape = [
        jax.ShapeDtypeStruct((m + t, 3 * D_MODEL), BF16),
        jax.ShapeDtypeStruct((m + t, D_MODEL), BF16),
        jax.ShapeDtypeStruct((m, SSD_WIDTH), BF16),
        jax.ShapeDtypeStruct((m, D_MODEL), BF16),
        jax.ShapeDtypeStruct((nseq, POOL_HIST, D_MODEL), F32),
        jax.ShapeDtypeStruct((nseq, CONV_WIDTH - 1, CONV_DIM), F32),
        jax.ShapeDtypeStruct((nseq, SSD_WIDTH, SSD_STATE), F32),
    ]
    scratch = [
        pltpu.VMEM((1, POOL_PAD + t, D_MODEL), F32),
        pltpu.VMEM((1, CONV_PAD + t, CONV_DIM), F32),
        pltpu.VMEM((t, SSD_WIDTH), F32),
        pltpu.VMEM((t, GROUP_WIDTH), F32),
        pltpu.VMEM((t, GROUP_WIDTH), F32),
        pltpu.VMEM((t, LANES), F32),
        pltpu.VMEM((t, SSD_WIDTH), F32),
        pltpu.VMEM((SSD_STATE, SSD_WIDTH), F32),
        pltpu.VMEM((t, SSD_WIDTH), BF16),
        pltpu.VMEM((t, D_MODEL), BF16),
        pltpu.VMEM((t, D_MODEL), BF16),
    ]
    return pl.pallas_call(
        functools.partial(_seq_prompt_kernel, ntile),
        grid=(ntiles + 1,),
        in_specs=in_specs,
        out_specs=out_specs,
        out_shape=out_shape,
        scratch_shapes=scratch,
        compiler_params=pltpu.CompilerParams(dimension_semantics=("arbitrary",),
                                             vmem_limit_bytes=SEQ_PROMPT_VMEM_LIMIT),
        name="seq_prompt",
    )(x2d, nw, w_a, w_b, w_dt, kb, vb, *params)


SAMPLE_BLOCK = 32
STATE_PRE_SUB = 4


def _state_pre_kernel(u_ref, zp_ref, xbc_ref, dt_ref, ph_ref, ch_ref,
                      wgrp_ref, pscale_ref, convw_ref, convb_ref, dtb_ref, alog_ref,
                      ypool_ref, xs_ref, bc_ref, dts_ref, cd_ref, pool_o_ref, conv_o_ref, pext, cext):
    nb, t = SAMPLE_BLOCK, SUBLANES
    pext[:, 0:1, :] = jnp.zeros((nb, 1, D_MODEL), F32)
    pext[:, 1:POOL_PAD, :] = ph_ref[...]
    cext[:, 0:CONV_PAD - (CONV_WIDTH - 1), :] = jnp.zeros((nb, CONV_PAD - (CONV_WIDTH - 1), CONV_DIM), F32)
    cext[:, CONV_PAD - (CONV_WIDTH - 1):CONV_PAD, :] = ch_ref[...]

    pext[:, POOL_PAD:, :] = u_ref[...].astype(F32).reshape(nb, t, D_MODEL)
    cext[:, CONV_PAD:, :] = xbc_ref[...].astype(F32).reshape(nb, t, CONV_DIM)
    pos = PAST_LEN + lax.broadcasted_iota(jnp.int32, (1, t, 1), 1)
    for sb in range(nb // STATE_PRE_SUB):
        seqs = slice(sb * STATE_PRE_SUB, (sb + 1) * STATE_PRE_SUB)
        rsl = slice(sb * STATE_PRE_SUB * t, (sb + 1) * STATE_PRE_SUB * t)
        pv, cv = pext.at[seqs], cext.at[seqs]
        ypool_ref[rsl, :] = _pool_branch(pv, pv[:, POOL_PAD:, :], pos, wgrp_ref, pscale_ref,
                                         zp_ref[rsl, :].astype(F32)).astype(BF16)

        def store_conv(cc, val, rsl=rsl):
            val = val.reshape(STATE_PRE_SUB * t, GROUP_WIDTH)
            if cc < SSD_GROUPS:
                xs_ref[rsl, cc * GROUP_WIDTH:(cc + 1) * GROUP_WIDTH] = val
            else:
                bc_ref[rsl, (cc - SSD_GROUPS) * GROUP_WIDTH:(cc - SSD_GROUPS + 1) * GROUP_WIDTH] = val

        _conv_branch(cv, convw_ref, convb_ref, store_conv)
    dt = _softplus(dt_ref[...] + dtb_ref[...])
    dts_ref[...] = dt
    a = dt * -jnp.exp(alog_ref[...])
    cd_ref[...] = jnp.exp(jnp.sum(a.reshape(nb, t, LANES), axis=1))
    pool_o_ref[...] = pext[:, t + POOL_PAD - POOL_HIST:t + POOL_PAD, :]
    conv_o_ref[...] = cext[:, t + CONV_PAD - (CONV_WIDTH - 1):t + CONV_PAD, :]


def _col_block(piece):
    return piece[0] // (piece[1] - piece[0])


def _state_pre(main, dt, state_pool, state_conv, params):
    nb, t = SAMPLE_BLOCK, SUBLANES
    rows = nb * t
    m = main.shape[0]
    nseq = m // t
    col_spec = lambda width, idx: pl.BlockSpec((rows, width), lambda i: (i, idx))
    seq_spec = lambda shape: pl.BlockSpec((nb,) + shape, lambda i: (i,) + (0,) * len(shape))
    const_spec = lambda a: pl.BlockSpec(a.shape, lambda i: (0,) * a.ndim)
    return pl.pallas_call(
        _state_pre_kernel,
        grid=(nseq // nb,),
        in_specs=[col_spec(D_MODEL, _col_block(COL_U)), col_spec(D_MODEL, _col_block(COL_ZP)),
                  col_spec(CONV_DIM, _col_block(COL_XBC)), col_spec(LANES, 0),
                  seq_spec((POOL_HIST, D_MODEL)), seq_spec((CONV_WIDTH - 1, CONV_DIM))]
        + [const_spec(p) for p in params],
        out_specs=[col_spec(D_MODEL, 0), col_spec(SSD_WIDTH, 0), col_spec(2 * GROUP_WIDTH, 0), col_spec(LANES, 0),
                   pl.BlockSpec((nb, LANES), lambda i: (i, 0)),
                   seq_spec((POOL_HIST, D_MODEL)), seq_spec((CONV_WIDTH - 1, CONV_DIM))],
        out_shape=[jax.ShapeDtypeStruct((m, D_MODEL), BF16), jax.ShapeDtypeStruct((m, SSD_WIDTH), F32),
                   jax.ShapeDtypeStruct((m, 2 * GROUP_WIDTH), F32), jax.ShapeDtypeStruct((m, LANES), F32),
                   jax.ShapeDtypeStruct((nseq, LANES), F32),
                   jax.ShapeDtypeStruct((nseq, POOL_HIST, D_MODEL), F32),
                   jax.ShapeDtypeStruct((nseq, CONV_WIDTH - 1, CONV_DIM), F32)],
        scratch_shapes=[pltpu.VMEM((nb, POOL_PAD + t, D_MODEL), F32), pltpu.VMEM((nb, CONV_PAD + t, CONV_DIM), F32)],
        compiler_params=pltpu.CompilerParams(dimension_semantics=("arbitrary",), vmem_limit_bytes=VMEM_LIMIT),
        name="state_pre",
    )(main, main, main, dt, state_pool, state_conv, *params)


PIPE_SLOTS = 4


def _ring_pipeline(n, start_in, wait_in, compute, start_out=None, wait_out=None):
    ns = PIPE_SLOTS
    for i in range(ns - 1):
        start_in(i, i)

    def body(bb, _):
        for k in range(ns):
            i = ns * bb + k
            nxt = i + ns - 1
            pl.when(nxt < n)(functools.partial(start_in, nxt, (k + ns - 1) % ns))
            wait_in(i, k)
            if wait_out is not None:
                pl.when(bb > 0)(functools.partial(wait_out, i - ns, k))
            compute(i, k)
            if start_out is not None:
                start_out(i, k)
        return 0

    lax.fori_loop(0, n // ns, body, 0)
    if wait_out is not None:
        for k in range(ns):
            wait_out(n - ns + k, k)


SSD_ITEM = 4


def _ssd_state_kernel(xs_ref, bc_ref, dt_ref, zs_ref, cd_ref, hin_hbm, expand_ref, segsum_ref, alog_ref, dexp_ref,
                      ssdnw_ref, yssd_ref, hout_hbm, hbuf, obuf, y_scr, sem_in, sem_out):
    nb, t = SAMPLE_BLOCK, SUBLANES
    base = pl.program_id(0) * nb
    a_neg = -jnp.exp(alog_ref[...])
    ridx = lax.broadcasted_iota(jnp.int32, (t, LANES), 0)

    def in_copies(item, slot):
        return [pltpu.make_async_copy(hin_hbm.at[base + item * SSD_ITEM + j, g * GROUP_WIDTH:(g + 1) * GROUP_WIDTH, :],
                                      hbuf.at[slot, j, g * GROUP_WIDTH:(g + 1) * GROUP_WIDTH, :],
                                      sem_in.at[slot, j * SSD_GROUPS + g])
                for j in range(SSD_ITEM) for g in range(SSD_GROUPS)]

    def out_copies(item, slot):
        return [pltpu.make_async_copy(obuf.at[slot, j, g * GROUP_WIDTH:(g + 1) * GROUP_WIDTH, :],
                                      hout_hbm.at[base + item * SSD_ITEM + j, g * GROUP_WIDTH:(g + 1) * GROUP_WIDTH, :],
                                      sem_out.at[slot, j * SSD_GROUPS + g])
                for j in range(SSD_ITEM) for g in range(SSD_GROUPS)]

    def start_all(copies):
        for c in copies:
            c.start()

    def wait_all(copies):
        for c in copies:
            c.wait()

    def per_head_factors(b):
        rsl = pl.ds(pl.multiple_of(b * t, t), t)
        dtc = dt_ref[rsl, :]
        acs = dtc * a_neg
        for sh in (1, 2, 4):
            acs = acs + jnp.where(ridx >= sh, pltpu.roll(acs, sh, axis=0), 0.0)
        tot = acs[t - 1:t, :]
        bc = bc_ref[rsl, :]
        bm, cm = bc[:, :GROUP_WIDTH], bc[:, GROUP_WIDTH:]
        bm_r, cm_r = bm.astype(BF16).astype(F32), cm.astype(BF16).astype(F32)
        gs, ps = [], []
        for k in range(t):
            gs.append(jnp.exp(jnp.where(ridx >= k, acs - acs[k:k + 1, :], NEG_BIG)) * dtc[k:k + 1, :])
            ps.append(cm_r * bm_r[k:k + 1, :])
        cb_heads = _dot(jnp.concatenate(ps, axis=0).astype(BF16), segsum_ref[...])
        per_head = jnp.concatenate([jnp.concatenate(gs, axis=0) * cb_heads, jnp.exp(acs),
                                    dtc * jnp.exp(tot - acs)], axis=0)
        return dict(rsl=rsl, b=b, bm=bm, cm=cm, per_head=per_head)

    def widen(st):
        hi = st["per_head"].astype(BF16)
        lo = (st["per_head"] - hi.astype(F32)).astype(BF16)
        st["wide"] = _dot(hi, expand_ref[...]) + _dot(lo, expand_ref[...])

    def intra(st):
        x = xs_ref[st["rsl"], :]
        wide = st["wide"]
        y = dexp_ref[...] * x
        for k in range(t):
            y = y + wide[k * t:(k + 1) * t, :] * x[k:k + 1, :]
        st["y"] = y
        st["ea_wide"] = wide[t * t:t * t + t, :]
        st["xw"] = x * wide[t * t + t:, :]

    def state_group(st, slot, j, g):
        gsl = slice(g * SSD_STATE, (g + 1) * SSD_STATE)
        wsl = slice(g * GROUP_WIDTH, (g + 1) * GROUP_WIDTH)
        hg = hbuf[slot, j, wsl, :]
        z_g = lax.dot_general(st["cm"][:, gsl].astype(BF16), hg.astype(BF16), _NT, preferred_element_type=F32)
        y_scr[st["rsl"], wsl] = st["y"][:, wsl] + st["ea_wide"][:, wsl] * z_g
        upd = lax.dot_general(st["xw"][:, wsl].astype(BF16), st["bm"][:, gsl].astype(BF16), _TN,
                              preferred_element_type=F32)
        for r8 in range(SSD_HEADS // SSD_GROUPS):
            r = g * (SSD_HEADS // SSD_GROUPS) + r8
            rows_r = slice(r * SSD_HEAD_DIM, (r + 1) * SSD_HEAD_DIM)
            obuf[slot, j, rows_r, :] = (hbuf[slot, j, rows_r, :] * cd_ref[base + st["b"], r]
                                        + upd[r8 * SSD_HEAD_DIM:(r8 + 1) * SSD_HEAD_DIM, :])

    def compute(item, slot):
        sts = [per_head_factors(item * SSD_ITEM + j) for j in range(SSD_ITEM)]
        for st in sts:
            widen(st)
        for st in sts:
            intra(st)
        for g in range(SSD_GROUPS):
            for j, st in enumerate(sts):
                state_group(st, slot, j, g)

    _ring_pipeline(nb // SSD_ITEM,
                   lambda i, slot: start_all(in_copies(i, slot)), lambda i, slot: wait_all(in_copies(i, slot)),
                   compute,
                   lambda i, slot: start_all(out_copies(i, slot)), lambda i, slot: wait_all(out_copies(i, slot)))
    yz = y_scr[...] * _silu(zs_ref[...].astype(F32))
    yssd_ref[...] = _rms(yz, ssdnw_ref[...]).astype(BF16)


def _ssd_state(xs, bc, dts, main, cd, hin, expand, segsum, alog, dexp, ssdnw):
    nb, t = SAMPLE_BLOCK, SUBLANES
    rows = nb * t
    m = xs.shape[0]
    nseq = m // t
    col_spec = lambda width, idx: pl.BlockSpec((rows, width), lambda i: (i, idx))
    const_spec = lambda a: pl.BlockSpec(a.shape, lambda i: (0,) * a.ndim)
    return pl.pallas_call(
        _ssd_state_kernel,
        grid=(nseq // nb,),
        in_specs=[col_spec(SSD_WIDTH, 0), col_spec(2 * GROUP_WIDTH, 0), col_spec(LANES, 0),
                  col_spec(SSD_WIDTH, _col_block(COL_ZS)),
                  pl.BlockSpec(memory_space=pltpu.SMEM), pl.BlockSpec(memory_space=pl.ANY),
                  const_spec(expand), const_spec(segsum), const_spec(alog), const_spec(dexp), const_spec(ssdnw)],
        out_specs=[col_spec(SSD_WIDTH, 0), pl.BlockSpec(memory_space=pl.ANY)],
        out_shape=[jax.ShapeDtypeStruct((m, SSD_WIDTH), BF16), jax.ShapeDtypeStruct(hin.shape, F32)],
        scratch_shapes=[pltpu.VMEM((PIPE_SLOTS, SSD_ITEM, SSD_WIDTH, SSD_STATE), F32),
                        pltpu.VMEM((PIPE_SLOTS, SSD_ITEM, SSD_WIDTH, SSD_STATE), F32),
                        pltpu.VMEM((rows, SSD_WIDTH), F32),
                        pltpu.SemaphoreType.DMA((PIPE_SLOTS, SSD_ITEM * SSD_GROUPS)),
                        pltpu.SemaphoreType.DMA((PIPE_SLOTS, SSD_ITEM * SSD_GROUPS))],
        compiler_params=pltpu.CompilerParams(dimension_semantics=("arbitrary",), vmem_limit_bytes=VMEM_LIMIT),
        name="ssd_state",
    )(xs, bc, dts, main, cd, hin, expand, segsum, alog, dexp, ssdnw)


ATT_ITEM = 4


def _att_state_kernel(q_ref, za_ref, k_hbm, v_hbm, yatt_ref, kbuf, vbuf, q_scr, att_scr, sem):
    nb, t = SAMPLE_BLOCK, SUBLANES
    base = pl.program_id(0) * nb
    scale = ATT_HEAD_DIM ** -0.5
    q_scr[...] = q_ref[...].astype(F32)
    head_of_lane = lax.broadcasted_iota(jnp.int32, (1, D_MODEL), 1) // ATT_HEAD_DIM

    def copies(item, slot):
        out = []
        for j in range(ATT_ITEM):
            for kv, (src, buf) in enumerate(((k_hbm, kbuf), (v_hbm, vbuf))):
                for hd in range(ATT_HEADS):
                    hsl = slice(hd * ATT_HEAD_DIM, (hd + 1) * ATT_HEAD_DIM)
                    out.append(pltpu.make_async_copy(
                        src.at[base + item * ATT_ITEM + j, :, hd, :], buf.at[slot, j, :, hsl],
                        sem.at[slot, (j * 2 + kv) * ATT_HEADS + hd]))
        return out

    def start_in(item, slot):
        for c in copies(item, slot):
            c.start()

    def wait_in(item, slot):
        for c in copies(item, slot):
            c.wait()

    def compute(item, slot):
        seqs = range(ATT_ITEM)
        rsl = [pl.ds(pl.multiple_of((item * ATT_ITEM + j) * t, t), t) for j in seqs]
        scs = []
        for j in seqs:
            qf = q_scr[rsl[j], :]
            q_bd = jnp.concatenate([jnp.where(head_of_lane == hd, qf, 0.0) for hd in range(ATT_HEADS)],
                                   axis=0).astype(BF16)
            scs.append(lax.dot_general(q_bd, kbuf[slot, j].astype(BF16), _NT, preferred_element_type=F32) * scale)
        ps = [_softmax_rows(sc).astype(BF16) for sc in scs]
        outs = [_dot(ps[j], vbuf[slot, j].astype(BF16)) for j in seqs]
        for j in seqs:
            att_scr[rsl[j], :] = jnp.concatenate(
                [outs[j][hd * t:(hd + 1) * t, hd * ATT_HEAD_DIM:(hd + 1) * ATT_HEAD_DIM] for hd in range(ATT_HEADS)],
                axis=1)

    _ring_pipeline(nb // ATT_ITEM, start_in, wait_in, compute)
    yatt_ref[...] = (att_scr[...] * _silu(za_ref[...].astype(F32))).astype(BF16)


def _att_state(main, k, v):
    nb, t = SAMPLE_BLOCK, SUBLANES
    rows = nb * t
    m = main.shape[0]
    col_spec = lambda width, idx: pl.BlockSpec((rows, width), lambda i: (i, idx))
    kv_buf = pltpu.VMEM((PIPE_SLOTS, ATT_ITEM, MEM_LEN, D_MODEL), F32)
    return pl.pallas_call(
        _att_state_kernel,
        grid=(m // rows,),
        in_specs=[col_spec(D_MODEL, _col_block(COL_Q)), col_spec(D_MODEL, _col_block(COL_ZA)),
                  pl.BlockSpec(memory_space=pl.ANY), pl.BlockSpec(memory_space=pl.ANY)],
        out_specs=col_spec(D_MODEL, 0),
        out_shape=jax.ShapeDtypeStruct((m, D_MODEL), BF16),
        scratch_shapes=[kv_buf, kv_buf, pltpu.VMEM((rows, D_MODEL), F32), pltpu.VMEM((rows, D_MODEL), F32),
                        pltpu.SemaphoreType.DMA((PIPE_SLOTS, ATT_ITEM * 2 * ATT_HEADS))],
        compiler_params=pltpu.CompilerParams(dimension_semantics=("arbitrary",), vmem_limit_bytes=VMEM_LIMIT),
        name="att_state",
    )(main, main, k, v)


DENSE_ROWS = 512


def _dense_kernel(x_ref, gt_ref, yp_ref, ys_ref, ya_ref, wpo_ref, wso_ref, wao_ref, wo_ref, fnw_ref, y_ref):
    gates = gt_ref[...].astype(F32)
    merged = (gates[:, 0:D_MODEL] * _dot(yp_ref[...], wpo_ref[...])
              + gates[:, D_MODEL:2 * D_MODEL] * _dot(ys_ref[...], wso_ref[...])
              + gates[:, 2 * D_MODEL:] * _dot(ya_ref[...], wao_ref[...]))
    x_out = x_ref[...] + _dot(merged.astype(BF16), wo_ref[...])
    y_ref[...] = _rms(x_out, fnw_ref[...])


def _dense(x2d, gates, gate_idx, yp, ys, ya, wpo, wso, wao, wo, fnw):
    m = x2d.shape[0]
    row = lambda width, idx=0: pl.BlockSpec((DENSE_ROWS, width), lambda i: (i, idx))
    resident = lambda a: pl.BlockSpec(a.shape, lambda i: (0,) * a.ndim, pipeline_mode=pl.Buffered(1))
    return pl.pallas_call(
        _dense_kernel,
        grid=(m // DENSE_ROWS,),
        in_specs=[row(D_MODEL), row(3 * D_MODEL, gate_idx), row(D_MODEL), row(SSD_WIDTH), row(D_MODEL),
                  resident(wpo), resident(wso), resident(wao), resident(wo), resident(fnw)],
        out_specs=row(D_MODEL),
        out_shape=jax.ShapeDtypeStruct((m, D_MODEL), F32),
        compiler_params=pltpu.CompilerParams(dimension_semantics=("arbitrary",), vmem_limit_bytes=VMEM_LIMIT),
        name="dense",
    )(x2d, gates, yp, ys, ya, wpo, wso, wao, wo, fnw)


def kernel(x_prompt, x_sample, mem_prompt, state_pool, state_conv, state_ssm, cache_mem_k, cache_mem_v,
           norm_w, w_in, w_pool_grp, pool_scale, conv_w, conv_b, dt_bias, a_log, d_skip, ssd_norm_w,
           mem_norm_w, w_mem_k, w_mem_v, w_pool_out, w_ssd_out, w_att_out, w_out, final_norm_w):
    assert w_in.shape[0] == 1
    bp, sp, d = x_prompt.shape
    bs, ss, _ = x_sample.shape
    assert ss == SUBLANES and sp % PROMPT_TILE == 0 and bs % SAMPLE_BLOCK == 0

    w_a = w_in[0][:, :W_SPLIT[0]].astype(BF16)
    w_b = w_in[0][:, W_SPLIT[1]:].astype(BF16)
    w_dt = jnp.pad(w_in[0][:, W_SPLIT[0]:W_SPLIT[1]], ((0, 0), (0, LANES - SSD_HEADS))).astype(BF16)
    nw = norm_w[0].reshape(1, d)
    pad_heads = lambda a: jnp.pad(a.reshape(1, SSD_HEADS), ((0, 0), (0, LANES - SSD_HEADS)))
    wgrp = w_pool_grp[0].astype(BF16)
    pscale = pool_scale[0].reshape(1, d)
    convb = conv_b[0].reshape(1, CONV_DIM)
    dtb, alog = pad_heads(dt_bias[0]), pad_heads(a_log[0])
    dexp = jnp.repeat(d_skip[0], SSD_HEAD_DIM).reshape(1, SSD_WIDTH)
    ssdnw = ssd_norm_w[0].reshape(1, SSD_WIDTH)
    dense_w = (w_pool_out[0].astype(BF16), w_ssd_out[0].astype(BF16), w_att_out[0].astype(BF16),
               w_out[0].astype(BF16), final_norm_w.reshape(1, d))
    head_of_lane = jnp.arange(SSD_WIDTH) // SSD_HEAD_DIM
    expand = (jnp.arange(LANES)[:, None] == head_of_lane[None, :]).astype(BF16)
    group_of_head = jnp.where(jnp.arange(LANES) < SSD_HEADS, jnp.arange(LANES) // (SSD_HEADS // SSD_GROUPS), -1)
    segsum = ((jnp.arange(GROUP_WIDTH) // SSD_STATE)[:, None] == group_of_head[None, :]).astype(BF16)

    mk, mv, mkb, mvb = _memkv(mem_prompt, mem_norm_w[0].reshape(1, d), w_mem_k[0].astype(BF16),
                              w_mem_v[0].astype(BF16))
    xp2 = x_prompt.reshape(bp * sp, d)
    gates_p, yp, ysd, ya, pool_p, conv_p, ssm_p = _seq_prompt(
        xp2, nw, w_a, w_b, w_dt, mkb, mvb, (wgrp, pscale, conv_w[0], convb, dtb, alog, dexp, ssdnw),
        nseq=bp, ntile=sp // PROMPT_TILE)
    y_prompt = _dense(xp2, gates_p, 0, yp, ysd, ya, *dense_w).reshape(bp, sp, d)

    xs2 = x_sample.reshape(bs * ss, d)
    main_s, dt_s = _inproj(xs2, nw, w_a, w_b, w_dt)
    yp, xs, bc, dts, cd, pool_s, conv_s = _state_pre(
        main_s, dt_s, state_pool[0], state_conv[0], (wgrp, pscale, conv_w[0], convb, dtb, alog))
    ysd, ssm_s = _ssd_state(xs, bc, dts, main_s, cd, state_ssm[0].reshape(bs, SSD_WIDTH, SSD_STATE),
                            expand, segsum, alog, dexp, ssdnw)
    ya = _att_state(main_s, cache_mem_k[0], cache_mem_v[0])
    y_sample = _dense(xs2, main_s, _col_block(COL_GATES), yp, ysd, ya, *dense_w).reshape(bs, ss, d)

    ssm_shape = (SSD_GROUPS, SSD_HEADS // SSD_GROUPS, SSD_HEAD_DIM, SSD_STATE)
    return (y_prompt, y_sample,
            pool_p[None], conv_p[None], ssm_p.reshape((1, bp) + ssm_shape),
            mk[None], mv[None],
            pool_s[None], conv_s[None], ssm_s.reshape((1, bs) + ssm_shape))
```

```python
import functools

import jax
import jax.numpy as jnp
from jax import lax
from jax.experimental import pallas as pl
from jax.experimental.pallas import tpu as pltpu

F32 = jnp.float32
BF16 = jnp.bfloat16

D_MODEL = 1024
POOL_WINDOWS = (2, 4, 8, 16)
POOL_GROUP = 256
POOL_HIST = 15
POOL_PAD = 16
SSD_WIDTH = 2048
SSD_HEADS = 32
SSD_HEAD_DIM = 64
SSD_GROUPS = 4
SSD_STATE = 128
GROUP_WIDTH = SSD_WIDTH // SSD_GROUPS
CONV_WIDTH = 4
CONV_DIM = 3072
CONV_PAD = 8
SSD_CHUNK = 128
MEM_LEN = 256
ATT_HEADS = 4
ATT_HEAD_DIM = 256
PAST_LEN = 16384
EPS = 1e-6
NEG_BIG = -1e30
SUBLANES = 8
LANES = 128
MAIN_COLS = 12288
COL_XBC, COL_GATES, COL_ZS = (0, 3072), (3072, 6144), (6144, 8192)
COL_U, COL_ZP, COL_Q, COL_ZA = (8192, 9216), (9216, 10240), (10240, 11264), (11264, 12288)
W_SPLIT = (7168, 7200)
W_SRC = {COL_U: (0, 0), COL_ZP: (0, 1024), COL_ZS: (0, 2048), COL_XBC: (0, 4096),
         COL_Q: (1, 0), COL_ZA: (1, 1024), COL_GATES: (1, 2048)}
VMEM_LIMIT = 56 * 1024 * 1024
SEQ_PROMPT_VMEM_LIMIT = 60 * 1024 * 1024

_NT = (((1,), (1,)), ((), ()))
_TN = (((0,), (0,)), ((), ()))


def _sigmoid(x):
    return 1.0 / (1.0 + jnp.exp(-x))


def _silu(x):
    return x * _sigmoid(x)


def _softplus(x):
    return jnp.maximum(x, 0.0) + jnp.log1p(jnp.exp(-jnp.abs(x)))


def _rms(x, w):
    return x * lax.rsqrt(jnp.mean(x * x, axis=-1, keepdims=True) + EPS) * w


def _dot(a, b):
    return jnp.dot(a, b, preferred_element_type=F32)


def _softmax_rows(sc):
    e = jnp.exp(sc - jnp.max(sc, axis=-1, keepdims=True))
    return e / jnp.sum(e, axis=-1, keepdims=True)


def _weight_cols(w_refs, piece, lo, hi):
    idx, c0 = W_SRC[piece]
    return w_refs[idx][:, c0 + lo:c0 + hi]


def _memkv_kernel(mem_ref, nw_ref, wk_ref, wv_ref, k_ref, v_ref, kb_ref, vb_ref):
    mh = _rms(mem_ref[0], nw_ref[...]).astype(BF16)
    k = _dot(mh, wk_ref[...])
    v = _dot(mh, wv_ref[...])
    for hd in range(ATT_HEADS):
        hsl = slice(hd * ATT_HEAD_DIM, (hd + 1) * ATT_HEAD_DIM)
        k_ref[0, :, hd, :] = k[:, hsl]
        v_ref[0, :, hd, :] = v[:, hsl]
        kb_ref[0, hd] = k[:, hsl].astype(BF16)
        vb_ref[0, hd] = v[:, hsl].astype(BF16)


def _memkv(mem, nw, wk, wv):
    b, m, d = mem.shape
    full = lambda shape: pl.BlockSpec(shape, lambda i: (0,) * len(shape))
    blk = pl.BlockSpec((1, m, d), lambda i: (i, 0, 0))
    oblk = pl.BlockSpec((1, m, ATT_HEADS, ATT_HEAD_DIM), lambda i: (i, 0, 0, 0))
    hblk = pl.BlockSpec((1, ATT_HEADS, m, ATT_HEAD_DIM), lambda i: (i, 0, 0, 0))
    return pl.pallas_call(
        _memkv_kernel,
        grid=(b,),
        in_specs=[blk, full((1, d)), full((d, d)), full((d, d))],
        out_specs=[oblk, oblk, hblk, hblk],
        out_shape=[jax.ShapeDtypeStruct((b, m, ATT_HEADS, ATT_HEAD_DIM), F32)] * 2
        + [jax.ShapeDtypeStruct((b, ATT_HEADS, m, ATT_HEAD_DIM), BF16)] * 2,
        compiler_params=pltpu.CompilerParams(dimension_semantics=("arbitrary",), vmem_limit_bytes=VMEM_LIMIT),
        name="memkv",
    )(mem, nw, wk, wv)


INPROJ_COL_CHUNK = 1024


def _inproj_plan():
    plan = []
    for piece in sorted(W_SRC):
        idx, c0 = W_SRC[piece]
        for lo in range(0, piece[1] - piece[0], INPROJ_COL_CHUNK):
            plan.append((idx, (c0 + lo) // INPROJ_COL_CHUNK, piece == COL_GATES))
    return plan


def _inproj_kernel(plan, x_ref, nw_ref, wa_ref, wb_ref, wdt_ref, main_ref, dt_ref, h_scr):
    c = pl.program_id(0)

    @pl.when(c == 0)
    def _():
        h = _rms(x_ref[...], nw_ref[...]).astype(BF16)
        h_scr[...] = h
        dt_ref[...] = _dot(h, wdt_ref[...])

    def emit(w_ref, gate):
        def run():
            val = _dot(h_scr[...], w_ref[...])
            if gate:
                val = _sigmoid(val)
            main_ref[...] = val.astype(BF16)
        return run

    for idx, gate in ((0, False), (1, False), (1, True)):
        steps = [i for i, (pi, _, pg) in enumerate(plan) if (pi, pg) == (idx, gate)]
        cond = functools.reduce(jnp.logical_or, [c == i for i in steps])
        pl.when(cond)(emit((wa_ref, wb_ref)[idx], gate))


def _inproj(x2d, nw, w_a, w_b, w_dt):
    m = x2d.shape[0]
    plan = _inproj_plan()

    def block_of(idx):
        tbl, nxt = [0] * len(plan), None
        for i in reversed(range(len(plan))):
            if plan[i][0] == idx:
                nxt = plan[i][1]
            tbl[i] = nxt
        last = next(b for b in reversed(tbl) if b is not None)
        tbl = [last if b is None else b for b in tbl]

        def index_map(c):
            blk = tbl[0]
            for i in range(1, len(tbl)):
                blk = jnp.where(c >= i, tbl[i], blk)
            return (0, blk)
        return index_map

    const = lambda a: pl.BlockSpec(a.shape, lambda c: (0,) * a.ndim)
    return pl.pallas_call(
        functools.partial(_inproj_kernel, plan),
        grid=(len(plan),),
        in_specs=[const(x2d), const(nw),
                  pl.BlockSpec((D_MODEL, INPROJ_COL_CHUNK), block_of(0)),
                  pl.BlockSpec((D_MODEL, INPROJ_COL_CHUNK), block_of(1)),
                  const(w_dt)],
        out_specs=[pl.BlockSpec((m, INPROJ_COL_CHUNK), lambda c: (0, c)),
                   pl.BlockSpec((m, LANES), lambda c: (0, 0))],
        out_shape=[jax.ShapeDtypeStruct((m, MAIN_COLS), BF16), jax.ShapeDtypeStruct((m, LANES), F32)],
        scratch_shapes=[pltpu.VMEM((m, D_MODEL), BF16)],
        compiler_params=pltpu.CompilerParams(dimension_semantics=("arbitrary",), vmem_limit_bytes=VMEM_LIMIT),
        name="inproj",
    )(x2d, nw, w_a, w_b, w_dt)


def _pool_branch(pext, u, pos, wgrp_ref, pscale_ref, zp):
    nb, t, _ = u.shape
    ys = []
    for g, w in enumerate(POOL_WINDOWS):
        cols = slice(g * POOL_GROUP, (g + 1) * POOL_GROUP)
        win = pext[:, :, cols]
        for sh in [1 << e for e in range(g + 1)]:
            win = win + pltpu.roll(win, sh, axis=1)
        win = win[:, POOL_PAD:, :]
        inv_cnt = 1.0 / jnp.minimum(w, pos + 1).astype(F32)
        d = (win * inv_cnt - u[:, :, cols]).astype(BF16).reshape(nb * t, POOL_GROUP)
        ys.append(_dot(d, wgrp_ref[g]))
    return jnp.concatenate(ys, axis=1) * pscale_ref[...] * _silu(zp)


def _conv_branch(cext, convw_ref, convb_ref, store):
    for cc in range(CONV_DIM // GROUP_WIDTH):
        csl = slice(cc * GROUP_WIDTH, (cc + 1) * GROUP_WIDTH)
        ext = cext[:, :, csl]
        conv = convb_ref[:, csl].reshape(1, 1, GROUP_WIDTH)
        for kk in range(CONV_WIDTH):
            tap = ext if kk == CONV_WIDTH - 1 else pltpu.roll(ext, CONV_WIDTH - 1 - kk, axis=1)
            conv = conv + tap * convw_ref[kk:kk + 1, csl].reshape(1, 1, GROUP_WIDTH)
        store(cc, _silu(conv[:, CONV_PAD:, :]))


PROMPT_TILE = 256


def _seq_prompt_kernel(ntile, x_ref, nw_ref, wa_ref, wb_ref, wdt_ref, k_ref, v_ref,
                       wgrp_ref, pscale_ref, convw_ref, convb_ref, dtb_ref, alog_ref, dexp_ref, ssdnw_ref,
                       gates_ref, ypool_ref, yssd_ref, yatt_ref, pool_o_ref, conv_o_ref, ssm_o_ref,
                       pext, cext, xs_scr, b_scr, c_scr, dt_scr, y_scr, h_scr, zs_scr, q_scr, za_scr):
    t, q = PROMPT_TILE, SSD_CHUNK
    k = pl.program_id(0)
    s = k % ntile
    last = ntile - 1

    @pl.when(k == 0)
    def _():
        y_scr[...] = jnp.zeros(y_scr.shape, F32)
        zs_scr[...] = jnp.zeros(zs_scr.shape, BF16)
        q_scr[...] = jnp.zeros(q_scr.shape, BF16)
        za_scr[...] = jnp.zeros(za_scr.shape, BF16)

    @pl.when(s == 0)
    def _():
        pext[:, 0:POOL_PAD, :] = jnp.zeros((1, POOL_PAD, D_MODEL), F32)
        cext[:, 0:CONV_PAD, :] = jnp.zeros((1, CONV_PAD, CONV_DIM), F32)
        h_scr[...] = jnp.zeros(h_scr.shape, F32)

    @pl.when(s > 0)
    def _():
        carry_p = pext[:, t:t + POOL_PAD, :]
        carry_c = cext[:, t:t + CONV_PAD, :]
        pext[:, 0:POOL_PAD, :] = carry_p
        cext[:, 0:CONV_PAD, :] = carry_c

    yz = y_scr[...] * zs_scr[...].astype(F32)
    yssd_ref[...] = _rms(yz, ssdnw_ref[...]).astype(BF16)
    scale = ATT_HEAD_DIM ** -0.5
    outs = []
    for hd in range(ATT_HEADS):
        hsl = slice(hd * ATT_HEAD_DIM, (hd + 1) * ATT_HEAD_DIM)
        p = _softmax_rows(lax.dot_general(q_scr[:, hsl], k_ref[0, hd], _NT, preferred_element_type=F32) * scale)
        outs.append(_dot(p.astype(BF16), v_ref[0, hd]))
    yatt_ref[...] = (jnp.concatenate(outs, axis=1) * za_scr[...].astype(F32)).astype(BF16)

    hn = _rms(x_ref[...], nw_ref[...]).astype(BF16)

    def proj(piece, lo=0, hi=None):
        return _dot(hn, _weight_cols((wa_ref, wb_ref), piece, lo, piece[1] - piece[0] if hi is None else hi))

    def gates_piece(c):
        def run():
            gates_ref[:, c * D_MODEL:(c + 1) * D_MODEL] = _sigmoid(
                proj(COL_GATES, c * D_MODEL, (c + 1) * D_MODEL).astype(BF16))
        return run

    def zs_piece(c):
        def run():
            zs_scr[:, c * D_MODEL:(c + 1) * D_MODEL] = _silu(proj(COL_ZS, c * D_MODEL, (c + 1) * D_MODEL).astype(BF16))
        return run

    def q_piece():
        q_scr[...] = proj(COL_Q).astype(BF16)

    def za_piece():
        za_scr[...] = _silu(proj(COL_ZA).astype(BF16))

    fillers = [gates_piece(0), gates_piece(1), gates_piece(2), zs_piece(0), zs_piece(1), q_piece, za_piece]

    def run_filler():
        if fillers:
            fillers.pop(0)()

    u = proj(COL_U).reshape(1, t, D_MODEL)
    pext[:, POOL_PAD:, :] = u
    pos = s * t + lax.broadcasted_iota(jnp.int32, (1, t, 1), 1)
    ypool_ref[...] = _pool_branch(pext, u, pos, wgrp_ref, pscale_ref, proj(COL_ZP)).astype(BF16)

    for c in range(CONV_DIM // D_MODEL):
        cext[:, CONV_PAD:, c * D_MODEL:(c + 1) * D_MODEL] = proj(
            COL_XBC, c * D_MODEL, (c + 1) * D_MODEL).reshape(1, t, D_MODEL)
    dt_scr[...] = _softplus(_dot(hn, wdt_ref[...]) + dtb_ref[...])

    def store_conv(cc, val):
        if cc < SSD_GROUPS:
            xs_scr[:, cc * GROUP_WIDTH:(cc + 1) * GROUP_WIDTH] = val[0]
        elif cc == SSD_GROUPS:
            b_scr[...] = val[0]
        else:
            c_scr[...] = val[0]
        run_filler()

    _conv_branch(cext, convw_ref, convb_ref, store_conv)

    a_neg = -jnp.exp(alog_ref[...])
    rq = lax.broadcasted_iota(jnp.int32, (q, q), 0)
    cq = lax.broadcasted_iota(jnp.int32, (q, q), 1)
    tril = rq >= cq
    tri_f = tril.astype(F32)
    lane_lo = lax.broadcasted_iota(jnp.int32, (1, LANES), 1) < SSD_HEAD_DIM
    pairs_per_group = SSD_HEADS // SSD_GROUPS // 2

    def chunk(c):
        rsl = slice(c * q, (c + 1) * q)
        dtc = dt_scr[rsl, :]
        acs = jnp.dot(tri_f, dtc * a_neg, precision=lax.Precision.HIGHEST, preferred_element_type=F32)
        acs_t = acs.T
        dt_t = dtc.T
        wdec_t = (dt_t * jnp.exp(acs_t[:, q - 1:q] - acs_t)).astype(BF16)
        row_t = acs_t - jnp.log(dt_t)
        cdec = jnp.exp(acs[q - 1:q, :])
        for g in range(SSD_GROUPS):
            gsl = slice(g * SSD_STATE, (g + 1) * SSD_STATE)
            bg = b_scr[rsl, gsl]
            cg_b = c_scr[rsl, gsl].astype(BF16)
            cb = lax.dot_general(cg_b, bg.astype(BF16), _NT, preferred_element_type=F32).astype(BF16)
            bg_t = bg.T.astype(BF16)
            hsl = slice(g * GROUP_WIDTH, (g + 1) * GROUP_WIDTH)
            z_g = _dot(cg_b, h_scr[:, hsl].astype(BF16))
            for jp in range(pairs_per_group):
                j = g * pairs_per_group + jp
                lsl = slice(j * LANES, (j + 1) * LANES)
                xp = xs_scr[rsl, lsl]
                xp_b = xp.astype(BF16)
                zero_b = jnp.zeros_like(xp_b)
                x_bd = jnp.concatenate([jnp.where(lane_lo, xp_b, zero_b), jnp.where(lane_lo, zero_b, xp_b)],
                                       axis=0)
                ms, bws, cols = [], [], []
                for hh in range(2):
                    r = 2 * j + hh
                    cols.append(jnp.broadcast_to(acs[:, r:r + 1], (q, q)))
                    seg = cols[hh] - row_t[r:r + 1, :]
                    ms.append(cb * jnp.exp(jnp.where(tril, seg, NEG_BIG)).astype(BF16))
                    bws.append(bg_t * wdec_t[r:r + 1, :])
                ea_pair = jnp.exp(jnp.where(lane_lo, cols[0], cols[1]))
                y = (_dot(jnp.concatenate(ms, axis=1), x_bd)
                     + ea_pair * z_g[:, jp * LANES:(jp + 1) * LANES] + dexp_ref[:, lsl] * xp)
                y_scr[rsl, lsl] = y
                cd_pair = jnp.where(lane_lo, cdec[:, 2 * j:2 * j + 1], cdec[:, 2 * j + 1:2 * j + 2])
                h_scr[:, lsl] = h_scr[:, lsl] * cd_pair + _dot(jnp.concatenate(bws, axis=1), x_bd)
            run_filler()

    for c in range(t // q):
        chunk(c)
    while fillers:
        run_filler()

    @pl.when(jnp.logical_and(s == last, k < pl.num_programs(0) - 1))
    def _():
        pool_o_ref[...] = pext[:, t + POOL_PAD - POOL_HIST:t + POOL_PAD, :]
        conv_o_ref[...] = cext[:, t + CONV_PAD - (CONV_WIDTH - 1):t + CONV_PAD, :]
        for j in range(SSD_HEADS // 2):
            lsl = slice(j * LANES, (j + 1) * LANES)
            ssm_o_ref[0, lsl, :] = h_scr[:, lsl].T


def _seq_prompt(x2d, nw, w_a, w_b, w_dt, kb, vb, params, *, nseq, ntile):
    t = PROMPT_TILE
    m = x2d.shape[0]
    ntiles = nseq * ntile
    tile = lambda k: jnp.minimum(k, ntiles - 1)
    closed = lambda k: jnp.maximum(k - 1, 0)
    own_rows = lambda width: pl.BlockSpec((t, width), lambda k: (k, 0))
    closed_rows = lambda width: pl.BlockSpec((t, width), lambda k: (closed(k), 0))
    seq_spec = lambda shape, which: pl.BlockSpec((1,) + shape, lambda k: (which(k) // ntile,) + (0,) * len(shape))
    const_spec = lambda a: pl.BlockSpec(a.shape, lambda k: (0,) * a.ndim)
    resident = lambda a: pl.BlockSpec(a.shape, lambda k: (0,) * a.ndim, pipeline_mode=pl.Buffered(1))
    in_specs = [
        pl.BlockSpec((t, D_MODEL), lambda k: (tile(k), 0)),
        resident(nw),
        pl.BlockSpec((D_MODEL, W_SPLIT[0]), lambda k: (0, 0), pipeline_mode=pl.Buffered(1)),
        resident(w_b), resident(w_dt),
        seq_spec((ATT_HEADS, MEM_LEN, ATT_HEAD_DIM), closed),
        seq_spec((ATT_HEADS, MEM_LEN, ATT_HEAD_DIM), closed),
    ] + [const_spec(p) for p in params]
    out_specs = [
        own_rows(3 * D_MODEL), own_rows(D_MODEL), closed_rows(SSD_WIDTH), closed_rows(D_MODEL),
        seq_spec((POOL_HIST, D_MODEL), tile),
        seq_spec((CONV_WIDTH - 1, CONV_DIM), tile),
        seq_spec((SSD_WIDTH, SSD_STATE), tile),
    ]
    out_shape = [
        jax.ShapeDtypeStruct((m + t, 3 * D_MODEL), BF16),
        jax.ShapeDtypeStruct((m + t, D_MODEL), BF16),
        jax.ShapeDtypeStruct((m, SSD_WIDTH), BF16),
        jax.ShapeDtypeStruct((m, D_MODEL), BF16),
        jax.ShapeDtypeStruct((nseq, POOL_HIST, D_MODEL), F32),
        jax.ShapeDtypeStruct((nseq, CONV_WIDTH - 1, CONV_DIM), F32),
        jax.ShapeDtypeStruct((nseq, SSD_WIDTH, SSD_STATE), F32),
    ]
    scratch = [
        pltpu.VMEM((1, POOL_PAD + t, D_MODEL), F32),
        pltpu.VMEM((1, CONV_PAD + t, CONV_DIM), F32),
        pltpu.VMEM((t, SSD_WIDTH), F32),
        pltpu.VMEM((t, GROUP_WIDTH), F32),
        pltpu.VMEM((t, GROUP_WIDTH), F32),
        pltpu.VMEM((t, LANES), F32),
        pltpu.VMEM((t, SSD_WIDTH), F32),
        pltpu.VMEM((SSD_STATE, SSD_WIDTH), F32),
        pltpu.VMEM((t, SSD_WIDTH), BF16),
        pltpu.VMEM((t, D_MODEL), BF16),
        pltpu.VMEM((t, D_MODEL), BF16),
    ]
    return pl.pallas_call(
        functools.partial(_seq_prompt_kernel, ntile),
        grid=(ntiles + 1,),
        in_specs=in_specs,
        out_specs=out_specs,
        out_shape=out_shape,
        scratch_shapes=scratch,
        compiler_params=pltpu.CompilerParams(dimension_semantics=("arbitrary",),
                                             vmem_limit_bytes=SEQ_PROMPT_VMEM_LIMIT),
        name="seq_prompt",
    )(x2d, nw, w_a, w_b, w_dt, kb, vb, *params)


SAMPLE_BLOCK = 32
STATE_PRE_SUB = 4


def _state_pre_kernel(u_ref, zp_ref, xbc_ref, dt_ref, ph_ref, ch_ref,
                      wgrp_ref, pscale_ref, convw_ref, convb_ref, dtb_ref, alog_ref,
                      ypool_ref, xs_ref, bc_ref, dts_ref, cd_ref, pool_o_ref, conv_o_ref, pext, cext):
    nb, t = SAMPLE_BLOCK, SUBLANES
    pext[:, 0:1, :] = jnp.zeros((nb, 1, D_MODEL), F32)
    pext[:, 1:POOL_PAD, :] = ph_ref[...]
    cext[:, 0:CONV_PAD - (CONV_WIDTH - 1), :] = jnp.zeros((nb, CONV_PAD - (CONV_WIDTH - 1), CONV_DIM), F32)
    cext[:, CONV_PAD - (CONV_WIDTH - 1):CONV_PAD, :] = ch_ref[...]

    pext[:, POOL_PAD:, :] = u_ref[...].astype(F32).reshape(nb, t, D_MODEL)
    cext[:, CONV_PAD:, :] = xbc_ref[...].astype(F32).reshape(nb, t, CONV_DIM)
    pos = PAST_LEN + lax.broadcasted_iota(jnp.int32, (1, t, 1), 1)
    for sb in range(nb // STATE_PRE_SUB):
        seqs = slice(sb * STATE_PRE_SUB, (sb + 1) * STATE_PRE_SUB)
        rsl = slice(sb * STATE_PRE_SUB * t, (sb + 1) * STATE_PRE_SUB * t)
        pv, cv = pext.at[seqs], cext.at[seqs]
        ypool_ref[rsl, :] = _pool_branch(pv, pv[:, POOL_PAD:, :], pos, wgrp_ref, pscale_ref,
                                         zp_ref[rsl, :].astype(F32)).astype(BF16)

        def store_conv(cc, val, rsl=rsl):
            val = val.reshape(STATE_PRE_SUB * t, GROUP_WIDTH)
            if cc < SSD_GROUPS:
                xs_ref[rsl, cc * GROUP_WIDTH:(cc + 1) * GROUP_WIDTH] = val
            else:
                bc_ref[rsl, (cc - SSD_GROUPS) * GROUP_WIDTH:(cc - SSD_GROUPS + 1) * GROUP_WIDTH] = val

        _conv_branch(cv, convw_ref, convb_ref, store_conv)
    dt = _softplus(dt_ref[...] + dtb_ref[...])
    dts_ref[...] = dt
    a = dt * -jnp.exp(alog_ref[...])
    cd_ref[...] = jnp.exp(jnp.sum(a.reshape(nb, t, LANES), axis=1))
    pool_o_ref[...] = pext[:, t + POOL_PAD - POOL_HIST:t + POOL_PAD, :]
    conv_o_ref[...] = cext[:, t + CONV_PAD - (CONV_WIDTH - 1):t + CONV_PAD, :]


def _col_block(piece):
    return piece[0] // (piece[1] - piece[0])


def _state_pre(main, dt, state_pool, state_conv, params):
    nb, t = SAMPLE_BLOCK, SUBLANES
    rows = nb * t
    m = main.shape[0]
    nseq = m // t
    col_spec = lambda width, idx: pl.BlockSpec((rows, width), lambda i: (i, idx))
    seq_spec = lambda shape: pl.BlockSpec((nb,) + shape, lambda i: (i,) + (0,) * len(shape))
    const_spec = lambda a: pl.BlockSpec(a.shape, lambda i: (0,) * a.ndim)
    return pl.pallas_call(
        _state_pre_kernel,
        grid=(nseq // nb,),
        in_specs=[col_spec(D_MODEL, _col_block(COL_U)), col_spec(D_MODEL, _col_block(COL_ZP)),
                  col_spec(CONV_DIM, _col_block(COL_XBC)), col_spec(LANES, 0),
                  seq_spec((POOL_HIST, D_MODEL)), seq_spec((CONV_WIDTH - 1, CONV_DIM))]
        + [const_spec(p) for p in params],
        out_specs=[col_spec(D_MODEL, 0), col_spec(SSD_WIDTH, 0), col_spec(2 * GROUP_WIDTH, 0), col_spec(LANES, 0),
                   pl.BlockSpec((nb, LANES), lambda i: (i, 0)),
                   seq_spec((POOL_HIST, D_MODEL)), seq_spec((CONV_WIDTH - 1, CONV_DIM))],
        out_shape=[jax.ShapeDtypeStruct((m, D_MODEL), BF16), jax.ShapeDtypeStruct((m, SSD_WIDTH), F32),
                   jax.ShapeDtypeStruct((m, 2 * GROUP_WIDTH), F32), jax.ShapeDtypeStruct((m, LANES), F32),
                   jax.ShapeDtypeStruct((nseq, LANES), F32),
                   jax.ShapeDtypeStruct((nseq, POOL_HIST, D_MODEL), F32),
                   jax.ShapeDtypeStruct((nseq, CONV_WIDTH - 1, CONV_DIM), F32)],
        scratch_shapes=[pltpu.VMEM((nb, POOL_PAD + t, D_MODEL), F32), pltpu.VMEM((nb, CONV_PAD + t, CONV_DIM), F32)],
        compiler_params=pltpu.CompilerParams(dimension_semantics=("arbitrary",), vmem_limit_bytes=VMEM_LIMIT),
        name="state_pre",
    )(main, main, main, dt, state_pool, state_conv, *params)


PIPE_SLOTS = 4


def _ring_pipeline(n, start_in, wait_in, compute, start_out=None, wait_out=None):
    ns = PIPE_SLOTS
    for i in range(ns - 1):
        start_in(i, i)

    def body(bb, _):
        for k in range(ns):
            i = ns * bb + k
            nxt = i + ns - 1
            pl.when(nxt < n)(functools.partial(start_in, nxt, (k + ns - 1) % ns))
            wait_in(i, k)
            if wait_out is not None:
                pl.when(bb > 0)(functools.partial(wait_out, i - ns, k))
            compute(i, k)
            if start_out is not None:
                start_out(i, k)
        return 0

    lax.fori_loop(0, n // ns, body, 0)
    if wait_out is not None:
        for k in range(ns):
            wait_out(n - ns + k, k)


SSD_ITEM = 4


def _ssd_state_kernel(xs_ref, bc_ref, dt_ref, zs_ref, cd_ref, hin_hbm, expand_ref, segsum_ref, alog_ref, dexp_ref,
                      ssdnw_ref, yssd_ref, hout_hbm, hbuf, obuf, y_scr, sem_in, sem_out):
    nb, t = SAMPLE_BLOCK, SUBLANES
    base = pl.program_id(0) * nb
    a_neg = -jnp.exp(alog_ref[...])
    ridx = lax.broadcasted_iota(jnp.int32, (t, LANES), 0)

    def in_copies(item, slot):
        return [pltpu.make_async_copy(hin_hbm.at[base + item * SSD_ITEM + j, g * GROUP_WIDTH:(g + 1) * GROUP_WIDTH, :],
                                      hbuf.at[slot, j, g * GROUP_WIDTH:(g + 1) * GROUP_WIDTH, :],
                                      sem_in.at[slot, j * SSD_GROUPS + g])
                for j in range(SSD_ITEM) for g in range(SSD_GROUPS)]

    def out_copies(item, slot):
        return [pltpu.make_async_copy(obuf.at[slot, j, g * GROUP_WIDTH:(g + 1) * GROUP_WIDTH, :],
                                      hout_hbm.at[base + item * SSD_ITEM + j, g * GROUP_WIDTH:(g + 1) * GROUP_WIDTH, :],
                                      sem_out.at[slot, j * SSD_GROUPS + g])
                for j in range(SSD_ITEM) for g in range(SSD_GROUPS)]

    def start_all(copies):
        for c in copies:
            c.start()

    def wait_all(copies):
        for c in copies:
            c.wait()

    def per_head_factors(b):
        rsl = pl.ds(pl.multiple_of(b * t, t), t)
        dtc = dt_ref[rsl, :]
        acs = dtc * a_neg
        for sh in (1, 2, 4):
            acs = acs + jnp.where(ridx >= sh, pltpu.roll(acs, sh, axis=0), 0.0)
        tot = acs[t - 1:t, :]
        bc = bc_ref[rsl, :]
        bm, cm = bc[:, :GROUP_WIDTH], bc[:, GROUP_WIDTH:]
        bm_r, cm_r = bm.astype(BF16).astype(F32), cm.astype(BF16).astype(F32)
        gs, ps = [], []
        for k in range(t):
            gs.append(jnp.exp(jnp.where(ridx >= k, acs - acs[k:k + 1, :], NEG_BIG)) * dtc[k:k + 1, :])
            ps.append(cm_r * bm_r[k:k + 1, :])
        cb_heads = _dot(jnp.concatenate(ps, axis=0).astype(BF16), segsum_ref[...])
        per_head = jnp.concatenate([jnp.concatenate(gs, axis=0) * cb_heads, jnp.exp(acs),
                                    dtc * jnp.exp(tot - acs)], axis=0)
        return dict(rsl=rsl, b=b, bm=bm, cm=cm, per_head=per_head)

    def widen(st):
        hi = st["per_head"].astype(BF16)
        lo = (st["per_head"] - hi.astype(F32)).astype(BF16)
        st["wide"] = _dot(hi, expand_ref[...]) + _dot(lo, expand_ref[...])

    def intra(st):
        x = xs_ref[st["rsl"], :]
        wide = st["wide"]
        y = dexp_ref[...] * x
        for k in range(t):
            y = y + wide[k * t:(k + 1) * t, :] * x[k:k + 1, :]
        st["y"] = y
        st["ea_wide"] = wide[t * t:t * t + t, :]
        st["xw"] = x * wide[t * t + t:, :]

    def state_group(st, slot, j, g):
        gsl = slice(g * SSD_STATE, (g + 1) * SSD_STATE)
        wsl = slice(g * GROUP_WIDTH, (g + 1) * GROUP_WIDTH)
        hg = hbuf[slot, j, wsl, :]
        z_g = lax.dot_general(st["cm"][:, gsl].astype(BF16), hg.astype(BF16), _NT, preferred_element_type=F32)
        y_scr[st["rsl"], wsl] = st["y"][:, wsl] + st["ea_wide"][:, wsl] * z_g
        upd = lax.dot_general(st["xw"][:, wsl].astype(BF16), st["bm"][:, gsl].astype(BF16), _TN,
                              preferred_element_type=F32)
        for r8 in range(SSD_HEADS // SSD_GROUPS):
            r = g * (SSD_HEADS // SSD_GROUPS) + r8
            rows_r = slice(r * SSD_HEAD_DIM, (r + 1) * SSD_HEAD_DIM)
            obuf[slot, j, rows_r, :] = (hbuf[slot, j, rows_r, :] * cd_ref[base + st["b"], r]
                                        + upd[r8 * SSD_HEAD_DIM:(r8 + 1) * SSD_HEAD_DIM, :])

    def compute(item, slot):
        sts = [per_head_factors(item * SSD_ITEM + j) for j in range(SSD_ITEM)]
        for st in sts:
            widen(st)
        for st in sts:
            intra(st)
        for g in range(SSD_GROUPS):
            for j, st in enumerate(sts):
                state_group(st, slot, j, g)

    _ring_pipeline(nb // SSD_ITEM,
                   lambda i, slot: start_all(in_copies(i, slot)), lambda i, slot: wait_all(in_copies(i, slot)),
                   compute,
                   lambda i, slot: start_all(out_copies(i, slot)), lambda i, slot: wait_all(out_copies(i, slot)))
    yz = y_scr[...] * _silu(zs_ref[...].astype(F32))
    yssd_ref[...] = _rms(yz, ssdnw_ref[...]).astype(BF16)


def _ssd_state(xs, bc, dts, main, cd, hin, expand, segsum, alog, dexp, ssdnw):
    nb, t = SAMPLE_BLOCK, SUBLANES
    rows = nb * t
    m = xs.shape[0]
    nseq = m // t
    col_spec = lambda width, idx: pl.BlockSpec((rows, width), lambda i: (i, idx))
    const_spec = lambda a: pl.BlockSpec(a.shape, lambda i: (0,) * a.ndim)
    return pl.pallas_call(
        _ssd_state_kernel,
        grid=(nseq // nb,),
        in_specs=[col_spec(SSD_WIDTH, 0), col_spec(2 * GROUP_WIDTH, 0), col_spec(LANES, 0),
                  col_spec(SSD_WIDTH, _col_block(COL_ZS)),
                  pl.BlockSpec(memory_space=pltpu.SMEM), pl.BlockSpec(memory_space=pl.ANY),
                  const_spec(expand), const_spec(segsum), const_spec(alog), const_spec(dexp), const_spec(ssdnw)],
        out_specs=[col_spec(SSD_WIDTH, 0), pl.BlockSpec(memory_space=pl.ANY)],
        out_shape=[jax.ShapeDtypeStruct((m, SSD_WIDTH), BF16), jax.ShapeDtypeStruct(hin.shape, F32)],
        scratch_shapes=[pltpu.VMEM((PIPE_SLOTS, SSD_ITEM, SSD_WIDTH, SSD_STATE), F32),
                        pltpu.VMEM((PIPE_SLOTS, SSD_ITEM, SSD_WIDTH, SSD_STATE), F32),
                        pltpu.VMEM((rows, SSD_WIDTH), F32),
                        pltpu.SemaphoreType.DMA((PIPE_SLOTS, SSD_ITEM * SSD_GROUPS)),
                        pltpu.SemaphoreType.DMA((PIPE_SLOTS, SSD_ITEM * SSD_GROUPS))],
        compiler_params=pltpu.CompilerParams(dimension_semantics=("arbitrary",), vmem_limit_bytes=VMEM_LIMIT),
        name="ssd_state",
    )(xs, bc, dts, main, cd, hin, expand, segsum, alog, dexp, ssdnw)


ATT_ITEM = 4


def _att_state_kernel(q_ref, za_ref, k_hbm, v_hbm, yatt_ref, kbuf, vbuf, q_scr, att_scr, sem):
    nb, t = SAMPLE_BLOCK, SUBLANES
    base = pl.program_id(0) * nb
    scale = ATT_HEAD_DIM ** -0.5
    q_scr[...] = q_ref[...].astype(F32)
    head_of_lane = lax.broadcasted_iota(jnp.int32, (1, D_MODEL), 1) // ATT_HEAD_DIM

    def copies(item, slot):
        out = []
        for j in range(ATT_ITEM):
            for kv, (src, buf) in enumerate(((k_hbm, kbuf), (v_hbm, vbuf))):
                for hd in range(ATT_HEADS):
                    hsl = slice(hd * ATT_HEAD_DIM, (hd + 1) * ATT_HEAD_DIM)
                    out.append(pltpu.make_async_copy(
                        src.at[base + item * ATT_ITEM + j, :, hd, :], buf.at[slot, j, :, hsl],
                        sem.at[slot, (j * 2 + kv) * ATT_HEADS + hd]))
        return out

    def start_in(item, slot):
        for c in copies(item, slot):
            c.start()

    def wait_in(item, slot):
        for c in copies(item, slot):
            c.wait()

    def compute(item, slot):
        seqs = range(ATT_ITEM)
        rsl = [pl.ds(pl.multiple_of((item * ATT_ITEM + j) * t, t), t) for j in seqs]
        scs = []
        for j in seqs:
            qf = q_scr[rsl[j], :]
            q_bd = jnp.concatenate([jnp.where(head_of_lane == hd, qf, 0.0) for hd in range(ATT_HEADS)],
                                   axis=0).astype(BF16)
            scs.append(lax.dot_general(q_bd, kbuf[slot, j].astype(BF16), _NT, preferred_element_type=F32) * scale)
        ps = [_softmax_rows(sc).astype(BF16) for sc in scs]
        outs = [_dot(ps[j], vbuf[slot, j].astype(BF16)) for j in seqs]
        for j in seqs:
            att_scr[rsl[j], :] = jnp.concatenate(
                [outs[j][hd * t:(hd + 1) * t, hd * ATT_HEAD_DIM:(hd + 1) * ATT_HEAD_DIM] for hd in range(ATT_HEADS)],
                axis=1)

    _ring_pipeline(nb // ATT_ITEM, start_in, wait_in, compute)
    yatt_ref[...] = (att_scr[...] * _silu(za_ref[...].astype(F32))).astype(BF16)


def _att_state(main, k, v):
    nb, t = SAMPLE_BLOCK, SUBLANES
    rows = nb * t
    m = main.shape[0]
    col_spec = lambda width, idx: pl.BlockSpec((rows, width), lambda i: (i, idx))
    kv_buf = pltpu.VMEM((PIPE_SLOTS, ATT_ITEM, MEM_LEN, D_MODEL), F32)
    return pl.pallas_call(
        _att_state_kernel,
        grid=(m // rows,),
        in_specs=[col_spec(D_MODEL, _col_block(COL_Q)), col_spec(D_MODEL, _col_block(COL_ZA)),
                  pl.BlockSpec(memory_space=pl.ANY), pl.BlockSpec(memory_space=pl.ANY)],
        out_specs=col_spec(D_MODEL, 0),
        out_shape=jax.ShapeDtypeStruct((m, D_MODEL), BF16),
        scratch_shapes=[kv_buf, kv_buf, pltpu.VMEM((rows, D_MODEL), F32), pltpu.VMEM((rows, D_MODEL), F32),
                        pltpu.SemaphoreType.DMA((PIPE_SLOTS, ATT_ITEM * 2 * ATT_HEADS))],
        compiler_params=pltpu.CompilerParams(dimension_semantics=("arbitrary",), vmem_limit_bytes=VMEM_LIMIT),
        name="att_state",
    )(main, main, k, v)


DENSE_ROWS = 512


def _dense_kernel(x_ref, gt_ref, yp_ref, ys_ref, ya_ref, wpo_ref, wso_ref, wao_ref, wo_ref, fnw_ref, y_ref):
    gates = gt_ref[...].astype(F32)
    merged = (gates[:, 0:D_MODEL] * _dot(yp_ref[...], wpo_ref[...])
              + gates[:, D_MODEL:2 * D_MODEL] * _dot(ys_ref[...], wso_ref[...])
              + gates[:, 2 * D_MODEL:] * _dot(ya_ref[...], wao_ref[...]))
    x_out = x_ref[...] + _dot(merged.astype(BF16), wo_ref[...])
    y_ref[...] = _rms(x_out, fnw_ref[...])


def _dense(x2d, gates, gate_idx, yp, ys, ya, wpo, wso, wao, wo, fnw):
    m = x2d.shape[0]
    row = lambda width, idx=0: pl.BlockSpec((DENSE_ROWS, width), lambda i: (i, idx))
    resident = lambda a: pl.BlockSpec(a.shape, lambda i: (0,) * a.ndim, pipeline_mode=pl.Buffered(1))
    return pl.pallas_call(
        _dense_kernel,
        grid=(m // DENSE_ROWS,),
        in_specs=[row(D_MODEL), row(3 * D_MODEL, gate_idx), row(D_MODEL), row(SSD_WIDTH), row(D_MODEL),
                  resident(wpo), resident(wso), resident(wao), resident(wo), resident(fnw)],
        out_specs=row(D_MODEL),
        out_shape=jax.ShapeDtypeStruct((m, D_MODEL), F32),
        compiler_params=pltpu.CompilerParams(dimension_semantics=("arbitrary",), vmem_limit_bytes=VMEM_LIMIT),
        name="dense",
    )(x2d, gates, yp, ys, ya, wpo, wso, wao, wo, fnw)


def kernel(x_prompt, x_sample, mem_prompt, state_pool, state_conv, state_ssm, cache_mem_k, cache_mem_v,
           norm_w, w_in, w_pool_grp, pool_scale, conv_w, conv_b, dt_bias, a_log, d_skip, ssd_norm_w,
           mem_norm_w, w_mem_k, w_mem_v, w_pool_out, w_ssd_out, w_att_out, w_out, final_norm_w):
    assert w_in.shape[0] == 1
    bp, sp, d = x_prompt.shape
    bs, ss, _ = x_sample.shape
    assert ss == SUBLANES and sp % PROMPT_TILE == 0 and bs % SAMPLE_BLOCK == 0

    w_a = w_in[0].astype(BF16)
    w_b = w_a[:, W_SPLIT[1]:]
    w_dt = jnp.pad(w_a[:, W_SPLIT[0]:W_SPLIT[1]], ((0, 0), (0, LANES - SSD_HEADS)))
    nw = norm_w[0].reshape(1, d)
    pad_heads = lambda a: jnp.pad(a.reshape(1, SSD_HEADS), ((0, 0), (0, LANES - SSD_HEADS)))
    wgrp = w_pool_grp[0].astype(BF16)
    pscale = pool_scale[0].reshape(1, d)
    convb = conv_b[0].reshape(1, CONV_DIM)
    dtb, alog = pad_heads(dt_bias[0]), pad_heads(a_log[0])
    dexp = jnp.repeat(d_skip[0], SSD_HEAD_DIM).reshape(1, SSD_WIDTH)
    ssdnw = ssd_norm_w[0].reshape(1, SSD_WIDTH)
    dense_w = (w_pool_out[0].astype(BF16), w_ssd_out[0].astype(BF16), w_att_out[0].astype(BF16),
               w_out[0].astype(BF16), final_norm_w.reshape(1, d))
    head_of_lane = jnp.arange(SSD_WIDTH) // SSD_HEAD_DIM
    expand = (jnp.arange(LANES)[:, None] == head_of_lane[None, :]).astype(BF16)
    group_of_head = jnp.where(jnp.arange(LANES) < SSD_HEADS, jnp.arange(LANES) // (SSD_HEADS // SSD_GROUPS), -1)
    segsum = ((jnp.arange(GROUP_WIDTH) // SSD_STATE)[:, None] == group_of_head[None, :]).astype(BF16)

    mk, mv, mkb, mvb = _memkv(mem_prompt, mem_norm_w[0].reshape(1, d), w_mem_k[0].astype(BF16),
                              w_mem_v[0].astype(BF16))
    xp2 = x_prompt.reshape(bp * sp, d)
    gates_p, yp, ysd, ya, pool_p, conv_p, ssm_p = _seq_prompt(
        xp2, nw, w_a, w_b, w_dt, mkb, mvb, (wgrp, pscale, conv_w[0], convb, dtb, alog, dexp, ssdnw),
        nseq=bp, ntile=sp // PROMPT_TILE)
    y_prompt = _dense(xp2, gates_p, 0, yp, ysd, ya, *dense_w).reshape(bp, sp, d)

    xs2 = x_sample.reshape(bs * ss, d)
    main_s, dt_s = _inproj(xs2, nw, w_a, w_b, w_dt)
    yp, xs, bc, dts, cd, pool_s, conv_s = _state_pre(
        main_s, dt_s, state_pool[0], state_conv[0], (wgrp, pscale, conv_w[0], convb, dtb, alog))
    ysd, ssm_s = _ssd_state(xs, bc, dts, main_s, cd, state_ssm[0].reshape(bs, SSD_WIDTH, SSD_STATE),
                            expand, segsum, alog, dexp, ssdnw)
    ya = _att_state(main_s, cache_mem_k[0], cache_mem_v[0])
    y_sample = _dense(xs2, main_s, _col_block(COL_GATES), yp, ysd, ya, *dense_w).reshape(bs, ss, d)

    ssm_shape = (SSD_GROUPS, SSD_HEADS // SSD_GROUPS, SSD_HEAD_DIM, SSD_STATE)
    return (y_prompt, y_sample,
            pool_p[None], conv_p[None], ssm_p.reshape((1, bp) + ssm_shape),
            mk[None], mv[None],
            pool_s[None], conv_s[None], ssm_s.reshape((1, bs) + ssm_shape))
```

```python
import functools

import jax
import jax.numpy as jnp
from jax import lax
from jax.experimental import pallas as pl
from jax.experimental.pallas import tpu as pltpu

F32 = jnp.float32
BF16 = jnp.bfloat16

D_MODEL = 1024
POOL_WINDOWS = (2, 4, 8, 16)
POOL_GROUP = 256
POOL_HIST = 15
POOL_PAD = 16
SSD_WIDTH = 2048
SSD_HEADS = 32
SSD_HEAD_DIM = 64
SSD_GROUPS = 4
SSD_STATE = 128
GROUP_WIDTH = SSD_WIDTH // SSD_GROUPS
CONV_WIDTH = 4
CONV_DIM = 3072
CONV_PAD = 8
SSD_CHUNK = 128
MEM_LEN = 256
ATT_HEADS = 4
ATT_HEAD_DIM = 256
PAST_LEN = 16384
EPS = 1e-6
NEG_BIG = -1e30
SUBLANES = 8
LANES = 128
MAIN_COLS = 12288
COL_XBC, COL_GATES, COL_ZS = (0, 3072), (3072, 6144), (6144, 8192)
COL_U, COL_ZP, COL_Q, COL_ZA = (8192, 9216), (9216, 10240), (10240, 11264), (11264, 12288)
W_SPLIT = (7168, 7200)
W_SRC = {COL_U: (0, 0), COL_ZP: (0, 1024), COL_ZS: (0, 2048), COL_XBC: (0, 4096),
         COL_Q: (1, 0), COL_ZA: (1, 1024), COL_GATES: (1, 2048)}
VMEM_LIMIT = 56 * 1024 * 1024
SEQ_PROMPT_VMEM_LIMIT = 60 * 1024 * 1024

_NT = (((1,), (1,)), ((), ()))
_TN = (((0,), (0,)), ((), ()))


def _sigmoid(x):
    return 1.0 / (1.0 + jnp.exp(-x))


def _silu(x):
    return x * _sigmoid(x)


def _softplus(x):
    return jnp.maximum(x, 0.0) + jnp.log1p(jnp.exp(-jnp.abs(x)))


def _rms(x, w):
    return x * lax.rsqrt(jnp.mean(x * x, axis=-1, keepdims=True) + EPS) * w


def _dot(a, b):
    return jnp.dot(a, b, preferred_element_type=F32)


def _softmax_rows(sc):
    e = jnp.exp(sc - jnp.max(sc, axis=-1, keepdims=True))
    return e / jnp.sum(e, axis=-1, keepdims=True)


def _weight_cols(w_refs, piece, lo, hi):
    idx, c0 = W_SRC[piece]
    return w_refs[idx][:, c0 + lo:c0 + hi]


def _memkv_kernel(mem_ref, nw_ref, wk_ref, wv_ref, k_ref, v_ref, kb_ref, vb_ref):
    mh = _rms(mem_ref[0], nw_ref[...]).astype(BF16)
    k = _dot(mh, wk_ref[...])
    v = _dot(mh, wv_ref[...])
    for hd in range(ATT_HEADS):
        hsl = slice(hd * ATT_HEAD_DIM, (hd + 1) * ATT_HEAD_DIM)
        k_ref[0, :, hd, :] = k[:, hsl]
        v_ref[0, :, hd, :] = v[:, hsl]
        kb_ref[0, hd] = k[:, hsl].astype(BF16)
        vb_ref[0, hd] = v[:, hsl].astype(BF16)


def _memkv(mem, nw, wk, wv):
    b, m, d = mem.shape
    full = lambda shape: pl.BlockSpec(shape, lambda i: (0,) * len(shape))
    blk = pl.BlockSpec((1, m, d), lambda i: (i, 0, 0))
    oblk = pl.BlockSpec((1, m, ATT_HEADS, ATT_HEAD_DIM), lambda i: (i, 0, 0, 0))
    hblk = pl.BlockSpec((1, ATT_HEADS, m, ATT_HEAD_DIM), lambda i: (i, 0, 0, 0))
    return pl.pallas_call(
        _memkv_kernel,
        grid=(b,),
        in_specs=[blk, full((1, d)), full((d, d)), full((d, d))],
        out_specs=[oblk, oblk, hblk, hblk],
        out_shape=[jax.ShapeDtypeStruct((b, m, ATT_HEADS, ATT_HEAD_DIM), F32)] * 2
        + [jax.ShapeDtypeStruct((b, ATT_HEADS, m, ATT_HEAD_DIM), BF16)] * 2,
        compiler_params=pltpu.CompilerParams(dimension_semantics=("arbitrary",), vmem_limit_bytes=VMEM_LIMIT),
        name="memkv",
    )(mem, nw, wk, wv)


INPROJ_COL_CHUNK = 1024


def _inproj_plan():
    plan = []
    for piece in sorted(W_SRC):
        idx, c0 = W_SRC[piece]
        for lo in range(0, piece[1] - piece[0], INPROJ_COL_CHUNK):
            plan.append((idx, (c0 + lo) // INPROJ_COL_CHUNK, piece == COL_GATES))
    return plan


def _inproj_kernel(plan, x_ref, nw_ref, wa_ref, wb_ref, wdt_ref, main_ref, dt_ref, h_scr):
    c = pl.program_id(0)

    @pl.when(c == 0)
    def _():
        h = _rms(x_ref[...], nw_ref[...]).astype(BF16)
        h_scr[...] = h
        dt_ref[...] = _dot(h, wdt_ref[...])

    def emit(w_ref, gate):
        def run():
            val = _dot(h_scr[...], w_ref[...])
            if gate:
                val = _sigmoid(val)
            main_ref[...] = val.astype(BF16)
        return run

    for idx, gate in ((0, False), (1, False), (1, True)):
        steps = [i for i, (pi, _, pg) in enumerate(plan) if (pi, pg) == (idx, gate)]
        cond = functools.reduce(jnp.logical_or, [c == i for i in steps])
        pl.when(cond)(emit((wa_ref, wb_ref)[idx], gate))


def _inproj(x2d, nw, w_a, w_b, w_dt):
    m = x2d.shape[0]
    plan = _inproj_plan()

    def block_of(idx):
        tbl, nxt = [0] * len(plan), None
        for i in reversed(range(len(plan))):
            if plan[i][0] == idx:
                nxt = plan[i][1]
            tbl[i] = nxt
        last = next(b for b in reversed(tbl) if b is not None)
        tbl = [last if b is None else b for b in tbl]

        def index_map(c):
            blk = tbl[0]
            for i in range(1, len(tbl)):
                blk = jnp.where(c >= i, tbl[i], blk)
            return (0, blk)
        return index_map

    const = lambda a: pl.BlockSpec(a.shape, lambda c: (0,) * a.ndim)
    return pl.pallas_call(
        functools.partial(_inproj_kernel, plan),
        grid=(len(plan),),
        in_specs=[const(x2d), const(nw),
                  pl.BlockSpec((D_MODEL, INPROJ_COL_CHUNK), block_of(0)),
                  pl.BlockSpec((D_MODEL, INPROJ_COL_CHUNK), block_of(1)),
                  const(w_dt)],
        out_specs=[pl.BlockSpec((m, INPROJ_COL_CHUNK), lambda c: (0, c)),
                   pl.BlockSpec((m, LANES), lambda c: (0, 0))],
        out_shape=[jax.ShapeDtypeStruct((m, MAIN_COLS), BF16), jax.ShapeDtypeStruct((m, LANES), F32)],
        scratch_shapes=[pltpu.VMEM((m, D_MODEL), BF16)],
        compiler_params=pltpu.CompilerParams(dimension_semantics=("arbitrary",), vmem_limit_bytes=VMEM_LIMIT),
        name="inproj",
    )(x2d, nw, w_a, w_b, w_dt)


def _pool_branch(pext, u, pos, wgrp_ref, pscale_ref, zp):
    nb, t, _ = u.shape
    ys = []
    for g, w in enumerate(POOL_WINDOWS):
        cols = slice(g * POOL_GROUP, (g + 1) * POOL_GROUP)
        win = pext[:, :, cols]
        for sh in [1 << e for e in range(g + 1)]:
            win = win + pltpu.roll(win, sh, axis=1)
        win = win[:, POOL_PAD:, :]
        inv_cnt = 1.0 / jnp.minimum(w, pos + 1).astype(F32)
        d = (win * inv_cnt - u[:, :, cols]).astype(BF16).reshape(nb * t, POOL_GROUP)
        ys.append(_dot(d, wgrp_ref[g]))
    return jnp.concatenate(ys, axis=1) * pscale_ref[...] * _silu(zp)


def _conv_branch(cext, convw_ref, convb_ref, store):
    for cc in range(CONV_DIM // GROUP_WIDTH):
        csl = slice(cc * GROUP_WIDTH, (cc + 1) * GROUP_WIDTH)
        ext = cext[:, :, csl]
        conv = convb_ref[:, csl].reshape(1, 1, GROUP_WIDTH)
        for kk in range(CONV_WIDTH):
            tap = ext if kk == CONV_WIDTH - 1 else pltpu.roll(ext, CONV_WIDTH - 1 - kk, axis=1)
            conv = conv + tap * convw_ref[kk:kk + 1, csl].reshape(1, 1, GROUP_WIDTH)
        store(cc, _silu(conv[:, CONV_PAD:, :]))


PROMPT_TILE = 256


def _seq_prompt_kernel(ntile, x_ref, nw_ref, wa_ref, wb_ref, wdt_ref, k_ref, v_ref,
                       wgrp_ref, pscale_ref, convw_ref, convb_ref, dtb_ref, alog_ref, dexp_ref, ssdnw_ref,
                       gates_ref, ypool_ref, yssd_ref, yatt_ref, pool_o_ref, conv_o_ref, ssm_o_ref,
                       pext, cext, xs_scr, b_scr, c_scr, dt_scr, y_scr, h_scr, zs_scr, q_scr, za_scr):
    t, q = PROMPT_TILE, SSD_CHUNK
    k = pl.program_id(0)
    s = k % ntile
    last = ntile - 1

    @pl.when(k == 0)
    def _():
        y_scr[...] = jnp.zeros(y_scr.shape, F32)
        zs_scr[...] = jnp.zeros(zs_scr.shape, BF16)
        q_scr[...] = jnp.zeros(q_scr.shape, BF16)
        za_scr[...] = jnp.zeros(za_scr.shape, BF16)

    @pl.when(s == 0)
    def _():
        pext[:, 0:POOL_PAD, :] = jnp.zeros((1, POOL_PAD, D_MODEL), F32)
        cext[:, 0:CONV_PAD, :] = jnp.zeros((1, CONV_PAD, CONV_DIM), F32)
        h_scr[...] = jnp.zeros(h_scr.shape, F32)

    @pl.when(s > 0)
    def _():
        carry_p = pext[:, t:t + POOL_PAD, :]
        carry_c = cext[:, t:t + CONV_PAD, :]
        pext[:, 0:POOL_PAD, :] = carry_p
        cext[:, 0:CONV_PAD, :] = carry_c

    yz = y_scr[...] * zs_scr[...].astype(F32)
    yssd_ref[...] = _rms(yz, ssdnw_ref[...]).astype(BF16)
    scale = ATT_HEAD_DIM ** -0.5
    outs = []
    for hd in range(ATT_HEADS):
        hsl = slice(hd * ATT_HEAD_DIM, (hd + 1) * ATT_HEAD_DIM)
        p = _softmax_rows(lax.dot_general(q_scr[:, hsl], k_ref[0, hd], _NT, preferred_element_type=F32) * scale)
        outs.append(_dot(p.astype(BF16), v_ref[0, hd]))
    yatt_ref[...] = (jnp.concatenate(outs, axis=1) * za_scr[...].astype(F32)).astype(BF16)

    hn = _rms(x_ref[...], nw_ref[...]).astype(BF16)

    def proj(piece, lo=0, hi=None):
        return _dot(hn, _weight_cols((wa_ref, wb_ref), piece, lo, piece[1] - piece[0] if hi is None else hi))

    def gates_piece(c):
        def run():
            gates_ref[:, c * D_MODEL:(c + 1) * D_MODEL] = _sigmoid(
                proj(COL_GATES, c * D_MODEL, (c + 1) * D_MODEL).astype(BF16))
        return run

    def zs_piece(c):
        def run():
            zs_scr[:, c * D_MODEL:(c + 1) * D_MODEL] = _silu(proj(COL_ZS, c * D_MODEL, (c + 1) * D_MODEL).astype(BF16))
        return run

    def q_piece():
        q_scr[...] = proj(COL_Q).astype(BF16)

    def za_piece():
        za_scr[...] = _silu(proj(COL_ZA).astype(BF16))

    fillers = [gates_piece(0), gates_piece(1), gates_piece(2), zs_piece(0), zs_piece(1), q_piece, za_piece]

    def run_filler():
        if fillers:
            fillers.pop(0)()

    u = proj(COL_U).reshape(1, t, D_MODEL)
    pext[:, POOL_PAD:, :] = u
    pos = s * t + lax.broadcasted_iota(jnp.int32, (1, t, 1), 1)
    ypool_ref[...] = _pool_branch(pext, u, pos, wgrp_ref, pscale_ref, proj(COL_ZP)).astype(BF16)

    for c in range(CONV_DIM // D_MODEL):
        cext[:, CONV_PAD:, c * D_MODEL:(c + 1) * D_MODEL] = proj(
            COL_XBC, c * D_MODEL, (c + 1) * D_MODEL).reshape(1, t, D_MODEL)
    dt_scr[...] = _softplus(_dot(hn, wdt_ref[...]) + dtb_ref[...])

    def store_conv(cc, val):
        if cc < SSD_GROUPS:
            xs_scr[:, cc * GROUP_WIDTH:(cc + 1) * GROUP_WIDTH] = val[0]
        elif cc == SSD_GROUPS:
            b_scr[...] = val[0]
        else:
            c_scr[...] = val[0]
        run_filler()

    _conv_branch(cext, convw_ref, convb_ref, store_conv)

    a_neg = -jnp.exp(alog_ref[...])
    rq = lax.broadcasted_iota(jnp.int32, (q, q), 0)
    cq = lax.broadcasted_iota(jnp.int32, (q, q), 1)
    tril = rq >= cq
    tri_f = tril.astype(F32)
    lane_lo = lax.broadcasted_iota(jnp.int32, (1, LANES), 1) < SSD_HEAD_DIM
    pairs_per_group = SSD_HEADS // SSD_GROUPS // 2

    def chunk(c):
        rsl = slice(c * q, (c + 1) * q)
        dtc = dt_scr[rsl, :]
        acs = jnp.dot(tri_f, dtc * a_neg, precision=lax.Precision.HIGHEST, preferred_element_type=F32)
        acs_t = acs.T
        dt_t = dtc.T
        wdec_t = (dt_t * jnp.exp(acs_t[:, q - 1:q] - acs_t)).astype(BF16)
        row_t = acs_t - jnp.log(dt_t)
        cdec = jnp.exp(acs[q - 1:q, :])
        for g in range(SSD_GROUPS):
            gsl = slice(g * SSD_STATE, (g + 1) * SSD_STATE)
            bg = b_scr[rsl, gsl]
            cg_b = c_scr[rsl, gsl].astype(BF16)
            cb = lax.dot_general(cg_b, bg.astype(BF16), _NT, preferred_element_type=F32).astype(BF16)
            bg_t = bg.T.astype(BF16)
            hsl = slice(g * GROUP_WIDTH, (g + 1) * GROUP_WIDTH)
            z_g = _dot(cg_b, h_scr[:, hsl].astype(BF16))
            for jp in range(pairs_per_group):
                j = g * pairs_per_group + jp
                lsl = slice(j * LANES, (j + 1) * LANES)
                xp = xs_scr[rsl, lsl]
                xp_b = xp.astype(BF16)
                zero_b = jnp.zeros_like(xp_b)
                x_bd = jnp.concatenate([jnp.where(lane_lo, xp_b, zero_b), jnp.where(lane_lo, zero_b, xp_b)],
                                       axis=0)
                ms, bws, cols = [], [], []
                for hh in range(2):
                    r = 2 * j + hh
                    cols.append(jnp.broadcast_to(acs[:, r:r + 1], (q, q)))
                    seg = cols[hh] - row_t[r:r + 1, :]
                    ms.append(cb * jnp.exp(jnp.where(tril, seg, NEG_BIG)).astype(BF16))
                    bws.append(bg_t * wdec_t[r:r + 1, :])
                ea_pair = jnp.exp(jnp.where(lane_lo, cols[0], cols[1]))
                y = (_dot(jnp.concatenate(ms, axis=1), x_bd)
                     + ea_pair * z_g[:, jp * LANES:(jp + 1) * LANES] + dexp_ref[:, lsl] * xp)
                y_scr[rsl, lsl] = y
                cd_pair = jnp.where(lane_lo, cdec[:, 2 * j:2 * j + 1], cdec[:, 2 * j + 1:2 * j + 2])
                h_scr[:, lsl] = h_scr[:, lsl] * cd_pair + _dot(jnp.concatenate(bws, axis=1), x_bd)
            run_filler()

    for c in range(t // q):
        chunk(c)
    while fillers:
        run_filler()

    @pl.when(jnp.logical_and(s == last, k < pl.num_programs(0) - 1))
    def _():
        pool_o_ref[...] = pext[:, t + POOL_PAD - POOL_HIST:t + POOL_PAD, :]
        conv_o_ref[...] = cext[:, t + CONV_PAD - (CONV_WIDTH - 1):t + CONV_PAD, :]
        for j in range(SSD_HEADS // 2):
            lsl = slice(j * LANES, (j + 1) * LANES)
            ssm_o_ref[0, lsl, :] = h_scr[:, lsl].T


def _seq_prompt(x2d, nw, w_a, w_b, w_dt, kb, vb, params, *, nseq, ntile):
    t = PROMPT_TILE
    m = x2d.shape[0]
    ntiles = nseq * ntile
    tile = lambda k: jnp.minimum(k, ntiles - 1)
    closed = lambda k: jnp.maximum(k - 1, 0)
    own_rows = lambda width: pl.BlockSpec((t, width), lambda k: (k, 0))
    closed_rows = lambda width: pl.BlockSpec((t, width), lambda k: (closed(k), 0))
    seq_spec = lambda shape, which: pl.BlockSpec((1,) + shape, lambda k: (which(k) // ntile,) + (0,) * len(shape))
    const_spec = lambda a: pl.BlockSpec(a.shape, lambda k: (0,) * a.ndim)
    resident = lambda a: pl.BlockSpec(a.shape, lambda k: (0,) * a.ndim, pipeline_mode=pl.Buffered(1))
    in_specs = [
        pl.BlockSpec((t, D_MODEL), lambda k: (tile(k), 0)),
        resident(nw),
        pl.BlockSpec((D_MODEL, W_SPLIT[0]), lambda k: (0, 0), pipeline_mode=pl.Buffered(1)),
        resident(w_b), resident(w_dt),
        seq_spec((ATT_HEADS, MEM_LEN, ATT_HEAD_DIM), closed),
        seq_spec((ATT_HEADS, MEM_LEN, ATT_HEAD_DIM), closed),
    ] + [const_spec(p) for p in params]
    out_specs = [
        own_rows(3 * D_MODEL), own_rows(D_MODEL), closed_rows(SSD_WIDTH), closed_rows(D_MODEL),
        seq_spec((POOL_HIST, D_MODEL), tile),
        seq_spec((CONV_WIDTH - 1, CONV_DIM), tile),
        seq_spec((SSD_WIDTH, SSD_STATE), tile),
    ]
    out_shape = [
        jax.ShapeDtypeStruct((m + t, 3 * D_MODEL), BF16),
        jax.ShapeDtypeStruct((m + t, D_MODEL), BF16),
        jax.ShapeDtypeStruct((m, SSD_WIDTH), BF16),
        jax.ShapeDtypeStruct((m, D_MODEL), BF16),
        jax.ShapeDtypeStruct((nseq, POOL_HIST, D_MODEL), F32),
        jax.ShapeDtypeStruct((nseq, CONV_WIDTH - 1, CONV_DIM), F32),
        jax.ShapeDtypeStruct((nseq, SSD_WIDTH, SSD_STATE), F32),
    ]
    scratch = [
        pltpu.VMEM((1, POOL_PAD + t, D_MODEL), F32),
        pltpu.VMEM((1, CONV_PAD + t, CONV_DIM), F32),
        pltpu.VMEM((t, SSD_WIDTH), F32),
        pltpu.VMEM((t, GROUP_WIDTH), F32),
        pltpu.VMEM((t, GROUP_WIDTH), F32),
        pltpu.VMEM((t, LANES), F32),
        pltpu.VMEM((t, SSD_WIDTH), F32),
        pltpu.VMEM((SSD_STATE, SSD_WIDTH), F32),
        pltpu.VMEM((t, SSD_WIDTH), BF16),
        pltpu.VMEM((t, D_MODEL), BF16),
        pltpu.VMEM((t, D_MODEL), BF16),
    ]
    return pl.pallas_call(
        functools.partial(_seq_prompt_kernel, ntile),
        grid=(ntiles + 1,),
        in_specs=in_specs,
        out_specs=out_specs,
        out_shape=out_shape,
        scratch_shapes=scratch,
        compiler_params=pltpu.CompilerParams(dimension_semantics=("arbitrary",),
                                             vmem_limit_bytes=SEQ_PROMPT_VMEM_LIMIT),
        name="seq_prompt",
    )(x2d, nw, w_a, w_b, w_dt, kb, vb, *params)


SAMPLE_BLOCK = 32
STATE_PRE_SUB = 4


def _state_pre_kernel(u_ref, zp_ref, xbc_ref, dt_ref, ph_ref, ch_ref,
                      wgrp_ref, pscale_ref, convw_ref, convb_ref, dtb_ref, alog_ref,
                      ypool_ref, xs_ref, bc_ref, dts_ref, cd_ref, pool_o_ref, conv_o_ref, pext, cext):
    nb, t = SAMPLE_BLOCK, SUBLANES
    pext[:, 0:1, :] = jnp.zeros((nb, 1, D_MODEL), F32)
    pext[:, 1:POOL_PAD, :] = ph_ref[...]
    cext[:, 0:CONV_PAD - (CONV_WIDTH - 1), :] = jnp.zeros((nb, CONV_PAD - (CONV_WIDTH - 1), CONV_DIM), F32)
    cext[:, CONV_PAD - (CONV_WIDTH - 1):CONV_PAD, :] = ch_ref[...]

    pext[:, POOL_PAD:, :] = u_ref[...].astype(F32).reshape(nb, t, D_MODEL)
    cext[:, CONV_PAD:, :] = xbc_ref[...].astype(F32).reshape(nb, t, CONV_DIM)
    pos = PAST_LEN + lax.broadcasted_iota(jnp.int32, (1, t, 1), 1)
    for sb in range(nb // STATE_PRE_SUB):
        seqs = slice(sb * STATE_PRE_SUB, (sb + 1) * STATE_PRE_SUB)
        rsl = slice(sb * STATE_PRE_SUB * t, (sb + 1) * STATE_PRE_SUB * t)
        pv, cv = pext.at[seqs], cext.at[seqs]
        ypool_ref[rsl, :] = _pool_branch(pv, pv[:, POOL_PAD:, :], pos, wgrp_ref, pscale_ref,
                                         zp_ref[rsl, :].astype(F32)).astype(BF16)

        def store_conv(cc, val, rsl=rsl):
            val = val.reshape(STATE_PRE_SUB * t, GROUP_WIDTH)
            if cc < SSD_GROUPS:
                xs_ref[rsl, cc * GROUP_WIDTH:(cc + 1) * GROUP_WIDTH] = val
            else:
                bc_ref[rsl, (cc - SSD_GROUPS) * GROUP_WIDTH:(cc - SSD_GROUPS + 1) * GROUP_WIDTH] = val

        _conv_branch(cv, convw_ref, convb_ref, store_conv)
    dt = _softplus(dt_ref[...] + dtb_ref[...])
    dts_ref[...] = dt
    a = dt * -jnp.exp(alog_ref[...])
    cd_ref[...] = jnp.exp(jnp.sum(a.reshape(nb, t, LANES), axis=1))
    pool_o_ref[...] = pext[:, t + POOL_PAD - POOL_HIST:t + POOL_PAD, :]
    conv_o_ref[...] = cext[:, t + CONV_PAD - (CONV_WIDTH - 1):t + CONV_PAD, :]


def _col_block(piece):
    return piece[0] // (piece[1] - piece[0])


def _state_pre(main, dt, state_pool, state_conv, params):
    nb, t = SAMPLE_BLOCK, SUBLANES
    rows = nb * t
    m = main.shape[0]
    nseq = m // t
    col_spec = lambda width, idx: pl.BlockSpec((rows, width), lambda i: (i, idx))
    seq_spec = lambda shape: pl.BlockSpec((nb,) + shape, lambda i: (i,) + (0,) * len(shape))
    const_spec = lambda a: pl.BlockSpec(a.shape, lambda i: (0,) * a.ndim)
    return pl.pallas_call(
        _state_pre_kernel,
        grid=(nseq // nb,),
        in_specs=[col_spec(D_MODEL, _col_block(COL_U)), col_spec(D_MODEL, _col_block(COL_ZP)),
                  col_spec(CONV_DIM, _col_block(COL_XBC)), col_spec(LANES, 0),
                  seq_spec((POOL_HIST, D_MODEL)), seq_spec((CONV_WIDTH - 1, CONV_DIM))]
        + [const_spec(p) for p in params],
        out_specs=[col_spec(D_MODEL, 0), col_spec(SSD_WIDTH, 0), col_spec(2 * GROUP_WIDTH, 0), col_spec(LANES, 0),
                   pl.BlockSpec((nb, LANES), lambda i: (i, 0)),
                   seq_spec((POOL_HIST, D_MODEL)), seq_spec((CONV_WIDTH - 1, CONV_DIM))],
        out_shape=[jax.ShapeDtypeStruct((m, D_MODEL), BF16), jax.ShapeDtypeStruct((m, SSD_WIDTH), F32),
                   jax.ShapeDtypeStruct((m, 2 * GROUP_WIDTH), F32), jax.ShapeDtypeStruct((m, LANES), F32),
                   jax.ShapeDtypeStruct((nseq, LANES), F32),
                   jax.ShapeDtypeStruct((nseq, POOL_HIST, D_MODEL), F32),
                   jax.ShapeDtypeStruct((nseq, CONV_WIDTH - 1, CONV_DIM), F32)],
        scratch_shapes=[pltpu.VMEM((nb, POOL_PAD + t, D_MODEL), F32), pltpu.VMEM((nb, CONV_PAD + t, CONV_DIM), F32)],
        compiler_params=pltpu.CompilerParams(dimension_semantics=("arbitrary",), vmem_limit_bytes=VMEM_LIMIT),
        name="state_pre",
    )(main, main, main, dt, state_pool, state_conv, *params)


PIPE_SLOTS = 4


def _ring_pipeline(n, start_in, wait_in, compute, start_out=None, wait_out=None):
    ns = PIPE_SLOTS
    for i in range(ns - 1):
        start_in(i, i)

    def body(bb, _):
        for k in range(ns):
            i = ns * bb + k
            nxt = i + ns - 1
            pl.when(nxt < n)(functools.partial(start_in, nxt, (k + ns - 1) % ns))
            wait_in(i, k)
            if wait_out is not None:
                pl.when(bb > 0)(functools.partial(wait_out, i - ns, k))
            compute(i, k)
            if start_out is not None:
                start_out(i, k)
        return 0

    lax.fori_loop(0, n // ns, body, 0)
    if wait_out is not None:
        for k in range(ns):
            wait_out(n - ns + k, k)


SSD_ITEM = 4


def _ssd_state_kernel(xs_ref, bc_ref, dt_ref, zs_ref, cd_ref, hin_hbm, expand_ref, segsum_ref, alog_ref, dexp_ref,
                      ssdnw_ref, yssd_ref, hout_hbm, hbuf, obuf, y_scr, sem_in, sem_out):
    nb, t = SAMPLE_BLOCK, SUBLANES
    base = pl.program_id(0) * nb
    a_neg = -jnp.exp(alog_ref[...])
    ridx = lax.broadcasted_iota(jnp.int32, (t, LANES), 0)

    def in_copies(item, slot):
        return [pltpu.make_async_copy(hin_hbm.at[base + item * SSD_ITEM + j, g * GROUP_WIDTH:(g + 1) * GROUP_WIDTH, :],
                                      hbuf.at[slot, j, g * GROUP_WIDTH:(g + 1) * GROUP_WIDTH, :],
                                      sem_in.at[slot, j * SSD_GROUPS + g])
                for j in range(SSD_ITEM) for g in range(SSD_GROUPS)]

    def out_copies(item, slot):
        return [pltpu.make_async_copy(obuf.at[slot, j, g * GROUP_WIDTH:(g + 1) * GROUP_WIDTH, :],
                                      hout_hbm.at[base + item * SSD_ITEM + j, g * GROUP_WIDTH:(g + 1) * GROUP_WIDTH, :],
                                      sem_out.at[slot, j * SSD_GROUPS + g])
                for j in range(SSD_ITEM) for g in range(SSD_GROUPS)]

    def start_all(copies):
        for c in copies:
            c.start()

    def wait_all(copies):
        for c in copies:
            c.wait()

    def per_head_factors(b):
        rsl = pl.ds(pl.multiple_of(b * t, t), t)
        dtc = dt_ref[rsl, :]
        acs = dtc * a_neg
        for sh in (1, 2, 4):
            acs = acs + jnp.where(ridx >= sh, pltpu.roll(acs, sh, axis=0), 0.0)
        tot = acs[t - 1:t, :]
        bc = bc_ref[rsl, :]
        bm, cm = bc[:, :GROUP_WIDTH], bc[:, GROUP_WIDTH:]
        bm_r, cm_r = bm.astype(BF16).astype(F32), cm.astype(BF16).astype(F32)
        gs, ps = [], []
        for k in range(t):
            gs.append(jnp.exp(jnp.where(ridx >= k, acs - acs[k:k + 1, :], NEG_BIG)) * dtc[k:k + 1, :])
            ps.append(cm_r * bm_r[k:k + 1, :])
        cb_heads = _dot(jnp.concatenate(ps, axis=0).astype(BF16), segsum_ref[...])
        per_head = jnp.concatenate([jnp.concatenate(gs, axis=0) * cb_heads, jnp.exp(acs),
                                    dtc * jnp.exp(tot - acs)], axis=0)
        return dict(rsl=rsl, b=b, bm=bm, cm=cm, per_head=per_head)

    def widen(st):
        hi = st["per_head"].astype(BF16)
        lo = (st["per_head"] - hi.astype(F32)).astype(BF16)
        st["wide"] = _dot(hi, expand_ref[...]) + _dot(lo, expand_ref[...])

    def intra(st):
        x = xs_ref[st["rsl"], :]
        wide = st["wide"]
        y = dexp_ref[...] * x
        for k in range(t):
            y = y + wide[k * t:(k + 1) * t, :] * x[k:k + 1, :]
        st["y"] = y
        st["ea_wide"] = wide[t * t:t * t + t, :]
        st["xw"] = x * wide[t * t + t:, :]

    def state_group(st, slot, j, g):
        gsl = slice(g * SSD_STATE, (g + 1) * SSD_STATE)
        wsl = slice(g * GROUP_WIDTH, (g + 1) * GROUP_WIDTH)
        hg = hbuf[slot, j, wsl, :]
        z_g = lax.dot_general(st["cm"][:, gsl].astype(BF16), hg.astype(BF16), _NT, preferred_element_type=F32)
        y_scr[st["rsl"], wsl] = st["y"][:, wsl] + st["ea_wide"][:, wsl] * z_g
        upd = lax.dot_general(st["xw"][:, wsl].astype(BF16), st["bm"][:, gsl].astype(BF16), _TN,
                              preferred_element_type=F32)
        for r8 in range(SSD_HEADS // SSD_GROUPS):
            r = g * (SSD_HEADS // SSD_GROUPS) + r8
            rows_r = slice(r * SSD_HEAD_DIM, (r + 1) * SSD_HEAD_DIM)
            obuf[slot, j, rows_r, :] = (hbuf[slot, j, rows_r, :] * cd_ref[base + st["b"], r]
                                        + upd[r8 * SSD_HEAD_DIM:(r8 + 1) * SSD_HEAD_DIM, :])

    def compute(item, slot):
        sts = [per_head_factors(item * SSD_ITEM + j) for j in range(SSD_ITEM)]
        for st in sts:
            widen(st)
        for st in sts:
            intra(st)
        for g in range(SSD_GROUPS):
            for j, st in enumerate(sts):
                state_group(st, slot, j, g)

    _ring_pipeline(nb // SSD_ITEM,
                   lambda i, slot: start_all(in_copies(i, slot)), lambda i, slot: wait_all(in_copies(i, slot)),
                   compute,
                   lambda i, slot: start_all(out_copies(i, slot)), lambda i, slot: wait_all(out_copies(i, slot)))
    yz = y_scr[...] * _silu(zs_ref[...].astype(F32))
    yssd_ref[...] = _rms(yz, ssdnw_ref[...]).astype(BF16)


def _ssd_state(xs, bc, dts, main, cd, hin, expand, segsum, alog, dexp, ssdnw):
    nb, t = SAMPLE_BLOCK, SUBLANES
    rows = nb * t
    m = xs.shape[0]
    nseq = m // t
    col_spec = lambda width, idx: pl.BlockSpec((rows, width), lambda i: (i, idx))
    const_spec = lambda a: pl.BlockSpec(a.shape, lambda i: (0,) * a.ndim)
    return pl.pallas_call(
        _ssd_state_kernel,
        grid=(nseq // nb,),
        in_specs=[col_spec(SSD_WIDTH, 0), col_spec(2 * GROUP_WIDTH, 0), col_spec(LANES, 0),
                  col_spec(SSD_WIDTH, _col_block(COL_ZS)),
                  pl.BlockSpec(memory_space=pltpu.SMEM), pl.BlockSpec(memory_space=pl.ANY),
                  const_spec(expand), const_spec(segsum), const_spec(alog), const_spec(dexp), const_spec(ssdnw)],
        out_specs=[col_spec(SSD_WIDTH, 0), pl.BlockSpec(memory_space=pl.ANY)],
        out_shape=[jax.ShapeDtypeStruct((m, SSD_WIDTH), BF16), jax.ShapeDtypeStruct(hin.shape, F32)],
        scratch_shapes=[pltpu.VMEM((PIPE_SLOTS, SSD_ITEM, SSD_WIDTH, SSD_STATE), F32),
                        pltpu.VMEM((PIPE_SLOTS, SSD_ITEM, SSD_WIDTH, SSD_STATE), F32),
                        pltpu.VMEM((rows, SSD_WIDTH), F32),
                        pltpu.SemaphoreType.DMA((PIPE_SLOTS, SSD_ITEM * SSD_GROUPS)),
                        pltpu.SemaphoreType.DMA((PIPE_SLOTS, SSD_ITEM * SSD_GROUPS))],
        compiler_params=pltpu.CompilerParams(dimension_semantics=("arbitrary",), vmem_limit_bytes=VMEM_LIMIT),
        name="ssd_state",
    )(xs, bc, dts, main, cd, hin, expand, segsum, alog, dexp, ssdnw)


ATT_ITEM = 4
ATT_BLOCK = 64


def _att_state_kernel(q_ref, za_ref, k_hbm, v_hbm, yatt_ref, kbuf, vbuf, q_scr, att_scr, sem):
    nb, t = ATT_BLOCK, SUBLANES
    base = pl.program_id(0) * nb
    scale = ATT_HEAD_DIM ** -0.5
    q_scr[...] = q_ref[...].astype(F32)
    head_of_lane = lax.broadcasted_iota(jnp.int32, (1, D_MODEL), 1) // ATT_HEAD_DIM

    def copies(item, slot):
        out = []
        for j in range(ATT_ITEM):
            for kv, (src, buf) in enumerate(((k_hbm, kbuf), (v_hbm, vbuf))):
                for hd in range(ATT_HEADS):
                    hsl = slice(hd * ATT_HEAD_DIM, (hd + 1) * ATT_HEAD_DIM)
                    out.append(pltpu.make_async_copy(
                        src.at[base + item * ATT_ITEM + j, :, hd, :], buf.at[slot, j, :, hsl],
                        sem.at[slot, (j * 2 + kv) * ATT_HEADS + hd]))
        return out

    def start_in(item, slot):
        for c in copies(item, slot):
            c.start()

    def wait_in(item, slot):
        for c in copies(item, slot):
            c.wait()

    def compute(item, slot):
        seqs = range(ATT_ITEM)
        rsl = [pl.ds(pl.multiple_of((item * ATT_ITEM + j) * t, t), t) for j in seqs]
        scs = []
        for j in seqs:
            qf = q_scr[rsl[j], :]
            q_bd = jnp.concatenate([jnp.where(head_of_lane == hd, qf, 0.0) for hd in range(ATT_HEADS)],
                                   axis=0).astype(BF16)
            scs.append(lax.dot_general(q_bd, kbuf[slot, j].astype(BF16), _NT, preferred_element_type=F32) * scale)
        ps = [_softmax_rows(sc).astype(BF16) for sc in scs]
        outs = [_dot(ps[j], vbuf[slot, j].astype(BF16)) for j in seqs]
        for j in seqs:
            att_scr[rsl[j], :] = jnp.concatenate(
                [outs[j][hd * t:(hd + 1) * t, hd * ATT_HEAD_DIM:(hd + 1) * ATT_HEAD_DIM] for hd in range(ATT_HEADS)],
                axis=1)

    _ring_pipeline(nb // ATT_ITEM, start_in, wait_in, compute)
    yatt_ref[...] = (att_scr[...] * _silu(za_ref[...].astype(F32))).astype(BF16)


def _att_state(main, k, v):
    nb, t = ATT_BLOCK, SUBLANES
    rows = nb * t
    m = main.shape[0]
    col_spec = lambda width, idx: pl.BlockSpec((rows, width), lambda i: (i, idx))
    kv_buf = pltpu.VMEM((PIPE_SLOTS, ATT_ITEM, MEM_LEN, D_MODEL), F32)
    return pl.pallas_call(
        _att_state_kernel,
        grid=(m // rows,),
        in_specs=[col_spec(D_MODEL, _col_block(COL_Q)), col_spec(D_MODEL, _col_block(COL_ZA)),
                  pl.BlockSpec(memory_space=pl.ANY), pl.BlockSpec(memory_space=pl.ANY)],
        out_specs=col_spec(D_MODEL, 0),
        out_shape=jax.ShapeDtypeStruct((m, D_MODEL), BF16),
        scratch_shapes=[kv_buf, kv_buf, pltpu.VMEM((rows, D_MODEL), F32), pltpu.VMEM((rows, D_MODEL), F32),
                        pltpu.SemaphoreType.DMA((PIPE_SLOTS, ATT_ITEM * 2 * ATT_HEADS))],
        compiler_params=pltpu.CompilerParams(dimension_semantics=("arbitrary",), vmem_limit_bytes=VMEM_LIMIT),
        name="att_state",
    )(main, main, k, v)


DENSE_ROWS = 512


def _dense_kernel(x_ref, gt_ref, yp_ref, ys_ref, ya_ref, wpo_ref, wso_ref, wao_ref, wo_ref, fnw_ref, y_ref):
    gates = gt_ref[...].astype(F32)
    merged = (gates[:, 0:D_MODEL] * _dot(yp_ref[...], wpo_ref[...])
              + gates[:, D_MODEL:2 * D_MODEL] * _dot(ys_ref[...], wso_ref[...])
              + gates[:, 2 * D_MODEL:] * _dot(ya_ref[...], wao_ref[...]))
    x_out = x_ref[...] + _dot(merged.astype(BF16), wo_ref[...])
    y_ref[...] = _rms(x_out, fnw_ref[...])


def _dense(x2d, gates, gate_idx, yp, ys, ya, wpo, wso, wao, wo, fnw):
    m = x2d.shape[0]
    row = lambda width, idx=0: pl.BlockSpec((DENSE_ROWS, width), lambda i: (i, idx))
    resident = lambda a: pl.BlockSpec(a.shape, lambda i: (0,) * a.ndim, pipeline_mode=pl.Buffered(1))
    return pl.pallas_call(
        _dense_kernel,
        grid=(m // DENSE_ROWS,),
        in_specs=[row(D_MODEL), row(3 * D_MODEL, gate_idx), row(D_MODEL), row(SSD_WIDTH), row(D_MODEL),
                  resident(wpo), resident(wso), resident(wao), resident(wo), resident(fnw)],
        out_specs=row(D_MODEL),
        out_shape=jax.ShapeDtypeStruct((m, D_MODEL), F32),
        compiler_params=pltpu.CompilerParams(dimension_semantics=("arbitrary",), vmem_limit_bytes=VMEM_LIMIT),
        name="dense",
    )(x2d, gates, yp, ys, ya, wpo, wso, wao, wo, fnw)


def kernel(x_prompt, x_sample, mem_prompt, state_pool, state_conv, state_ssm, cache_mem_k, cache_mem_v,
           norm_w, w_in, w_pool_grp, pool_scale, conv_w, conv_b, dt_bias, a_log, d_skip, ssd_norm_w,
           mem_norm_w, w_mem_k, w_mem_v, w_pool_out, w_ssd_out, w_att_out, w_out, final_norm_w):
    assert w_in.shape[0] == 1
    bp, sp, d = x_prompt.shape
    bs, ss, _ = x_sample.shape
    assert ss == SUBLANES and sp % PROMPT_TILE == 0 and bs % SAMPLE_BLOCK == 0 and bs % ATT_BLOCK == 0

    w_a = w_in[0].astype(BF16)
    w_b = w_a[:, W_SPLIT[1]:]
    w_dt = jnp.pad(w_a[:, W_SPLIT[0]:W_SPLIT[1]], ((0, 0), (0, LANES - SSD_HEADS)))
    nw = norm_w[0].reshape(1, d)
    pad_heads = lambda a: jnp.pad(a.reshape(1, SSD_HEADS), ((0, 0), (0, LANES - SSD_HEADS)))
    wgrp = w_pool_grp[0].astype(BF16)
    pscale = pool_scale[0].reshape(1, d)
    convb = conv_b[0].reshape(1, CONV_DIM)
    dtb, alog = pad_heads(dt_bias[0]), pad_heads(a_log[0])
    dexp = jnp.repeat(d_skip[0], SSD_HEAD_DIM).reshape(1, SSD_WIDTH)
    ssdnw = ssd_norm_w[0].reshape(1, SSD_WIDTH)
    dense_w = (w_pool_out[0].astype(BF16), w_ssd_out[0].astype(BF16), w_att_out[0].astype(BF16),
               w_out[0].astype(BF16), final_norm_w.reshape(1, d))
    head_of_lane = jnp.arange(SSD_WIDTH) // SSD_HEAD_DIM
    expand = (jnp.arange(LANES)[:, None] == head_of_lane[None, :]).astype(BF16)
    group_of_head = jnp.where(jnp.arange(LANES) < SSD_HEADS, jnp.arange(LANES) // (SSD_HEADS // SSD_GROUPS), -1)
    segsum = ((jnp.arange(GROUP_WIDTH) // SSD_STATE)[:, None] == group_of_head[None, :]).astype(BF16)

    mk, mv, mkb, mvb = _memkv(mem_prompt, mem_norm_w[0].reshape(1, d), w_mem_k[0].astype(BF16),
                              w_mem_v[0].astype(BF16))
    xp2 = x_prompt.reshape(bp * sp, d)
    gates_p, yp, ysd, ya, pool_p, conv_p, ssm_p = _seq_prompt(
        xp2, nw, w_a, w_b, w_dt, mkb, mvb, (wgrp, pscale, conv_w[0], convb, dtb, alog, dexp, ssdnw),
        nseq=bp, ntile=sp // PROMPT_TILE)
    y_prompt = _dense(xp2, gates_p, 0, yp, ysd, ya, *dense_w).reshape(bp, sp, d)

    xs2 = x_sample.reshape(bs * ss, d)
    main_s, dt_s = _inproj(xs2, nw, w_a, w_b, w_dt)
    yp, xs, bc, dts, cd, pool_s, conv_s = _state_pre(
        main_s, dt_s, state_pool[0], state_conv[0], (wgrp, pscale, conv_w[0], convb, dtb, alog))
    ysd, ssm_s = _ssd_state(xs, bc, dts, main_s, cd, state_ssm[0].reshape(bs, SSD_WIDTH, SSD_STATE),
                            expand, segsum, alog, dexp, ssdnw)
    ya = _att_state(main_s, cache_mem_k[0], cache_mem_v[0])
    y_sample = _dense(xs2, main_s, _col_block(COL_GATES), yp, ysd, ya, *dense_w).reshape(bs, ss, d)

    ssm_shape = (SSD_GROUPS, SSD_HEADS // SSD_GROUPS, SSD_HEAD_DIM, SSD_STATE)
    return (y_prompt, y_sample,
            pool_p[None], conv_p[None], ssm_p.reshape((1, bp) + ssm_shape),
            mk[None], mv[None],
            pool_s[None], conv_s[None], ssm_s.reshape((1, bs) + ssm_shape))
```

```python
import functools

import jax
import jax.numpy as jnp
from jax import lax
from jax.experimental import pallas as pl
from jax.experimental.pallas import tpu as pltpu

F32 = jnp.float32
BF16 = jnp.bfloat16

D_MODEL = 1024
POOL_WINDOWS = (2, 4, 8, 16)
POOL_GROUP = 256
POOL_HIST = 15
POOL_PAD = 16
SSD_WIDTH = 2048
SSD_HEADS = 32
SSD_HEAD_DIM = 64
SSD_GROUPS = 4
SSD_STATE = 128
GROUP_WIDTH = SSD_WIDTH // SSD_GROUPS
CONV_WIDTH = 4
CONV_DIM = 3072
CONV_PAD = 8
SSD_CHUNK = 128
MEM_LEN = 256
ATT_HEADS = 4
ATT_HEAD_DIM = 256
PAST_LEN = 16384
EPS = 1e-6
NEG_BIG = -1e30
SUBLANES = 8
LANES = 128
MAIN_COLS = 12288
COL_XBC, COL_GATES, COL_ZS = (0, 3072), (3072, 6144), (6144, 8192)
COL_U, COL_ZP, COL_Q, COL_ZA = (8192, 9216), (9216, 10240), (10240, 11264), (11264, 12288)
W_SPLIT = (7168, 7200)
W_SRC = {COL_U: (0, 0), COL_ZP: (0, 1024), COL_ZS: (0, 2048), COL_XBC: (0, 4096),
         COL_Q: (1, 0), COL_ZA: (1, 1024), COL_GATES: (1, 2048)}
VMEM_LIMIT = 56 * 1024 * 1024
SEQ_PROMPT_VMEM_LIMIT = 60 * 1024 * 1024

_NT = (((1,), (1,)), ((), ()))
_TN = (((0,), (0,)), ((), ()))


def _sigmoid(x):
    return 1.0 / (1.0 + jnp.exp(-x))


def _silu(x):
    return x * _sigmoid(x)


def _softplus(x):
    return jnp.maximum(x, 0.0) + jnp.log1p(jnp.exp(-jnp.abs(x)))


def _rms(x, w):
    return x * lax.rsqrt(jnp.mean(x * x, axis=-1, keepdims=True) + EPS) * w


def _dot(a, b):
    return jnp.dot(a, b, preferred_element_type=F32)


def _softmax_rows(sc):
    e = jnp.exp(sc - jnp.max(sc, axis=-1, keepdims=True))
    return e / jnp.sum(e, axis=-1, keepdims=True)


def _weight_cols(w_refs, piece, lo, hi):
    idx, c0 = W_SRC[piece]
    return w_refs[idx][:, c0 + lo:c0 + hi]


def _memkv_kernel(mem_ref, nw_ref, wk_ref, wv_ref, k_ref, v_ref, kb_ref, vb_ref):
    mh = _rms(mem_ref[0], nw_ref[...]).astype(BF16)
    k = _dot(mh, wk_ref[...])
    v = _dot(mh, wv_ref[...])
    for hd in range(ATT_HEADS):
        hsl = slice(hd * ATT_HEAD_DIM, (hd + 1) * ATT_HEAD_DIM)
        k_ref[0, :, hd, :] = k[:, hsl]
        v_ref[0, :, hd, :] = v[:, hsl]
        kb_ref[0, hd] = k[:, hsl].astype(BF16)
        vb_ref[0, hd] = v[:, hsl].astype(BF16)


def _memkv(mem, nw, wk, wv):
    b, m, d = mem.shape
    full = lambda shape: pl.BlockSpec(shape, lambda i: (0,) * len(shape))
    blk = pl.BlockSpec((1, m, d), lambda i: (i, 0, 0))
    oblk = pl.BlockSpec((1, m, ATT_HEADS, ATT_HEAD_DIM), lambda i: (i, 0, 0, 0))
    hblk = pl.BlockSpec((1, ATT_HEADS, m, ATT_HEAD_DIM), lambda i: (i, 0, 0, 0))
    return pl.pallas_call(
        _memkv_kernel,
        grid=(b,),
        in_specs=[blk, full((1, d)), full((d, d)), full((d, d))],
        out_specs=[oblk, oblk, hblk, hblk],
        out_shape=[jax.ShapeDtypeStruct((b, m, ATT_HEADS, ATT_HEAD_DIM), F32)] * 2
        + [jax.ShapeDtypeStruct((b, ATT_HEADS, m, ATT_HEAD_DIM), BF16)] * 2,
        compiler_params=pltpu.CompilerParams(dimension_semantics=("arbitrary",), vmem_limit_bytes=VMEM_LIMIT),
        name="memkv",
    )(mem, nw, wk, wv)


INPROJ_COL_CHUNK = 1024


def _inproj_plan():
    plan = []
    for piece in sorted(W_SRC):
        idx, c0 = W_SRC[piece]
        for lo in range(0, piece[1] - piece[0], INPROJ_COL_CHUNK):
            plan.append((idx, (c0 + lo) // INPROJ_COL_CHUNK, piece == COL_GATES))
    return plan


def _inproj_kernel(plan, x_ref, nw_ref, wa_ref, wb_ref, wdt_ref, main_ref, dt_ref, h_scr):
    c = pl.program_id(0)

    @pl.when(c == 0)
    def _():
        h = _rms(x_ref[...], nw_ref[...]).astype(BF16)
        h_scr[...] = h
        dt_ref[...] = _dot(h, wdt_ref[...])

    def emit(w_ref, gate):
        def run():
            val = _dot(h_scr[...], w_ref[...])
            if gate:
                val = _sigmoid(val)
            main_ref[...] = val.astype(BF16)
        return run

    for idx, gate in ((0, False), (1, False), (1, True)):
        steps = [i for i, (pi, _, pg) in enumerate(plan) if (pi, pg) == (idx, gate)]
        cond = functools.reduce(jnp.logical_or, [c == i for i in steps])
        pl.when(cond)(emit((wa_ref, wb_ref)[idx], gate))


def _inproj(x2d, nw, w_a, w_b, w_dt):
    m = x2d.shape[0]
    plan = _inproj_plan()

    def block_of(idx):
        tbl, nxt = [0] * len(plan), None
        for i in reversed(range(len(plan))):
            if plan[i][0] == idx:
                nxt = plan[i][1]
            tbl[i] = nxt
        last = next(b for b in reversed(tbl) if b is not None)
        tbl = [last if b is None else b for b in tbl]

        def index_map(c):
            blk = tbl[0]
            for i in range(1, len(tbl)):
                blk = jnp.where(c >= i, tbl[i], blk)
            return (0, blk)
        return index_map

    const = lambda a: pl.BlockSpec(a.shape, lambda c: (0,) * a.ndim)
    return pl.pallas_call(
        functools.partial(_inproj_kernel, plan),
        grid=(len(plan),),
        in_specs=[const(x2d), const(nw),
                  pl.BlockSpec((D_MODEL, INPROJ_COL_CHUNK), block_of(0)),
                  pl.BlockSpec((D_MODEL, INPROJ_COL_CHUNK), block_of(1)),
                  const(w_dt)],
        out_specs=[pl.BlockSpec((m, INPROJ_COL_CHUNK), lambda c: (0, c)),
                   pl.BlockSpec((m, LANES), lambda c: (0, 0))],
        out_shape=[jax.ShapeDtypeStruct((m, MAIN_COLS), BF16), jax.ShapeDtypeStruct((m, LANES), F32)],
        scratch_shapes=[pltpu.VMEM((m, D_MODEL), BF16)],
        compiler_params=pltpu.CompilerParams(dimension_semantics=("arbitrary",), vmem_limit_bytes=VMEM_LIMIT),
        name="inproj",
    )(x2d, nw, w_a, w_b, w_dt)


def _pool_branch(pext, u, pos, wgrp_ref, pscale_ref, zp):
    nb, t, _ = u.shape
    ys = []
    for g, w in enumerate(POOL_WINDOWS):
        cols = slice(g * POOL_GROUP, (g + 1) * POOL_GROUP)
        win = pext[:, :, cols]
        for sh in [1 << e for e in range(g + 1)]:
            win = win + pltpu.roll(win, sh, axis=1)
        win = win[:, POOL_PAD:, :]
        inv_cnt = 1.0 / jnp.minimum(w, pos + 1).astype(F32)
        d = (win * inv_cnt - u[:, :, cols]).astype(BF16).reshape(nb * t, POOL_GROUP)
        ys.append(_dot(d, wgrp_ref[g]))
    return jnp.concatenate(ys, axis=1) * pscale_ref[...] * _silu(zp)


def _conv_branch(cext, convw_ref, convb_ref, store):
    for cc in range(CONV_DIM // GROUP_WIDTH):
        csl = slice(cc * GROUP_WIDTH, (cc + 1) * GROUP_WIDTH)
        ext = cext[:, :, csl]
        conv = convb_ref[:, csl].reshape(1, 1, GROUP_WIDTH)
        for kk in range(CONV_WIDTH):
            tap = ext if kk == CONV_WIDTH - 1 else pltpu.roll(ext, CONV_WIDTH - 1 - kk, axis=1)
            conv = conv + tap * convw_ref[kk:kk + 1, csl].reshape(1, 1, GROUP_WIDTH)
        store(cc, _silu(conv[:, CONV_PAD:, :]))


PROMPT_TILE = 256


def _seq_prompt_kernel(ntile, x_ref, nw_ref, wa_ref, wb_ref, wdt_ref, k_ref, v_ref,
                       wgrp_ref, pscale_ref, convw_ref, convb_ref, dtb_ref, alog_ref, dexp_ref, ssdnw_ref,
                       gates_ref, ypool_ref, yssd_ref, yatt_ref, pool_o_ref, conv_o_ref, ssm_o_ref,
                       pext, cext, xs_scr, b_scr, c_scr, dt_scr, y_scr, h_scr, zs_scr, q_scr, za_scr):
    t, q = PROMPT_TILE, SSD_CHUNK
    k = pl.program_id(0)
    s = k % ntile
    last = ntile - 1

    @pl.when(k == 0)
    def _():
        y_scr[...] = jnp.zeros(y_scr.shape, F32)
        zs_scr[...] = jnp.zeros(zs_scr.shape, BF16)
        q_scr[...] = jnp.zeros(q_scr.shape, BF16)
        za_scr[...] = jnp.zeros(za_scr.shape, BF16)

    @pl.when(s == 0)
    def _():
        pext[:, 0:POOL_PAD, :] = jnp.zeros((1, POOL_PAD, D_MODEL), F32)
        cext[:, 0:CONV_PAD, :] = jnp.zeros((1, CONV_PAD, CONV_DIM), F32)
        h_scr[...] = jnp.zeros(h_scr.shape, F32)

    @pl.when(s > 0)
    def _():
        carry_p = pext[:, t:t + POOL_PAD, :]
        carry_c = cext[:, t:t + CONV_PAD, :]
        pext[:, 0:POOL_PAD, :] = carry_p
        cext[:, 0:CONV_PAD, :] = carry_c

    yz = y_scr[...] * zs_scr[...].astype(F32)
    yssd_ref[...] = _rms(yz, ssdnw_ref[...]).astype(BF16)
    scale = ATT_HEAD_DIM ** -0.5
    outs = []
    for hd in range(ATT_HEADS):
        hsl = slice(hd * ATT_HEAD_DIM, (hd + 1) * ATT_HEAD_DIM)
        p = _softmax_rows(lax.dot_general(q_scr[:, hsl], k_ref[0, hd], _NT, preferred_element_type=F32) * scale)
        outs.append(_dot(p.astype(BF16), v_ref[0, hd]))
    yatt_ref[...] = (jnp.concatenate(outs, axis=1) * za_scr[...].astype(F32)).astype(BF16)

    hn = _rms(x_ref[...], nw_ref[...]).astype(BF16)

    def proj(piece, lo=0, hi=None):
        return _dot(hn, _weight_cols((wa_ref, wb_ref), piece, lo, piece[1] - piece[0] if hi is None else hi))

    def gates_piece(c):
        def run():
            gates_ref[:, c * D_MODEL:(c + 1) * D_MODEL] = _sigmoid(
                proj(COL_GATES, c * D_MODEL, (c + 1) * D_MODEL).astype(BF16))
        return run

    def zs_piece(c):
        def run():
            zs_scr[:, c * D_MODEL:(c + 1) * D_MODEL] = _silu(proj(COL_ZS, c * D_MODEL, (c + 1) * D_MODEL).astype(BF16))
        return run

    def q_piece():
        q_scr[...] = proj(COL_Q).astype(BF16)

    def za_piece():
        za_scr[...] = _silu(proj(COL_ZA).astype(BF16))

    fillers = [gates_piece(0), gates_piece(1), gates_piece(2), zs_piece(0), zs_piece(1), q_piece, za_piece]

    def run_filler():
        if fillers:
            fillers.pop(0)()

    u = proj(COL_U).reshape(1, t, D_MODEL)
    pext[:, POOL_PAD:, :] = u
    pos = s * t + lax.broadcasted_iota(jnp.int32, (1, t, 1), 1)
    ypool_ref[...] = _pool_branch(pext, u, pos, wgrp_ref, pscale_ref, proj(COL_ZP)).astype(BF16)

    for c in range(CONV_DIM // D_MODEL):
        cext[:, CONV_PAD:, c * D_MODEL:(c + 1) * D_MODEL] = proj(
            COL_XBC, c * D_MODEL, (c + 1) * D_MODEL).reshape(1, t, D_MODEL)
    dt_scr[...] = _softplus(_dot(hn, wdt_ref[...]) + dtb_ref[...])

    def store_conv(cc, val):
        if cc < SSD_GROUPS:
            xs_scr[:, cc * GROUP_WIDTH:(cc + 1) * GROUP_WIDTH] = val[0]
        elif cc == SSD_GROUPS:
            b_scr[...] = val[0]
        else:
            c_scr[...] = val[0]
        run_filler()

    _conv_branch(cext, convw_ref, convb_ref, store_conv)

    a_neg = -jnp.exp(alog_ref[...])
    rq = lax.broadcasted_iota(jnp.int32, (q, q), 0)
    cq = lax.broadcasted_iota(jnp.int32, (q, q), 1)
    tril = rq >= cq
    tri_f = tril.astype(F32)
    lane_lo = lax.broadcasted_iota(jnp.int32, (1, LANES), 1) < SSD_HEAD_DIM
    pairs_per_group = SSD_HEADS // SSD_GROUPS // 2

    def chunk(c):
        rsl = slice(c * q, (c + 1) * q)
        dtc = dt_scr[rsl, :]
        acs = jnp.dot(tri_f, dtc * a_neg, precision=lax.Precision.HIGHEST, preferred_element_type=F32)
        acs_t = acs.T
        dt_t = dtc.T
        wdec_t = (dt_t * jnp.exp(acs_t[:, q - 1:q] - acs_t)).astype(BF16)
        row_t = acs_t - jnp.log(dt_t)
        cdec = jnp.exp(acs[q - 1:q, :])
        for g in range(SSD_GROUPS):
            gsl = slice(g * SSD_STATE, (g + 1) * SSD_STATE)
            bg = b_scr[rsl, gsl]
            cg_b = c_scr[rsl, gsl].astype(BF16)
            cb = lax.dot_general(cg_b, bg.astype(BF16), _NT, preferred_element_type=F32).astype(BF16)
            bg_t = bg.T.astype(BF16)
            hsl = slice(g * GROUP_WIDTH, (g + 1) * GROUP_WIDTH)
            z_g = _dot(cg_b, h_scr[:, hsl].astype(BF16))
            for jp in range(pairs_per_group):
                j = g * pairs_per_group + jp
                lsl = slice(j * LANES, (j + 1) * LANES)
                xp = xs_scr[rsl, lsl]
                xp_b = xp.astype(BF16)
                zero_b = jnp.zeros_like(xp_b)
                x_bd = jnp.concatenate([jnp.where(lane_lo, xp_b, zero_b), jnp.where(lane_lo, zero_b, xp_b)],
                                       axis=0)
                ms, bws, cols = [], [], []
                for hh in range(2):
                    r = 2 * j + hh
                    cols.append(jnp.broadcast_to(acs[:, r:r + 1], (q, q)))
                    seg = cols[hh] - row_t[r:r + 1, :]
                    ms.append(cb * jnp.exp(jnp.where(tril, seg, NEG_BIG)).astype(BF16))
                    bws.append(bg_t * wdec_t[r:r + 1, :])
                ea_pair = jnp.exp(jnp.where(lane_lo, cols[0], cols[1]))
                y = (_dot(jnp.concatenate(ms, axis=1), x_bd)
                     + ea_pair * z_g[:, jp * LANES:(jp + 1) * LANES] + dexp_ref[:, lsl] * xp)
                y_scr[rsl, lsl] = y
                cd_pair = jnp.where(lane_lo, cdec[:, 2 * j:2 * j + 1], cdec[:, 2 * j + 1:2 * j + 2])
                h_scr[:, lsl] = h_scr[:, lsl] * cd_pair + _dot(jnp.concatenate(bws, axis=1), x_bd)
            run_filler()

    for c in range(t // q):
        chunk(c)
    while fillers:
        run_filler()

    @pl.when(jnp.logical_and(s == last, k < pl.num_programs(0) - 1))
    def _():
        pool_o_ref[...] = pext[:, t + POOL_PAD - POOL_HIST:t + POOL_PAD, :]
        conv_o_ref[...] = cext[:, t + CONV_PAD - (CONV_WIDTH - 1):t + CONV_PAD, :]
        for j in range(SSD_HEADS // 2):
            lsl = slice(j * LANES, (j + 1) * LANES)
            ssm_o_ref[0, lsl, :] = h_scr[:, lsl].T


def _seq_prompt(x2d, nw, w_a, w_b, w_dt, kb, vb, params, *, nseq, ntile):
    t = PROMPT_TILE
    m = x2d.shape[0]
    ntiles = nseq * ntile
    tile = lambda k: jnp.minimum(k, ntiles - 1)
    closed = lambda k: jnp.maximum(k - 1, 0)
    own_rows = lambda width: pl.BlockSpec((t, width), lambda k: (k, 0))
    closed_rows = lambda width: pl.BlockSpec((t, width), lambda k: (closed(k), 0))
    seq_spec = lambda shape, which: pl.BlockSpec((1,) + shape, lambda k: (which(k) // ntile,) + (0,) * len(shape))
    const_spec = lambda a: pl.BlockSpec(a.shape, lambda k: (0,) * a.ndim)
    resident = lambda a: pl.BlockSpec(a.shape, lambda k: (0,) * a.ndim, pipeline_mode=pl.Buffered(1))
    in_specs = [
        pl.BlockSpec((t, D_MODEL), lambda k: (tile(k), 0)),
        resident(nw),
        pl.BlockSpec((D_MODEL, W_SPLIT[0]), lambda k: (0, 0), pipeline_mode=pl.Buffered(1)),
        resident(w_b), resident(w_dt),
        seq_spec((ATT_HEADS, MEM_LEN, ATT_HEAD_DIM), closed),
        seq_spec((ATT_HEADS, MEM_LEN, ATT_HEAD_DIM), closed),
    ] + [const_spec(p) for p in params]
    out_specs = [
        own_rows(3 * D_MODEL), own_rows(D_MODEL), closed_rows(SSD_WIDTH), closed_rows(D_MODEL),
        seq_spec((POOL_HIST, D_MODEL), tile),
        seq_spec((CONV_WIDTH - 1, CONV_DIM), tile),
        seq_spec((SSD_WIDTH, SSD_STATE), tile),
    ]
    out_shape = [
        jax.ShapeDtypeStruct((m + t, 3 * D_MODEL), BF16),
        jax.ShapeDtypeStruct((m + t, D_MODEL), BF16),
        jax.ShapeDtypeStruct((m, SSD_WIDTH), BF16),
        jax.ShapeDtypeStruct((m, D_MODEL), BF16),
        jax.ShapeDtypeStruct((nseq, POOL_HIST, D_MODEL), F32),
        jax.ShapeDtypeStruct((nseq, CONV_WIDTH - 1, CONV_DIM), F32),
        jax.ShapeDtypeStruct((nseq, SSD_WIDTH, SSD_STATE), F32),
    ]
    scratch = [
        pltpu.VMEM((1, POOL_PAD + t, D_MODEL), F32),
        pltpu.VMEM((1, CONV_PAD + t, CONV_DIM), F32),
        pltpu.VMEM((t, SSD_WIDTH), F32),
        pltpu.VMEM((t, GROUP_WIDTH), F32),
        pltpu.VMEM((t, GROUP_WIDTH), F32),
        pltpu.VMEM((t, LANES), F32),
        pltpu.VMEM((t, SSD_WIDTH), F32),
        pltpu.VMEM((SSD_STATE, SSD_WIDTH), F32),
        pltpu.VMEM((t, SSD_WIDTH), BF16),
        pltpu.VMEM((t, D_MODEL), BF16),
        pltpu.VMEM((t, D_MODEL), BF16),
    ]
    return pl.pallas_call(
        functools.partial(_seq_prompt_kernel, ntile),
        grid=(ntiles + 1,),
        in_specs=in_specs,
        out_specs=out_specs,
        out_shape=out_shape,
        scratch_shapes=scratch,
        compiler_params=pltpu.CompilerParams(dimension_semantics=("arbitrary",),
                                             vmem_limit_bytes=SEQ_PROMPT_VMEM_LIMIT),
        name="seq_prompt",
    )(x2d, nw, w_a, w_b, w_dt, kb, vb, *params)


SAMPLE_BLOCK = 32
STATE_PRE_SUB = 4


def _state_pre_kernel(u_ref, zp_ref, xbc_ref, dt_ref, ph_ref, ch_ref,
                      wgrp_ref, pscale_ref, convw_ref, convb_ref, dtb_ref, alog_ref,
                      ypool_ref, xs_ref, bc_ref, dts_ref, cd_ref, pool_o_ref, conv_o_ref, pext, cext):
    nb, t = SAMPLE_BLOCK, SUBLANES
    pext[:, 0:1, :] = jnp.zeros((nb, 1, D_MODEL), F32)
    pext[:, 1:POOL_PAD, :] = ph_ref[...]
    cext[:, 0:CONV_PAD - (CONV_WIDTH - 1), :] = jnp.zeros((nb, CONV_PAD - (CONV_WIDTH - 1), CONV_DIM), F32)
    cext[:, CONV_PAD - (CONV_WIDTH - 1):CONV_PAD, :] = ch_ref[...]

    pext[:, POOL_PAD:, :] = u_ref[...].astype(F32).reshape(nb, t, D_MODEL)
    cext[:, CONV_PAD:, :] = xbc_ref[...].astype(F32).reshape(nb, t, CONV_DIM)
    pos = PAST_LEN + lax.broadcasted_iota(jnp.int32, (1, t, 1), 1)
    for sb in range(nb // STATE_PRE_SUB):
        seqs = slice(sb * STATE_PRE_SUB, (sb + 1) * STATE_PRE_SUB)
        rsl = slice(sb * STATE_PRE_SUB * t, (sb + 1) * STATE_PRE_SUB * t)
        pv, cv = pext.at[seqs], cext.at[seqs]
        ypool_ref[rsl, :] = _pool_branch(pv, pv[:, POOL_PAD:, :], pos, wgrp_ref, pscale_ref,
                                         zp_ref[rsl, :].astype(F32)).astype(BF16)

        def store_conv(cc, val, rsl=rsl):
            val = val.reshape(STATE_PRE_SUB * t, GROUP_WIDTH)
            if cc < SSD_GROUPS:
                xs_ref[rsl, cc * GROUP_WIDTH:(cc + 1) * GROUP_WIDTH] = val
            else:
                bc_ref[rsl, (cc - SSD_GROUPS) * GROUP_WIDTH:(cc - SSD_GROUPS + 1) * GROUP_WIDTH] = val

        _conv_branch(cv, convw_ref, convb_ref, store_conv)
    dt = _softplus(dt_ref[...] + dtb_ref[...])
    dts_ref[...] = dt
    a = dt * -jnp.exp(alog_ref[...])
    cd_ref[...] = jnp.exp(jnp.sum(a.reshape(nb, t, LANES), axis=1))
    pool_o_ref[...] = pext[:, t + POOL_PAD - POOL_HIST:t + POOL_PAD, :]
    conv_o_ref[...] = cext[:, t + CONV_PAD - (CONV_WIDTH - 1):t + CONV_PAD, :]


def _col_block(piece):
    return piece[0] // (piece[1] - piece[0])


def _state_pre(main, dt, state_pool, state_conv, params):
    nb, t = SAMPLE_BLOCK, SUBLANES
    rows = nb * t
    m = main.shape[0]
    nseq = m // t
    col_spec = lambda width, idx: pl.BlockSpec((rows, width), lambda i: (i, idx))
    seq_spec = lambda shape: pl.BlockSpec((nb,) + shape, lambda i: (i,) + (0,) * len(shape))
    const_spec = lambda a: pl.BlockSpec(a.shape, lambda i: (0,) * a.ndim)
    return pl.pallas_call(
        _state_pre_kernel,
        grid=(nseq // nb,),
        in_specs=[col_spec(D_MODEL, _col_block(COL_U)), col_spec(D_MODEL, _col_block(COL_ZP)),
                  col_spec(CONV_DIM, _col_block(COL_XBC)), col_spec(LANES, 0),
                  seq_spec((POOL_HIST, D_MODEL)), seq_spec((CONV_WIDTH - 1, CONV_DIM))]
        + [const_spec(p) for p in params],
        out_specs=[col_spec(D_MODEL, 0), col_spec(SSD_WIDTH, 0), col_spec(2 * GROUP_WIDTH, 0), col_spec(LANES, 0),
                   pl.BlockSpec((nb, LANES), lambda i: (i, 0)),
                   seq_spec((POOL_HIST, D_MODEL)), seq_spec((CONV_WIDTH - 1, CONV_DIM))],
        out_shape=[jax.ShapeDtypeStruct((m, D_MODEL), BF16), jax.ShapeDtypeStruct((m, SSD_WIDTH), F32),
                   jax.ShapeDtypeStruct((m, 2 * GROUP_WIDTH), F32), jax.ShapeDtypeStruct((m, LANES), F32),
                   jax.ShapeDtypeStruct((nseq, LANES), F32),
                   jax.ShapeDtypeStruct((nseq, POOL_HIST, D_MODEL), F32),
                   jax.ShapeDtypeStruct((nseq, CONV_WIDTH - 1, CONV_DIM), F32)],
        scratch_shapes=[pltpu.VMEM((nb, POOL_PAD + t, D_MODEL), F32), pltpu.VMEM((nb, CONV_PAD + t, CONV_DIM), F32)],
        compiler_params=pltpu.CompilerParams(dimension_semantics=("arbitrary",), vmem_limit_bytes=VMEM_LIMIT),
        name="state_pre",
    )(main, main, main, dt, state_pool, state_conv, *params)


PIPE_SLOTS = 4


def _ring_pipeline(n, start_in, wait_in, compute, start_out=None, wait_out=None):
    ns = PIPE_SLOTS
    for i in range(ns - 1):
        start_in(i, i)

    def body(bb, _):
        for k in range(ns):
            i = ns * bb + k
            nxt = i + ns - 1
            pl.when(nxt < n)(functools.partial(start_in, nxt, (k + ns - 1) % ns))
            wait_in(i, k)
            if wait_out is not None:
                pl.when(bb > 0)(functools.partial(wait_out, i - ns, k))
            compute(i, k)
            if start_out is not None:
                start_out(i, k)
        return 0

    lax.fori_loop(0, n // ns, body, 0)
    if wait_out is not None:
        for k in range(ns):
            wait_out(n - ns + k, k)


SSD_ITEM = 4


def _ssd_state_kernel(xs_ref, bc_ref, dt_ref, zs_ref, cd_ref, hin_hbm, expand_ref, segsum_ref, alog_ref, dexp_ref,
                      ssdnw_ref, yssd_ref, hout_hbm, hbuf, obuf, y_scr, sem_in, sem_out):
    nb, t = SAMPLE_BLOCK, SUBLANES
    base = pl.program_id(0) * nb
    a_neg = -jnp.exp(alog_ref[...])
    ridx = lax.broadcasted_iota(jnp.int32, (t, LANES), 0)

    def in_copies(item, slot):
        return [pltpu.make_async_copy(hin_hbm.at[base + item * SSD_ITEM + j, g * GROUP_WIDTH:(g + 1) * GROUP_WIDTH, :],
                                      hbuf.at[slot, j, g * GROUP_WIDTH:(g + 1) * GROUP_WIDTH, :],
                                      sem_in.at[slot, j * SSD_GROUPS + g])
                for j in range(SSD_ITEM) for g in range(SSD_GROUPS)]

    def out_copies(item, slot):
        return [pltpu.make_async_copy(obuf.at[slot, j, g * GROUP_WIDTH:(g + 1) * GROUP_WIDTH, :],
                                      hout_hbm.at[base + item * SSD_ITEM + j, g * GROUP_WIDTH:(g + 1) * GROUP_WIDTH, :],
                                      sem_out.at[slot, j * SSD_GROUPS + g])
                for j in range(SSD_ITEM) for g in range(SSD_GROUPS)]

    def start_all(copies):
        for c in copies:
            c.start()

    def wait_all(copies):
        for c in copies:
            c.wait()

    def per_head_factors(b):
        rsl = pl.ds(pl.multiple_of(b * t, t), t)
        dtc = dt_ref[rsl, :]
        acs = dtc * a_neg
        for sh in (1, 2, 4):
            acs = acs + jnp.where(ridx >= sh, pltpu.roll(acs, sh, axis=0), 0.0)
        tot = acs[t - 1:t, :]
        bc = bc_ref[rsl, :]
        bm, cm = bc[:, :GROUP_WIDTH], bc[:, GROUP_WIDTH:]
        bm_r, cm_r = bm.astype(BF16).astype(F32), cm.astype(BF16).astype(F32)
        gs, ps = [], []
        for k in range(t):
            gs.append(jnp.exp(jnp.where(ridx >= k, acs - acs[k:k + 1, :], NEG_BIG)) * dtc[k:k + 1, :])
            ps.append(cm_r * bm_r[k:k + 1, :])
        cb_heads = _dot(jnp.concatenate(ps, axis=0).astype(BF16), segsum_ref[...])
        per_head = jnp.concatenate([jnp.concatenate(gs, axis=0) * cb_heads, jnp.exp(acs),
                                    dtc * jnp.exp(tot - acs)], axis=0)
        return dict(rsl=rsl, b=b, bm=bm, cm=cm, per_head=per_head)

    def widen(st):
        hi = st["per_head"].astype(BF16)
        lo = (st["per_head"] - hi.astype(F32)).astype(BF16)
        st["wide"] = _dot(hi, expand_ref[...]) + _dot(lo, expand_ref[...])

    def intra(st):
        x = xs_ref[st["rsl"], :]
        wide = st["wide"]
        y = dexp_ref[...] * x
        for k in range(t):
            y = y + wide[k * t:(k + 1) * t, :] * x[k:k + 1, :]
        st["y"] = y
        st["ea_wide"] = wide[t * t:t * t + t, :]
        st["xw"] = x * wide[t * t + t:, :]

    def state_group(st, slot, j, g):
        gsl = slice(g * SSD_STATE, (g + 1) * SSD_STATE)
        wsl = slice(g * GROUP_WIDTH, (g + 1) * GROUP_WIDTH)
        hg = hbuf[slot, j, wsl, :]
        z_g = lax.dot_general(st["cm"][:, gsl].astype(BF16), hg.astype(BF16), _NT, preferred_element_type=F32)
        y_scr[st["rsl"], wsl] = st["y"][:, wsl] + st["ea_wide"][:, wsl] * z_g
        upd = lax.dot_general(st["xw"][:, wsl].astype(BF16), st["bm"][:, gsl].astype(BF16), _TN,
                              preferred_element_type=F32)
        for r8 in range(SSD_HEADS // SSD_GROUPS):
            r = g * (SSD_HEADS // SSD_GROUPS) + r8
            rows_r = slice(r * SSD_HEAD_DIM, (r + 1) * SSD_HEAD_DIM)
            obuf[slot, j, rows_r, :] = (hbuf[slot, j, rows_r, :] * cd_ref[base + st["b"], r]
                                        + upd[r8 * SSD_HEAD_DIM:(r8 + 1) * SSD_HEAD_DIM, :])

    def compute(item, slot):
        sts = [per_head_factors(item * SSD_ITEM + j) for j in range(SSD_ITEM)]
        for st in sts:
            widen(st)
        for st in sts:
            intra(st)
        for g in range(SSD_GROUPS):
            for j, st in enumerate(sts):
                state_group(st, slot, j, g)

    _ring_pipeline(nb // SSD_ITEM,
                   lambda i, slot: start_all(in_copies(i, slot)), lambda i, slot: wait_all(in_copies(i, slot)),
                   compute,
                   lambda i, slot: start_all(out_copies(i, slot)), lambda i, slot: wait_all(out_copies(i, slot)))
    yz = y_scr[...] * _silu(zs_ref[...].astype(F32))
    yssd_ref[...] = _rms(yz, ssdnw_ref[...]).astype(BF16)


def _ssd_state(xs, bc, dts, main, cd, hin, expand, segsum, alog, dexp, ssdnw):
    nb, t = SAMPLE_BLOCK, SUBLANES
    rows = nb * t
    m = xs.shape[0]
    nseq = m // t
    col_spec = lambda width, idx: pl.BlockSpec((rows, width), lambda i: (i, idx))
    const_spec = lambda a: pl.BlockSpec(a.shape, lambda i: (0,) * a.ndim)
    return pl.pallas_call(
        _ssd_state_kernel,
        grid=(nseq // nb,),
        in_specs=[col_spec(SSD_WIDTH, 0), col_spec(2 * GROUP_WIDTH, 0), col_spec(LANES, 0),
                  col_spec(SSD_WIDTH, _col_block(COL_ZS)),
                  pl.BlockSpec(memory_space=pltpu.SMEM), pl.BlockSpec(memory_space=pl.ANY),
                  const_spec(expand), const_spec(segsum), const_spec(alog), const_spec(dexp), const_spec(ssdnw)],
        out_specs=[col_spec(SSD_WIDTH, 0), pl.BlockSpec(memory_space=pl.ANY)],
        out_shape=[jax.ShapeDtypeStruct((m, SSD_WIDTH), BF16), jax.ShapeDtypeStruct(hin.shape, F32)],
        scratch_shapes=[pltpu.VMEM((PIPE_SLOTS, SSD_ITEM, SSD_WIDTH, SSD_STATE), F32),
                        pltpu.VMEM((PIPE_SLOTS, SSD_ITEM, SSD_WIDTH, SSD_STATE), F32),
                        pltpu.VMEM((rows, SSD_WIDTH), F32),
                        pltpu.SemaphoreType.DMA((PIPE_SLOTS, SSD_ITEM * SSD_GROUPS)),
                        pltpu.SemaphoreType.DMA((PIPE_SLOTS, SSD_ITEM * SSD_GROUPS))],
        compiler_params=pltpu.CompilerParams(dimension_semantics=("arbitrary",), vmem_limit_bytes=VMEM_LIMIT),
        name="ssd_state",
    )(xs, bc, dts, main, cd, hin, expand, segsum, alog, dexp, ssdnw)


ATT_ITEM = 4
ATT_BLOCK = 64


def _att_state_kernel(q_ref, za_ref, k_hbm, v_hbm, yatt_ref, kbuf, vbuf, q_scr, att_scr, sem):
    nb, t = ATT_BLOCK, SUBLANES
    base = pl.program_id(0) * nb
    scale = ATT_HEAD_DIM ** -0.5
    q_scr[...] = q_ref[...].astype(F32)
    head_of_lane = lax.broadcasted_iota(jnp.int32, (1, D_MODEL), 1) // ATT_HEAD_DIM

    def copies(item, slot):
        out = []
        for j in range(ATT_ITEM):
            for kv, (src, buf) in enumerate(((k_hbm, kbuf), (v_hbm, vbuf))):
                for hd in range(ATT_HEADS):
                    hsl = slice(hd * ATT_HEAD_DIM, (hd + 1) * ATT_HEAD_DIM)
                    out.append(pltpu.make_async_copy(
                        src.at[base + item * ATT_ITEM + j, :, hd, :], buf.at[slot, j, :, hsl],
                        sem.at[slot, (j * 2 + kv) * ATT_HEADS + hd]))
        return out

    def start_in(item, slot):
        for n, c in enumerate(copies(item, slot)):
            c.start(priority=n % 2)

    def wait_in(item, slot):
        for c in copies(item, slot):
            c.wait()

    def compute(item, slot):
        seqs = range(ATT_ITEM)
        rsl = [pl.ds(pl.multiple_of((item * ATT_ITEM + j) * t, t), t) for j in seqs]
        scs = []
        for j in seqs:
            qf = q_scr[rsl[j], :]
            q_bd = jnp.concatenate([jnp.where(head_of_lane == hd, qf, 0.0) for hd in range(ATT_HEADS)],
                                   axis=0).astype(BF16)
            scs.append(lax.dot_general(q_bd, kbuf[slot, j].astype(BF16), _NT, preferred_element_type=F32) * scale)
        ps = [_softmax_rows(sc).astype(BF16) for sc in scs]
        outs = [_dot(ps[j], vbuf[slot, j].astype(BF16)) for j in seqs]
        for j in seqs:
            att_scr[rsl[j], :] = jnp.concatenate(
                [outs[j][hd * t:(hd + 1) * t, hd * ATT_HEAD_DIM:(hd + 1) * ATT_HEAD_DIM] for hd in range(ATT_HEADS)],
                axis=1)

    _ring_pipeline(nb // ATT_ITEM, start_in, wait_in, compute)
    yatt_ref[...] = (att_scr[...] * _silu(za_ref[...].astype(F32))).astype(BF16)


def _att_state(main, k, v):
    nb, t = ATT_BLOCK, SUBLANES
    rows = nb * t
    m = main.shape[0]
    col_spec = lambda width, idx: pl.BlockSpec((rows, width), lambda i: (i, idx))
    kv_buf = pltpu.VMEM((PIPE_SLOTS, ATT_ITEM, MEM_LEN, D_MODEL), F32)
    return pl.pallas_call(
        _att_state_kernel,
        grid=(m // rows,),
        in_specs=[col_spec(D_MODEL, _col_block(COL_Q)), col_spec(D_MODEL, _col_block(COL_ZA)),
                  pl.BlockSpec(memory_space=pl.ANY), pl.BlockSpec(memory_space=pl.ANY)],
        out_specs=col_spec(D_MODEL, 0),
        out_shape=jax.ShapeDtypeStruct((m, D_MODEL), BF16),
        scratch_shapes=[kv_buf, kv_buf, pltpu.VMEM((rows, D_MODEL), F32), pltpu.VMEM((rows, D_MODEL), F32),
                        pltpu.SemaphoreType.DMA((PIPE_SLOTS, ATT_ITEM * 2 * ATT_HEADS))],
        compiler_params=pltpu.CompilerParams(dimension_semantics=("arbitrary",), vmem_limit_bytes=VMEM_LIMIT),
        name="att_state",
    )(main, main, k, v)


DENSE_ROWS = 512


def _dense_kernel(x_ref, gt_ref, yp_ref, ys_ref, ya_ref, wpo_ref, wso_ref, wao_ref, wo_ref, fnw_ref, y_ref):
    gates = gt_ref[...].astype(F32)
    merged = (gates[:, 0:D_MODEL] * _dot(yp_ref[...], wpo_ref[...])
              + gates[:, D_MODEL:2 * D_MODEL] * _dot(ys_ref[...], wso_ref[...])
              + gates[:, 2 * D_MODEL:] * _dot(ya_ref[...], wao_ref[...]))
    x_out = x_ref[...] + _dot(merged.astype(BF16), wo_ref[...])
    y_ref[...] = _rms(x_out, fnw_ref[...])


def _dense(x2d, gates, gate_idx, yp, ys, ya, wpo, wso, wao, wo, fnw):
    m = x2d.shape[0]
    row = lambda width, idx=0: pl.BlockSpec((DENSE_ROWS, width), lambda i: (i, idx))
    resident = lambda a: pl.BlockSpec(a.shape, lambda i: (0,) * a.ndim, pipeline_mode=pl.Buffered(1))
    return pl.pallas_call(
        _dense_kernel,
        grid=(m // DENSE_ROWS,),
        in_specs=[row(D_MODEL), row(3 * D_MODEL, gate_idx), row(D_MODEL), row(SSD_WIDTH), row(D_MODEL),
                  resident(wpo), resident(wso), resident(wao), resident(wo), resident(fnw)],
        out_specs=row(D_MODEL),
        out_shape=jax.ShapeDtypeStruct((m, D_MODEL), F32),
        compiler_params=pltpu.CompilerParams(dimension_semantics=("arbitrary",), vmem_limit_bytes=VMEM_LIMIT),
        name="dense",
    )(x2d, gates, yp, ys, ya, wpo, wso, wao, wo, fnw)


def kernel(x_prompt, x_sample, mem_prompt, state_pool, state_conv, state_ssm, cache_mem_k, cache_mem_v,
           norm_w, w_in, w_pool_grp, pool_scale, conv_w, conv_b, dt_bias, a_log, d_skip, ssd_norm_w,
           mem_norm_w, w_mem_k, w_mem_v, w_pool_out, w_ssd_out, w_att_out, w_out, final_norm_w):
    assert w_in.shape[0] == 1
    bp, sp, d = x_prompt.shape
    bs, ss, _ = x_sample.shape
    assert ss == SUBLANES and sp % PROMPT_TILE == 0 and bs % SAMPLE_BLOCK == 0 and bs % ATT_BLOCK == 0

    w_a = w_in[0].astype(BF16)
    w_b = w_a[:, W_SPLIT[1]:]
    w_dt = jnp.pad(w_a[:, W_SPLIT[0]:W_SPLIT[1]], ((0, 0), (0, LANES - SSD_HEADS)))
    nw = norm_w[0].reshape(1, d)
    pad_heads = lambda a: jnp.pad(a.reshape(1, SSD_HEADS), ((0, 0), (0, LANES - SSD_HEADS)))
    wgrp = w_pool_grp[0].astype(BF16)
    pscale = pool_scale[0].reshape(1, d)
    convb = conv_b[0].reshape(1, CONV_DIM)
    dtb, alog = pad_heads(dt_bias[0]), pad_heads(a_log[0])
    dexp = jnp.repeat(d_skip[0], SSD_HEAD_DIM).reshape(1, SSD_WIDTH)
    ssdnw = ssd_norm_w[0].reshape(1, SSD_WIDTH)
    dense_w = (w_pool_out[0].astype(BF16), w_ssd_out[0].astype(BF16), w_att_out[0].astype(BF16),
               w_out[0].astype(BF16), final_norm_w.reshape(1, d))
    head_of_lane = jnp.arange(SSD_WIDTH) // SSD_HEAD_DIM
    expand = (jnp.arange(LANES)[:, None] == head_of_lane[None, :]).astype(BF16)
    group_of_head = jnp.where(jnp.arange(LANES) < SSD_HEADS, jnp.arange(LANES) // (SSD_HEADS // SSD_GROUPS), -1)
    segsum = ((jnp.arange(GROUP_WIDTH) // SSD_STATE)[:, None] == group_of_head[None, :]).astype(BF16)

    mk, mv, mkb, mvb = _memkv(mem_prompt, mem_norm_w[0].reshape(1, d), w_mem_k[0].astype(BF16),
                              w_mem_v[0].astype(BF16))
    xp2 = x_prompt.reshape(bp * sp, d)
    gates_p, yp, ysd, ya, pool_p, conv_p, ssm_p = _seq_prompt(
        xp2, nw, w_a, w_b, w_dt, mkb, mvb, (wgrp, pscale, conv_w[0], convb, dtb, alog, dexp, ssdnw),
        nseq=bp, ntile=sp // PROMPT_TILE)
    y_prompt = _dense(xp2, gates_p, 0, yp, ysd, ya, *dense_w).reshape(bp, sp, d)

    xs2 = x_sample.reshape(bs * ss, d)
    main_s, dt_s = _inproj(xs2, nw, w_a, w_b, w_dt)
    yp, xs, bc, dts, cd, pool_s, conv_s = _state_pre(
        main_s, dt_s, state_pool[0], state_conv[0], (wgrp, pscale, conv_w[0], convb, dtb, alog))
    ysd, ssm_s = _ssd_state(xs, bc, dts, main_s, cd, state_ssm[0].reshape(bs, SSD_WIDTH, SSD_STATE),
                            expand, segsum, alog, dexp, ssdnw)
    ya = _att_state(main_s, cache_mem_k[0], cache_mem_v[0])
    y_sample = _dense(xs2, main_s, _col_block(COL_GATES), yp, ysd, ya, *dense_w).reshape(bs, ss, d)

    ssm_shape = (SSD_GROUPS, SSD_HEADS // SSD_GROUPS, SSD_HEAD_DIM, SSD_STATE)
    return (y_prompt, y_sample,
            pool_p[None], conv_p[None], ssm_p.reshape((1, bp) + ssm_shape),
            mk[None], mv[None],
            pool_s[None], conv_s[None], ssm_s.reshape((1, bs) + ssm_shape))
```
